```python
import jax
import jax.numpy as jnp
from jax import lax
import numpy as np

D_MODEL = 1024
BATCH = 4
SEQ = 4096
DEPTH = 4

N_EVEN = (DEPTH + 1) // 2
N_ODD = DEPTH // 2

MLA_HEADS = 8
MLA_NOPE = 64
MLA_ROPE = 32
MLA_V = 64
MLA_Q_RANK = 256
MLA_KV_RANK = 128
MLA_COLS = MLA_Q_RANK + MLA_KV_RANK + MLA_ROPE
MLA_SCALE = (MLA_NOPE + MLA_ROPE) ** -0.5
Q_BLOCK = 128

GDN_HEADS = 8
GDN_DK = 64
GDN_DV = 64
GDN_CONV = 4
GDN_CHUNK = 64
GDN_QKV = GDN_HEADS * (2 * GDN_DK + GDN_DV)
GDN_COLS = GDN_QKV + 2 * GDN_HEADS + GDN_HEADS * GDN_DV

HY_IN = MLA_COLS + GDN_COLS
HY_MIX = MLA_HEADS * MLA_V + GDN_HEADS * GDN_DV

RET_HEADS = 8
RET_DK = D_MODEL // RET_HEADS
RET_DV = 2 * D_MODEL // RET_HEADS
RET_CHUNK = 128
RET_IN = RET_HEADS * (2 * RET_DK + 2 * RET_DV)
RET_MIX = RET_HEADS * RET_DV

N_GROUPS = 4
EXPERTS_PER_GROUP = 8
N_EXPERTS = N_GROUPS * EXPERTS_PER_GROUP
TOP_K_IN_GROUP = 2
EXPERT_FF = 512
MOE_BLOCK = 128

ROPE_BASE = 10000.0
LN_EPS = 1e-5
RMS_EPS = 1e-6
DEEPNORM_ALPHA = (2.0 * DEPTH) ** 0.25
DEEPNORM_BETA = (8.0 * DEPTH) ** -0.25
N_MOD = 6

kernel_name = 'hybrid_mla_gdn_retention_hmoe_deepnorm'


def layer_norm(x, g, b):
    xf = x.astype(jnp.float32)
    mu = jnp.mean(xf, -1, keepdims=True)
    var = jnp.mean(jnp.square(xf - mu), -1, keepdims=True)
    return ((xf - mu) * lax.rsqrt(var + LN_EPS)).astype(x.dtype) * g + b


def rms_norm(x, g):
    xf = x.astype(jnp.float32)
    return (xf * lax.rsqrt(jnp.mean(xf * xf, -1, keepdims=True) + RMS_EPS)).astype(x.dtype) * g


def l2_normalize(x):
    xf = x.astype(jnp.float32)
    return (xf * lax.rsqrt(jnp.sum(xf * xf, -1, keepdims=True) + RMS_EPS)).astype(x.dtype)


def rope_tables(positions, dim):
    inv_freq = ROPE_BASE ** (-jnp.arange(0, dim, 2, dtype=jnp.float32) / dim)
    ang = positions.astype(jnp.float32)[..., None] * inv_freq
    return jnp.cos(ang), jnp.sin(ang)


def apply_rope(x, cos, sin):
    x1, x2 = jnp.split(x, 2, axis=-1)
    c = cos[:, :, None, :].astype(x.dtype)
    s = sin[:, :, None, :].astype(x.dtype)
    return jnp.concatenate([x1 * c - x2 * s, x1 * s + x2 * c], axis=-1)


def to_chunks(x, chunk):
    B, S, H, d = x.shape
    return x.reshape(B, S // chunk, chunk, H, d).transpose(0, 3, 1, 2, 4).astype(jnp.float32)


def from_chunks(x):
    B, H, N, C, d = x.shape
    return x.transpose(0, 2, 3, 1, 4).reshape(B, N * C, H, d)


def causal_attention_blocks(q, k, v):
    B, S, H, Dq = q.shape
    Dv = v.shape[-1]
    nb = S // Q_BLOCK
    qb = jnp.moveaxis(q.reshape(B, nb, Q_BLOCK, H, Dq), 1, 0)
    kpos = jnp.arange(S)

    def one_block(args):
        q_blk, blk = args
        s = jnp.einsum('bqhd,bkhd->bhqk', q_blk, k).astype(jnp.float32)
        qpos = blk * Q_BLOCK + jnp.arange(Q_BLOCK)
        s = jnp.where(kpos[None, :] <= qpos[:, None], s, -jnp.inf)
        p = jax.nn.softmax(s, axis=-1).astype(v.dtype)
        return jnp.einsum('bhqk,bkhd->bqhd', p, v)

    out = lax.map(one_block, (qb, jnp.arange(nb)))
    return jnp.moveaxis(out, 0, 1).reshape(B, S, H, Dv)


def latent_attention(h_mla, positions, q_norm, w_uq, kv_norm, w_ukv):
    B, S, _ = h_mla.shape
    q_lat, kv_lat, k_rope = jnp.split(h_mla, [MLA_Q_RANK, MLA_Q_RANK + MLA_KV_RANK], axis=-1)
    q = (rms_norm(q_lat, q_norm) @ w_uq).reshape(B, S, MLA_HEADS, MLA_NOPE + MLA_ROPE)
    kv = (rms_norm(kv_lat, kv_norm) @ w_ukv).reshape(B, S, MLA_HEADS, MLA_NOPE + MLA_V)
    q_nope, q_rope = jnp.split(q, [MLA_NOPE], axis=-1)
    k_nope, v = jnp.split(kv, [MLA_NOPE], axis=-1)
    cos, sin = rope_tables(positions, MLA_ROPE)
    q_rope = apply_rope(q_rope, cos, sin)
    k_rope = jnp.broadcast_to(apply_rope(k_rope[:, :, None, :], cos, sin), (B, S, MLA_HEADS, MLA_ROPE))
    q = jnp.concatenate([q_nope, q_rope], axis=-1) * MLA_SCALE
    k = jnp.concatenate([k_nope, k_rope], axis=-1)
    o = causal_attention_blocks(q, k, v)
    return o.reshape(B, S, MLA_HEADS * MLA_V)


def causal_depthwise_conv(x, w):
    K = w.shape[0]
    return lax.conv_general_dilated(x, w[:, None, :].astype(x.dtype), window_strides=(1,), padding=[(K - 1, 0)],
                                    dimension_numbers=('NWC', 'WIO', 'NWC'), feature_group_count=x.shape[-1])


def chunk_gated_delta_rule(q, k, v, log_alpha, beta):
    B, S, H, DK = q.shape
    DV = v.shape[-1]
    C = GDN_CHUNK
    N = S // C
    qc, kc, vc = (to_chunks(t, C) for t in (q, k, v))
    bc = beta.reshape(B, N, C, H).transpose(0, 3, 1, 2).astype(jnp.float32)
    g = jnp.cumsum(log_alpha.reshape(B, N, C, H).transpose(0, 3, 1, 2), axis=-1)
    tril = jnp.tril(jnp.ones((C, C), bool))
    strict = jnp.tril(jnp.ones((C, C), bool), -1)
    gdiff = g[..., :, None] - g[..., None, :]
    decay = jnp.where(tril, jnp.exp(jnp.where(tril, gdiff, 0.0)), 0.0)
    kb = kc * bc[..., None]
    L = jnp.where(strict, jnp.einsum('bhnid,bhnjd->bhnij', kb, kc) * decay, 0.0)
    eye = jnp.eye(C, dtype=jnp.float32)
    T = lax.linalg.triangular_solve(eye + L, jnp.broadcast_to(eye, L.shape), left_side=True, lower=True)
    u = jnp.einsum('bhnij,bhnje->bhnie', T, vc * bc[..., None])
    w = jnp.einsum('bhnij,bhnjd->bhnid', T, kb * jnp.exp(g)[..., None])
    a_intra = jnp.where(tril, jnp.einsum('bhnid,bhnjd->bhnij', qc, kc) * decay, 0.0)

    def step(state, xs):
        q_i, k_i, u_i, w_i, g_i, a_i = xs
        v_new = u_i - jnp.einsum('bhcd,bhde->bhce', w_i, state)
        o_i = jnp.einsum('bhcd,bhde->bhce', q_i * jnp.exp(g_i)[..., None], state) + jnp.einsum('bhij,bhje->bhie', a_i, v_new)
        g_last = g_i[..., -1]
        k_dec = k_i * jnp.exp(g_last[..., None] - g_i)[..., None]
        state = state * jnp.exp(g_last)[..., None, None] + jnp.einsum('bhcd,bhce->bhde', k_dec, v_new)
        return state, o_i

    state0 = jnp.zeros((B, H, DK, DV), jnp.float32)
    xs = tuple(jnp.moveaxis(t, 2, 0) for t in (qc, kc, u, w, g, a_intra))
    _, o = lax.scan(step, state0, xs)
    return from_chunks(jnp.moveaxis(o, 0, 2)).astype(v.dtype)


def gated_deltanet(h_gdn, conv_w, a_log, dt_bias, o_norm):
    B, S, _ = h_gdn.shape
    qkv, a, b, gate = jnp.split(h_gdn, [GDN_QKV, GDN_QKV + GDN_HEADS, GDN_QKV + 2 * GDN_HEADS], axis=-1)
    qkv = jax.nn.silu(causal_depthwise_conv(qkv, conv_w))
    q, k, v = jnp.split(qkv, [GDN_HEADS * GDN_DK, 2 * GDN_HEADS * GDN_DK], axis=-1)
    q = l2_normalize(q.reshape(B, S, GDN_HEADS, GDN_DK)) * (GDN_DK ** -0.5)
    k = l2_normalize(k.reshape(B, S, GDN_HEADS, GDN_DK))
    v = v.reshape(B, S, GDN_HEADS, GDN_DV)
    beta = jax.nn.sigmoid(b.astype(jnp.float32))
    log_alpha = -jnp.exp(a_log.astype(jnp.float32)) * jax.nn.softplus(a.astype(jnp.float32) + dt_bias.astype(jnp.float32))
    o = chunk_gated_delta_rule(q, k, v, log_alpha, beta)
    o = rms_norm(o, o_norm) * jax.nn.silu(gate.reshape(B, S, GDN_HEADS, GDN_DV))
    return o.reshape(B, S, GDN_HEADS * GDN_DV)


def mla_gdn_mixer(h, positions, w_in, q_norm, w_uq, kv_norm, w_ukv, conv_w, a_log, dt_bias, o_norm, w_out):
    proj = h @ w_in
    h_mla, h_gdn = jnp.split(proj, [MLA_COLS], axis=-1)
    y_mla = latent_attention(h_mla, positions, q_norm, w_uq, kv_norm, w_ukv)
    y_gdn = gated_deltanet(h_gdn, conv_w, a_log, dt_bias, o_norm)
    return jnp.concatenate([y_mla, y_gdn], axis=-1) @ w_out


def chunkwise_retention(q, k, v):
    B, S, H, DK = q.shape
    DV = v.shape[-1]
    C = RET_CHUNK
    log_gamma = jnp.log(1.0 - 2.0 ** (-5.0 - jnp.arange(H, dtype=jnp.float32)))
    idx = jnp.arange(C, dtype=jnp.float32)
    diff = idx[:, None] - idx[None, :]
    causal = diff >= 0
    decay = jnp.where(causal, jnp.exp(jnp.where(causal, diff, 0.0) * log_gamma[:, None, None]), 0.0)
    cross_decay = jnp.exp((idx + 1.0) * log_gamma[:, None])
    key_decay = jnp.exp((C - 1.0 - idx) * log_gamma[:, None])
    chunk_decay = jnp.exp(C * log_gamma)
    qc, kc, vc = (to_chunks(t, C) for t in (q, k, v))
    scores = jnp.einsum('bhnid,bhnjd->bhnij', qc, kc) * decay[None, :, None]
    inner = jnp.einsum('bhnij,bhnje->bhnie', scores, vc)

    def step(state, xs):
        q_i, k_i, v_i = xs
        o_cross = jnp.einsum('bhcd,bhde->bhce', q_i, state) * cross_decay[None, :, :, None]
        state = state * chunk_decay[None, :, None, None] + jnp.einsum('bhcd,bhce->bhde', k_i * key_decay[None, :, :, None], v_i)
        return state, o_cross

    state0 = jnp.zeros((B, H, DK, DV), jnp.float32)
    xs = tuple(jnp.moveaxis(t, 2, 0) for t in (qc, kc, vc))
    _, cross = lax.scan(step, state0, xs)
    return from_chunks(inner + jnp.moveaxis(cross, 0, 2)).astype(v.dtype)


def retention_mixer(h, positions, w_in, gn_g, gn_b, w_out):
    B, S, _ = h.shape
    proj = h @ w_in
    q, k, v, gate = jnp.split(proj, [RET_HEADS * RET_DK, 2 * RET_HEADS * RET_DK, 2 * RET_HEADS * RET_DK + RET_MIX], axis=-1)
    cos, sin = rope_tables(positions, RET_DK)
    q = apply_rope(q.reshape(B, S, RET_HEADS, RET_DK), cos, sin) * (RET_DK ** -0.5)
    k = apply_rope(k.reshape(B, S, RET_HEADS, RET_DK), cos, sin)
    v = v.reshape(B, S, RET_HEADS, RET_DV)
    o = chunkwise_retention(q, k, v)
    o = layer_norm(o, gn_g.reshape(RET_HEADS, RET_DV), gn_b.reshape(RET_HEADS, RET_DV))
    return (jax.nn.silu(gate) * o.reshape(B, S, RET_MIX)) @ w_out


def routed_experts(xt, expert_id, weights, w_gate, w_up, w_down):
    T, D = xt.shape
    K = expert_id.shape[1]
    n = T * K
    flat_e = expert_id.reshape(n).astype(jnp.int32)
    flat_tok = jnp.repeat(jnp.arange(T, dtype=jnp.int32), K)
    flat_w = weights.reshape(n)
    order = jnp.argsort(flat_e)
    e_sorted = flat_e[order]
    tok_sorted = flat_tok[order]
    w_sorted = flat_w[order]
    counts = jnp.zeros((N_EXPERTS,), jnp.int32).at[flat_e].add(1)
    padded = (counts + MOE_BLOCK - 1) // MOE_BLOCK * MOE_BLOCK
    start = jnp.cumsum(counts) - counts
    pstart = jnp.cumsum(padded) - padded
    slot = pstart[e_sorted] + (jnp.arange(n, dtype=jnp.int32) - start[e_sorted])
    n_blocks = n // MOE_BLOCK + N_EXPERTS
    n_slots = n_blocks * MOE_BLOCK
    slot_tok = jnp.zeros((n_slots,), jnp.int32).at[slot].set(tok_sorted)
    block_start = jnp.arange(n_blocks, dtype=jnp.int32) * MOE_BLOCK
    block_expert = jnp.minimum(jnp.searchsorted(jnp.cumsum(padded), block_start, side='right'), N_EXPERTS - 1)
    xb = xt[slot_tok].reshape(n_blocks, MOE_BLOCK, D)

    def expert_block(args):
        xs, e = args
        hid = jax.nn.silu(xs @ w_gate[e]) * (xs @ w_up[e])
        return hid @ w_down[e]

    yb = lax.map(expert_block, (xb, block_expert)).reshape(n_slots, D)
    contrib = yb[slot] * w_sorted[:, None].astype(yb.dtype)
    return jax.ops.segment_sum(contrib, tok_sorted, num_segments=T)


def hierarchical_moe(h, w_group, b_group, w_expert, b_expert, w_gate, w_up, w_down):
    B, S, D = h.shape
    T = B * S
    xt = h.reshape(T, D)
    g_prob = jax.nn.softmax((xt @ w_group + b_group).astype(jnp.float32), axis=-1)
    g_top, g_idx = lax.top_k(g_prob, 1)
    e_logits_all = (xt @ w_expert + b_expert).astype(jnp.float32).reshape(T, N_GROUPS, EXPERTS_PER_GROUP)
    e_logits = jnp.take_along_axis(e_logits_all, g_idx[:, :, None], axis=1)[:, 0]
    e_top, e_idx = lax.top_k(e_logits, TOP_K_IN_GROUP)
    e_w = jax.nn.softmax(e_top, axis=-1)
    weights = g_top * e_w
    expert_id = g_idx * EXPERTS_PER_GROUP + e_idx
    y = routed_experts(xt, expert_id, weights, w_gate, w_up, w_down)
    return y.reshape(B, S, D)


def setup_inputs(seed: int = 0) -> dict:
    key = jax.random.key(seed)
    ks = iter(jax.random.split(key, 40))
    f32 = jnp.float32

    def nrm(shape, scale):
        return scale * jax.random.normal(next(ks), shape, f32)

    def gain(shape):
        return 1.0 + 0.01 * jax.random.normal(next(ks), shape, f32)

    x = nrm((BATCH, SEQ, D_MODEL), 1.0)
    c = nrm((BATCH, D_MODEL), 1.0)
    positions = jax.random.randint(next(ks), (BATCH, 1), 0, 1024, jnp.int32) + jnp.arange(SEQ, dtype=jnp.int32)[None, :]
    ada_w = nrm((DEPTH, D_MODEL, N_MOD * D_MODEL), 0.1 * D_MODEL ** -0.5)
    ada_b = nrm((DEPTH, N_MOD * D_MODEL), 0.01)
    ln_mix_g = gain((DEPTH, D_MODEL))
    ln_mix_b = nrm((DEPTH, D_MODEL), 0.01)
    ln_ffn_g = gain((DEPTH, D_MODEL))
    ln_ffn_b = nrm((DEPTH, D_MODEL), 0.01)
    hy_w_in = nrm((N_EVEN, D_MODEL, HY_IN), D_MODEL ** -0.5)
    mla_q_norm = gain((N_EVEN, MLA_Q_RANK))
    mla_w_uq = nrm((N_EVEN, MLA_Q_RANK, MLA_HEADS * (MLA_NOPE + MLA_ROPE)), MLA_Q_RANK ** -0.5)
    mla_kv_norm = gain((N_EVEN, MLA_KV_RANK))
    mla_w_ukv = nrm((N_EVEN, MLA_KV_RANK, MLA_HEADS * (MLA_NOPE + MLA_V)), MLA_KV_RANK ** -0.5)
    gdn_conv_w = nrm((N_EVEN, GDN_CONV, GDN_QKV), GDN_CONV ** -0.5)
    gdn_a_log = jnp.log(jax.random.uniform(next(ks), (N_EVEN, GDN_HEADS), f32, 1.0, 16.0))
    dt = jnp.exp(jnp.log(1e-3) + jax.random.uniform(next(ks), (N_EVEN, GDN_HEADS), f32) * (jnp.log(1e-1) - jnp.log(1e-3)))
    gdn_dt_bias = dt + jnp.log(-jnp.expm1(-dt))
    gdn_norm = gain((N_EVEN, GDN_DV))
    hy_w_out = nrm((N_EVEN, HY_MIX, D_MODEL), DEEPNORM_BETA * HY_MIX ** -0.5)
    ret_w_in = nrm((N_ODD, D_MODEL, RET_IN), D_MODEL ** -0.5)
    ret_gn_g = gain((N_ODD, RET_MIX))
    ret_gn_b = nrm((N_ODD, RET_MIX), 0.01)
    ret_w_out = nrm((N_ODD, RET_MIX, D_MODEL), DEEPNORM_BETA * RET_MIX ** -0.5)
    moe_w_group = nrm((DEPTH, D_MODEL, N_GROUPS), D_MODEL ** -0.5)
    moe_b_group = nrm((DEPTH, N_GROUPS), 0.01)
    moe_w_expert = nrm((DEPTH, D_MODEL, N_EXPERTS), D_MODEL ** -0.5)
    moe_b_expert = nrm((DEPTH, N_EXPERTS), 0.01)
    moe_w_gate = nrm((DEPTH, N_EXPERTS, D_MODEL, EXPERT_FF), D_MODEL ** -0.5)
    moe_w_up = nrm((DEPTH, N_EXPERTS, D_MODEL, EXPERT_FF), D_MODEL ** -0.5)
    moe_w_down = nrm((DEPTH, N_EXPERTS, EXPERT_FF, D_MODEL), DEEPNORM_BETA * EXPERT_FF ** -0.5)
    return {'x': x, 'c': c, 'positions': positions, 'ada_w': ada_w, 'ada_b': ada_b,
            'ln_mix_g': ln_mix_g, 'ln_mix_b': ln_mix_b, 'ln_ffn_g': ln_ffn_g, 'ln_ffn_b': ln_ffn_b,
            'hy_w_in': hy_w_in, 'mla_q_norm': mla_q_norm, 'mla_w_uq': mla_w_uq, 'mla_kv_norm': mla_kv_norm,
            'mla_w_ukv': mla_w_ukv, 'gdn_conv_w': gdn_conv_w, 'gdn_a_log': gdn_a_log, 'gdn_dt_bias': gdn_dt_bias,
            'gdn_norm': gdn_norm, 'hy_w_out': hy_w_out, 'ret_w_in': ret_w_in, 'ret_gn_g': ret_gn_g,
            'ret_gn_b': ret_gn_b, 'ret_w_out': ret_w_out, 'moe_w_group': moe_w_group, 'moe_b_group': moe_b_group,
            'moe_w_expert': moe_w_expert, 'moe_b_expert': moe_b_expert, 'moe_w_gate': moe_w_gate,
            'moe_w_up': moe_w_up, 'moe_w_down': moe_w_down}


def reference(x, c, positions, ada_w, ada_b, ln_mix_g, ln_mix_b, ln_ffn_g, ln_ffn_b,
              hy_w_in, mla_q_norm, mla_w_uq, mla_kv_norm, mla_w_ukv, gdn_conv_w, gdn_a_log, gdn_dt_bias,
              gdn_norm, hy_w_out, ret_w_in, ret_gn_g, ret_gn_b, ret_w_out, moe_w_group, moe_b_group,
              moe_w_expert, moe_b_expert, moe_w_gate, moe_w_up, moe_w_down):
    c_act = jax.nn.silu(c)
    for layer in range(DEPTH):
        mod = (c_act @ ada_w[layer] + ada_b[layer])[:, None, :]
        sh_m, sc_m, gt_m, sh_f, sc_f, gt_f = jnp.split(mod, N_MOD, axis=-1)
        h = x * (1.0 + sc_m) + sh_m
        i = layer // 2
        if layer % 2 == 0:
            y = mla_gdn_mixer(h, positions, hy_w_in[i], mla_q_norm[i], mla_w_uq[i], mla_kv_norm[i], mla_w_ukv[i],
                              gdn_conv_w[i], gdn_a_log[i], gdn_dt_bias[i], gdn_norm[i], hy_w_out[i])
        else:
            y = retention_mixer(h, positions, ret_w_in[i], ret_gn_g[i], ret_gn_b[i], ret_w_out[i])
        x = layer_norm(DEEPNORM_ALPHA * x + (1.0 + gt_m) * y, ln_mix_g[layer], ln_mix_b[layer])
        h = x * (1.0 + sc_f) + sh_f
        y = hierarchical_moe(h, moe_w_group[layer], moe_b_group[layer], moe_w_expert[layer], moe_b_expert[layer],
                             moe_w_gate[layer], moe_w_up[layer], moe_w_down[layer])
        x = layer_norm(DEEPNORM_ALPHA * x + (1.0 + gt_f) * y, ln_ffn_g[layer], ln_ffn_b[layer])
    return x
```

```python
import functools

import jax
import jax.numpy as jnp
from jax import lax
from jax.experimental import pallas as pl
from jax.experimental.pallas import tpu as pltpu

F32 = jnp.float32
BF16 = jnp.bfloat16

DEPTH = 4
N_MOD = 6
LANES = 128

MLA_HEADS = 8
MLA_NOPE = 64
MLA_ROPE = 32
MLA_V = 64
MLA_Q_RANK = 256
MLA_KV_RANK = 128
MLA_SCALE = (MLA_NOPE + MLA_ROPE) ** -0.5

GDN_HEADS = 8
GDN_DK = 64
GDN_DV = 64
GDN_CONV = 4
GDN_CHUNK = 64
GDN_HD = GDN_HEADS * GDN_DK
GDN_QKV = 3 * GDN_HD
GDN_PAIRS = GDN_HEADS // 2

RET_HEADS = 8
RET_DK = 128
RET_DV = 256
RET_CHUNK = 256

N_GROUPS = 4
EXPERTS_PER_GROUP = 8
N_EXPERTS = 32
EXPERT_FF = 512
MOE_BLOCK = 256

ROPE_BASE = 10000.0
LN_EPS = 1e-5
RMS_EPS = 1e-6
DEEPNORM_ALPHA = (2.0 * DEPTH) ** 0.25

ROW_TILE = 256
ATTN_TILE = 512
VMEM_LIMIT = 56 * 1024 * 1024

_C_MLA = 0
_C_QKV = 512
_C_GATE = 2048
_C_AB = 2560
_C_END = 2688


def _bf(x):
    return x.astype(BF16)


def _dot(a, b):
    return jnp.dot(a, b, preferred_element_type=F32)


def _dot_nt(a, b):
    return lax.dot_general(a, b, (((1,), (1,)), ((), ())), preferred_element_type=F32)


def _dot_tn(a, b):
    return lax.dot_general(a, b, (((0,), (0,)), ((), ())), preferred_element_type=F32)


def _split2(x):
    hi = _bf(x)
    return hi, _bf(x - hi.astype(F32))


def _split3(x):
    hi = _bf(x)
    r = x - hi.astype(F32)
    mid = _bf(r)
    return hi, mid, _bf(r - mid.astype(F32))


def _silu(x):
    return x * jax.nn.sigmoid(x)


def _layer_norm(z, g, b):
    mu = jnp.mean(z, -1, keepdims=True)
    zc = z - mu
    var = jnp.mean(zc * zc, -1, keepdims=True)
    return zc * lax.rsqrt(var + LN_EPS) * g + b


def _rms_norm(x, g):
    return x * lax.rsqrt(jnp.mean(x * x, -1, keepdims=True) + RMS_EPS) * g


def _params(sem):
    return pltpu.CompilerParams(dimension_semantics=sem, vmem_limit_bytes=VMEM_LIMIT)


def _const_spec(shape):
    nd = len(shape)
    return pl.BlockSpec(shape, lambda *_: (0,) * nd)


def _ada_kernel(c_ref, w_ref, b_ref, o_ref):
    c = c_ref[...]
    o_ref[...] = _dot(_bf(_silu(c)), _bf(w_ref[...])) + b_ref[...]


def _ada_mod(c, ada_w, ada_b):
    depth, d, n = ada_w.shape
    bsz = c.shape[0]
    rows = 16
    tn = 1536
    cp = jnp.zeros((rows, d), F32).at[:bsz].set(c)
    out = pl.pallas_call(
        _ada_kernel,
        grid=(depth, n // tn),
        in_specs=[pl.BlockSpec((rows, d), lambda l, j: (0, 0)),
                  pl.BlockSpec((None, d, tn), lambda l, j: (l, 0, j)),
                  pl.BlockSpec((None, 1, tn), lambda l, j: (l, 0, j))],
        out_specs=pl.BlockSpec((None, rows, tn), lambda l, j: (l, 0, j)),
        out_shape=jax.ShapeDtypeStruct((depth, rows, n), F32),
        compiler_params=_params(("parallel", "parallel")),
    )(cp, ada_w, ada_b.reshape(depth, 1, n))
    return out[:, :bsz].reshape(depth, bsz, N_MOD, d)


def _even_front_kernel(x_ref, mod_ref, w_ref, qn_ref, wuq_ref, kvn_ref, wuk_ref, wuv_ref,
                       cw_ref, gp_ref, tab_ref,
                       q_ref, k_ref, v_ref, gq_ref, gk_ref, gv_ref, gate_ref, gcol_ref, grow_ref,
                       xbuf, *, tm, tiles_per_seq):
    i = pl.program_id(0)
    hb = _bf(x_ref[...] * (1.0 + mod_ref[1:2, :]) + mod_ref[0:1, :])

    pm = _dot(hb, w_ref[:, _C_MLA:_C_QKV])
    ct = tab_ref[:, 0:LANES]
    s1 = tab_ref[:, LANES:2 * LANES]
    s2 = tab_ref[:, 2 * LANES:3 * LANES]

    def rope(b):
        return b * ct + pltpu.roll(b, LANES - 16, 1) * s1 + pltpu.roll(b, 16, 1) * s2

    qa = _dot(_bf(_rms_norm(pm[:, 0:MLA_Q_RANK], qn_ref[...])), wuq_ref[...])
    for h in range(MLA_HEADS):
        sl = slice(h * LANES, (h + 1) * LANES)
        q_ref[:, sl] = _bf(rope(qa[:, sl]) * MLA_SCALE)
    kvn = _bf(_rms_norm(pm[:, MLA_Q_RANK:MLA_Q_RANK + MLA_KV_RANK], kvn_ref[...]))
    ka = _dot(kvn, wuk_ref[...])
    kr = rope(pm[:, MLA_Q_RANK + MLA_KV_RANK:_C_QKV])
    for h in range(MLA_HEADS):
        sl = slice(h * LANES, (h + 1) * LANES)
        k_ref[:, sl] = _bf(ka[:, sl] + kr)
    v_ref[...] = _bf(_dot(kvn, wuv_ref[...]))

    @pl.when(i % tiles_per_seq == 0)
    def _():
        xbuf[0:8, :] = jnp.zeros((8, GDN_QKV), F32)

    xbuf[8:8 + tm, :] = _dot(hb, w_ref[:, _C_QKV:_C_GATE])
    cw = cw_ref[...]
    y = cw[0:1, :] * xbuf[5:5 + tm, :]
    for t in range(1, GDN_CONV):
        y = y + cw[t:t + 1, :] * xbuf[5 + t:5 + t + tm, :]
    xbuf[0:8, :] = xbuf[tm:tm + 8, :]
    y = _silu(y)

    r = lax.broadcasted_iota(jnp.int32, (GDN_HD, GDN_HD), 0) // GDN_DK
    c = lax.broadcasted_iota(jnp.int32, (GDN_HD, GDN_HD), 1) // GDN_DK
    head_ones = jnp.where(r == c, 1.0, 0.0).astype(BF16)

    def head_sumsq(z):
        hi, lo = _split2(z * z)
        return _dot(hi, head_ones) + _dot(lo, head_ones)

    qg = y[:, 0:GDN_HD]
    kg = y[:, GDN_HD:2 * GDN_HD]
    gq_ref[...] = qg * lax.rsqrt(head_sumsq(qg) + RMS_EPS) * (GDN_DK ** -0.5)
    gk_ref[...] = kg * lax.rsqrt(head_sumsq(kg) + RMS_EPS)
    gv_ref[...] = y[:, 2 * GDN_HD:3 * GDN_HD]
    gate_ref[...] = _dot(hb, w_ref[:, _C_GATE:_C_AB])

    ab = _dot(hb, w_ref[:, _C_AB:_C_END])
    lane = lax.broadcasted_iota(jnp.int32, (tm, LANES), 1)
    z = ab + gp_ref[1:2, :]
    softplus = jnp.maximum(z, 0.0) + jnp.log(1.0 + jnp.exp(-jnp.abs(z)))
    la = jnp.where(lane < GDN_HEADS, -jnp.exp(gp_ref[0:1, :]) * softplus, 0.0)
    rr = lax.broadcasted_iota(jnp.int32, (tm, tm), 0)
    cc = lax.broadcasted_iota(jnp.int32, (tm, tm), 1)
    tri = jnp.where((rr // GDN_CHUNK == cc // GDN_CHUNK) & (cc <= rr), 1.0, 0.0).astype(BF16)
    l0, l1, l2 = _split3(la)
    g = _dot(tri, l0) + _dot(tri, l1) + _dot(tri, l2)
    gcol = jnp.where(lane < GDN_HEADS, g, jax.nn.sigmoid(ab))
    gcol_ref[...] = gcol
    grow_ref[...] = gcol.T[0:16, :]


def _even_front(x2, mod_l, w_in_p, qn, wuq_p, kvn, wuk_p, wuv, conv_w, gparam, tabs, seq):
    t, d = x2.shape
    tm = min(ROW_TILE, seq)
    tiles_per_seq = seq // tm
    row = lambda n: pl.BlockSpec((tm, n), lambda i: (i, 0))
    out_shapes = (
        jax.ShapeDtypeStruct((t, MLA_HEADS * LANES), BF16),
        jax.ShapeDtypeStruct((t, MLA_HEADS * LANES), BF16),
        jax.ShapeDtypeStruct((t, MLA_HEADS * MLA_V), BF16),
        jax.ShapeDtypeStruct((t, GDN_HD), F32),
        jax.ShapeDtypeStruct((t, GDN_HD), F32),
        jax.ShapeDtypeStruct((t, GDN_HD), F32),
        jax.ShapeDtypeStruct((t, GDN_HD), F32),
        jax.ShapeDtypeStruct((t, LANES), F32),
        jax.ShapeDtypeStruct((16, t), F32),
    )
    return pl.pallas_call(
        functools.partial(_even_front_kernel, tm=tm, tiles_per_seq=tiles_per_seq),
        grid=(t // tm,),
        in_specs=[row(d),
                  pl.BlockSpec((None, N_MOD, d), lambda i: (i // tiles_per_seq, 0, 0)),
                  _const_spec(w_in_p.shape), _const_spec(qn.shape), _const_spec(wuq_p.shape),
                  _const_spec(kvn.shape), _const_spec(wuk_p.shape), _const_spec(wuv.shape),
                  _const_spec(conv_w.shape), _const_spec(gparam.shape), row(3 * LANES)],
        out_specs=(row(MLA_HEADS * LANES), row(MLA_HEADS * LANES), row(MLA_HEADS * MLA_V),
                   row(GDN_HD), row(GDN_HD), row(GDN_HD), row(GDN_HD), row(LANES),
                   pl.BlockSpec((16, tm), lambda i: (0, i))),
        out_shape=out_shapes,
        scratch_shapes=[pltpu.VMEM((tm + 8, GDN_QKV), F32)],
        compiler_params=_params(("arbitrary",)),
    )(x2, mod_l, w_in_p, qn, wuq_p, kvn, wuk_p, wuv, conv_w, gparam, tabs)


def _attn_kernel(q_ref, k_ref, v_ref, o_ref, m_ref, l_ref, acc_ref, *, tq):
    qi = pl.program_id(1)
    ki = pl.program_id(2)

    @pl.when(ki == 0)
    def _():
        m_ref[...] = jnp.full(m_ref.shape, -jnp.inf, F32)
        l_ref[...] = jnp.zeros(l_ref.shape, F32)
        acc_ref[...] = jnp.zeros(acc_ref.shape, F32)

    def step(diagonal):
        if diagonal:
            row = lax.broadcasted_iota(jnp.int32, (tq, tq), 0)
            col = lax.broadcasted_iota(jnp.int32, (tq, tq), 1)
            keep = col <= row
        for h in range(MLA_HEADS):
            sl = slice(h * LANES, (h + 1) * LANES)
            s = _dot_nt(q_ref[:, sl], k_ref[:, sl])
            if diagonal:
                s = jnp.where(keep, s, -jnp.inf)
            m_prev = m_ref[h]
            m_new = jnp.maximum(m_prev, jnp.max(s, -1, keepdims=True))
            alpha = jnp.exp(m_prev - m_new)
            p = jnp.exp(s - m_new)
            l_ref[h] = alpha * l_ref[h] + jnp.sum(p, -1, keepdims=True)
            acc_ref[h] = alpha * acc_ref[h] + _dot(_bf(p), v_ref[:, h * MLA_V:(h + 1) * MLA_V])
            m_ref[h] = m_new

    @pl.when(ki < qi)
    def _():
        step(False)

    @pl.when(ki == qi)
    def _():
        step(True)
        for h in range(MLA_HEADS):
            o_ref[:, h * MLA_V:(h + 1) * MLA_V] = _bf(acc_ref[h] / l_ref[h])


def _attention(qm, km, vm, bsz, seq):
    t = qm.shape[0]
    tq = min(ATTN_TILE, seq)
    nq = seq // tq
    return pl.pallas_call(
        functools.partial(_attn_kernel, tq=tq),
        grid=(bsz, nq, nq),
        in_specs=[pl.BlockSpec((tq, MLA_HEADS * LANES), lambda b, qi, ki: (b * nq + qi, 0)),
                  pl.BlockSpec((tq, MLA_HEADS * LANES), lambda b, qi, ki: (b * nq + jnp.minimum(ki, qi), 0)),
                  pl.BlockSpec((tq, MLA_HEADS * MLA_V), lambda b, qi, ki: (b * nq + jnp.minimum(ki, qi), 0))],
        out_specs=pl.BlockSpec((tq, MLA_HEADS * MLA_V), lambda b, qi, ki: (b * nq + qi, 0)),
        out_shape=jax.ShapeDtypeStruct((t, MLA_HEADS * MLA_V), BF16),
        scratch_shapes=[pltpu.VMEM((MLA_HEADS, tq, 1), F32),
                        pltpu.VMEM((MLA_HEADS, tq, 1), F32),
                        pltpu.VMEM((MLA_HEADS, tq, MLA_V), F32)],
        compiler_params=_params(("parallel", "parallel", "arbitrary")),
    )(qm, km, vm)


def _gdn_kernel(q_ref, k_ref, v_ref, gate_ref, gcol_ref, grow_ref, on_ref, y_ref, state_ref, *, n_chunks):
    C = GDN_CHUNK

    @pl.when(pl.program_id(1) == 0)
    def _():
        state_ref[...] = jnp.zeros(state_ref.shape, F32)

    lane = lax.broadcasted_iota(jnp.int32, (C, LANES), 1)
    rowi = lax.broadcasted_iota(jnp.int32, (C, LANES), 0)
    first = lane < GDN_DK
    col = jnp.where(first, lane, lane - GDN_DK)
    tril = col <= rowi
    strict = col < rowi
    eye = jnp.where(col == rowi, 1.0, 0.0)
    blk16 = (col // 16) == (rowi // 16)
    blk32 = (col // 32) == (rowi // 32)
    r2 = lax.broadcasted_iota(jnp.int32, (LANES, LANES), 0)
    c2 = lax.broadcasted_iota(jnp.int32, (LANES, LANES), 1)
    first_rows = r2 < GDN_DK
    head_ones = jnp.where((r2 < GDN_DK) == (c2 < GDN_DK), 1.0, 0.0).astype(BF16)

    def bd(y):
        return jnp.concatenate([jnp.where(first, y, 0), jnp.where(first, 0, y)], axis=0)

    def pmm(x, y):
        return _dot(_bf(x), bd(_bf(y)))

    def pmm_hi(x, y):
        xh, xl = _split2(x)
        yh, yl = _split2(y)
        bh = bd(yh)
        return _dot(xh, bh) + _dot(xh, bd(yl)) + _dot(xl, bh)

    def pair_cols(a, c0, c1):
        return jnp.where(first, a[:, c0:c0 + 1], a[:, c1:c1 + 1])

    def chunk(ci, carry):
        rows = pl.ds(pl.multiple_of(ci * C, C), C)
        gc = gcol_ref[rows, :]
        gr_all = grow_ref[ci]
        for p in range(GDN_PAIRS):
            sl = slice(p * LANES, (p + 1) * LANES)
            q = q_ref[rows, sl]
            k = k_ref[rows, sl]
            v = v_ref[rows, sl]
            g = pair_cols(gc, 2 * p, 2 * p + 1)
            beta = pair_cols(gc, GDN_HEADS + 2 * p, GDN_HEADS + 2 * p + 1)
            gj = gr_all[p:p + 1, :]
            decay = jnp.where(tril, jnp.exp(jnp.where(tril, g - gj, 0.0)), 0.0)
            kb = k * beta
            kbd = bd(_bf(k))
            lmat = jnp.where(strict, _dot_nt(_bf(kb), kbd) * decay, 0.0)
            n1 = jnp.where(blk16, lmat, 0.0)
            x = eye - n1
            n2 = pmm_hi(n1, n1)
            x = x + pmm_hi(x, n2)
            n4 = pmm_hi(n2, n2)
            x = x + pmm_hi(x, n4)
            n8 = pmm_hi(n4, n4)
            x = x + pmm_hi(x, n8)
            b32 = jnp.where(blk32 & jnp.logical_not(blk16), lmat, 0.0)
            x = x - pmm_hi(pmm_hi(x, b32), x)
            b64 = jnp.where(blk32, 0.0, lmat)
            x = x - pmm_hi(pmm_hi(x, b64), x)
            u = pmm(x, v * beta)
            w = pmm(x, kb * jnp.exp(g))
            a_intra = jnp.where(tril, _dot_nt(_bf(q), kbd) * decay, 0.0)
            state = state_ref[p]
            sb = _bf(state)
            v_new = u - _dot(_bf(w), sb)
            o = _dot(_bf(q * jnp.exp(g)), sb) + pmm(a_intra, v_new)
            g_last = g[C - 1:C, :]
            k_dec = k * jnp.exp(g_last - g)
            gl0 = g_last[:, 0:1]
            gl1 = g_last[:, GDN_DK:GDN_DK + 1]
            scale = jnp.exp(jnp.where(first_rows, gl0, gl1))
            state_ref[p] = state * scale + _dot_tn(bd(_bf(k_dec)), bd(_bf(v_new)))
            hi, lo = _split2(o * o)
            ms = (_dot(hi, head_ones) + _dot(lo, head_ones)) * (1.0 / GDN_DV)
            on = o * lax.rsqrt(ms + RMS_EPS) * on_ref[...]
            y_ref[rows, sl] = _bf(on * _silu(gate_ref[rows, sl]))
        return carry

    lax.fori_loop(0, n_chunks, chunk, 0)


def _gdn(gq, gk, gv, gate, gcol, growp, onorm2, bsz, seq):
    t = gq.shape[0]
    r = min(ROW_TILE, seq)
    nblk = seq // r
    n_chunks = r // GDN_CHUNK
    row = lambda n: pl.BlockSpec((r, n), lambda b, j: (b * nblk + j, 0))
    return pl.pallas_call(
        functools.partial(_gdn_kernel, n_chunks=n_chunks),
        grid=(bsz, nblk),
        in_specs=[row(GDN_HD), row(GDN_HD), row(GDN_HD), row(GDN_HD), row(LANES),
                  pl.BlockSpec((n_chunks, GDN_PAIRS, LANES), lambda b, j: (b * nblk + j, 0, 0)),
                  _const_spec(onorm2.shape)],
        out_specs=row(GDN_HD),
        out_shape=jax.ShapeDtypeStruct((t, GDN_HD), BF16),
        scratch_shapes=[pltpu.VMEM((GDN_PAIRS, LANES, LANES), F32)],
        compiler_params=_params(("parallel", "arbitrary")),
    )(gq, gk, gv, gate, gcol, growp, onorm2)


def _back_kernel(*refs, n_in):
    a_refs = refs[:n_in]
    (w_ref, x_ref, mod_ref, lng_ref, lnb_ref, wrh_ref, wrl_ref, br_ref,
     x1_ref, h2_ref, route_ref) = refs[n_in:]
    off = 0
    y = None
    for a_ref in a_refs:
        kk = a_ref.shape[1]
        part = _dot(a_ref[...], w_ref[off:off + kk, :])
        y = part if y is None else y + part
        off += kk
    z = DEEPNORM_ALPHA * x_ref[...] + (1.0 + mod_ref[2:3, :]) * y
    x1 = _layer_norm(z, lng_ref[...], lnb_ref[...])
    x1_ref[...] = x1
    h2 = x1 * (1.0 + mod_ref[4:5, :]) + mod_ref[3:4, :]
    h2_ref[...] = _bf(h2)

    hh, hl = _split2(h2)
    logits = _dot(hh, wrh_ref[...]) + _dot(hh, wrl_ref[...]) + _dot(hl, wrh_ref[...]) + br_ref[...]
    lane = lax.broadcasted_iota(jnp.int32, logits.shape, 1)
    lane_f = lane.astype(F32)
    neg = -jnp.inf
    gmask = (lane >= N_EXPERTS) & (lane < N_EXPERTS + N_GROUPS)
    gl = jnp.where(gmask, logits, neg)
    gmax = jnp.max(gl, -1, keepdims=True)
    g_top = 1.0 / jnp.sum(jnp.where(gmask, jnp.exp(gl - gmax), 0.0), -1, keepdims=True)
    g_lane = jnp.min(jnp.where(gl == gmax, lane_f, 1e9), -1, keepdims=True)
    g_idx = g_lane.astype(jnp.int32) - N_EXPERTS
    emask = (lane < N_EXPERTS) & ((lane // EXPERTS_PER_GROUP) == g_idx)
    el = jnp.where(emask, logits, neg)
    m1 = jnp.max(el, -1, keepdims=True)
    i1 = jnp.min(jnp.where(el == m1, lane_f, 1e9), -1, keepdims=True)
    el2 = jnp.where(lane_f == i1, neg, el)
    m2 = jnp.max(el2, -1, keepdims=True)
    i2 = jnp.min(jnp.where(el2 == m2, lane_f, 1e9), -1, keepdims=True)
    e2 = jnp.exp(m2 - m1)
    w1 = g_top / (1.0 + e2)
    w2 = g_top * e2 / (1.0 + e2)
    route_ref[...] = jnp.where(lane == 0, i1, jnp.where(lane == 1, i2, jnp.where(lane == 2, w1, jnp.where(lane == 3, w2, 0.0))))


def _mixer_back(acts, w_out, x2, mod_l, ln_g, ln_b, wr_hi, wr_lo, br, seq):
    t, d = x2.shape
    tm = min(ROW_TILE, seq)
    tiles_per_seq = seq // tm
    row = lambda n: pl.BlockSpec((tm, n), lambda i: (i, 0))
    return pl.pallas_call(
        functools.partial(_back_kernel, n_in=len(acts)),
        grid=(t // tm,),
        in_specs=[row(a.shape[1]) for a in acts] + [
            _const_spec(w_out.shape), row(d),
            pl.BlockSpec((None, N_MOD, d), lambda i: (i // tiles_per_seq, 0, 0)),
            _const_spec(ln_g.shape), _const_spec(ln_b.shape),
            _const_spec(wr_hi.shape), _const_spec(wr_lo.shape), _const_spec(br.shape)],
        out_specs=(row(d), row(d), row(LANES)),
        out_shape=(jax.ShapeDtypeStruct((t, d), F32), jax.ShapeDtypeStruct((t, d), BF16),
                   jax.ShapeDtypeStruct((t, LANES), F32)),
        compiler_params=_params(("parallel",)),
    )(*acts, w_out, x2, mod_l, ln_g, ln_b, wr_hi, wr_lo, br)


def _moe_kernel(be_ref, nb_ref, x_ref, wg_ref, wu_ref, wd_ref, y_ref, wg_s, wu_s, wd_s):
    i = pl.program_id(0)

    @pl.when(i < nb_ref[0])
    def _():
        prev = be_ref[jnp.maximum(i - 1, 0)]

        @pl.when((i == 0) | (be_ref[i] != prev))
        def _():
            wg_s[...] = _bf(wg_ref[...])
            wu_s[...] = _bf(wu_ref[...])
            wd_s[...] = _bf(wd_ref[...])

        x = x_ref[...]
        hid = _silu(_dot(x, wg_s[...])) * _dot(x, wu_s[...])
        y_ref[...] = _dot(_bf(hid), wd_s[...])


def _moe_experts(block_expert, n_used, xb, w_gate, w_up, w_down):
    n_slots, d = xb.shape
    ff = w_gate.shape[-1]
    n_blocks = n_slots // MOE_BLOCK
    grid_spec = pltpu.PrefetchScalarGridSpec(
        num_scalar_prefetch=2,
        grid=(n_blocks,),
        in_specs=[pl.BlockSpec((MOE_BLOCK, d), lambda i, be, nb: (i, 0)),
                  pl.BlockSpec((None, d, ff), lambda i, be, nb: (be[i], 0, 0)),
                  pl.BlockSpec((None, d, ff), lambda i, be, nb: (be[i], 0, 0)),
                  pl.BlockSpec((None, ff, d), lambda i, be, nb: (be[i], 0, 0))],
        out_specs=pl.BlockSpec((MOE_BLOCK, d), lambda i, be, nb: (i, 0)),
        scratch_shapes=[pltpu.VMEM((d, ff), BF16), pltpu.VMEM((d, ff), BF16), pltpu.VMEM((ff, d), BF16)],
    )
    return pl.pallas_call(
        _moe_kernel,
        grid_spec=grid_spec,
        out_shape=jax.ShapeDtypeStruct((n_slots, d), F32),
        compiler_params=_params(("arbitrary",)),
    )(block_expert, n_used, xb, w_gate, w_up, w_down)


def _dispatch(route, t):
    n = 2 * t
    flat_e = route[:, 0:2].astype(jnp.int32).reshape(n)
    onehot = (flat_e[:, None] == jnp.arange(N_EXPERTS, dtype=jnp.int32)[None, :]).astype(jnp.int32)
    csum = jnp.cumsum(onehot, axis=0)
    rank = jnp.sum(onehot * csum, axis=1) - 1
    counts = csum[-1]
    padded = (counts + MOE_BLOCK - 1) // MOE_BLOCK * MOE_BLOCK
    pend = jnp.cumsum(padded)
    pstart = pend - padded
    slot = pstart[flat_e] + rank
    n_blocks = n // MOE_BLOCK + N_EXPERTS
    flat_tok = jnp.arange(n, dtype=jnp.int32) // 2
    slot_tok = jnp.zeros((n_blocks * MOE_BLOCK,), jnp.int32).at[slot].set(flat_tok)
    block_start = jnp.arange(n_blocks, dtype=jnp.int32) * MOE_BLOCK
    block_expert = jnp.minimum(jnp.searchsorted(pend, block_start, side='right'), N_EXPERTS - 1).astype(jnp.int32)
    n_used = (pend[-1] // MOE_BLOCK).astype(jnp.int32).reshape(1)
    return slot.reshape(t, 2), slot_tok, block_expert, n_used


def _moe_back_kernel(x_ref, ya_ref, yb_ref, route_ref, mod_ref, lng_ref, lnb_ref, o_ref):
    rt = route_ref[...]
    y = ya_ref[...] * rt[:, 2:3] + yb_ref[...] * rt[:, 3:4]
    z = DEEPNORM_ALPHA * x_ref[...] + (1.0 + mod_ref[5:6, :]) * y
    o_ref[...] = _layer_norm(z, lng_ref[...], lnb_ref[...])


def _moe_back(x1, ya, yb, route, mod_l, ln_g, ln_b, seq):
    t, d = x1.shape
    tm = min(ROW_TILE, seq)
    tiles_per_seq = seq // tm
    row = lambda n: pl.BlockSpec((tm, n), lambda i: (i, 0))
    return pl.pallas_call(
        _moe_back_kernel,
        grid=(t // tm,),
        in_specs=[row(d), row(d), row(d), row(LANES),
                  pl.BlockSpec((None, N_MOD, d), lambda i: (i // tiles_per_seq, 0, 0)),
                  _const_spec(ln_g.shape), _const_spec(ln_b.shape)],
        out_specs=row(d),
        out_shape=jax.ShapeDtypeStruct((t, d), F32),
        compiler_params=_params(("parallel",)),
    )(x1, ya, yb, route, mod_l, ln_g, ln_b)


def _odd_front_kernel(x_ref, mod_ref, w_ref, tab_ref, q_ref, k_ref, v_ref, gate_ref):
    hb = _bf(x_ref[...] * (1.0 + mod_ref[1:2, :]) + mod_ref[0:1, :])
    c2 = tab_ref[:, 0:LANES]
    s2 = tab_ref[:, LANES:2 * LANES]
    nqk = RET_HEADS * RET_DK
    nv = RET_HEADS * RET_DV

    def rope(b):
        return b * c2 + pltpu.roll(b, RET_DK // 2, 1) * s2

    qa = _dot(hb, w_ref[:, 0:nqk])
    ka = _dot(hb, w_ref[:, nqk:2 * nqk])
    for h in range(RET_HEADS):
        sl = slice(h * LANES, (h + 1) * LANES)
        q_ref[:, sl] = _bf(rope(qa[:, sl]) * (RET_DK ** -0.5))
        k_ref[:, sl] = _bf(rope(ka[:, sl]))
    v_ref[...] = _bf(_dot(hb, w_ref[:, 2 * nqk:2 * nqk + nv]))
    gate_ref[...] = _silu(_dot(hb, w_ref[:, 2 * nqk + nv:2 * nqk + 2 * nv]))


def _odd_front(x2, mod_l, w_in, tabs, seq):
    t, d = x2.shape
    tm = min(ROW_TILE, seq)
    tiles_per_seq = seq // tm
    nqk = RET_HEADS * RET_DK
    nv = RET_HEADS * RET_DV
    row = lambda n: pl.BlockSpec((tm, n), lambda i: (i, 0))
    return pl.pallas_call(
        _odd_front_kernel,
        grid=(t // tm,),
        in_specs=[row(d), pl.BlockSpec((None, N_MOD, d), lambda i: (i // tiles_per_seq, 0, 0)),
                  _const_spec(w_in.shape), row(2 * LANES)],
        out_specs=(row(nqk), row(nqk), row(nv), row(nv)),
        out_shape=(jax.ShapeDtypeStruct((t, nqk), BF16), jax.ShapeDtypeStruct((t, nqk), BF16),
                   jax.ShapeDtypeStruct((t, nv), BF16), jax.ShapeDtypeStruct((t, nv), F32)),
        compiler_params=_params(("parallel",)),
    )(x2, mod_l, w_in, tabs)


def _ret_kernel(q_ref, k_ref, v_ref, gate_ref, lg_ref, gng_ref, gnb_ref, o_ref, state_ref, *, C):
    @pl.when(pl.program_id(2) == 0)
    def _():
        state_ref[...] = jnp.zeros(state_ref.shape, F32)

    lg = lg_ref[:, 0:1]
    ri = lax.broadcasted_iota(jnp.int32, (C, C), 0)
    ci = lax.broadcasted_iota(jnp.int32, (C, C), 1)
    diff = (ri - ci).astype(F32)
    causal = ri >= ci
    decay = jnp.where(causal, jnp.exp(jnp.where(causal, diff, 0.0) * lg), 0.0)
    idx = lax.broadcasted_iota(jnp.int32, (C, 1), 0).astype(F32)
    q = q_ref[...]
    k = k_ref[...]
    v = v_ref[...]
    state = state_ref[...]
    scores = _dot_nt(q, k) * decay
    o = _dot(_bf(scores), v) + _dot(q, _bf(state)) * jnp.exp((idx + 1.0) * lg)
    kd = _bf(k.astype(F32) * jnp.exp((C - 1.0 - idx) * lg))
    state_ref[...] = state * jnp.exp(C * lg) + _dot_tn(kd, v)
    o_ref[...] = _bf(gate_ref[...] * _layer_norm(o, gng_ref[...], gnb_ref[...]))


def _retention(q, k, v, gate, lg, gn_g, gn_b, bsz, seq):
    t = q.shape[0]
    C = min(RET_CHUNK, seq)
    nc = seq // C
    blk = lambda n: pl.BlockSpec((C, n), lambda b, h, j: (b * nc + j, h))
    hrow = lambda n: pl.BlockSpec((None, 1, n), lambda b, h, j: (h, 0, 0))
    return pl.pallas_call(
        functools.partial(_ret_kernel, C=C),
        grid=(bsz, RET_HEADS, nc),
        in_specs=[blk(RET_DK), blk(RET_DK), blk(RET_DV), blk(RET_DV), hrow(LANES), hrow(RET_DV), hrow(RET_DV)],
        out_specs=blk(RET_DV),
        out_shape=jax.ShapeDtypeStruct((t, RET_HEADS * RET_DV), BF16),
        scratch_shapes=[pltpu.VMEM((RET_DK, RET_DV), F32)],
        compiler_params=_params(("parallel", "parallel", "arbitrary")),
    )(q, k, v, gate, lg, gn_g, gn_b)


def _relayout_hy_w_in(w):
    d = w.shape[0]
    z = lambda n: jnp.zeros((d, n), w.dtype)
    c_rope = MLA_Q_RANK + MLA_KV_RANK
    c_gdn = c_rope + MLA_ROPE
    c_a = c_gdn + GDN_QKV
    c_gate = c_a + 2 * GDN_HEADS
    return _bf(jnp.concatenate([
        w[:, 0:c_rope], z(MLA_NOPE), w[:, c_rope:c_gdn], z(LANES - MLA_NOPE - MLA_ROPE),
        w[:, c_gdn:c_a], w[:, c_gate:], w[:, c_a:c_gate], z(LANES - 2 * GDN_HEADS)], axis=1))


def _relayout_w_uq(w):
    r = w.shape[0]
    wh = w.reshape(r, MLA_HEADS, MLA_NOPE + MLA_ROPE)
    pad = jnp.zeros((r, MLA_HEADS, LANES - MLA_NOPE - MLA_ROPE), w.dtype)
    return _bf(jnp.concatenate([wh, pad], axis=-1).reshape(r, MLA_HEADS * LANES))


def _relayout_w_ukv(w):
    r = w.shape[0]
    wh = w.reshape(r, MLA_HEADS, MLA_NOPE + MLA_V)
    pad = jnp.zeros((r, MLA_HEADS, LANES - MLA_NOPE), w.dtype)
    wk = jnp.concatenate([wh[:, :, :MLA_NOPE], pad], axis=-1).reshape(r, MLA_HEADS * LANES)
    wv = wh[:, :, MLA_NOPE:].reshape(r, MLA_HEADS * MLA_V)
    return _bf(wk), _bf(wv)


def _rope_angles(positions, dim):
    inv_freq = ROPE_BASE ** (-jnp.arange(0, dim, 2, dtype=F32) / dim)
    ang = positions.astype(F32).reshape(-1)[:, None] * inv_freq
    return jnp.cos(ang), jnp.sin(ang)


def _mla_rope_tables(positions):
    cos, sin = _rope_angles(positions, MLA_ROPE)
    t = cos.shape[0]
    z = lambda n: jnp.zeros((t, n), F32)
    ct = jnp.concatenate([jnp.ones((t, MLA_NOPE), F32), cos, cos, z(32)], axis=1)
    s1 = jnp.concatenate([z(MLA_NOPE), -sin, z(48)], axis=1)
    s2 = jnp.concatenate([z(MLA_NOPE + 16), sin, z(32)], axis=1)
    return jnp.concatenate([ct, s1, s2], axis=1)


def _ret_rope_tables(positions):
    cos, sin = _rope_angles(positions, RET_DK)
    return jnp.concatenate([cos, cos, -sin, sin], axis=1)


def _router_weights(w_group, b_group, w_expert, b_expert):
    d = w_group.shape[0]
    pad = LANES - N_EXPERTS - N_GROUPS
    w = jnp.concatenate([w_expert, w_group, jnp.zeros((d, pad), F32)], axis=1)
    b = jnp.concatenate([b_expert, b_group, jnp.zeros((pad,), F32)]).reshape(1, LANES)
    hi = _bf(w)
    return hi, _bf(w - hi.astype(F32)), b


def _pad_lanes(v):
    return jnp.zeros((LANES,), F32).at[:v.shape[0]].set(v)


def kernel(x, c, positions, ada_w, ada_b, ln_mix_g, ln_mix_b, ln_ffn_g, ln_ffn_b, hy_w_in, mla_q_norm, mla_w_uq, mla_kv_norm, mla_w_ukv, gdn_conv_w, gdn_a_log, gdn_dt_bias, gdn_norm, hy_w_out, ret_w_in, ret_gn_g, ret_gn_b, ret_w_out, moe_w_group, moe_b_group, moe_w_expert, moe_b_expert, moe_w_gate, moe_w_up, moe_w_down):
    bsz, seq, d = x.shape
    t = bsz * seq
    mods = _ada_mod(c, ada_w, ada_b)
    mla_tabs = _mla_rope_tables(positions)
    ret_tabs = _ret_rope_tables(positions)
    log_gamma = jnp.log(1.0 - 2.0 ** (-5.0 - jnp.arange(RET_HEADS, dtype=F32)))
    lg = jnp.broadcast_to(log_gamma[:, None, None], (RET_HEADS, 1, LANES))
    xc = x.reshape(t, d)
    for layer in range(DEPTH):
        i = layer // 2
        mod_l = mods[layer]
        if layer % 2 == 0:
            wuk_p, wuv = _relayout_w_ukv(mla_w_ukv[i])
            gparam = jnp.stack([_pad_lanes(gdn_a_log[i]), _pad_lanes(gdn_dt_bias[i])])
            (qm, km, vm, gq, gk, gv, gate, gcol, grow) = _even_front(
                xc, mod_l, _relayout_hy_w_in(hy_w_in[i]), mla_q_norm[i].reshape(1, -1), _relayout_w_uq(mla_w_uq[i]),
                mla_kv_norm[i].reshape(1, -1), wuk_p, wuv, gdn_conv_w[i], gparam, mla_tabs, seq)
            o_mla = _attention(qm, km, vm, bsz, seq)
            n_ch = t // GDN_CHUNK
            growp = grow[:GDN_HEADS].reshape(GDN_PAIRS, 2, n_ch, GDN_CHUNK).transpose(2, 0, 1, 3).reshape(n_ch, GDN_PAIRS, LANES)
            onorm2 = jnp.concatenate([gdn_norm[i], gdn_norm[i]]).reshape(1, LANES)
            y_gdn = _gdn(gq, gk, gv, gate, gcol, growp, onorm2, bsz, seq)
            acts = [o_mla, y_gdn]
            w_out = _bf(hy_w_out[i])
        else:
            q, k, v, gate = _odd_front(xc, mod_l, _bf(ret_w_in[i]), ret_tabs, seq)
            og = _retention(q, k, v, gate, lg, ret_gn_g[i].reshape(RET_HEADS, 1, RET_DV),
                            ret_gn_b[i].reshape(RET_HEADS, 1, RET_DV), bsz, seq)
            acts = [og]
            w_out = _bf(ret_w_out[i])
        wr_hi, wr_lo, br = _router_weights(moe_w_group[layer], moe_b_group[layer], moe_w_expert[layer], moe_b_expert[layer])
        x1, h2, route = _mixer_back(acts, w_out, xc, mod_l, ln_mix_g[layer].reshape(1, d), ln_mix_b[layer].reshape(1, d),
                                    wr_hi, wr_lo, br, seq)
        slot, slot_tok, block_expert, n_used = _dispatch(route, t)
        xb = jnp.take(h2, slot_tok, axis=0)
        yb = _moe_experts(block_expert, n_used, xb, moe_w_gate[layer], moe_w_up[layer], moe_w_down[layer])
        ya0 = jnp.take(yb, slot[:, 0], axis=0)
        ya1 = jnp.take(yb, slot[:, 1], axis=0)
        xc = _moe_back(x1, ya0, ya1, route, mod_l, ln_ffn_g[layer].reshape(1, d), ln_ffn_b[layer].reshape(1, d), seq)
    return xc.reshape(bsz, seq, d)
```

```python
import functools

import jax
import jax.numpy as jnp
from jax import lax
from jax.experimental import pallas as pl
from jax.experimental.pallas import tpu as pltpu

F32 = jnp.float32
BF16 = jnp.bfloat16

DEPTH = 4
N_MOD = 6
LANES = 128

MLA_HEADS = 8
MLA_NOPE = 64
MLA_ROPE = 32
MLA_V = 64
MLA_Q_RANK = 256
MLA_KV_RANK = 128
MLA_SCALE = (MLA_NOPE + MLA_ROPE) ** -0.5
LOG2_E = 1.4426950408889634

GDN_HEADS = 8
GDN_DK = 64
GDN_DV = 64
GDN_CONV = 4
GDN_CHUNK = 64
GDN_HD = GDN_HEADS * GDN_DK
GDN_QKV = 3 * GDN_HD
GDN_PAIRS = GDN_HEADS // 2

RET_HEADS = 8
RET_DK = 128
RET_DV = 256
RET_CHUNK = 256

N_GROUPS = 4
EXPERTS_PER_GROUP = 8
N_EXPERTS = 32
EXPERT_FF = 512
MOE_BLOCK = 256

ROPE_BASE = 10000.0
LN_EPS = 1e-5
RMS_EPS = 1e-6
DEEPNORM_ALPHA = (2.0 * DEPTH) ** 0.25

ROW_TILE = 256
ATTN_TILE = 512
VMEM_LIMIT = 56 * 1024 * 1024

_C_MLA = 0
_C_QKV = 512
_C_GATE = 2048
_C_AB = 2560
_C_END = 2688


def _bf(x):
    return x.astype(BF16)


def _dot(a, b):
    return jnp.dot(a, b, preferred_element_type=F32)


def _dot_nt(a, b):
    return lax.dot_general(a, b, (((1,), (1,)), ((), ())), preferred_element_type=F32)


def _dot_tn(a, b):
    return lax.dot_general(a, b, (((0,), (0,)), ((), ())), preferred_element_type=F32)


def _split2(x):
    hi = _bf(x)
    return hi, _bf(x - hi.astype(F32))


def _split3(x):
    hi = _bf(x)
    r = x - hi.astype(F32)
    mid = _bf(r)
    return hi, mid, _bf(r - mid.astype(F32))


def _silu(x):
    return x * jax.nn.sigmoid(x)


def _layer_norm(z, g, b):
    mu = jnp.mean(z, -1, keepdims=True)
    zc = z - mu
    var = jnp.mean(zc * zc, -1, keepdims=True)
    return zc * lax.rsqrt(var + LN_EPS) * g + b


def _rms_norm(x, g):
    return x * lax.rsqrt(jnp.mean(x * x, -1, keepdims=True) + RMS_EPS) * g


def _params(sem):
    return pltpu.CompilerParams(dimension_semantics=sem, vmem_limit_bytes=VMEM_LIMIT)


def _const_spec(shape):
    nd = len(shape)
    return pl.BlockSpec(shape, lambda *_: (0,) * nd)


def _ada_kernel(c_ref, w_ref, b_ref, o_ref):
    c = c_ref[...]
    o_ref[...] = _dot(_bf(_silu(c)), _bf(w_ref[...])) + b_ref[...]


def _ada_mod(c, ada_w, ada_b):
    depth, d, n = ada_w.shape
    bsz = c.shape[0]
    rows = 16
    tn = 1536
    cp = jnp.zeros((rows, d), F32).at[:bsz].set(c)
    out = pl.pallas_call(
        _ada_kernel,
        grid=(depth, n // tn),
        in_specs=[pl.BlockSpec((rows, d), lambda l, j: (0, 0)),
                  pl.BlockSpec((None, d, tn), lambda l, j: (l, 0, j)),
                  pl.BlockSpec((None, 1, tn), lambda l, j: (l, 0, j))],
        out_specs=pl.BlockSpec((None, rows, tn), lambda l, j: (l, 0, j)),
        out_shape=jax.ShapeDtypeStruct((depth, rows, n), F32),
        compiler_params=_params(("parallel", "parallel")),
    )(cp, ada_w, ada_b.reshape(depth, 1, n))
    return out[:, :bsz].reshape(depth, bsz, N_MOD, d)


def _even_front_kernel(x_ref, mod_ref, w_ref, qn_ref, wuq_ref, kvn_ref, wuk_ref, wuv_ref,
                       cw_ref, gp_ref, tab_ref,
                       q_ref, k_ref, v_ref, gq_ref, gk_ref, gv_ref, gate_ref, gcol_ref, grow_ref,
                       xbuf, *, tm, tiles_per_seq):
    i = pl.program_id(0)
    hb = _bf(x_ref[...] * (1.0 + mod_ref[1:2, :]) + mod_ref[0:1, :])

    pm = _dot(hb, w_ref[:, _C_MLA:_C_QKV])
    ct = tab_ref[:, 0:LANES]
    s1 = tab_ref[:, LANES:2 * LANES]
    s2 = tab_ref[:, 2 * LANES:3 * LANES]

    def rope(b):
        return b * ct + pltpu.roll(b, LANES - 16, 1) * s1 + pltpu.roll(b, 16, 1) * s2

    qa = _dot(_bf(_rms_norm(pm[:, 0:MLA_Q_RANK], qn_ref[...])), wuq_ref[...])
    for h in range(MLA_HEADS):
        sl = slice(h * LANES, (h + 1) * LANES)
        q_ref[:, sl] = _bf(rope(qa[:, sl]) * (MLA_SCALE * LOG2_E))
    kvn = _bf(_rms_norm(pm[:, MLA_Q_RANK:MLA_Q_RANK + MLA_KV_RANK], kvn_ref[...]))
    ka = _dot(kvn, wuk_ref[...])
    kr = rope(pm[:, MLA_Q_RANK + MLA_KV_RANK:_C_QKV])
    for h in range(MLA_HEADS):
        sl = slice(h * LANES, (h + 1) * LANES)
        k_ref[:, sl] = _bf(ka[:, sl] + kr)
    v_ref[...] = _bf(_dot_nt(wuv_ref[...], kvn))

    @pl.when(i % tiles_per_seq == 0)
    def _():
        xbuf[0:8, :] = jnp.zeros((8, GDN_QKV), F32)

    xbuf[8:8 + tm, :] = _dot(hb, w_ref[:, _C_QKV:_C_GATE])
    cw = cw_ref[...]
    y = cw[0:1, :] * xbuf[5:5 + tm, :]
    for t in range(1, GDN_CONV):
        y = y + cw[t:t + 1, :] * xbuf[5 + t:5 + t + tm, :]
    xbuf[0:8, :] = xbuf[tm:tm + 8, :]
    y = _silu(y)

    r = lax.broadcasted_iota(jnp.int32, (GDN_HD, GDN_HD), 0) // GDN_DK
    c = lax.broadcasted_iota(jnp.int32, (GDN_HD, GDN_HD), 1) // GDN_DK
    head_ones = jnp.where(r == c, 1.0, 0.0).astype(BF16)

    def head_sumsq(z):
        hi, lo = _split2(z * z)
        return _dot(hi, head_ones) + _dot(lo, head_ones)

    qg = y[:, 0:GDN_HD]
    kg = y[:, GDN_HD:2 * GDN_HD]
    gq_ref[...] = qg * lax.rsqrt(head_sumsq(qg) + RMS_EPS) * (GDN_DK ** -0.5)
    gk_ref[...] = kg * lax.rsqrt(head_sumsq(kg) + RMS_EPS)
    gv_ref[...] = y[:, 2 * GDN_HD:3 * GDN_HD]
    gate_ref[...] = _dot(hb, w_ref[:, _C_GATE:_C_AB])

    ab = _dot(hb, w_ref[:, _C_AB:_C_END])
    lane = lax.broadcasted_iota(jnp.int32, (tm, LANES), 1)
    z = ab + gp_ref[1:2, :]
    softplus = jnp.maximum(z, 0.0) + jnp.log(1.0 + jnp.exp(-jnp.abs(z)))
    la = jnp.where(lane < GDN_HEADS, -jnp.exp(gp_ref[0:1, :]) * softplus, 0.0)
    rr = lax.broadcasted_iota(jnp.int32, (tm, tm), 0)
    cc = lax.broadcasted_iota(jnp.int32, (tm, tm), 1)
    tri = jnp.where((rr // GDN_CHUNK == cc // GDN_CHUNK) & (cc <= rr), 1.0, 0.0).astype(BF16)
    l0, l1, l2 = _split3(la)
    g = _dot(tri, l0) + _dot(tri, l1) + _dot(tri, l2)
    gcol = jnp.where(lane < GDN_HEADS, g, jax.nn.sigmoid(ab))
    gcol_ref[...] = gcol
    grow_ref[...] = gcol.T[0:16, :]


def _even_front(x2, mod_l, w_in_p, qn, wuq_p, kvn, wuk_p, wuv, conv_w, gparam, tabs, seq):
    t, d = x2.shape
    tm = min(ROW_TILE, seq)
    tiles_per_seq = seq // tm
    row = lambda n: pl.BlockSpec((tm, n), lambda i: (i, 0))
    out_shapes = (
        jax.ShapeDtypeStruct((t, MLA_HEADS * LANES), BF16),
        jax.ShapeDtypeStruct((t, MLA_HEADS * LANES), BF16),
        jax.ShapeDtypeStruct((MLA_HEADS * MLA_V, t), BF16),
        jax.ShapeDtypeStruct((t, GDN_HD), F32),
        jax.ShapeDtypeStruct((t, GDN_HD), F32),
        jax.ShapeDtypeStruct((t, GDN_HD), F32),
        jax.ShapeDtypeStruct((t, GDN_HD), F32),
        jax.ShapeDtypeStruct((t, LANES), F32),
        jax.ShapeDtypeStruct((16, t), F32),
    )
    return pl.pallas_call(
        functools.partial(_even_front_kernel, tm=tm, tiles_per_seq=tiles_per_seq),
        grid=(t // tm,),
        in_specs=[row(d),
                  pl.BlockSpec((None, N_MOD, d), lambda i: (i // tiles_per_seq, 0, 0)),
                  _const_spec(w_in_p.shape), _const_spec(qn.shape), _const_spec(wuq_p.shape),
                  _const_spec(kvn.shape), _const_spec(wuk_p.shape), _const_spec(wuv.shape),
                  _const_spec(conv_w.shape), _const_spec(gparam.shape), row(3 * LANES)],
        out_specs=(row(MLA_HEADS * LANES), row(MLA_HEADS * LANES),
                   pl.BlockSpec((MLA_HEADS * MLA_V, tm), lambda i: (0, i)),
                   row(GDN_HD), row(GDN_HD), row(GDN_HD), row(GDN_HD), row(LANES),
                   pl.BlockSpec((16, tm), lambda i: (0, i))),
        out_shape=out_shapes,
        scratch_shapes=[pltpu.VMEM((tm + 8, GDN_QKV), F32)],
        compiler_params=_params(("arbitrary",)),
    )(x2, mod_l, w_in_p, qn, wuq_p, kvn, wuk_p, wuv, conv_w, gparam, tabs)


def _attn_kernel(q_ref, k_ref, vt_ref, o_ref, m_ref, l_ref, acc_ref, *, tq):
    qi = pl.program_id(1)
    ki = pl.program_id(2)

    @pl.when(ki == 0)
    def _():
        m_ref[...] = jnp.full(m_ref.shape, -jnp.inf, F32)
        l_ref[...] = jnp.zeros(l_ref.shape, F32)
        acc_ref[...] = jnp.zeros(acc_ref.shape, F32)

    def scores(h):
        sl = slice(h * LANES, (h + 1) * LANES)
        return _dot_nt(k_ref[:, sl], q_ref[:, sl])

    def step(diagonal):
        if diagonal:
            key = lax.broadcasted_iota(jnp.int32, (tq, tq), 0)
            qry = lax.broadcasted_iota(jnp.int32, (tq, tq), 1)
            keep = key <= qry
        s_next = scores(0)
        for h in range(MLA_HEADS):
            s = s_next
            if h + 1 < MLA_HEADS:
                s_next = scores(h + 1)
            if diagonal:
                s = jnp.where(keep, s, -jnp.inf)
            rows = slice(h * MLA_V, (h + 1) * MLA_V)
            m_prev = m_ref[h:h + 1, :]
            m_new = jnp.maximum(m_prev, jnp.max(s, 0, keepdims=True))
            alpha = jnp.exp2(m_prev - m_new)
            p = jnp.exp2(s - m_new)
            l_ref[h:h + 1, :] = alpha * l_ref[h:h + 1, :] + jnp.sum(p, 0, keepdims=True)
            acc_ref[rows, :] = alpha * acc_ref[rows, :] + _dot(vt_ref[rows, :], _bf(p))
            m_ref[h:h + 1, :] = m_new

    @pl.when(ki < qi)
    def _():
        step(False)

    @pl.when(ki == qi)
    def _():
        step(True)
        for h in range(MLA_HEADS):
            rows = slice(h * MLA_V, (h + 1) * MLA_V)
            acc_ref[rows, :] = acc_ref[rows, :] / l_ref[h:h + 1, :]
        o_ref[...] = _bf(acc_ref[...].T)


def _attention(qm, km, vm, bsz, seq):
    t = qm.shape[0]
    tq = min(ATTN_TILE, seq)
    nq = seq // tq
    return pl.pallas_call(
        functools.partial(_attn_kernel, tq=tq),
        grid=(bsz, nq, nq),
        in_specs=[pl.BlockSpec((tq, MLA_HEADS * LANES), lambda b, qi, ki: (b * nq + qi, 0)),
                  pl.BlockSpec((tq, MLA_HEADS * LANES), lambda b, qi, ki: (b * nq + jnp.minimum(ki, qi), 0)),
                  pl.BlockSpec((MLA_HEADS * MLA_V, tq), lambda b, qi, ki: (0, b * nq + jnp.minimum(ki, qi)))],
        out_specs=pl.BlockSpec((tq, MLA_HEADS * MLA_V), lambda b, qi, ki: (b * nq + qi, 0)),
        out_shape=jax.ShapeDtypeStruct((t, MLA_HEADS * MLA_V), BF16),
        scratch_shapes=[pltpu.VMEM((MLA_HEADS, tq), F32),
                        pltpu.VMEM((MLA_HEADS, tq), F32),
                        pltpu.VMEM((MLA_HEADS * MLA_V, tq), F32)],
        compiler_params=_params(("parallel", "parallel", "arbitrary")),
    )(qm, km, vm)


def _gdn_kernel(q_ref, k_ref, v_ref, gate_ref, gcol_ref, grow_ref, on_ref, y_ref, state_ref, *, n_chunks):
    C = GDN_CHUNK

    @pl.when(pl.program_id(1) == 0)
    def _():
        state_ref[...] = jnp.zeros(state_ref.shape, F32)

    lane = lax.broadcasted_iota(jnp.int32, (C, LANES), 1)
    rowi = lax.broadcasted_iota(jnp.int32, (C, LANES), 0)
    first = lane < GDN_DK
    col = jnp.where(first, lane, lane - GDN_DK)
    tril = col <= rowi
    strict = col < rowi
    eye = jnp.where(col == rowi, 1.0, 0.0)
    blk16 = (col // 16) == (rowi // 16)
    blk32 = (col // 32) == (rowi // 32)
    r2 = lax.broadcasted_iota(jnp.int32, (LANES, LANES), 0)
    c2 = lax.broadcasted_iota(jnp.int32, (LANES, LANES), 1)
    first_rows = r2 < GDN_DK
    head_ones = jnp.where((r2 < GDN_DK) == (c2 < GDN_DK), 1.0, 0.0).astype(BF16)

    def bd(y):
        return jnp.concatenate([jnp.where(first, y, 0), jnp.where(first, 0, y)], axis=0)

    def pmm(x, y):
        return _dot(_bf(x), bd(_bf(y)))

    def pmm_hi(x, y):
        xh, xl = _split2(x)
        yh, yl = _split2(y)
        bh = bd(yh)
        return _dot(xh, bh) + _dot(xh, bd(yl)) + _dot(xl, bh)

    def pair_cols(a, c0, c1):
        return jnp.where(first, a[:, c0:c0 + 1], a[:, c1:c1 + 1])

    def chunk(ci, carry):
        rows = pl.ds(pl.multiple_of(ci * C, C), C)
        pairs = range(GDN_PAIRS)
        sls = [slice(p * LANES, (p + 1) * LANES) for p in pairs]
        gc = gcol_ref[rows, :]
        gr_all = grow_ref[ci]
        q = [q_ref[rows, sl] for sl in sls]
        k = [k_ref[rows, sl] for sl in sls]
        v = [v_ref[rows, sl] for sl in sls]
        g = [pair_cols(gc, 2 * p, 2 * p + 1) for p in pairs]
        beta = [pair_cols(gc, GDN_HEADS + 2 * p, GDN_HEADS + 2 * p + 1) for p in pairs]
        decay = [jnp.where(tril, jnp.exp(jnp.where(tril, g[p] - gr_all[p:p + 1, :], 0.0)), 0.0) for p in pairs]
        kb = [k[p] * beta[p] for p in pairs]
        kbd = [bd(_bf(k[p])) for p in pairs]
        lmat = [jnp.where(strict, _dot_nt(_bf(kb[p]), kbd[p]) * decay[p], 0.0) for p in pairs]
        a_intra = [jnp.where(tril, _dot_nt(_bf(q[p]), kbd[p]) * decay[p], 0.0) for p in pairs]
        n1 = [jnp.where(blk16, lmat[p], 0.0) for p in pairs]
        x = [eye - n1[p] for p in pairs]
        nk = n1
        for _ in range(3):
            nk = [pmm_hi(nk[p], nk[p]) for p in pairs]
            x = [x[p] + pmm_hi(x[p], nk[p]) for p in pairs]
        for off in ([jnp.where(blk32 & jnp.logical_not(blk16), lmat[p], 0.0) for p in pairs],
                    [jnp.where(blk32, 0.0, lmat[p]) for p in pairs]):
            xb = [pmm_hi(x[p], off[p]) for p in pairs]
            x = [x[p] - pmm_hi(xb[p], x[p]) for p in pairs]
        u = [pmm(x[p], v[p] * beta[p]) for p in pairs]
        w = [pmm(x[p], kb[p] * jnp.exp(g[p])) for p in pairs]
        state = [state_ref[p] for p in pairs]
        sb = [_bf(state[p]) for p in pairs]
        v_new = [u[p] - _dot(_bf(w[p]), sb[p]) for p in pairs]
        o = [_dot(_bf(q[p] * jnp.exp(g[p])), sb[p]) + pmm(a_intra[p], v_new[p]) for p in pairs]
        for p in pairs:
            g_last = g[p][C - 1:C, :]
            k_dec = k[p] * jnp.exp(g_last - g[p])
            scale = jnp.exp(jnp.where(first_rows, g_last[:, 0:1], g_last[:, GDN_DK:GDN_DK + 1]))
            state_ref[p] = state[p] * scale + _dot_tn(bd(_bf(k_dec)), bd(_bf(v_new[p])))
        for p in pairs:
            hi, lo = _split2(o[p] * o[p])
            ms = (_dot(hi, head_ones) + _dot(lo, head_ones)) * (1.0 / GDN_DV)
            on = o[p] * lax.rsqrt(ms + RMS_EPS) * on_ref[...]
            y_ref[rows, sls[p]] = _bf(on * _silu(gate_ref[rows, sls[p]]))
        return carry

    lax.fori_loop(0, n_chunks, chunk, 0)


def _gdn(gq, gk, gv, gate, gcol, growp, onorm2, bsz, seq):
    t = gq.shape[0]
    r = min(ROW_TILE, seq)
    nblk = seq // r
    n_chunks = r // GDN_CHUNK
    row = lambda n: pl.BlockSpec((r, n), lambda b, j: (b * nblk + j, 0))
    return pl.pallas_call(
        functools.partial(_gdn_kernel, n_chunks=n_chunks),
        grid=(bsz, nblk),
        in_specs=[row(GDN_HD), row(GDN_HD), row(GDN_HD), row(GDN_HD), row(LANES),
                  pl.BlockSpec((n_chunks, GDN_PAIRS, LANES), lambda b, j: (b * nblk + j, 0, 0)),
                  _const_spec(onorm2.shape)],
        out_specs=row(GDN_HD),
        out_shape=jax.ShapeDtypeStruct((t, GDN_HD), BF16),
        scratch_shapes=[pltpu.VMEM((GDN_PAIRS, LANES, LANES), F32)],
        compiler_params=_params(("parallel", "arbitrary")),
    )(gq, gk, gv, gate, gcol, growp, onorm2)


def _back_kernel(*refs, n_in):
    a_refs = refs[:n_in]
    (w_ref, x_ref, mod_ref, lng_ref, lnb_ref, wrh_ref, wrl_ref, br_ref,
     x1_ref, h2_ref, route_ref) = refs[n_in:]
    off = 0
    y = None
    for a_ref in a_refs:
        kk = a_ref.shape[1]
        part = _dot(a_ref[...], w_ref[off:off + kk, :])
        y = part if y is None else y + part
        off += kk
    z = DEEPNORM_ALPHA * x_ref[...] + (1.0 + mod_ref[2:3, :]) * y
    x1 = _layer_norm(z, lng_ref[...], lnb_ref[...])
    x1_ref[...] = x1
    h2 = x1 * (1.0 + mod_ref[4:5, :]) + mod_ref[3:4, :]
    h2_ref[...] = h2

    hh, hl = _split2(h2)
    logits = _dot(hh, wrh_ref[...]) + _dot(hh, wrl_ref[...]) + _dot(hl, wrh_ref[...]) + br_ref[...]
    lane = lax.broadcasted_iota(jnp.int32, logits.shape, 1)
    lane_f = lane.astype(F32)
    neg = -jnp.inf
    gmask = (lane >= N_EXPERTS) & (lane < N_EXPERTS + N_GROUPS)
    gl = jnp.where(gmask, logits, neg)
    gmax = jnp.max(gl, -1, keepdims=True)
    g_top = 1.0 / jnp.sum(jnp.where(gmask, jnp.exp(gl - gmax), 0.0), -1, keepdims=True)
    g_lane = jnp.min(jnp.where(gl == gmax, lane_f, 1e9), -1, keepdims=True)
    g_idx = g_lane.astype(jnp.int32) - N_EXPERTS
    emask = (lane < N_EXPERTS) & ((lane // EXPERTS_PER_GROUP) == g_idx)
    el = jnp.where(emask, logits, neg)
    m1 = jnp.max(el, -1, keepdims=True)
    i1 = jnp.min(jnp.where(el == m1, lane_f, 1e9), -1, keepdims=True)
    el2 = jnp.where(lane_f == i1, neg, el)
    m2 = jnp.max(el2, -1, keepdims=True)
    i2 = jnp.min(jnp.where(el2 == m2, lane_f, 1e9), -1, keepdims=True)
    e2 = jnp.exp(m2 - m1)
    w1 = g_top / (1.0 + e2)
    w2 = g_top * e2 / (1.0 + e2)
    route_ref[...] = jnp.where(lane == 0, i1, jnp.where(lane == 1, i2, jnp.where(lane == 2, w1, jnp.where(lane == 3, w2, 0.0))))


def _mixer_back(acts, w_out, x2, mod_l, ln_g, ln_b, wr_hi, wr_lo, br, seq):
    t, d = x2.shape
    tm = min(ROW_TILE, seq)
    tiles_per_seq = seq // tm
    row = lambda n: pl.BlockSpec((tm, n), lambda i: (i, 0))
    return pl.pallas_call(
        functools.partial(_back_kernel, n_in=len(acts)),
        grid=(t // tm,),
        in_specs=[row(a.shape[1]) for a in acts] + [
            _const_spec(w_out.shape), row(d),
            pl.BlockSpec((None, N_MOD, d), lambda i: (i // tiles_per_seq, 0, 0)),
            _const_spec(ln_g.shape), _const_spec(ln_b.shape),
            _const_spec(wr_hi.shape), _const_spec(wr_lo.shape), _const_spec(br.shape)],
        out_specs=(row(d), row(d), row(LANES)),
        out_shape=(jax.ShapeDtypeStruct((t, d), F32), jax.ShapeDtypeStruct((t, d), F32),
                   jax.ShapeDtypeStruct((t, LANES), F32)),
        compiler_params=_params(("parallel",)),
    )(*acts, w_out, x2, mod_l, ln_g, ln_b, wr_hi, wr_lo, br)


def _moe_kernel(be_ref, nb_ref, x_ref, wg_ref, wu_ref, wd_ref, y_ref, wg_s, wu_s, wd_s):
    i = pl.program_id(0)

    @pl.when(i < nb_ref[0])
    def _():
        prev = be_ref[jnp.maximum(i - 1, 0)]

        @pl.when((i == 0) | (be_ref[i] != prev))
        def _():
            wg_s[...] = _bf(wg_ref[...])
            wu_s[...] = _bf(wu_ref[...])
            wd_s[...] = _bf(wd_ref[...])

        x = _bf(x_ref[...])
        hid = _silu(_dot(x, wg_s[...])) * _dot(x, wu_s[...])
        y_ref[...] = _dot(_bf(hid), wd_s[...])


def _moe_experts(block_expert, n_used, xb, w_gate, w_up, w_down):
    n_slots, d = xb.shape
    ff = w_gate.shape[-1]
    n_blocks = n_slots // MOE_BLOCK
    grid_spec = pltpu.PrefetchScalarGridSpec(
        num_scalar_prefetch=2,
        grid=(n_blocks,),
        in_specs=[pl.BlockSpec((MOE_BLOCK, d), lambda i, be, nb: (i, 0)),
                  pl.BlockSpec((None, d, ff), lambda i, be, nb: (be[i], 0, 0)),
                  pl.BlockSpec((None, d, ff), lambda i, be, nb: (be[i], 0, 0)),
                  pl.BlockSpec((None, ff, d), lambda i, be, nb: (be[i], 0, 0))],
        out_specs=pl.BlockSpec((MOE_BLOCK, d), lambda i, be, nb: (i, 0)),
        scratch_shapes=[pltpu.VMEM((d, ff), BF16), pltpu.VMEM((d, ff), BF16), pltpu.VMEM((ff, d), BF16)],
    )
    return pl.pallas_call(
        _moe_kernel,
        grid_spec=grid_spec,
        out_shape=jax.ShapeDtypeStruct((n_slots, d), F32),
        compiler_params=_params(("arbitrary",)),
    )(block_expert, n_used, xb, w_gate, w_up, w_down)


def _dispatch(route, t):
    n = 2 * t
    flat_e = route[:, 0:2].astype(jnp.int32).reshape(n)
    onehot = (flat_e[:, None] == jnp.arange(N_EXPERTS, dtype=jnp.int32)[None, :]).astype(jnp.int32)
    csum = jnp.cumsum(onehot, axis=0)
    rank = jnp.sum(onehot * csum, axis=1) - 1
    counts = csum[-1]
    padded = (counts + MOE_BLOCK - 1) // MOE_BLOCK * MOE_BLOCK
    pend = jnp.cumsum(padded)
    pstart = pend - padded
    slot = pstart[flat_e] + rank
    n_blocks = n // MOE_BLOCK + N_EXPERTS
    flat_tok = jnp.arange(n, dtype=jnp.int32) // 2
    slot_tok = jnp.zeros((n_blocks * MOE_BLOCK,), jnp.int32).at[slot].set(flat_tok)
    block_start = jnp.arange(n_blocks, dtype=jnp.int32) * MOE_BLOCK
    block_expert = jnp.minimum(jnp.searchsorted(pend, block_start, side='right'), N_EXPERTS - 1).astype(jnp.int32)
    n_used = (pend[-1] // MOE_BLOCK).astype(jnp.int32).reshape(1)
    return slot.reshape(t, 2), slot_tok, block_expert, n_used


def _moe_back_kernel(x_ref, ya_ref, yb_ref, route_ref, mod_ref, lng_ref, lnb_ref, o_ref):
    rt = route_ref[...]
    y = ya_ref[...] * rt[:, 2:3] + yb_ref[...] * rt[:, 3:4]
    z = DEEPNORM_ALPHA * x_ref[...] + (1.0 + mod_ref[5:6, :]) * y
    o_ref[...] = _layer_norm(z, lng_ref[...], lnb_ref[...])


def _moe_back(x1, ya, yb, route, mod_l, ln_g, ln_b, seq):
    t, d = x1.shape
    tm = min(ROW_TILE, seq)
    tiles_per_seq = seq // tm
    row = lambda n: pl.BlockSpec((tm, n), lambda i: (i, 0))
    return pl.pallas_call(
        _moe_back_kernel,
        grid=(t // tm,),
        in_specs=[row(d), row(d), row(d), row(LANES),
                  pl.BlockSpec((None, N_MOD, d), lambda i: (i // tiles_per_seq, 0, 0)),
                  _const_spec(ln_g.shape), _const_spec(ln_b.shape)],
        out_specs=row(d),
        out_shape=jax.ShapeDtypeStruct((t, d), F32),
        compiler_params=_params(("parallel",)),
    )(x1, ya, yb, route, mod_l, ln_g, ln_b)


def _odd_front_kernel(x_ref, mod_ref, w_ref, tab_ref, q_ref, k_ref, v_ref, gate_ref):
    hb = _bf(x_ref[...] * (1.0 + mod_ref[1:2, :]) + mod_ref[0:1, :])
    c2 = tab_ref[:, 0:LANES]
    s2 = tab_ref[:, LANES:2 * LANES]
    nqk = RET_HEADS * RET_DK
    nv = RET_HEADS * RET_DV

    def rope(b):
        return b * c2 + pltpu.roll(b, RET_DK // 2, 1) * s2

    qa = _dot(hb, w_ref[:, 0:nqk])
    ka = _dot(hb, w_ref[:, nqk:2 * nqk])
    for h in range(RET_HEADS):
        sl = slice(h * LANES, (h + 1) * LANES)
        q_ref[:, sl] = _bf(rope(qa[:, sl]) * (RET_DK ** -0.5))
        k_ref[:, sl] = _bf(rope(ka[:, sl]))
    v_ref[...] = _bf(_dot(hb, w_ref[:, 2 * nqk:2 * nqk + nv]))
    gate_ref[...] = _silu(_dot(hb, w_ref[:, 2 * nqk + nv:2 * nqk + 2 * nv]))


def _odd_front(x2, mod_l, w_in, tabs, seq):
    t, d = x2.shape
    tm = min(ROW_TILE, seq)
    tiles_per_seq = seq // tm
    nqk = RET_HEADS * RET_DK
    nv = RET_HEADS * RET_DV
    row = lambda n: pl.BlockSpec((tm, n), lambda i: (i, 0))
    return pl.pallas_call(
        _odd_front_kernel,
        grid=(t // tm,),
        in_specs=[row(d), pl.BlockSpec((None, N_MOD, d), lambda i: (i // tiles_per_seq, 0, 0)),
                  _const_spec(w_in.shape), row(2 * LANES)],
        out_specs=(row(nqk), row(nqk), row(nv), row(nv)),
        out_shape=(jax.ShapeDtypeStruct((t, nqk), BF16), jax.ShapeDtypeStruct((t, nqk), BF16),
                   jax.ShapeDtypeStruct((t, nv), BF16), jax.ShapeDtypeStruct((t, nv), F32)),
        compiler_params=_params(("parallel",)),
    )(x2, mod_l, w_in, tabs)


def _ret_kernel(q_ref, k_ref, v_ref, gate_ref, lg_ref, gng_ref, gnb_ref, o_ref, state_ref, *, C):
    @pl.when(pl.program_id(2) == 0)
    def _():
        state_ref[...] = jnp.zeros(state_ref.shape, F32)

    lg = lg_ref[:, 0:1]
    ri = lax.broadcasted_iota(jnp.int32, (C, C), 0)
    ci = lax.broadcasted_iota(jnp.int32, (C, C), 1)
    diff = (ri - ci).astype(F32)
    causal = ri >= ci
    decay = jnp.where(causal, jnp.exp(jnp.where(causal, diff, 0.0) * lg), 0.0)
    idx = lax.broadcasted_iota(jnp.int32, (C, 1), 0).astype(F32)
    q = q_ref[...]
    k = k_ref[...]
    v = v_ref[...]
    state = state_ref[...]
    scores = _dot_nt(q, k) * decay
    o = _dot(_bf(scores), v) + _dot(q, _bf(state)) * jnp.exp((idx + 1.0) * lg)
    kd = _bf(k.astype(F32) * jnp.exp((C - 1.0 - idx) * lg))
    state_ref[...] = state * jnp.exp(C * lg) + _dot_tn(kd, v)
    o_ref[...] = _bf(gate_ref[...] * _layer_norm(o, gng_ref[...], gnb_ref[...]))


def _retention(q, k, v, gate, lg, gn_g, gn_b, bsz, seq):
    t = q.shape[0]
    C = min(RET_CHUNK, seq)
    nc = seq // C
    blk = lambda n: pl.BlockSpec((C, n), lambda b, h, j: (b * nc + j, h))
    hrow = lambda n: pl.BlockSpec((None, 1, n), lambda b, h, j: (h, 0, 0))
    return pl.pallas_call(
        functools.partial(_ret_kernel, C=C),
        grid=(bsz, RET_HEADS, nc),
        in_specs=[blk(RET_DK), blk(RET_DK), blk(RET_DV), blk(RET_DV), hrow(LANES), hrow(RET_DV), hrow(RET_DV)],
        out_specs=blk(RET_DV),
        out_shape=jax.ShapeDtypeStruct((t, RET_HEADS * RET_DV), BF16),
        scratch_shapes=[pltpu.VMEM((RET_DK, RET_DV), F32)],
        compiler_params=_params(("parallel", "parallel", "arbitrary")),
    )(q, k, v, gate, lg, gn_g, gn_b)


def _relayout_hy_w_in(w):
    d = w.shape[0]
    z = lambda n: jnp.zeros((d, n), w.dtype)
    c_rope = MLA_Q_RANK + MLA_KV_RANK
    c_gdn = c_rope + MLA_ROPE
    c_a = c_gdn + GDN_QKV
    c_gate = c_a + 2 * GDN_HEADS
    return _bf(jnp.concatenate([
        w[:, 0:c_rope], z(MLA_NOPE), w[:, c_rope:c_gdn], z(LANES - MLA_NOPE - MLA_ROPE),
        w[:, c_gdn:c_a], w[:, c_gate:], w[:, c_a:c_gate], z(LANES - 2 * GDN_HEADS)], axis=1))


def _relayout_w_uq(w):
    r = w.shape[0]
    wh = w.reshape(r, MLA_HEADS, MLA_NOPE + MLA_ROPE)
    pad = jnp.zeros((r, MLA_HEADS, LANES - MLA_NOPE - MLA_ROPE), w.dtype)
    return _bf(jnp.concatenate([wh, pad], axis=-1).reshape(r, MLA_HEADS * LANES))


def _relayout_w_ukv(w):
    r = w.shape[0]
    wh = w.reshape(r, MLA_HEADS, MLA_NOPE + MLA_V)
    pad = jnp.zeros((r, MLA_HEADS, LANES - MLA_NOPE), w.dtype)
    wk = jnp.concatenate([wh[:, :, :MLA_NOPE], pad], axis=-1).reshape(r, MLA_HEADS * LANES)
    wv = wh[:, :, MLA_NOPE:].reshape(r, MLA_HEADS * MLA_V).T
    return _bf(wk), _bf(wv)


def _rope_angles(positions, dim):
    inv_freq = ROPE_BASE ** (-jnp.arange(0, dim, 2, dtype=F32) / dim)
    ang = positions.astype(F32).reshape(-1)[:, None] * inv_freq
    return jnp.cos(ang), jnp.sin(ang)


def _mla_rope_tables(positions):
    cos, sin = _rope_angles(positions, MLA_ROPE)
    t = cos.shape[0]
    z = lambda n: jnp.zeros((t, n), F32)
    ct = jnp.concatenate([jnp.ones((t, MLA_NOPE), F32), cos, cos, z(32)], axis=1)
    s1 = jnp.concatenate([z(MLA_NOPE), -sin, z(48)], axis=1)
    s2 = jnp.concatenate([z(MLA_NOPE + 16), sin, z(32)], axis=1)
    return jnp.concatenate([ct, s1, s2], axis=1)


def _ret_rope_tables(positions):
    cos, sin = _rope_angles(positions, RET_DK)
    return jnp.concatenate([cos, cos, -sin, sin], axis=1)


def _router_weights(w_group, b_group, w_expert, b_expert):
    d = w_group.shape[0]
    pad = LANES - N_EXPERTS - N_GROUPS
    w = jnp.concatenate([w_expert, w_group, jnp.zeros((d, pad), F32)], axis=1)
    b = jnp.concatenate([b_expert, b_group, jnp.zeros((pad,), F32)]).reshape(1, LANES)
    hi = _bf(w)
    return hi, _bf(w - hi.astype(F32)), b


def _pad_lanes(v):
    return jnp.zeros((LANES,), F32).at[:v.shape[0]].set(v)


def kernel(x, c, positions, ada_w, ada_b, ln_mix_g, ln_mix_b, ln_ffn_g, ln_ffn_b, hy_w_in, mla_q_norm, mla_w_uq, mla_kv_norm, mla_w_ukv, gdn_conv_w, gdn_a_log, gdn_dt_bias, gdn_norm, hy_w_out, ret_w_in, ret_gn_g, ret_gn_b, ret_w_out, moe_w_group, moe_b_group, moe_w_expert, moe_b_expert, moe_w_gate, moe_w_up, moe_w_down):
    bsz, seq, d = x.shape
    t = bsz * seq
    mods = _ada_mod(c, ada_w, ada_b)
    mla_tabs = _mla_rope_tables(positions)
    ret_tabs = _ret_rope_tables(positions)
    log_gamma = jnp.log(1.0 - 2.0 ** (-5.0 - jnp.arange(RET_HEADS, dtype=F32)))
    lg = jnp.broadcast_to(log_gamma[:, None, None], (RET_HEADS, 1, LANES))
    xc = x.reshape(t, d)
    for layer in range(DEPTH):
        i = layer // 2
        mod_l = mods[layer]
        if layer % 2 == 0:
            wuk_p, wuv = _relayout_w_ukv(mla_w_ukv[i])
            gparam = jnp.stack([_pad_lanes(gdn_a_log[i]), _pad_lanes(gdn_dt_bias[i])])
            (qm, km, vm, gq, gk, gv, gate, gcol, grow) = _even_front(
                xc, mod_l, _relayout_hy_w_in(hy_w_in[i]), mla_q_norm[i].reshape(1, -1), _relayout_w_uq(mla_w_uq[i]),
                mla_kv_norm[i].reshape(1, -1), wuk_p, wuv, gdn_conv_w[i], gparam, mla_tabs, seq)
            o_mla = _attention(qm, km, vm, bsz, seq)
            n_ch = t // GDN_CHUNK
            growp = grow[:GDN_HEADS].reshape(GDN_PAIRS, 2, n_ch, GDN_CHUNK).transpose(2, 0, 1, 3).reshape(n_ch, GDN_PAIRS, LANES)
            onorm2 = jnp.concatenate([gdn_norm[i], gdn_norm[i]]).reshape(1, LANES)
            y_gdn = _gdn(gq, gk, gv, gate, gcol, growp, onorm2, bsz, seq)
            acts = [o_mla, y_gdn]
            w_out = _bf(hy_w_out[i])
        else:
            q, k, v, gate = _odd_front(xc, mod_l, _bf(ret_w_in[i]), ret_tabs, seq)
            og = _retention(q, k, v, gate, lg, ret_gn_g[i].reshape(RET_HEADS, 1, RET_DV),
                            ret_gn_b[i].reshape(RET_HEADS, 1, RET_DV), bsz, seq)
            acts = [og]
            w_out = _bf(ret_w_out[i])
        wr_hi, wr_lo, br = _router_weights(moe_w_group[layer], moe_b_group[layer], moe_w_expert[layer], moe_b_expert[layer])
        x1, h2, route = _mixer_back(acts, w_out, xc, mod_l, ln_mix_g[layer].reshape(1, d), ln_mix_b[layer].reshape(1, d),
                                    wr_hi, wr_lo, br, seq)
        slot, slot_tok, block_expert, n_used = _dispatch(route, t)
        xb = jnp.take(h2, slot_tok, axis=0)
        yb = _moe_experts(block_expert, n_used, xb, moe_w_gate[layer], moe_w_up[layer], moe_w_down[layer])
        ya0 = jnp.take(yb, slot[:, 0], axis=0)
        ya1 = jnp.take(yb, slot[:, 1], axis=0)
        xc = _moe_back(x1, ya0, ya1, route, mod_l, ln_ffn_g[layer].reshape(1, d), ln_ffn_b[layer].reshape(1, d), seq)
    return xc.reshape(bsz, seq, d)
```

```python
import functools

import jax
import jax.numpy as jnp
from jax import lax
from jax.experimental import pallas as pl
from jax.experimental.pallas import tpu as pltpu

F32 = jnp.float32
BF16 = jnp.bfloat16

DEPTH = 4
N_MOD = 6
LANES = 128

MLA_HEADS = 8
MLA_NOPE = 64
MLA_ROPE = 32
MLA_V = 64
MLA_Q_RANK = 256
MLA_KV_RANK = 128
MLA_SCALE = (MLA_NOPE + MLA_ROPE) ** -0.5
LOG2_E = 1.4426950408889634

GDN_HEADS = 8
GDN_DK = 64
GDN_DV = 64
GDN_CONV = 4
GDN_CHUNK = 64
GDN_HD = GDN_HEADS * GDN_DK
GDN_QKV = 3 * GDN_HD
GDN_PAIRS = GDN_HEADS // 2

RET_HEADS = 8
RET_DK = 128
RET_DV = 256
RET_CHUNK = 256

N_GROUPS = 4
EXPERTS_PER_GROUP = 8
N_EXPERTS = 32
EXPERT_FF = 512
MOE_BLOCK = 256

ROPE_BASE = 10000.0
LN_EPS = 1e-5
RMS_EPS = 1e-6
DEEPNORM_ALPHA = (2.0 * DEPTH) ** 0.25

ROW_TILE = 256
ATTN_TILE = 512
VMEM_LIMIT = 56 * 1024 * 1024

_C_MLA = 0
_C_QKV = 512
_C_GATE = 2048
_C_AB = 2560
_C_END = 2688


def _bf(x):
    return x.astype(BF16)


def _dot(a, b):
    return jnp.dot(a, b, preferred_element_type=F32)


def _dot_nt(a, b):
    return lax.dot_general(a, b, (((1,), (1,)), ((), ())), preferred_element_type=F32)


def _dot_tn(a, b):
    return lax.dot_general(a, b, (((0,), (0,)), ((), ())), preferred_element_type=F32)


def _split2(x):
    hi = _bf(x)
    return hi, _bf(x - hi.astype(F32))


def _split3(x):
    hi = _bf(x)
    r = x - hi.astype(F32)
    mid = _bf(r)
    return hi, mid, _bf(r - mid.astype(F32))


def _silu(x):
    return x * jax.nn.sigmoid(x)


def _layer_norm(z, g, b):
    mu = jnp.mean(z, -1, keepdims=True)
    zc = z - mu
    var = jnp.mean(zc * zc, -1, keepdims=True)
    return zc * lax.rsqrt(var + LN_EPS) * g + b


def _rms_norm(x, g):
    return x * lax.rsqrt(jnp.mean(x * x, -1, keepdims=True) + RMS_EPS) * g


def _params(sem):
    return pltpu.CompilerParams(dimension_semantics=sem, vmem_limit_bytes=VMEM_LIMIT)


def _const_spec(shape):
    nd = len(shape)
    return pl.BlockSpec(shape, lambda *_: (0,) * nd)


def _ada_kernel(c_ref, w_ref, b_ref, o_ref):
    c = c_ref[...]
    o_ref[...] = _dot(_bf(_silu(c)), _bf(w_ref[...])) + b_ref[...]


def _ada_mod(c, ada_w, ada_b):
    depth, d, n = ada_w.shape
    bsz = c.shape[0]
    rows = 16
    tn = 1536
    cp = jnp.zeros((rows, d), F32).at[:bsz].set(c)
    out = pl.pallas_call(
        _ada_kernel,
        grid=(depth, n // tn),
        in_specs=[pl.BlockSpec((rows, d), lambda l, j: (0, 0)),
                  pl.BlockSpec((None, d, tn), lambda l, j: (l, 0, j)),
                  pl.BlockSpec((None, 1, tn), lambda l, j: (l, 0, j))],
        out_specs=pl.BlockSpec((None, rows, tn), lambda l, j: (l, 0, j)),
        out_shape=jax.ShapeDtypeStruct((depth, rows, n), F32),
        compiler_params=_params(("parallel", "parallel")),
    )(cp, ada_w, ada_b.reshape(depth, 1, n))
    return out[:, :bsz].reshape(depth, bsz, N_MOD, d)


def _even_front_kernel(x_ref, mod_ref, w_ref, qn_ref, wuq_ref, kvn_ref, wuk_ref, wuv_ref,
                       cw_ref, gp_ref, tab_ref,
                       q_ref, k_ref, v_ref, gq_ref, gk_ref, gv_ref, gate_ref, gcol_ref, grow_ref,
                       xbuf, *, tm, tiles_per_seq):
    i = pl.program_id(0)
    hb = _bf(x_ref[...] * (1.0 + mod_ref[1:2, :]) + mod_ref[0:1, :])

    pm = _dot(hb, w_ref[:, _C_MLA:_C_QKV])
    ct = tab_ref[:, 0:LANES]
    s1 = tab_ref[:, LANES:2 * LANES]
    s2 = tab_ref[:, 2 * LANES:3 * LANES]

    def rope(b):
        return b * ct + pltpu.roll(b, LANES - 16, 1) * s1 + pltpu.roll(b, 16, 1) * s2

    qa = _dot(_bf(_rms_norm(pm[:, 0:MLA_Q_RANK], qn_ref[...])), wuq_ref[...])
    for h in range(MLA_HEADS):
        sl = slice(h * LANES, (h + 1) * LANES)
        q_ref[:, sl] = _bf(rope(qa[:, sl]) * (MLA_SCALE * LOG2_E))
    kvn = _bf(_rms_norm(pm[:, MLA_Q_RANK:MLA_Q_RANK + MLA_KV_RANK], kvn_ref[...]))
    ka = _dot(kvn, wuk_ref[...])
    kr = rope(pm[:, MLA_Q_RANK + MLA_KV_RANK:_C_QKV])
    for h in range(MLA_HEADS):
        sl = slice(h * LANES, (h + 1) * LANES)
        k_ref[:, sl] = _bf(ka[:, sl] + kr)
    v_ref[...] = _bf(_dot_nt(wuv_ref[...], kvn))

    @pl.when(i % tiles_per_seq == 0)
    def _():
        xbuf[0:8, :] = jnp.zeros((8, GDN_QKV), F32)

    xbuf[8:8 + tm, :] = _dot(hb, w_ref[:, _C_QKV:_C_GATE])
    cw = cw_ref[...]
    y = cw[0:1, :] * xbuf[5:5 + tm, :]
    for t in range(1, GDN_CONV):
        y = y + cw[t:t + 1, :] * xbuf[5 + t:5 + t + tm, :]
    xbuf[0:8, :] = xbuf[tm:tm + 8, :]
    y = _silu(y)

    r = lax.broadcasted_iota(jnp.int32, (GDN_HD, GDN_HD), 0) // GDN_DK
    c = lax.broadcasted_iota(jnp.int32, (GDN_HD, GDN_HD), 1) // GDN_DK
    head_ones = jnp.where(r == c, 1.0, 0.0).astype(BF16)

    def head_sumsq(z):
        hi, lo = _split2(z * z)
        return _dot(hi, head_ones) + _dot(lo, head_ones)

    qg = y[:, 0:GDN_HD]
    kg = y[:, GDN_HD:2 * GDN_HD]
    gq_ref[...] = qg * lax.rsqrt(head_sumsq(qg) + RMS_EPS) * (GDN_DK ** -0.5)
    gk_ref[...] = kg * lax.rsqrt(head_sumsq(kg) + RMS_EPS)
    gv_ref[...] = y[:, 2 * GDN_HD:3 * GDN_HD]
    gate_ref[...] = _dot(hb, w_ref[:, _C_GATE:_C_AB])

    ab = _dot(hb, w_ref[:, _C_AB:_C_END])
    lane = lax.broadcasted_iota(jnp.int32, (tm, LANES), 1)
    z = ab + gp_ref[1:2, :]
    softplus = jnp.maximum(z, 0.0) + jnp.log(1.0 + jnp.exp(-jnp.abs(z)))
    la = jnp.where(lane < GDN_HEADS, -jnp.exp(gp_ref[0:1, :]) * softplus, 0.0)
    rr = lax.broadcasted_iota(jnp.int32, (tm, tm), 0)
    cc = lax.broadcasted_iota(jnp.int32, (tm, tm), 1)
    tri = jnp.where((rr // GDN_CHUNK == cc // GDN_CHUNK) & (cc <= rr), 1.0, 0.0).astype(BF16)
    l0, l1, l2 = _split3(la)
    g = _dot(tri, l0) + _dot(tri, l1) + _dot(tri, l2)
    gcol = jnp.where(lane < GDN_HEADS, g, jax.nn.sigmoid(ab))
    gcol_ref[...] = gcol
    grow_ref[...] = gcol.T[0:16, :]


def _even_front(x2, mod_l, w_in_p, qn, wuq_p, kvn, wuk_p, wuv, conv_w, gparam, tabs, seq):
    t, d = x2.shape
    tm = min(ROW_TILE, seq)
    tiles_per_seq = seq // tm
    row = lambda n: pl.BlockSpec((tm, n), lambda i: (i, 0))
    out_shapes = (
        jax.ShapeDtypeStruct((t, MLA_HEADS * LANES), BF16),
        jax.ShapeDtypeStruct((t, MLA_HEADS * LANES), BF16),
        jax.ShapeDtypeStruct((MLA_HEADS * MLA_V, t), BF16),
        jax.ShapeDtypeStruct((t, GDN_HD), F32),
        jax.ShapeDtypeStruct((t, GDN_HD), F32),
        jax.ShapeDtypeStruct((t, GDN_HD), F32),
        jax.ShapeDtypeStruct((t, GDN_HD), F32),
        jax.ShapeDtypeStruct((t, LANES), F32),
        jax.ShapeDtypeStruct((16, t), F32),
    )
    return pl.pallas_call(
        functools.partial(_even_front_kernel, tm=tm, tiles_per_seq=tiles_per_seq),
        grid=(t // tm,),
        in_specs=[row(d),
                  pl.BlockSpec((None, N_MOD, d), lambda i: (i // tiles_per_seq, 0, 0)),
                  _const_spec(w_in_p.shape), _const_spec(qn.shape), _const_spec(wuq_p.shape),
                  _const_spec(kvn.shape), _const_spec(wuk_p.shape), _const_spec(wuv.shape),
                  _const_spec(conv_w.shape), _const_spec(gparam.shape), row(3 * LANES)],
        out_specs=(row(MLA_HEADS * LANES), row(MLA_HEADS * LANES),
                   pl.BlockSpec((MLA_HEADS * MLA_V, tm), lambda i: (0, i)),
                   row(GDN_HD), row(GDN_HD), row(GDN_HD), row(GDN_HD), row(LANES),
                   pl.BlockSpec((16, tm), lambda i: (0, i))),
        out_shape=out_shapes,
        scratch_shapes=[pltpu.VMEM((tm + 8, GDN_QKV), F32)],
        compiler_params=_params(("arbitrary",)),
    )(x2, mod_l, w_in_p, qn, wuq_p, kvn, wuk_p, wuv, conv_w, gparam, tabs)


def _attn_kernel(q_ref, k_ref, vt_ref, o_ref, m_ref, l_ref, acc_ref, *, tq):
    qi = pl.program_id(1)
    ki = pl.program_id(2)

    @pl.when(ki == 0)
    def _():
        m_ref[...] = jnp.full(m_ref.shape, -jnp.inf, F32)
        l_ref[...] = jnp.zeros(l_ref.shape, F32)
        acc_ref[...] = jnp.zeros(acc_ref.shape, F32)

    def scores(h):
        sl = slice(h * LANES, (h + 1) * LANES)
        return _dot_nt(k_ref[:, sl], q_ref[:, sl])

    def step(diagonal):
        if diagonal:
            key = lax.broadcasted_iota(jnp.int32, (tq, tq), 0)
            qry = lax.broadcasted_iota(jnp.int32, (tq, tq), 1)
            keep = key <= qry
        s_next = scores(0)
        for h in range(MLA_HEADS):
            s = s_next
            if h + 1 < MLA_HEADS:
                s_next = scores(h + 1)
            if diagonal:
                s = jnp.where(keep, s, -jnp.inf)
            rows = slice(h * MLA_V, (h + 1) * MLA_V)
            m_prev = m_ref[h:h + 1, :]
            m_new = jnp.maximum(m_prev, jnp.max(s, 0, keepdims=True))
            alpha = jnp.exp2(m_prev - m_new)
            p = jnp.exp2(s - m_new)
            l_ref[h:h + 1, :] = alpha * l_ref[h:h + 1, :] + jnp.sum(p, 0, keepdims=True)
            acc_ref[rows, :] = alpha * acc_ref[rows, :] + _dot(vt_ref[rows, :], _bf(p))
            m_ref[h:h + 1, :] = m_new

    @pl.when(ki < qi)
    def _():
        step(False)

    @pl.when(ki == qi)
    def _():
        step(True)
        for h in range(MLA_HEADS):
            rows = slice(h * MLA_V, (h + 1) * MLA_V)
            acc_ref[rows, :] = acc_ref[rows, :] / l_ref[h:h + 1, :]
        o_ref[...] = _bf(acc_ref[...].T)


def _attention(qm, km, vm, bsz, seq):
    t = qm.shape[0]
    tq = min(ATTN_TILE, seq)
    nq = seq // tq
    return pl.pallas_call(
        functools.partial(_attn_kernel, tq=tq),
        grid=(bsz, nq, nq),
        in_specs=[pl.BlockSpec((tq, MLA_HEADS * LANES), lambda b, qi, ki: (b * nq + qi, 0)),
                  pl.BlockSpec((tq, MLA_HEADS * LANES), lambda b, qi, ki: (b * nq + jnp.minimum(ki, qi), 0)),
                  pl.BlockSpec((MLA_HEADS * MLA_V, tq), lambda b, qi, ki: (0, b * nq + jnp.minimum(ki, qi)))],
        out_specs=pl.BlockSpec((tq, MLA_HEADS * MLA_V), lambda b, qi, ki: (b * nq + qi, 0)),
        out_shape=jax.ShapeDtypeStruct((t, MLA_HEADS * MLA_V), BF16),
        scratch_shapes=[pltpu.VMEM((MLA_HEADS, tq), F32),
                        pltpu.VMEM((MLA_HEADS, tq), F32),
                        pltpu.VMEM((MLA_HEADS * MLA_V, tq), F32)],
        compiler_params=_params(("parallel", "parallel", "arbitrary")),
    )(qm, km, vm)


def _gdn_kernel(q_ref, k_ref, v_ref, gate_ref, gcol_ref, grow_ref, on_ref, y_ref, state_ref, *, n_chunks):
    C = GDN_CHUNK

    @pl.when(pl.program_id(1) == 0)
    def _():
        state_ref[...] = jnp.zeros(state_ref.shape, F32)

    lane = lax.broadcasted_iota(jnp.int32, (C, LANES), 1)
    rowi = lax.broadcasted_iota(jnp.int32, (C, LANES), 0)
    first = lane < GDN_DK
    col = jnp.where(first, lane, lane - GDN_DK)
    tril = col <= rowi
    strict = col < rowi
    eye = jnp.where(col == rowi, 1.0, 0.0)
    blk16 = (col // 16) == (rowi // 16)
    blk32 = (col // 32) == (rowi // 32)
    r2 = lax.broadcasted_iota(jnp.int32, (LANES, LANES), 0)
    c2 = lax.broadcasted_iota(jnp.int32, (LANES, LANES), 1)
    first_rows = r2 < GDN_DK
    head_ones = jnp.where((r2 < GDN_DK) == (c2 < GDN_DK), 1.0, 0.0).astype(BF16)

    def bd(y):
        return jnp.concatenate([jnp.where(first, y, 0), jnp.where(first, 0, y)], axis=0)

    def pmm(x, y):
        return _dot(_bf(x), bd(_bf(y)))

    def pmm_hi(x, y):
        xh, xl = _split2(x)
        yh, yl = _split2(y)
        bh = bd(yh)
        return _dot(xh, bh) + _dot(xh, bd(yl)) + _dot(xl, bh)

    def pair_cols(a, c0, c1):
        return jnp.where(first, a[:, c0:c0 + 1], a[:, c1:c1 + 1])

    def chunk(ci, carry):
        rows = pl.ds(pl.multiple_of(ci * C, C), C)
        pairs = range(GDN_PAIRS)
        sls = [slice(p * LANES, (p + 1) * LANES) for p in pairs]
        gc = gcol_ref[rows, :]
        gr_all = grow_ref[ci]
        q = [q_ref[rows, sl] for sl in sls]
        k = [k_ref[rows, sl] for sl in sls]
        v = [v_ref[rows, sl] for sl in sls]
        g = [pair_cols(gc, 2 * p, 2 * p + 1) for p in pairs]
        beta = [pair_cols(gc, GDN_HEADS + 2 * p, GDN_HEADS + 2 * p + 1) for p in pairs]
        decay = [jnp.where(tril, jnp.exp(jnp.where(tril, g[p] - gr_all[p:p + 1, :], 0.0)), 0.0) for p in pairs]
        kb = [k[p] * beta[p] for p in pairs]
        kbd = [bd(_bf(k[p])) for p in pairs]
        lmat = [jnp.where(strict, _dot_nt(_bf(kb[p]), kbd[p]) * decay[p], 0.0) for p in pairs]
        a_intra = [jnp.where(tril, _dot_nt(_bf(q[p]), kbd[p]) * decay[p], 0.0) for p in pairs]
        n1 = [jnp.where(blk16, lmat[p], 0.0) for p in pairs]
        x = [eye - n1[p] for p in pairs]
        nk = n1
        for _ in range(3):
            nk = [pmm_hi(nk[p], nk[p]) for p in pairs]
            x = [x[p] + pmm_hi(x[p], nk[p]) for p in pairs]
        for off in ([jnp.where(blk32 & jnp.logical_not(blk16), lmat[p], 0.0) for p in pairs],
                    [jnp.where(blk32, 0.0, lmat[p]) for p in pairs]):
            xb = [pmm_hi(x[p], off[p]) for p in pairs]
            x = [x[p] - pmm_hi(xb[p], x[p]) for p in pairs]
        u = [pmm(x[p], v[p] * beta[p]) for p in pairs]
        w = [pmm(x[p], kb[p] * jnp.exp(g[p])) for p in pairs]
        state = [state_ref[p] for p in pairs]
        sb = [_bf(state[p]) for p in pairs]
        v_new = [u[p] - _dot(_bf(w[p]), sb[p]) for p in pairs]
        o = [_dot(_bf(q[p] * jnp.exp(g[p])), sb[p]) + pmm(a_intra[p], v_new[p]) for p in pairs]
        for p in pairs:
            g_last = g[p][C - 1:C, :]
            k_dec = k[p] * jnp.exp(g_last - g[p])
            scale = jnp.exp(jnp.where(first_rows, g_last[:, 0:1], g_last[:, GDN_DK:GDN_DK + 1]))
            state_ref[p] = state[p] * scale + _dot_tn(bd(_bf(k_dec)), bd(_bf(v_new[p])))
        for p in pairs:
            hi, lo = _split2(o[p] * o[p])
            ms = (_dot(hi, head_ones) + _dot(lo, head_ones)) * (1.0 / GDN_DV)
            on = o[p] * lax.rsqrt(ms + RMS_EPS) * on_ref[...]
            y_ref[rows, sls[p]] = _bf(on * _silu(gate_ref[rows, sls[p]]))
        return carry

    lax.fori_loop(0, n_chunks, chunk, 0)


def _gdn(gq, gk, gv, gate, gcol, growp, onorm2, bsz, seq):
    t = gq.shape[0]
    r = min(ROW_TILE, seq)
    nblk = seq // r
    n_chunks = r // GDN_CHUNK
    row = lambda n: pl.BlockSpec((r, n), lambda b, j: (b * nblk + j, 0))
    return pl.pallas_call(
        functools.partial(_gdn_kernel, n_chunks=n_chunks),
        grid=(bsz, nblk),
        in_specs=[row(GDN_HD), row(GDN_HD), row(GDN_HD), row(GDN_HD), row(LANES),
                  pl.BlockSpec((n_chunks, GDN_PAIRS, LANES), lambda b, j: (b * nblk + j, 0, 0)),
                  _const_spec(onorm2.shape)],
        out_specs=row(GDN_HD),
        out_shape=jax.ShapeDtypeStruct((t, GDN_HD), BF16),
        scratch_shapes=[pltpu.VMEM((GDN_PAIRS, LANES, LANES), F32)],
        compiler_params=_params(("parallel", "arbitrary")),
    )(gq, gk, gv, gate, gcol, growp, onorm2)


def _back_kernel(*refs, n_in):
    a_refs = refs[:n_in]
    (w_ref, x_ref, mod_ref, lng_ref, lnb_ref, wrh_ref, wrl_ref, br_ref,
     x1_ref, h2_ref, route_ref, counts_ref, base_ref) = refs[n_in:]
    off = 0
    y = None
    for a_ref in a_refs:
        kk = a_ref.shape[1]
        part = _dot(a_ref[...], w_ref[off:off + kk, :])
        y = part if y is None else y + part
        off += kk
    z = DEEPNORM_ALPHA * x_ref[...] + (1.0 + mod_ref[2:3, :]) * y
    x1 = _layer_norm(z, lng_ref[...], lnb_ref[...])
    x1_ref[...] = x1
    h2 = x1 * (1.0 + mod_ref[4:5, :]) + mod_ref[3:4, :]
    h2_ref[...] = h2

    hh, hl = _split2(h2)
    logits = _dot(hh, wrh_ref[...]) + _dot(hh, wrl_ref[...]) + _dot(hl, wrh_ref[...]) + br_ref[...]
    lane = lax.broadcasted_iota(jnp.int32, logits.shape, 1)
    lane_f = lane.astype(F32)
    neg = -jnp.inf
    gmask = (lane >= N_EXPERTS) & (lane < N_EXPERTS + N_GROUPS)
    gl = jnp.where(gmask, logits, neg)
    gmax = jnp.max(gl, -1, keepdims=True)
    g_top = 1.0 / jnp.sum(jnp.where(gmask, jnp.exp(gl - gmax), 0.0), -1, keepdims=True)
    g_lane = jnp.min(jnp.where(gl == gmax, lane_f, 1e9), -1, keepdims=True)
    g_idx = g_lane.astype(jnp.int32) - N_EXPERTS
    emask = (lane < N_EXPERTS) & ((lane // EXPERTS_PER_GROUP) == g_idx)
    el = jnp.where(emask, logits, neg)
    m1 = jnp.max(el, -1, keepdims=True)
    i1 = jnp.min(jnp.where(el == m1, lane_f, 1e9), -1, keepdims=True)
    el2 = jnp.where(lane_f == i1, neg, el)
    m2 = jnp.max(el2, -1, keepdims=True)
    i2 = jnp.min(jnp.where(el2 == m2, lane_f, 1e9), -1, keepdims=True)
    e2 = jnp.exp(m2 - m1)
    w1 = g_top / (1.0 + e2)
    w2 = g_top * e2 / (1.0 + e2)

    @pl.when(pl.program_id(0) == 0)
    def _():
        base_ref[...] = jnp.zeros(base_ref.shape, F32)

    tm = logits.shape[0]
    rr = lax.broadcasted_iota(jnp.int32, (tm, tm), 0)
    cc = lax.broadcasted_iota(jnp.int32, (tm, tm), 1)
    tri = jnp.where(cc <= rr, 1.0, 0.0).astype(BF16)
    oh1 = jnp.where(lane_f == i1, 1.0, 0.0)
    oh2 = jnp.where(lane_f == i2, 1.0, 0.0)
    cs1 = _dot(tri, _bf(oh1))
    cs2 = _dot(tri, _bf(oh2))
    base = base_ref[...]
    tot1 = cs1[tm - 1:tm, :]
    r1 = jnp.sum(oh1 * (base + cs1), -1, keepdims=True) - 1.0
    r2 = jnp.sum(oh2 * (base + tot1 + cs2), -1, keepdims=True) - 1.0
    total = base + tot1 + cs2[tm - 1:tm, :]
    base_ref[...] = total
    counts_ref[...] = total
    route = jnp.where(lane == 0, i1, jnp.where(lane == 1, i2, jnp.where(lane == 2, w1, jnp.where(lane == 3, w2, 0.0))))
    route_ref[...] = jnp.where(lane == 4, r1, jnp.where(lane == 5, r2, route))


def _mixer_back(acts, w_out, x2, mod_l, ln_g, ln_b, wr_hi, wr_lo, br, seq):
    t, d = x2.shape
    tm = min(ROW_TILE, seq)
    tiles_per_seq = seq // tm
    row = lambda n: pl.BlockSpec((tm, n), lambda i: (i, 0))
    return pl.pallas_call(
        functools.partial(_back_kernel, n_in=len(acts)),
        grid=(t // tm,),
        in_specs=[row(a.shape[1]) for a in acts] + [
            _const_spec(w_out.shape), row(d),
            pl.BlockSpec((None, N_MOD, d), lambda i: (i // tiles_per_seq, 0, 0)),
            _const_spec(ln_g.shape), _const_spec(ln_b.shape),
            _const_spec(wr_hi.shape), _const_spec(wr_lo.shape), _const_spec(br.shape)],
        out_specs=(row(d), row(d), row(LANES), _const_spec((1, LANES))),
        out_shape=(jax.ShapeDtypeStruct((t, d), F32), jax.ShapeDtypeStruct((t, d), F32),
                   jax.ShapeDtypeStruct((t, LANES), F32), jax.ShapeDtypeStruct((1, LANES), F32)),
        scratch_shapes=[pltpu.VMEM((1, LANES), F32)],
        compiler_params=_params(("arbitrary",)),
    )(*acts, w_out, x2, mod_l, ln_g, ln_b, wr_hi, wr_lo, br)


def _moe_kernel(be_ref, nb_ref, x_ref, wg_ref, wu_ref, wd_ref, y_ref, wg_s, wu_s, wd_s):
    i = pl.program_id(0)

    @pl.when(i < nb_ref[0])
    def _():
        prev = be_ref[jnp.maximum(i - 1, 0)]

        @pl.when((i == 0) | (be_ref[i] != prev))
        def _():
            wg_s[...] = _bf(wg_ref[...])
            wu_s[...] = _bf(wu_ref[...])
            wd_s[...] = _bf(wd_ref[...])

        x = _bf(x_ref[...])
        hid = _silu(_dot(x, wg_s[...])) * _dot(x, wu_s[...])
        y_ref[...] = _dot(_bf(hid), wd_s[...])

    @pl.when(i >= nb_ref[0])
    def _():
        y_ref[...] = jnp.zeros(y_ref.shape, F32)


def _moe_experts(block_expert, n_used, xb, w_gate, w_up, w_down):
    n_slots, d = xb.shape
    ff = w_gate.shape[-1]
    n_blocks = n_slots // MOE_BLOCK
    grid_spec = pltpu.PrefetchScalarGridSpec(
        num_scalar_prefetch=2,
        grid=(n_blocks,),
        in_specs=[pl.BlockSpec((MOE_BLOCK, d), lambda i, be, nb: (jnp.minimum(i, nb[0] - 1), 0)),
                  pl.BlockSpec((None, d, ff), lambda i, be, nb: (be[i], 0, 0)),
                  pl.BlockSpec((None, d, ff), lambda i, be, nb: (be[i], 0, 0)),
                  pl.BlockSpec((None, ff, d), lambda i, be, nb: (be[i], 0, 0))],
        out_specs=pl.BlockSpec((MOE_BLOCK, d), lambda i, be, nb: (i, 0)),
        scratch_shapes=[pltpu.VMEM((d, ff), BF16), pltpu.VMEM((d, ff), BF16), pltpu.VMEM((ff, d), BF16)],
    )
    return pl.pallas_call(
        _moe_kernel,
        grid_spec=grid_spec,
        out_shape=jax.ShapeDtypeStruct((n_slots, d), F32),
        compiler_params=_params(("arbitrary",)),
    )(block_expert, n_used, xb, w_gate, w_up, w_down)


def _slot_tables(route, counts_row, t):
    counts = counts_row[0, :N_EXPERTS].astype(jnp.int32)
    padded = (counts + MOE_BLOCK - 1) // MOE_BLOCK * MOE_BLOCK
    pend = jnp.cumsum(padded)
    pstart = pend - padded
    e = route[:, 0:2].astype(jnp.int32)
    rank = route[:, 4:6].astype(jnp.int32)
    slot = (jnp.take(pstart, e) + rank).reshape(2 * t)
    n_blocks = 2 * t // MOE_BLOCK + N_EXPERTS
    block_start = jnp.arange(n_blocks, dtype=jnp.int32) * MOE_BLOCK
    block_expert = jnp.minimum(jnp.searchsorted(pend, block_start, side='right'), N_EXPERTS - 1).astype(jnp.int32)
    n_used = (pend[-1] // MOE_BLOCK).astype(jnp.int32).reshape(1)
    n_slots = n_blocks * MOE_BLOCK
    tail = jnp.stack([pend[-1], (n_slots - pend[-1]) // (MOE_BLOCK // 2)])
    pad = jnp.concatenate([pstart + counts, padded - counts, tail]).astype(jnp.int32)
    return slot, block_expert, n_used, pad, n_slots


def _scatter_kernel(slot_ref, pad_ref, h_ref, xb_ref, zero_ref, sems, zsem, *, tm, n_steps):
    i = pl.program_id(0)

    def pad_copies(fn):
        def per_expert(e, carry):
            start = pad_ref[e]

            def body(j, c):
                fn(pltpu.make_async_copy(zero_ref.at[pl.ds(0, 1)], xb_ref.at[pl.ds(start + j, 1)], zsem))
                return c

            lax.fori_loop(0, pad_ref[N_EXPERTS + e], body, 0)
            return carry

        lax.fori_loop(0, N_EXPERTS, per_expert, 0)
        piece = zero_ref.shape[0]
        tail = pad_ref[2 * N_EXPERTS]

        def tail_body(j, c):
            pos = pl.multiple_of(tail + j * piece, piece)
            fn(pltpu.make_async_copy(zero_ref, xb_ref.at[pl.ds(pos, piece)], zsem))
            return c

        lax.fori_loop(0, pad_ref[2 * N_EXPERTS + 1], tail_body, 0)

    @pl.when(i == 0)
    def _():
        zero_ref[...] = jnp.zeros(zero_ref.shape, F32)
        pad_copies(lambda c: c.start())
        pad_copies(lambda c: c.wait())

    def rows(step, fn):
        sem = sems.at[step % 2]

        def body(r, carry):
            tok = step * tm + r
            for choice in range(2):
                dst = slot_ref[2 * tok + choice]
                fn(pltpu.make_async_copy(h_ref.at[pl.ds(tok, 1)], xb_ref.at[pl.ds(dst, 1)], sem))
            return carry

        lax.fori_loop(0, tm, body, 0)

    rows(i, lambda c: c.start())

    @pl.when(i > 0)
    def _():
        rows(i - 1, lambda c: c.wait())

    @pl.when(i == n_steps - 1)
    def _():
        rows(i, lambda c: c.wait())


def _scatter_rows(slot, pad, h2, n_slots, seq):
    t, d = h2.shape
    tm = min(ROW_TILE, seq)
    n_steps = t // tm
    grid_spec = pltpu.PrefetchScalarGridSpec(
        num_scalar_prefetch=2,
        grid=(n_steps,),
        in_specs=[pl.BlockSpec(memory_space=pl.ANY)],
        out_specs=pl.BlockSpec(memory_space=pl.ANY),
        scratch_shapes=[pltpu.VMEM((MOE_BLOCK // 2, d), F32), pltpu.SemaphoreType.DMA((2,)),
                        pltpu.SemaphoreType.DMA(())],
    )
    return pl.pallas_call(
        functools.partial(_scatter_kernel, tm=tm, n_steps=n_steps),
        grid_spec=grid_spec,
        out_shape=jax.ShapeDtypeStruct((n_slots, d), F32),
        compiler_params=_params(("arbitrary",)),
    )(slot, pad, h2)


def _moe_back_kernel(slot_ref, x_ref, route_ref, mod_ref, lng_ref, lnb_ref, yb_ref, o_ref, ybuf, sems,
                     *, tm, n_steps):
    i = pl.program_id(0)

    def rows(step, fn):
        buf = step % 2
        sem = sems.at[buf]

        def body(r, carry):
            tok = step * tm + r
            for choice in range(2):
                src = slot_ref[2 * tok + choice]
                fn(pltpu.make_async_copy(yb_ref.at[pl.ds(src, 1)], ybuf.at[buf, choice, pl.ds(r, 1)], sem))
            return carry

        lax.fori_loop(0, tm, body, 0)

    @pl.when(i == 0)
    def _():
        rows(0, lambda c: c.start())

    @pl.when(i + 1 < n_steps)
    def _():
        rows(i + 1, lambda c: c.start())

    rows(i, lambda c: c.wait())
    buf = i % 2
    rt = route_ref[...]
    y = ybuf[buf, 0] * rt[:, 2:3] + ybuf[buf, 1] * rt[:, 3:4]
    z = DEEPNORM_ALPHA * x_ref[...] + (1.0 + mod_ref[5:6, :]) * y
    o_ref[...] = _layer_norm(z, lng_ref[...], lnb_ref[...])


def _moe_back(slot, x1, yb, route, mod_l, ln_g, ln_b, seq):
    t, d = x1.shape
    tm = min(ROW_TILE, seq)
    tiles_per_seq = seq // tm
    n_steps = t // tm
    row = lambda n: pl.BlockSpec((tm, n), lambda i, sl: (i, 0))
    const = lambda shape: pl.BlockSpec(shape, lambda i, sl: (0,) * len(shape))
    grid_spec = pltpu.PrefetchScalarGridSpec(
        num_scalar_prefetch=1,
        grid=(n_steps,),
        in_specs=[row(d), row(LANES),
                  pl.BlockSpec((None, N_MOD, d), lambda i, sl: (i // tiles_per_seq, 0, 0)),
                  const(ln_g.shape), const(ln_b.shape), pl.BlockSpec(memory_space=pl.ANY)],
        out_specs=row(d),
        scratch_shapes=[pltpu.VMEM((2, 2, tm, d), F32), pltpu.SemaphoreType.DMA((2,))],
    )
    return pl.pallas_call(
        functools.partial(_moe_back_kernel, tm=tm, n_steps=n_steps),
        grid_spec=grid_spec,
        out_shape=jax.ShapeDtypeStruct((t, d), F32),
        compiler_params=_params(("arbitrary",)),
    )(slot, x1, route, mod_l, ln_g, ln_b, yb)


def _odd_front_kernel(x_ref, mod_ref, w_ref, tab_ref, q_ref, k_ref, v_ref, gate_ref):
    hb = _bf(x_ref[...] * (1.0 + mod_ref[1:2, :]) + mod_ref[0:1, :])
    c2 = tab_ref[:, 0:LANES]
    s2 = tab_ref[:, LANES:2 * LANES]
    nqk = RET_HEADS * RET_DK
    nv = RET_HEADS * RET_DV

    def rope(b):
        return b * c2 + pltpu.roll(b, RET_DK // 2, 1) * s2

    qa = _dot(hb, w_ref[:, 0:nqk])
    ka = _dot(hb, w_ref[:, nqk:2 * nqk])
    for h in range(RET_HEADS):
        sl = slice(h * LANES, (h + 1) * LANES)
        q_ref[:, sl] = _bf(rope(qa[:, sl]) * (RET_DK ** -0.5))
        k_ref[:, sl] = _bf(rope(ka[:, sl]))
    v_ref[...] = _bf(_dot(hb, w_ref[:, 2 * nqk:2 * nqk + nv]))
    gate_ref[...] = _silu(_dot(hb, w_ref[:, 2 * nqk + nv:2 * nqk + 2 * nv]))


def _odd_front(x2, mod_l, w_in, tabs, seq):
    t, d = x2.shape
    tm = min(ROW_TILE, seq)
    tiles_per_seq = seq // tm
    nqk = RET_HEADS * RET_DK
    nv = RET_HEADS * RET_DV
    row = lambda n: pl.BlockSpec((tm, n), lambda i: (i, 0))
    return pl.pallas_call(
        _odd_front_kernel,
        grid=(t // tm,),
        in_specs=[row(d), pl.BlockSpec((None, N_MOD, d), lambda i: (i // tiles_per_seq, 0, 0)),
                  _const_spec(w_in.shape), row(2 * LANES)],
        out_specs=(row(nqk), row(nqk), row(nv), row(nv)),
        out_shape=(jax.ShapeDtypeStruct((t, nqk), BF16), jax.ShapeDtypeStruct((t, nqk), BF16),
                   jax.ShapeDtypeStruct((t, nv), BF16), jax.ShapeDtypeStruct((t, nv), F32)),
        compiler_params=_params(("parallel",)),
    )(x2, mod_l, w_in, tabs)


def _ret_kernel(q_ref, k_ref, v_ref, gate_ref, lg_ref, gng_ref, gnb_ref, o_ref, state_ref, *, C):
    @pl.when(pl.program_id(2) == 0)
    def _():
        state_ref[...] = jnp.zeros(state_ref.shape, F32)

    lg = lg_ref[:, 0:1]
    ri = lax.broadcasted_iota(jnp.int32, (C, C), 0)
    ci = lax.broadcasted_iota(jnp.int32, (C, C), 1)
    diff = (ri - ci).astype(F32)
    causal = ri >= ci
    decay = jnp.where(causal, jnp.exp(jnp.where(causal, diff, 0.0) * lg), 0.0)
    idx = lax.broadcasted_iota(jnp.int32, (C, 1), 0).astype(F32)
    q = q_ref[...]
    k = k_ref[...]
    v = v_ref[...]
    state = state_ref[...]
    scores = _dot_nt(q, k) * decay
    o = _dot(_bf(scores), v) + _dot(q, _bf(state)) * jnp.exp((idx + 1.0) * lg)
    kd = _bf(k.astype(F32) * jnp.exp((C - 1.0 - idx) * lg))
    state_ref[...] = state * jnp.exp(C * lg) + _dot_tn(kd, v)
    o_ref[...] = _bf(gate_ref[...] * _layer_norm(o, gng_ref[...], gnb_ref[...]))


def _retention(q, k, v, gate, lg, gn_g, gn_b, bsz, seq):
    t = q.shape[0]
    C = min(RET_CHUNK, seq)
    nc = seq // C
    blk = lambda n: pl.BlockSpec((C, n), lambda b, h, j: (b * nc + j, h))
    hrow = lambda n: pl.BlockSpec((None, 1, n), lambda b, h, j: (h, 0, 0))
    return pl.pallas_call(
        functools.partial(_ret_kernel, C=C),
        grid=(bsz, RET_HEADS, nc),
        in_specs=[blk(RET_DK), blk(RET_DK), blk(RET_DV), blk(RET_DV), hrow(LANES), hrow(RET_DV), hrow(RET_DV)],
        out_specs=blk(RET_DV),
        out_shape=jax.ShapeDtypeStruct((t, RET_HEADS * RET_DV), BF16),
        scratch_shapes=[pltpu.VMEM((RET_DK, RET_DV), F32)],
        compiler_params=_params(("parallel", "parallel", "arbitrary")),
    )(q, k, v, gate, lg, gn_g, gn_b)


def _relayout_hy_w_in(w):
    d = w.shape[0]
    z = lambda n: jnp.zeros((d, n), w.dtype)
    c_rope = MLA_Q_RANK + MLA_KV_RANK
    c_gdn = c_rope + MLA_ROPE
    c_a = c_gdn + GDN_QKV
    c_gate = c_a + 2 * GDN_HEADS
    return _bf(jnp.concatenate([
        w[:, 0:c_rope], z(MLA_NOPE), w[:, c_rope:c_gdn], z(LANES - MLA_NOPE - MLA_ROPE),
        w[:, c_gdn:c_a], w[:, c_gate:], w[:, c_a:c_gate], z(LANES - 2 * GDN_HEADS)], axis=1))


def _relayout_w_uq(w):
    r = w.shape[0]
    wh = w.reshape(r, MLA_HEADS, MLA_NOPE + MLA_ROPE)
    pad = jnp.zeros((r, MLA_HEADS, LANES - MLA_NOPE - MLA_ROPE), w.dtype)
    return _bf(jnp.concatenate([wh, pad], axis=-1).reshape(r, MLA_HEADS * LANES))


def _relayout_w_ukv(w):
    r = w.shape[0]
    wh = w.reshape(r, MLA_HEADS, MLA_NOPE + MLA_V)
    pad = jnp.zeros((r, MLA_HEADS, LANES - MLA_NOPE), w.dtype)
    wk = jnp.concatenate([wh[:, :, :MLA_NOPE], pad], axis=-1).reshape(r, MLA_HEADS * LANES)
    wv = wh[:, :, MLA_NOPE:].reshape(r, MLA_HEADS * MLA_V).T
    return _bf(wk), _bf(wv)


def _rope_angles(positions, dim):
    inv_freq = ROPE_BASE ** (-jnp.arange(0, dim, 2, dtype=F32) / dim)
    ang = positions.astype(F32).reshape(-1)[:, None] * inv_freq
    return jnp.cos(ang), jnp.sin(ang)


def _mla_rope_tables(positions):
    cos, sin = _rope_angles(positions, MLA_ROPE)
    t = cos.shape[0]
    z = lambda n: jnp.zeros((t, n), F32)
    ct = jnp.concatenate([jnp.ones((t, MLA_NOPE), F32), cos, cos, z(32)], axis=1)
    s1 = jnp.concatenate([z(MLA_NOPE), -sin, z(48)], axis=1)
    s2 = jnp.concatenate([z(MLA_NOPE + 16), sin, z(32)], axis=1)
    return jnp.concatenate([ct, s1, s2], axis=1)


def _ret_rope_tables(positions):
    cos, sin = _rope_angles(positions, RET_DK)
    return jnp.concatenate([cos, cos, -sin, sin], axis=1)


def _router_weights(w_group, b_group, w_expert, b_expert):
    d = w_group.shape[0]
    pad = LANES - N_EXPERTS - N_GROUPS
    w = jnp.concatenate([w_expert, w_group, jnp.zeros((d, pad), F32)], axis=1)
    b = jnp.concatenate([b_expert, b_group, jnp.zeros((pad,), F32)]).reshape(1, LANES)
    hi = _bf(w)
    return hi, _bf(w - hi.astype(F32)), b


def _pad_lanes(v):
    return jnp.zeros((LANES,), F32).at[:v.shape[0]].set(v)


def kernel(x, c, positions, ada_w, ada_b, ln_mix_g, ln_mix_b, ln_ffn_g, ln_ffn_b, hy_w_in, mla_q_norm, mla_w_uq, mla_kv_norm, mla_w_ukv, gdn_conv_w, gdn_a_log, gdn_dt_bias, gdn_norm, hy_w_out, ret_w_in, ret_gn_g, ret_gn_b, ret_w_out, moe_w_group, moe_b_group, moe_w_expert, moe_b_expert, moe_w_gate, moe_w_up, moe_w_down):
    bsz, seq, d = x.shape
    t = bsz * seq
    mods = _ada_mod(c, ada_w, ada_b)
    mla_tabs = _mla_rope_tables(positions)
    ret_tabs = _ret_rope_tables(positions)
    log_gamma = jnp.log(1.0 - 2.0 ** (-5.0 - jnp.arange(RET_HEADS, dtype=F32)))
    lg = jnp.broadcast_to(log_gamma[:, None, None], (RET_HEADS, 1, LANES))
    xc = x.reshape(t, d)
    for layer in range(DEPTH):
        i = layer // 2
        mod_l = mods[layer]
        if layer % 2 == 0:
            wuk_p, wuv = _relayout_w_ukv(mla_w_ukv[i])
            gparam = jnp.stack([_pad_lanes(gdn_a_log[i]), _pad_lanes(gdn_dt_bias[i])])
            (qm, km, vm, gq, gk, gv, gate, gcol, grow) = _even_front(
                xc, mod_l, _relayout_hy_w_in(hy_w_in[i]), mla_q_norm[i].reshape(1, -1), _relayout_w_uq(mla_w_uq[i]),
                mla_kv_norm[i].reshape(1, -1), wuk_p, wuv, gdn_conv_w[i], gparam, mla_tabs, seq)
            o_mla = _attention(qm, km, vm, bsz, seq)
            n_ch = t // GDN_CHUNK
            growp = grow[:GDN_HEADS].reshape(GDN_PAIRS, 2, n_ch, GDN_CHUNK).transpose(2, 0, 1, 3).reshape(n_ch, GDN_PAIRS, LANES)
            onorm2 = jnp.concatenate([gdn_norm[i], gdn_norm[i]]).reshape(1, LANES)
            y_gdn = _gdn(gq, gk, gv, gate, gcol, growp, onorm2, bsz, seq)
            acts = [o_mla, y_gdn]
            w_out = _bf(hy_w_out[i])
        else:
            q, k, v, gate = _odd_front(xc, mod_l, _bf(ret_w_in[i]), ret_tabs, seq)
            og = _retention(q, k, v, gate, lg, ret_gn_g[i].reshape(RET_HEADS, 1, RET_DV),
                            ret_gn_b[i].reshape(RET_HEADS, 1, RET_DV), bsz, seq)
            acts = [og]
            w_out = _bf(ret_w_out[i])
        wr_hi, wr_lo, br = _router_weights(moe_w_group[layer], moe_b_group[layer], moe_w_expert[layer], moe_b_expert[layer])
        x1, h2, route, counts = _mixer_back(acts, w_out, xc, mod_l, ln_mix_g[layer].reshape(1, d), ln_mix_b[layer].reshape(1, d),
                                    wr_hi, wr_lo, br, seq)
        slot, block_expert, n_used, pad, n_slots = _slot_tables(route, counts, t)
        xb = _scatter_rows(slot, pad, h2, n_slots, seq)
        yb = _moe_experts(block_expert, n_used, xb, moe_w_gate[layer], moe_w_up[layer], moe_w_down[layer])
        xc = _moe_back(slot, x1, yb, route, mod_l, ln_ffn_g[layer].reshape(1, d), ln_ffn_b[layer].reshape(1, d), seq)
    return xc.reshape(bsz, seq, d)
```

```python
import functools

import jax
import jax.numpy as jnp
from jax import lax
from jax.experimental import pallas as pl
from jax.experimental.pallas import tpu as pltpu

F32 = jnp.float32
BF16 = jnp.bfloat16

DEPTH = 4
N_MOD = 6
LANES = 128

MLA_HEADS = 8
MLA_NOPE = 64
MLA_ROPE = 32
MLA_V = 64
MLA_Q_RANK = 256
MLA_KV_RANK = 128
MLA_SCALE = (MLA_NOPE + MLA_ROPE) ** -0.5
LOG2_E = 1.4426950408889634

GDN_HEADS = 8
GDN_DK = 64
GDN_DV = 64
GDN_CONV = 4
GDN_CHUNK = 64
GDN_HD = GDN_HEADS * GDN_DK
GDN_QKV = 3 * GDN_HD
GDN_PAIRS = GDN_HEADS // 2

RET_HEADS = 8
RET_DK = 128
RET_DV = 256
RET_CHUNK = 256

N_GROUPS = 4
EXPERTS_PER_GROUP = 8
N_EXPERTS = 32
EXPERT_FF = 512
MOE_BLOCK = 256

ROPE_BASE = 10000.0
LN_EPS = 1e-5
RMS_EPS = 1e-6
DEEPNORM_ALPHA = (2.0 * DEPTH) ** 0.25

ROW_TILE = 256
ATTN_TILE = 512
VMEM_LIMIT = 56 * 1024 * 1024

_C_MLA = 0
_C_QKV = 512
_C_GATE = 2048
_C_AB = 2560
_C_END = 2688


def _bf(x):
    return x.astype(BF16)


def _dot(a, b):
    return jnp.dot(a, b, preferred_element_type=F32)


def _dot_nt(a, b):
    return lax.dot_general(a, b, (((1,), (1,)), ((), ())), preferred_element_type=F32)


def _dot_tn(a, b):
    return lax.dot_general(a, b, (((0,), (0,)), ((), ())), preferred_element_type=F32)


def _split2(x):
    hi = _bf(x)
    return hi, _bf(x - hi.astype(F32))


def _split3(x):
    hi = _bf(x)
    r = x - hi.astype(F32)
    mid = _bf(r)
    return hi, mid, _bf(r - mid.astype(F32))


def _silu(x):
    return x * jax.nn.sigmoid(x)


def _layer_norm(z, g, b):
    mu = jnp.mean(z, -1, keepdims=True)
    zc = z - mu
    var = jnp.mean(zc * zc, -1, keepdims=True)
    return zc * lax.rsqrt(var + LN_EPS) * g + b


def _rms_norm(x, g):
    return x * lax.rsqrt(jnp.mean(x * x, -1, keepdims=True) + RMS_EPS) * g


def _params(sem):
    return pltpu.CompilerParams(dimension_semantics=sem, vmem_limit_bytes=VMEM_LIMIT)


def _const_spec(shape):
    nd = len(shape)
    return pl.BlockSpec(shape, lambda *_: (0,) * nd)


def _ada_kernel(c_ref, w_ref, b_ref, o_ref):
    c = c_ref[...]
    o_ref[...] = _dot(_bf(_silu(c)), _bf(w_ref[...])) + b_ref[...]


def _ada_mod(c, ada_w, ada_b):
    depth, d, n = ada_w.shape
    bsz = c.shape[0]
    rows = 16
    tn = 1536
    cp = jnp.zeros((rows, d), F32).at[:bsz].set(c)
    out = pl.pallas_call(
        _ada_kernel,
        grid=(depth, n // tn),
        in_specs=[pl.BlockSpec((rows, d), lambda l, j: (0, 0)),
                  pl.BlockSpec((None, d, tn), lambda l, j: (l, 0, j)),
                  pl.BlockSpec((None, 1, tn), lambda l, j: (l, 0, j))],
        out_specs=pl.BlockSpec((None, rows, tn), lambda l, j: (l, 0, j)),
        out_shape=jax.ShapeDtypeStruct((depth, rows, n), F32),
        compiler_params=_params(("parallel", "parallel")),
    )(cp, ada_w, ada_b.reshape(depth, 1, n))
    return out[:, :bsz].reshape(depth, bsz, N_MOD, d)


def _even_front_kernel(x_ref, mod_ref, w_ref, qn_ref, wuq_ref, kvn_ref, wuk_ref, wuv_ref,
                       cw_ref, gp_ref, tab_ref,
                       q_ref, k_ref, v_ref, gq_ref, gk_ref, gv_ref, gate_ref, gcol_ref, grow_ref,
                       xbuf, *, tm, tiles_per_seq):
    i = pl.program_id(0)
    hb = _bf(x_ref[...] * (1.0 + mod_ref[1:2, :]) + mod_ref[0:1, :])

    pm = _dot(hb, w_ref[:, _C_MLA:_C_QKV])
    ct = tab_ref[:, 0:LANES]
    s1 = tab_ref[:, LANES:2 * LANES]
    s2 = tab_ref[:, 2 * LANES:3 * LANES]

    def rope(b):
        return b * ct + pltpu.roll(b, LANES - 16, 1) * s1 + pltpu.roll(b, 16, 1) * s2

    qa = _dot(_bf(_rms_norm(pm[:, 0:MLA_Q_RANK], qn_ref[...])), wuq_ref[...])
    for h in range(MLA_HEADS):
        sl = slice(h * LANES, (h + 1) * LANES)
        q_ref[:, sl] = _bf(rope(qa[:, sl]) * (MLA_SCALE * LOG2_E))
    kvn = _bf(_rms_norm(pm[:, MLA_Q_RANK:MLA_Q_RANK + MLA_KV_RANK], kvn_ref[...]))
    ka = _dot(kvn, wuk_ref[...])
    kr = rope(pm[:, MLA_Q_RANK + MLA_KV_RANK:_C_QKV])
    for h in range(MLA_HEADS):
        sl = slice(h * LANES, (h + 1) * LANES)
        k_ref[:, sl] = _bf(ka[:, sl] + kr)
    v_ref[...] = _bf(_dot_nt(wuv_ref[...], kvn))

    @pl.when(i % tiles_per_seq == 0)
    def _():
        xbuf[0:8, :] = jnp.zeros((8, GDN_QKV), F32)

    xbuf[8:8 + tm, :] = _dot(hb, w_ref[:, _C_QKV:_C_GATE])
    cw = cw_ref[...]
    y = cw[0:1, :] * xbuf[5:5 + tm, :]
    for t in range(1, GDN_CONV):
        y = y + cw[t:t + 1, :] * xbuf[5 + t:5 + t + tm, :]
    xbuf[0:8, :] = xbuf[tm:tm + 8, :]
    y = _silu(y)

    r = lax.broadcasted_iota(jnp.int32, (GDN_HD, GDN_HD), 0) // GDN_DK
    c = lax.broadcasted_iota(jnp.int32, (GDN_HD, GDN_HD), 1) // GDN_DK
    head_ones = jnp.where(r == c, 1.0, 0.0).astype(BF16)

    def head_sumsq(z):
        hi, lo = _split2(z * z)
        return _dot(hi, head_ones) + _dot(lo, head_ones)

    qg = y[:, 0:GDN_HD]
    kg = y[:, GDN_HD:2 * GDN_HD]
    gq_ref[...] = qg * lax.rsqrt(head_sumsq(qg) + RMS_EPS) * (GDN_DK ** -0.5)
    gk_ref[...] = kg * lax.rsqrt(head_sumsq(kg) + RMS_EPS)
    gv_ref[...] = y[:, 2 * GDN_HD:3 * GDN_HD]
    gate_ref[...] = _dot(hb, w_ref[:, _C_GATE:_C_AB])

    ab = _dot(hb, w_ref[:, _C_AB:_C_END])
    lane = lax.broadcasted_iota(jnp.int32, (tm, LANES), 1)
    z = ab + gp_ref[1:2, :]
    softplus = jnp.maximum(z, 0.0) + jnp.log(1.0 + jnp.exp(-jnp.abs(z)))
    la = jnp.where(lane < GDN_HEADS, -jnp.exp(gp_ref[0:1, :]) * softplus, 0.0)
    rr = lax.broadcasted_iota(jnp.int32, (tm, tm), 0)
    cc = lax.broadcasted_iota(jnp.int32, (tm, tm), 1)
    tri = jnp.where((rr // GDN_CHUNK == cc // GDN_CHUNK) & (cc <= rr), 1.0, 0.0).astype(BF16)
    l0, l1, l2 = _split3(la)
    g = _dot(tri, l0) + _dot(tri, l1) + _dot(tri, l2)
    gcol = jnp.where(lane < GDN_HEADS, g, jax.nn.sigmoid(ab))
    gcol_ref[...] = gcol
    grow_ref[...] = gcol.T[0:16, :]


def _even_front(x2, mod_l, w_in_p, qn, wuq_p, kvn, wuk_p, wuv, conv_w, gparam, tabs, seq):
    t, d = x2.shape
    tm = min(ROW_TILE, seq)
    tiles_per_seq = seq // tm
    row = lambda n: pl.BlockSpec((tm, n), lambda i: (i, 0))
    out_shapes = (
        jax.ShapeDtypeStruct((t, MLA_HEADS * LANES), BF16),
        jax.ShapeDtypeStruct((t, MLA_HEADS * LANES), BF16),
        jax.ShapeDtypeStruct((MLA_HEADS * MLA_V, t), BF16),
        jax.ShapeDtypeStruct((t, GDN_HD), F32),
        jax.ShapeDtypeStruct((t, GDN_HD), F32),
        jax.ShapeDtypeStruct((t, GDN_HD), F32),
        jax.ShapeDtypeStruct((t, GDN_HD), F32),
        jax.ShapeDtypeStruct((t, LANES), F32),
        jax.ShapeDtypeStruct((16, t), F32),
    )
    return pl.pallas_call(
        functools.partial(_even_front_kernel, tm=tm, tiles_per_seq=tiles_per_seq),
        grid=(t // tm,),
        in_specs=[row(d),
                  pl.BlockSpec((None, N_MOD, d), lambda i: (i // tiles_per_seq, 0, 0)),
                  _const_spec(w_in_p.shape), _const_spec(qn.shape), _const_spec(wuq_p.shape),
                  _const_spec(kvn.shape), _const_spec(wuk_p.shape), _const_spec(wuv.shape),
                  _const_spec(conv_w.shape), _const_spec(gparam.shape), row(3 * LANES)],
        out_specs=(row(MLA_HEADS * LANES), row(MLA_HEADS * LANES),
                   pl.BlockSpec((MLA_HEADS * MLA_V, tm), lambda i: (0, i)),
                   row(GDN_HD), row(GDN_HD), row(GDN_HD), row(GDN_HD), row(LANES),
                   pl.BlockSpec((16, tm), lambda i: (0, i))),
        out_shape=out_shapes,
        scratch_shapes=[pltpu.VMEM((tm + 8, GDN_QKV), F32)],
        compiler_params=_params(("arbitrary",)),
    )(x2, mod_l, w_in_p, qn, wuq_p, kvn, wuk_p, wuv, conv_w, gparam, tabs)


def _attn_kernel(q_ref, k_ref, vt_ref, o_ref, m_ref, l_ref, acc_ref, *, tq):
    qi = pl.program_id(1)
    ki = pl.program_id(2)

    @pl.when(ki == 0)
    def _():
        m_ref[...] = jnp.full(m_ref.shape, -jnp.inf, F32)
        l_ref[...] = jnp.zeros(l_ref.shape, F32)
        acc_ref[...] = jnp.zeros(acc_ref.shape, F32)

    def scores(h):
        sl = slice(h * LANES, (h + 1) * LANES)
        return _dot_nt(k_ref[:, sl], q_ref[:, sl])

    def step(diagonal):
        if diagonal:
            key = lax.broadcasted_iota(jnp.int32, (tq, tq), 0)
            qry = lax.broadcasted_iota(jnp.int32, (tq, tq), 1)
            keep = key <= qry
        s_next = scores(0)
        for h in range(MLA_HEADS):
            s = s_next
            if h + 1 < MLA_HEADS:
                s_next = scores(h + 1)
            if diagonal:
                s = jnp.where(keep, s, -jnp.inf)
            rows = slice(h * MLA_V, (h + 1) * MLA_V)
            m_prev = m_ref[h:h + 1, :]
            m_new = jnp.maximum(m_prev, jnp.max(s, 0, keepdims=True))
            alpha = jnp.exp2(m_prev - m_new)
            p = jnp.exp2(s - m_new)
            l_ref[h:h + 1, :] = alpha * l_ref[h:h + 1, :] + jnp.sum(p, 0, keepdims=True)
            acc_ref[rows, :] = alpha * acc_ref[rows, :] + _dot(vt_ref[rows, :], _bf(p))
            m_ref[h:h + 1, :] = m_new

    @pl.when(ki < qi)
    def _():
        step(False)

    @pl.when(ki == qi)
    def _():
        step(True)
        for h in range(MLA_HEADS):
            rows = slice(h * MLA_V, (h + 1) * MLA_V)
            acc_ref[rows, :] = acc_ref[rows, :] / l_ref[h:h + 1, :]
        o_ref[...] = _bf(acc_ref[...].T)


def _attention(qm, km, vm, bsz, seq):
    t = qm.shape[0]
    tq = min(ATTN_TILE, seq)
    nq = seq // tq
    return pl.pallas_call(
        functools.partial(_attn_kernel, tq=tq),
        grid=(bsz, nq, nq),
        in_specs=[pl.BlockSpec((tq, MLA_HEADS * LANES), lambda b, qi, ki: (b * nq + qi, 0)),
                  pl.BlockSpec((tq, MLA_HEADS * LANES), lambda b, qi, ki: (b * nq + jnp.minimum(ki, qi), 0)),
                  pl.BlockSpec((MLA_HEADS * MLA_V, tq), lambda b, qi, ki: (0, b * nq + jnp.minimum(ki, qi)))],
        out_specs=pl.BlockSpec((tq, MLA_HEADS * MLA_V), lambda b, qi, ki: (b * nq + qi, 0)),
        out_shape=jax.ShapeDtypeStruct((t, MLA_HEADS * MLA_V), BF16),
        scratch_shapes=[pltpu.VMEM((MLA_HEADS, tq), F32),
                        pltpu.VMEM((MLA_HEADS, tq), F32),
                        pltpu.VMEM((MLA_HEADS * MLA_V, tq), F32)],
        compiler_params=_params(("parallel", "parallel", "arbitrary")),
    )(qm, km, vm)


def _gdn_kernel(q_ref, k_ref, v_ref, gate_ref, gcol_ref, grow_ref, on_ref, y_ref, state_ref, *, n_chunks):
    C = GDN_CHUNK

    @pl.when(pl.program_id(1) == 0)
    def _():
        state_ref[...] = jnp.zeros(state_ref.shape, F32)

    lane = lax.broadcasted_iota(jnp.int32, (C, LANES), 1)
    rowi = lax.broadcasted_iota(jnp.int32, (C, LANES), 0)
    first = lane < GDN_DK
    col = jnp.where(first, lane, lane - GDN_DK)
    tril = col <= rowi
    strict = col < rowi
    eye = jnp.where(col == rowi, 1.0, 0.0)
    blk16 = (col // 16) == (rowi // 16)
    blk32 = (col // 32) == (rowi // 32)
    r2 = lax.broadcasted_iota(jnp.int32, (LANES, LANES), 0)
    c2 = lax.broadcasted_iota(jnp.int32, (LANES, LANES), 1)
    first_rows = r2 < GDN_DK
    head_ones = jnp.where((r2 < GDN_DK) == (c2 < GDN_DK), 1.0, 0.0).astype(BF16)

    def bd(y):
        return jnp.concatenate([jnp.where(first, y, 0), jnp.where(first, 0, y)], axis=0)

    def pmm(x, y):
        return _dot(_bf(x), bd(_bf(y)))

    def pmm_hi(x, y):
        xh, xl = _split2(x)
        yh, yl = _split2(y)
        bh = bd(yh)
        return _dot(xh, bh) + _dot(xh, bd(yl)) + _dot(xl, bh)

    def pair_cols(a, c0, c1):
        return jnp.where(first, a[:, c0:c0 + 1], a[:, c1:c1 + 1])

    def chunk(ci, carry):
        rows = pl.ds(pl.multiple_of(ci * C, C), C)
        pairs = range(GDN_PAIRS)
        sls = [slice(p * LANES, (p + 1) * LANES) for p in pairs]
        gc = gcol_ref[rows, :]
        gr_all = grow_ref[ci]
        q = [q_ref[rows, sl] for sl in sls]
        k = [k_ref[rows, sl] for sl in sls]
        v = [v_ref[rows, sl] for sl in sls]
        g = [pair_cols(gc, 2 * p, 2 * p + 1) for p in pairs]
        beta = [pair_cols(gc, GDN_HEADS + 2 * p, GDN_HEADS + 2 * p + 1) for p in pairs]
        decay = [jnp.where(tril, jnp.exp(jnp.where(tril, g[p] - gr_all[p:p + 1, :], 0.0)), 0.0) for p in pairs]
        kb = [k[p] * beta[p] for p in pairs]
        kbd = [bd(_bf(k[p])) for p in pairs]
        lmat = [jnp.where(strict, _dot_nt(_bf(kb[p]), kbd[p]) * decay[p], 0.0) for p in pairs]
        a_intra = [jnp.where(tril, _dot_nt(_bf(q[p]), kbd[p]) * decay[p], 0.0) for p in pairs]
        n1 = [jnp.where(blk16, lmat[p], 0.0) for p in pairs]
        x = [eye - n1[p] for p in pairs]
        nk = n1
        for _ in range(3):
            nk = [pmm_hi(nk[p], nk[p]) for p in pairs]
            x = [x[p] + pmm_hi(x[p], nk[p]) for p in pairs]
        for off in ([jnp.where(blk32 & jnp.logical_not(blk16), lmat[p], 0.0) for p in pairs],
                    [jnp.where(blk32, 0.0, lmat[p]) for p in pairs]):
            xb = [pmm_hi(x[p], off[p]) for p in pairs]
            x = [x[p] - pmm_hi(xb[p], x[p]) for p in pairs]
        u = [pmm(x[p], v[p] * beta[p]) for p in pairs]
        w = [pmm(x[p], kb[p] * jnp.exp(g[p])) for p in pairs]
        state = [state_ref[p] for p in pairs]
        sb = [_bf(state[p]) for p in pairs]
        v_new = [u[p] - _dot(_bf(w[p]), sb[p]) for p in pairs]
        o = [_dot(_bf(q[p] * jnp.exp(g[p])), sb[p]) + pmm(a_intra[p], v_new[p]) for p in pairs]
        for p in pairs:
            g_last = g[p][C - 1:C, :]
            k_dec = k[p] * jnp.exp(g_last - g[p])
            scale = jnp.exp(jnp.where(first_rows, g_last[:, 0:1], g_last[:, GDN_DK:GDN_DK + 1]))
            state_ref[p] = state[p] * scale + _dot_tn(bd(_bf(k_dec)), bd(_bf(v_new[p])))
        for p in pairs:
            hi, lo = _split2(o[p] * o[p])
            ms = (_dot(hi, head_ones) + _dot(lo, head_ones)) * (1.0 / GDN_DV)
            on = o[p] * lax.rsqrt(ms + RMS_EPS) * on_ref[...]
            y_ref[rows, sls[p]] = _bf(on * _silu(gate_ref[rows, sls[p]]))
        return carry

    lax.fori_loop(0, n_chunks, chunk, 0)


def _gdn(gq, gk, gv, gate, gcol, growp, onorm2, bsz, seq):
    t = gq.shape[0]
    r = min(ROW_TILE, seq)
    nblk = seq // r
    n_chunks = r // GDN_CHUNK
    row = lambda n: pl.BlockSpec((r, n), lambda b, j: (b * nblk + j, 0))
    return pl.pallas_call(
        functools.partial(_gdn_kernel, n_chunks=n_chunks),
        grid=(bsz, nblk),
        in_specs=[row(GDN_HD), row(GDN_HD), row(GDN_HD), row(GDN_HD), row(LANES),
                  pl.BlockSpec((n_chunks, GDN_PAIRS, LANES), lambda b, j: (b * nblk + j, 0, 0)),
                  _const_spec(onorm2.shape)],
        out_specs=row(GDN_HD),
        out_shape=jax.ShapeDtypeStruct((t, GDN_HD), BF16),
        scratch_shapes=[pltpu.VMEM((GDN_PAIRS, LANES, LANES), F32)],
        compiler_params=_params(("parallel", "arbitrary")),
    )(gq, gk, gv, gate, gcol, growp, onorm2)


def _back_kernel(*refs, n_in):
    a_refs = refs[:n_in]
    (w_ref, x_ref, mod_ref, lng_ref, lnb_ref, wrh_ref, wrl_ref, br_ref,
     x1_ref, h2_ref, route_ref, counts_ref, base_ref) = refs[n_in:]
    off = 0
    y = None
    for a_ref in a_refs:
        kk = a_ref.shape[1]
        part = _dot(a_ref[...], w_ref[off:off + kk, :])
        y = part if y is None else y + part
        off += kk
    z = DEEPNORM_ALPHA * x_ref[...] + (1.0 + mod_ref[2:3, :]) * y
    x1 = _layer_norm(z, lng_ref[...], lnb_ref[...])
    x1_ref[...] = x1
    h2 = x1 * (1.0 + mod_ref[4:5, :]) + mod_ref[3:4, :]
    h2_ref[...] = h2

    hh, hl = _split2(h2)
    logits = _dot(hh, wrh_ref[...]) + _dot(hh, wrl_ref[...]) + _dot(hl, wrh_ref[...]) + br_ref[...]
    lane = lax.broadcasted_iota(jnp.int32, logits.shape, 1)
    lane_f = lane.astype(F32)
    neg = -jnp.inf
    gmask = (lane >= N_EXPERTS) & (lane < N_EXPERTS + N_GROUPS)
    gl = jnp.where(gmask, logits, neg)
    gmax = jnp.max(gl, -1, keepdims=True)
    g_top = 1.0 / jnp.sum(jnp.where(gmask, jnp.exp(gl - gmax), 0.0), -1, keepdims=True)
    g_lane = jnp.min(jnp.where(gl == gmax, lane_f, 1e9), -1, keepdims=True)
    g_idx = g_lane.astype(jnp.int32) - N_EXPERTS
    emask = (lane < N_EXPERTS) & ((lane // EXPERTS_PER_GROUP) == g_idx)
    el = jnp.where(emask, logits, neg)
    m1 = jnp.max(el, -1, keepdims=True)
    i1 = jnp.min(jnp.where(el == m1, lane_f, 1e9), -1, keepdims=True)
    el2 = jnp.where(lane_f == i1, neg, el)
    m2 = jnp.max(el2, -1, keepdims=True)
    i2 = jnp.min(jnp.where(el2 == m2, lane_f, 1e9), -1, keepdims=True)
    e2 = jnp.exp(m2 - m1)
    w1 = g_top / (1.0 + e2)
    w2 = g_top * e2 / (1.0 + e2)

    @pl.when(pl.program_id(0) == 0)
    def _():
        base_ref[...] = jnp.zeros(base_ref.shape, F32)

    tm = logits.shape[0]
    rr = lax.broadcasted_iota(jnp.int32, (tm, tm), 0)
    cc = lax.broadcasted_iota(jnp.int32, (tm, tm), 1)
    tri = jnp.where(cc <= rr, 1.0, 0.0).astype(BF16)
    oh1 = jnp.where(lane_f == i1, 1.0, 0.0)
    oh2 = jnp.where(lane_f == i2, 1.0, 0.0)
    cs1 = _dot(tri, _bf(oh1))
    cs2 = _dot(tri, _bf(oh2))
    base = base_ref[...]
    tot1 = cs1[tm - 1:tm, :]
    r1 = jnp.sum(oh1 * (base + cs1), -1, keepdims=True) - 1.0
    r2 = jnp.sum(oh2 * (base + tot1 + cs2), -1, keepdims=True) - 1.0
    total = base + tot1 + cs2[tm - 1:tm, :]
    base_ref[...] = total
    counts_ref[...] = total
    route = jnp.where(lane == 0, i1, jnp.where(lane == 1, i2, jnp.where(lane == 2, w1, jnp.where(lane == 3, w2, 0.0))))
    route_ref[...] = jnp.where(lane == 4, r1, jnp.where(lane == 5, r2, route))


def _mixer_back(acts, w_out, x2, mod_l, ln_g, ln_b, wr_hi, wr_lo, br, seq):
    t, d = x2.shape
    tm = min(ROW_TILE, seq)
    tiles_per_seq = seq // tm
    row = lambda n: pl.BlockSpec((tm, n), lambda i: (i, 0))
    return pl.pallas_call(
        functools.partial(_back_kernel, n_in=len(acts)),
        grid=(t // tm,),
        in_specs=[row(a.shape[1]) for a in acts] + [
            _const_spec(w_out.shape), row(d),
            pl.BlockSpec((None, N_MOD, d), lambda i: (i // tiles_per_seq, 0, 0)),
            _const_spec(ln_g.shape), _const_spec(ln_b.shape),
            _const_spec(wr_hi.shape), _const_spec(wr_lo.shape), _const_spec(br.shape)],
        out_specs=(row(d), row(d), row(LANES), _const_spec((1, LANES))),
        out_shape=(jax.ShapeDtypeStruct((t, d), F32), jax.ShapeDtypeStruct((t, d), F32),
                   jax.ShapeDtypeStruct((t, LANES), F32), jax.ShapeDtypeStruct((1, LANES), F32)),
        scratch_shapes=[pltpu.VMEM((1, LANES), F32)],
        compiler_params=_params(("arbitrary",)),
    )(*acts, w_out, x2, mod_l, ln_g, ln_b, wr_hi, wr_lo, br)


def _moe_kernel(be_ref, nb_ref, x_ref, wg_ref, wu_ref, wd_ref, y_ref, wg_s, wu_s, wd_s):
    i = pl.program_id(0)

    @pl.when(i < nb_ref[0])
    def _():
        prev = be_ref[jnp.maximum(i - 1, 0)]

        @pl.when((i == 0) | (be_ref[i] != prev))
        def _():
            wg_s[...] = _bf(wg_ref[...])
            wu_s[...] = _bf(wu_ref[...])
            wd_s[...] = _bf(wd_ref[...])

        x = _bf(x_ref[...])
        hid = _silu(_dot(x, wg_s[...])) * _dot(x, wu_s[...])
        y_ref[...] = _dot(_bf(hid), wd_s[...])

    @pl.when(i >= nb_ref[0])
    def _():
        y_ref[...] = jnp.zeros(y_ref.shape, F32)


def _moe_experts(block_expert, n_used, xb, w_gate, w_up, w_down, layer):
    n_slots, d = xb.shape
    ff = w_gate.shape[-1]
    n_blocks = n_slots // MOE_BLOCK
    grid_spec = pltpu.PrefetchScalarGridSpec(
        num_scalar_prefetch=2,
        grid=(n_blocks,),
        in_specs=[pl.BlockSpec((MOE_BLOCK, d), lambda i, be, nb: (jnp.minimum(i, nb[0] - 1), 0)),
                  pl.BlockSpec((None, None, d, ff), lambda i, be, nb: (layer, be[i], 0, 0)),
                  pl.BlockSpec((None, None, d, ff), lambda i, be, nb: (layer, be[i], 0, 0)),
                  pl.BlockSpec((None, None, ff, d), lambda i, be, nb: (layer, be[i], 0, 0))],
        out_specs=pl.BlockSpec((MOE_BLOCK, d), lambda i, be, nb: (i, 0)),
        scratch_shapes=[pltpu.VMEM((d, ff), BF16), pltpu.VMEM((d, ff), BF16), pltpu.VMEM((ff, d), BF16)],
    )
    return pl.pallas_call(
        _moe_kernel,
        grid_spec=grid_spec,
        out_shape=jax.ShapeDtypeStruct((n_slots, d), F32),
        compiler_params=_params(("arbitrary",)),
    )(block_expert, n_used, xb, w_gate, w_up, w_down)


def _slot_tables(route, counts_row, t):
    counts = counts_row[0, :N_EXPERTS].astype(jnp.int32)
    padded = (counts + MOE_BLOCK - 1) // MOE_BLOCK * MOE_BLOCK
    pend = jnp.cumsum(padded)
    pstart = pend - padded
    e = route[:, 0:2].astype(jnp.int32)
    rank = route[:, 4:6].astype(jnp.int32)
    slot = (jnp.take(pstart, e) + rank).reshape(2 * t)
    n_blocks = 2 * t // MOE_BLOCK + N_EXPERTS
    block_start = jnp.arange(n_blocks, dtype=jnp.int32) * MOE_BLOCK
    block_expert = jnp.minimum(jnp.sum((block_start[:, None] >= pend[None, :]).astype(jnp.int32), axis=1), N_EXPERTS - 1)
    n_used = (pend[-1] // MOE_BLOCK).astype(jnp.int32).reshape(1)
    n_slots = n_blocks * MOE_BLOCK
    tail = jnp.stack([pend[-1], (n_slots - pend[-1]) // (MOE_BLOCK // 2)])
    pad = jnp.concatenate([pstart + counts, padded - counts, tail]).astype(jnp.int32)
    return slot, block_expert, n_used, pad, n_slots


def _scatter_kernel(slot_ref, pad_ref, h_ref, xb_ref, stage_ref, zero_ref, sems, zsem, *, tm, n_steps):
    i = pl.program_id(0)

    def pad_copies(fn):
        def per_expert(e, carry):
            start = pad_ref[e]

            def body(j, c):
                fn(pltpu.make_async_copy(zero_ref.at[pl.ds(0, 1)], xb_ref.at[pl.ds(start + j, 1)], zsem))
                return c

            lax.fori_loop(0, pad_ref[N_EXPERTS + e], body, 0)
            return carry

        lax.fori_loop(0, N_EXPERTS, per_expert, 0)
        piece = zero_ref.shape[0]
        tail = pad_ref[2 * N_EXPERTS]

        def tail_body(j, c):
            pos = pl.multiple_of(tail + j * piece, piece)
            fn(pltpu.make_async_copy(zero_ref, xb_ref.at[pl.ds(pos, piece)], zsem))
            return c

        lax.fori_loop(0, pad_ref[2 * N_EXPERTS + 1], tail_body, 0)

    @pl.when(i == 0)
    def _():
        zero_ref[...] = jnp.zeros(zero_ref.shape, F32)
        pad_copies(lambda c: c.start())
        pad_copies(lambda c: c.wait())

    def start_rows(step):
        buf = step % 2

        def body(r, carry):
            tok = step * tm + r
            for choice in range(2):
                dst = slot_ref[2 * tok + choice]
                pltpu.make_async_copy(stage_ref.at[buf, pl.ds(r, 1)], xb_ref.at[pl.ds(dst, 1)], sems.at[buf]).start()
            return carry

        lax.fori_loop(0, tm, body, 0, unroll=8)

    def wait_rows(step):
        buf = step % 2
        for _ in range(2):
            pltpu.make_async_copy(stage_ref.at[buf], xb_ref.at[pl.ds(0, tm)], sems.at[buf]).wait()

    @pl.when(i >= 2)
    def _():
        wait_rows(i - 2)

    stage_ref[i % 2] = h_ref[...]
    start_rows(i)

    @pl.when(i == n_steps - 1)
    def _():
        @pl.when(i >= 1)
        def _():
            wait_rows(i - 1)

        wait_rows(i)


def _scatter_rows(slot, pad, h2, n_slots, seq):
    t, d = h2.shape
    tm = min(ROW_TILE, seq)
    n_steps = t // tm
    grid_spec = pltpu.PrefetchScalarGridSpec(
        num_scalar_prefetch=2,
        grid=(n_steps,),
        in_specs=[pl.BlockSpec((tm, d), lambda i, sl, pd: (i, 0))],
        out_specs=pl.BlockSpec(memory_space=pl.ANY),
        scratch_shapes=[pltpu.VMEM((2, tm, d), F32), pltpu.VMEM((MOE_BLOCK // 2, d), F32),
                        pltpu.SemaphoreType.DMA((2,)), pltpu.SemaphoreType.DMA(())],
    )
    return pl.pallas_call(
        functools.partial(_scatter_kernel, tm=tm, n_steps=n_steps),
        grid_spec=grid_spec,
        out_shape=jax.ShapeDtypeStruct((n_slots, d), F32),
        compiler_params=_params(("arbitrary",)),
    )(slot, pad, h2)


def _moe_back_kernel(slot_ref, x_ref, route_ref, mod_ref, lng_ref, lnb_ref, yb_ref, o_ref, ybuf, sems,
                     *, tm, n_steps):
    i = pl.program_id(0)

    def start_rows(step):
        buf = step % 2

        def body(r, carry):
            tok = step * tm + r
            for choice in range(2):
                src = slot_ref[2 * tok + choice]
                pltpu.make_async_copy(yb_ref.at[pl.ds(src, 1)], ybuf.at[buf, choice, pl.ds(r, 1)], sems.at[buf]).start()
            return carry

        lax.fori_loop(0, tm, body, 0, unroll=8)

    @pl.when(i == 0)
    def _():
        start_rows(0)

    @pl.when(i + 1 < n_steps)
    def _():
        start_rows(i + 1)

    buf = i % 2
    for choice in range(2):
        pltpu.make_async_copy(yb_ref.at[pl.ds(0, tm)], ybuf.at[buf, choice], sems.at[buf]).wait()
    rt = route_ref[...]
    y = ybuf[buf, 0] * rt[:, 2:3] + ybuf[buf, 1] * rt[:, 3:4]
    z = DEEPNORM_ALPHA * x_ref[...] + (1.0 + mod_ref[5:6, :]) * y
    o_ref[...] = _layer_norm(z, lng_ref[...], lnb_ref[...])


def _moe_back(slot, x1, yb, route, mod_l, ln_g, ln_b, seq):
    t, d = x1.shape
    tm = min(ROW_TILE, seq)
    tiles_per_seq = seq // tm
    n_steps = t // tm
    row = lambda n: pl.BlockSpec((tm, n), lambda i, sl: (i, 0))
    const = lambda shape: pl.BlockSpec(shape, lambda i, sl: (0,) * len(shape))
    grid_spec = pltpu.PrefetchScalarGridSpec(
        num_scalar_prefetch=1,
        grid=(n_steps,),
        in_specs=[row(d), row(LANES),
                  pl.BlockSpec((None, N_MOD, d), lambda i, sl: (i // tiles_per_seq, 0, 0)),
                  const(ln_g.shape), const(ln_b.shape), pl.BlockSpec(memory_space=pl.ANY)],
        out_specs=row(d),
        scratch_shapes=[pltpu.VMEM((2, 2, tm, d), F32), pltpu.SemaphoreType.DMA((2,))],
    )
    return pl.pallas_call(
        functools.partial(_moe_back_kernel, tm=tm, n_steps=n_steps),
        grid_spec=grid_spec,
        out_shape=jax.ShapeDtypeStruct((t, d), F32),
        compiler_params=_params(("arbitrary",)),
    )(slot, x1, route, mod_l, ln_g, ln_b, yb)


def _odd_front_kernel(x_ref, mod_ref, w_ref, tab_ref, q_ref, k_ref, v_ref, gate_ref):
    hb = _bf(x_ref[...] * (1.0 + mod_ref[1:2, :]) + mod_ref[0:1, :])
    c2 = tab_ref[:, 0:LANES]
    s2 = tab_ref[:, LANES:2 * LANES]
    nqk = RET_HEADS * RET_DK
    nv = RET_HEADS * RET_DV

    def rope(b):
        return b * c2 + pltpu.roll(b, RET_DK // 2, 1) * s2

    qa = _dot(hb, w_ref[:, 0:nqk])
    ka = _dot(hb, w_ref[:, nqk:2 * nqk])
    for h in range(RET_HEADS):
        sl = slice(h * LANES, (h + 1) * LANES)
        q_ref[:, sl] = _bf(rope(qa[:, sl]) * (RET_DK ** -0.5))
        k_ref[:, sl] = _bf(rope(ka[:, sl]))
    v_ref[...] = _bf(_dot(hb, w_ref[:, 2 * nqk:2 * nqk + nv]))
    gate_ref[...] = _silu(_dot(hb, w_ref[:, 2 * nqk + nv:2 * nqk + 2 * nv]))


def _odd_front(x2, mod_l, w_in, tabs, seq):
    t, d = x2.shape
    tm = min(ROW_TILE, seq)
    tiles_per_seq = seq // tm
    nqk = RET_HEADS * RET_DK
    nv = RET_HEADS * RET_DV
    row = lambda n: pl.BlockSpec((tm, n), lambda i: (i, 0))
    return pl.pallas_call(
        _odd_front_kernel,
        grid=(t // tm,),
        in_specs=[row(d), pl.BlockSpec((None, N_MOD, d), lambda i: (i // tiles_per_seq, 0, 0)),
                  _const_spec(w_in.shape), row(2 * LANES)],
        out_specs=(row(nqk), row(nqk), row(nv), row(nv)),
        out_shape=(jax.ShapeDtypeStruct((t, nqk), BF16), jax.ShapeDtypeStruct((t, nqk), BF16),
                   jax.ShapeDtypeStruct((t, nv), BF16), jax.ShapeDtypeStruct((t, nv), F32)),
        compiler_params=_params(("parallel",)),
    )(x2, mod_l, w_in, tabs)


def _ret_kernel(q_ref, k_ref, v_ref, gate_ref, lg_ref, gng_ref, gnb_ref, o_ref, state_ref, *, C):
    @pl.when(pl.program_id(2) == 0)
    def _():
        state_ref[...] = jnp.zeros(state_ref.shape, F32)

    lg = lg_ref[:, 0:1]
    ri = lax.broadcasted_iota(jnp.int32, (C, C), 0)
    ci = lax.broadcasted_iota(jnp.int32, (C, C), 1)
    diff = (ri - ci).astype(F32)
    causal = ri >= ci
    decay = jnp.where(causal, jnp.exp(jnp.where(causal, diff, 0.0) * lg), 0.0)
    idx = lax.broadcasted_iota(jnp.int32, (C, 1), 0).astype(F32)
    q = q_ref[...]
    k = k_ref[...]
    v = v_ref[...]
    state = state_ref[...]
    scores = _dot_nt(q, k) * decay
    o = _dot(_bf(scores), v) + _dot(q, _bf(state)) * jnp.exp((idx + 1.0) * lg)
    kd = _bf(k.astype(F32) * jnp.exp((C - 1.0 - idx) * lg))
    state_ref[...] = state * jnp.exp(C * lg) + _dot_tn(kd, v)
    o_ref[...] = _bf(gate_ref[...] * _layer_norm(o, gng_ref[...], gnb_ref[...]))


def _retention(q, k, v, gate, lg, gn_g, gn_b, bsz, seq):
    t = q.shape[0]
    C = min(RET_CHUNK, seq)
    nc = seq // C
    blk = lambda n: pl.BlockSpec((C, n), lambda b, h, j: (b * nc + j, h))
    hrow = lambda n: pl.BlockSpec((None, 1, n), lambda b, h, j: (h, 0, 0))
    return pl.pallas_call(
        functools.partial(_ret_kernel, C=C),
        grid=(bsz, RET_HEADS, nc),
        in_specs=[blk(RET_DK), blk(RET_DK), blk(RET_DV), blk(RET_DV), hrow(LANES), hrow(RET_DV), hrow(RET_DV)],
        out_specs=blk(RET_DV),
        out_shape=jax.ShapeDtypeStruct((t, RET_HEADS * RET_DV), BF16),
        scratch_shapes=[pltpu.VMEM((RET_DK, RET_DV), F32)],
        compiler_params=_params(("parallel", "parallel", "arbitrary")),
    )(q, k, v, gate, lg, gn_g, gn_b)


def _relayout_hy_w_in(w):
    d = w.shape[0]
    z = lambda n: jnp.zeros((d, n), w.dtype)
    c_rope = MLA_Q_RANK + MLA_KV_RANK
    c_gdn = c_rope + MLA_ROPE
    c_a = c_gdn + GDN_QKV
    c_gate = c_a + 2 * GDN_HEADS
    return _bf(jnp.concatenate([
        w[:, 0:c_rope], z(MLA_NOPE), w[:, c_rope:c_gdn], z(LANES - MLA_NOPE - MLA_ROPE),
        w[:, c_gdn:c_a], w[:, c_gate:], w[:, c_a:c_gate], z(LANES - 2 * GDN_HEADS)], axis=1))


def _relayout_w_uq(w):
    r = w.shape[0]
    wh = w.reshape(r, MLA_HEADS, MLA_NOPE + MLA_ROPE)
    pad = jnp.zeros((r, MLA_HEADS, LANES - MLA_NOPE - MLA_ROPE), w.dtype)
    return _bf(jnp.concatenate([wh, pad], axis=-1).reshape(r, MLA_HEADS * LANES))


def _relayout_w_ukv(w):
    r = w.shape[0]
    wh = w.reshape(r, MLA_HEADS, MLA_NOPE + MLA_V)
    pad = jnp.zeros((r, MLA_HEADS, LANES - MLA_NOPE), w.dtype)
    wk = jnp.concatenate([wh[:, :, :MLA_NOPE], pad], axis=-1).reshape(r, MLA_HEADS * LANES)
    wv = wh[:, :, MLA_NOPE:].reshape(r, MLA_HEADS * MLA_V).T
    return _bf(wk), _bf(wv)


def _rope_angles(positions, dim):
    inv_freq = ROPE_BASE ** (-jnp.arange(0, dim, 2, dtype=F32) / dim)
    ang = positions.astype(F32).reshape(-1)[:, None] * inv_freq
    return jnp.cos(ang), jnp.sin(ang)


def _mla_rope_tables(positions):
    cos, sin = _rope_angles(positions, MLA_ROPE)
    t = cos.shape[0]
    z = lambda n: jnp.zeros((t, n), F32)
    ct = jnp.concatenate([jnp.ones((t, MLA_NOPE), F32), cos, cos, z(32)], axis=1)
    s1 = jnp.concatenate([z(MLA_NOPE), -sin, z(48)], axis=1)
    s2 = jnp.concatenate([z(MLA_NOPE + 16), sin, z(32)], axis=1)
    return jnp.concatenate([ct, s1, s2], axis=1)


def _ret_rope_tables(positions):
    cos, sin = _rope_angles(positions, RET_DK)
    return jnp.concatenate([cos, cos, -sin, sin], axis=1)


def _router_weights(w_group, b_group, w_expert, b_expert):
    d = w_group.shape[0]
    pad = LANES - N_EXPERTS - N_GROUPS
    w = jnp.concatenate([w_expert, w_group, jnp.zeros((d, pad), F32)], axis=1)
    b = jnp.concatenate([b_expert, b_group, jnp.zeros((pad,), F32)]).reshape(1, LANES)
    hi = _bf(w)
    return hi, _bf(w - hi.astype(F32)), b


def _pad_lanes(v):
    return jnp.zeros((LANES,), F32).at[:v.shape[0]].set(v)


def kernel(x, c, positions, ada_w, ada_b, ln_mix_g, ln_mix_b, ln_ffn_g, ln_ffn_b, hy_w_in, mla_q_norm, mla_w_uq, mla_kv_norm, mla_w_ukv, gdn_conv_w, gdn_a_log, gdn_dt_bias, gdn_norm, hy_w_out, ret_w_in, ret_gn_g, ret_gn_b, ret_w_out, moe_w_group, moe_b_group, moe_w_expert, moe_b_expert, moe_w_gate, moe_w_up, moe_w_down):
    bsz, seq, d = x.shape
    t = bsz * seq
    mods = _ada_mod(c, ada_w, ada_b)
    mla_tabs = _mla_rope_tables(positions)
    ret_tabs = _ret_rope_tables(positions)
    log_gamma = jnp.log(1.0 - 2.0 ** (-5.0 - jnp.arange(RET_HEADS, dtype=F32)))
    lg = jnp.broadcast_to(log_gamma[:, None, None], (RET_HEADS, 1, LANES))
    xc = x.reshape(t, d)
    for layer in range(DEPTH):
        i = layer // 2
        mod_l = mods[layer]
        if layer % 2 == 0:
            wuk_p, wuv = _relayout_w_ukv(mla_w_ukv[i])
            gparam = jnp.stack([_pad_lanes(gdn_a_log[i]), _pad_lanes(gdn_dt_bias[i])])
            (qm, km, vm, gq, gk, gv, gate, gcol, grow) = _even_front(
                xc, mod_l, _relayout_hy_w_in(hy_w_in[i]), mla_q_norm[i].reshape(1, -1), _relayout_w_uq(mla_w_uq[i]),
                mla_kv_norm[i].reshape(1, -1), wuk_p, wuv, gdn_conv_w[i], gparam, mla_tabs, seq)
            o_mla = _attention(qm, km, vm, bsz, seq)
            n_ch = t // GDN_CHUNK
            growp = grow[:GDN_HEADS].reshape(GDN_PAIRS, 2, n_ch, GDN_CHUNK).transpose(2, 0, 1, 3).reshape(n_ch, GDN_PAIRS, LANES)
            onorm2 = jnp.concatenate([gdn_norm[i], gdn_norm[i]]).reshape(1, LANES)
            y_gdn = _gdn(gq, gk, gv, gate, gcol, growp, onorm2, bsz, seq)
            acts = [o_mla, y_gdn]
            w_out = _bf(hy_w_out[i])
        else:
            q, k, v, gate = _odd_front(xc, mod_l, _bf(ret_w_in[i]), ret_tabs, seq)
            og = _retention(q, k, v, gate, lg, ret_gn_g[i].reshape(RET_HEADS, 1, RET_DV),
                            ret_gn_b[i].reshape(RET_HEADS, 1, RET_DV), bsz, seq)
            acts = [og]
            w_out = _bf(ret_w_out[i])
        wr_hi, wr_lo, br = _router_weights(moe_w_group[layer], moe_b_group[layer], moe_w_expert[layer], moe_b_expert[layer])
        x1, h2, route, counts = _mixer_back(acts, w_out, xc, mod_l, ln_mix_g[layer].reshape(1, d), ln_mix_b[layer].reshape(1, d),
                                    wr_hi, wr_lo, br, seq)
        slot, block_expert, n_used, pad, n_slots = _slot_tables(route, counts, t)
        xb = _scatter_rows(slot, pad, h2, n_slots, seq)
        yb = _moe_experts(block_expert, n_used, xb, moe_w_gate, moe_w_up, moe_w_down, layer)
        xc = _moe_back(slot, x1, yb, route, mod_l, ln_ffn_g[layer].reshape(1, d), ln_ffn_b[layer].reshape(1, d), seq)
    return xc.reshape(bsz, seq, d)
```

```python
import functools

import jax
import jax.numpy as jnp
from jax import lax
from jax.experimental import pallas as pl
from jax.experimental.pallas import tpu as pltpu

F32 = jnp.float32
BF16 = jnp.bfloat16

DEPTH = 4
N_MOD = 6
LANES = 128

MLA_HEADS = 8
MLA_NOPE = 64
MLA_ROPE = 32
MLA_V = 64
MLA_Q_RANK = 256
MLA_KV_RANK = 128
MLA_SCALE = (MLA_NOPE + MLA_ROPE) ** -0.5
LOG2_E = 1.4426950408889634

GDN_HEADS = 8
GDN_DK = 64
GDN_DV = 64
GDN_CONV = 4
GDN_CHUNK = 64
GDN_HD = GDN_HEADS * GDN_DK
GDN_QKV = 3 * GDN_HD
GDN_PAIRS = GDN_HEADS // 2

RET_HEADS = 8
RET_DK = 128
RET_DV = 256
RET_CHUNK = 256

N_GROUPS = 4
EXPERTS_PER_GROUP = 8
N_EXPERTS = 32
EXPERT_FF = 512
MOE_BLOCK = 512

ROPE_BASE = 10000.0
LN_EPS = 1e-5
RMS_EPS = 1e-6
DEEPNORM_ALPHA = (2.0 * DEPTH) ** 0.25

ROW_TILE = 256
ATTN_TILE = 512
VMEM_LIMIT = 56 * 1024 * 1024

_C_MLA = 0
_C_QKV = 512
_C_GATE = 2048
_C_AB = 2560
_C_END = 2688


def _bf(x):
    return x.astype(BF16)


def _dot(a, b):
    return jnp.dot(a, b, preferred_element_type=F32)


def _dot_nt(a, b):
    return lax.dot_general(a, b, (((1,), (1,)), ((), ())), preferred_element_type=F32)


def _dot_tn(a, b):
    return lax.dot_general(a, b, (((0,), (0,)), ((), ())), preferred_element_type=F32)


def _split2(x):
    hi = _bf(x)
    return hi, _bf(x - hi.astype(F32))


def _split3(x):
    hi = _bf(x)
    r = x - hi.astype(F32)
    mid = _bf(r)
    return hi, mid, _bf(r - mid.astype(F32))


def _silu(x):
    return x * jax.nn.sigmoid(x)


def _layer_norm(z, g, b):
    mu = jnp.mean(z, -1, keepdims=True)
    zc = z - mu
    var = jnp.mean(zc * zc, -1, keepdims=True)
    return zc * lax.rsqrt(var + LN_EPS) * g + b


def _rms_norm(x, g):
    return x * lax.rsqrt(jnp.mean(x * x, -1, keepdims=True) + RMS_EPS) * g


def _params(sem):
    return pltpu.CompilerParams(dimension_semantics=sem, vmem_limit_bytes=VMEM_LIMIT)


def _const_spec(shape):
    nd = len(shape)
    return pl.BlockSpec(shape, lambda *_: (0,) * nd)


def _ada_kernel(c_ref, w_ref, b_ref, o_ref):
    c = c_ref[...]
    o_ref[...] = _dot(_bf(_silu(c)), _bf(w_ref[...])) + b_ref[...]


def _ada_mod(c, ada_w, ada_b):
    depth, d, n = ada_w.shape
    bsz = c.shape[0]
    rows = 16
    tn = 1536
    cp = jnp.zeros((rows, d), F32).at[:bsz].set(c)
    out = pl.pallas_call(
        _ada_kernel,
        grid=(depth, n // tn),
        in_specs=[pl.BlockSpec((rows, d), lambda l, j: (0, 0)),
                  pl.BlockSpec((None, d, tn), lambda l, j: (l, 0, j)),
                  pl.BlockSpec((None, 1, tn), lambda l, j: (l, 0, j))],
        out_specs=pl.BlockSpec((None, rows, tn), lambda l, j: (l, 0, j)),
        out_shape=jax.ShapeDtypeStruct((depth, rows, n), F32),
        compiler_params=_params(("parallel", "parallel")),
    )(cp, ada_w, ada_b.reshape(depth, 1, n))
    return out[:, :bsz].reshape(depth, bsz, N_MOD, d)


def _even_front_kernel(x_ref, mod_ref, w_ref, qn_ref, wuq_ref, kvn_ref, wuk_ref, wuv_ref,
                       cw_ref, gp_ref, tab_ref,
                       q_ref, k_ref, v_ref, gq_ref, gk_ref, gv_ref, gate_ref, gcol_ref, grow_ref,
                       xbuf, *, tm, tiles_per_seq):
    i = pl.program_id(0)
    hb = _bf(x_ref[...] * (1.0 + mod_ref[1:2, :]) + mod_ref[0:1, :])

    pm = _dot(hb, w_ref[:, _C_MLA:_C_QKV])
    ct = tab_ref[:, 0:LANES]
    s1 = tab_ref[:, LANES:2 * LANES]
    s2 = tab_ref[:, 2 * LANES:3 * LANES]

    def rope(b):
        return b * ct + pltpu.roll(b, LANES - 16, 1) * s1 + pltpu.roll(b, 16, 1) * s2

    qa = _dot(_bf(_rms_norm(pm[:, 0:MLA_Q_RANK], qn_ref[...])), wuq_ref[...])
    for h in range(MLA_HEADS):
        sl = slice(h * LANES, (h + 1) * LANES)
        q_ref[:, sl] = _bf(rope(qa[:, sl]) * (MLA_SCALE * LOG2_E))
    kvn = _bf(_rms_norm(pm[:, MLA_Q_RANK:MLA_Q_RANK + MLA_KV_RANK], kvn_ref[...]))
    ka = _dot(kvn, wuk_ref[...])
    kr = rope(pm[:, MLA_Q_RANK + MLA_KV_RANK:_C_QKV])
    for h in range(MLA_HEADS):
        sl = slice(h * LANES, (h + 1) * LANES)
        k_ref[:, sl] = _bf(ka[:, sl] + kr)
    v_ref[...] = _bf(_dot_nt(wuv_ref[...], kvn))

    @pl.when(i % tiles_per_seq == 0)
    def _():
        xbuf[0:8, :] = jnp.zeros((8, GDN_QKV), F32)

    xbuf[8:8 + tm, :] = _dot(hb, w_ref[:, _C_QKV:_C_GATE])
    cw = cw_ref[...]
    y = cw[0:1, :] * xbuf[5:5 + tm, :]
    for t in range(1, GDN_CONV):
        y = y + cw[t:t + 1, :] * xbuf[5 + t:5 + t + tm, :]
    xbuf[0:8, :] = xbuf[tm:tm + 8, :]
    y = _silu(y)

    r = lax.broadcasted_iota(jnp.int32, (GDN_HD, GDN_HD), 0) // GDN_DK
    c = lax.broadcasted_iota(jnp.int32, (GDN_HD, GDN_HD), 1) // GDN_DK
    head_ones = jnp.where(r == c, 1.0, 0.0).astype(BF16)

    def head_sumsq(z):
        hi, lo = _split2(z * z)
        return _dot(hi, head_ones) + _dot(lo, head_ones)

    qg = y[:, 0:GDN_HD]
    kg = y[:, GDN_HD:2 * GDN_HD]
    gq_ref[...] = qg * lax.rsqrt(head_sumsq(qg) + RMS_EPS) * (GDN_DK ** -0.5)
    gk_ref[...] = kg * lax.rsqrt(head_sumsq(kg) + RMS_EPS)
    gv_ref[...] = y[:, 2 * GDN_HD:3 * GDN_HD]
    gate_ref[...] = _dot(hb, w_ref[:, _C_GATE:_C_AB])

    ab = _dot(hb, w_ref[:, _C_AB:_C_END])
    lane = lax.broadcasted_iota(jnp.int32, (tm, LANES), 1)
    z = ab + gp_ref[1:2, :]
    softplus = jnp.maximum(z, 0.0) + jnp.log(1.0 + jnp.exp(-jnp.abs(z)))
    la = jnp.where(lane < GDN_HEADS, -jnp.exp(gp_ref[0:1, :]) * softplus, 0.0)
    rr = lax.broadcasted_iota(jnp.int32, (tm, tm), 0)
    cc = lax.broadcasted_iota(jnp.int32, (tm, tm), 1)
    tri = jnp.where((rr // GDN_CHUNK == cc // GDN_CHUNK) & (cc <= rr), 1.0, 0.0).astype(BF16)
    l0, l1, l2 = _split3(la)
    g = _dot(tri, l0) + _dot(tri, l1) + _dot(tri, l2)
    gcol = jnp.where(lane < GDN_HEADS, g, jax.nn.sigmoid(ab))
    gcol_ref[...] = gcol
    grow_ref[...] = gcol.T[0:16, :]


def _even_front(x2, mod_l, w_in_p, qn, wuq_p, kvn, wuk_p, wuv, conv_w, gparam, tabs, seq):
    t, d = x2.shape
    tm = min(ROW_TILE, seq)
    tiles_per_seq = seq // tm
    row = lambda n: pl.BlockSpec((tm, n), lambda i: (i, 0))
    out_shapes = (
        jax.ShapeDtypeStruct((t, MLA_HEADS * LANES), BF16),
        jax.ShapeDtypeStruct((t, MLA_HEADS * LANES), BF16),
        jax.ShapeDtypeStruct((MLA_HEADS * MLA_V, t), BF16),
        jax.ShapeDtypeStruct((t, GDN_HD), F32),
        jax.ShapeDtypeStruct((t, GDN_HD), F32),
        jax.ShapeDtypeStruct((t, GDN_HD), F32),
        jax.ShapeDtypeStruct((t, GDN_HD), F32),
        jax.ShapeDtypeStruct((t, LANES), F32),
        jax.ShapeDtypeStruct((16, t), F32),
    )
    return pl.pallas_call(
        functools.partial(_even_front_kernel, tm=tm, tiles_per_seq=tiles_per_seq),
        grid=(t // tm,),
        in_specs=[row(d),
                  pl.BlockSpec((None, N_MOD, d), lambda i: (i // tiles_per_seq, 0, 0)),
                  _const_spec(w_in_p.shape), _const_spec(qn.shape), _const_spec(wuq_p.shape),
                  _const_spec(kvn.shape), _const_spec(wuk_p.shape), _const_spec(wuv.shape),
                  _const_spec(conv_w.shape), _const_spec(gparam.shape), row(3 * LANES)],
        out_specs=(row(MLA_HEADS * LANES), row(MLA_HEADS * LANES),
                   pl.BlockSpec((MLA_HEADS * MLA_V, tm), lambda i: (0, i)),
                   row(GDN_HD), row(GDN_HD), row(GDN_HD), row(GDN_HD), row(LANES),
                   pl.BlockSpec((16, tm), lambda i: (0, i))),
        out_shape=out_shapes,
        scratch_shapes=[pltpu.VMEM((tm + 8, GDN_QKV), F32)],
        compiler_params=_params(("arbitrary",)),
    )(x2, mod_l, w_in_p, qn, wuq_p, kvn, wuk_p, wuv, conv_w, gparam, tabs)


def _attn_kernel(q_ref, k_ref, vt_ref, o_ref, m_ref, l_ref, acc_ref, *, tq):
    qi = pl.program_id(1)
    ki = pl.program_id(2)

    @pl.when(ki == 0)
    def _():
        m_ref[...] = jnp.full(m_ref.shape, -jnp.inf, F32)
        l_ref[...] = jnp.zeros(l_ref.shape, F32)
        acc_ref[...] = jnp.zeros(acc_ref.shape, F32)

    def scores(h):
        sl = slice(h * LANES, (h + 1) * LANES)
        return _dot_nt(k_ref[:, sl], q_ref[:, sl])

    def step(diagonal):
        if diagonal:
            key = lax.broadcasted_iota(jnp.int32, (tq, tq), 0)
            qry = lax.broadcasted_iota(jnp.int32, (tq, tq), 1)
            keep = key <= qry
        s_next = scores(0)
        for h in range(MLA_HEADS):
            s = s_next
            if h + 1 < MLA_HEADS:
                s_next = scores(h + 1)
            if diagonal:
                s = jnp.where(keep, s, -jnp.inf)
            rows = slice(h * MLA_V, (h + 1) * MLA_V)
            m_prev = m_ref[h:h + 1, :]
            m_new = jnp.maximum(m_prev, jnp.max(s, 0, keepdims=True))
            alpha = jnp.exp2(m_prev - m_new)
            p = jnp.exp2(s - m_new)
            l_ref[h:h + 1, :] = alpha * l_ref[h:h + 1, :] + jnp.sum(p, 0, keepdims=True)
            acc_ref[rows, :] = alpha * acc_ref[rows, :] + _dot(vt_ref[rows, :], _bf(p))
            m_ref[h:h + 1, :] = m_new

    @pl.when(ki < qi)
    def _():
        step(False)

    @pl.when(ki == qi)
    def _():
        step(True)
        for h in range(MLA_HEADS):
            rows = slice(h * MLA_V, (h + 1) * MLA_V)
            acc_ref[rows, :] = acc_ref[rows, :] / l_ref[h:h + 1, :]
        o_ref[...] = _bf(acc_ref[...].T)


def _attention(qm, km, vm, bsz, seq):
    t = qm.shape[0]
    tq = min(ATTN_TILE, seq)
    nq = seq // tq
    return pl.pallas_call(
        functools.partial(_attn_kernel, tq=tq),
        grid=(bsz, nq, nq),
        in_specs=[pl.BlockSpec((tq, MLA_HEADS * LANES), lambda b, qi, ki: (b * nq + qi, 0)),
                  pl.BlockSpec((tq, MLA_HEADS * LANES), lambda b, qi, ki: (b * nq + jnp.minimum(ki, qi), 0)),
                  pl.BlockSpec((MLA_HEADS * MLA_V, tq), lambda b, qi, ki: (0, b * nq + jnp.minimum(ki, qi)))],
        out_specs=pl.BlockSpec((tq, MLA_HEADS * MLA_V), lambda b, qi, ki: (b * nq + qi, 0)),
        out_shape=jax.ShapeDtypeStruct((t, MLA_HEADS * MLA_V), BF16),
        scratch_shapes=[pltpu.VMEM((MLA_HEADS, tq), F32),
                        pltpu.VMEM((MLA_HEADS, tq), F32),
                        pltpu.VMEM((MLA_HEADS * MLA_V, tq), F32)],
        compiler_params=_params(("parallel", "parallel", "arbitrary")),
    )(qm, km, vm)


def _gdn_kernel(q_ref, k_ref, v_ref, gate_ref, gcol_ref, grow_ref, on_ref, y_ref, state_ref, *, n_chunks):
    C = GDN_CHUNK

    @pl.when(pl.program_id(1) == 0)
    def _():
        state_ref[...] = jnp.zeros(state_ref.shape, F32)

    lane = lax.broadcasted_iota(jnp.int32, (C, LANES), 1)
    rowi = lax.broadcasted_iota(jnp.int32, (C, LANES), 0)
    first = lane < GDN_DK
    col = jnp.where(first, lane, lane - GDN_DK)
    tril = col <= rowi
    strict = col < rowi
    eye = jnp.where(col == rowi, 1.0, 0.0)
    blk16 = (col // 16) == (rowi // 16)
    blk32 = (col // 32) == (rowi // 32)
    r2 = lax.broadcasted_iota(jnp.int32, (LANES, LANES), 0)
    c2 = lax.broadcasted_iota(jnp.int32, (LANES, LANES), 1)
    first_rows = r2 < GDN_DK
    head_ones = jnp.where((r2 < GDN_DK) == (c2 < GDN_DK), 1.0, 0.0).astype(BF16)

    def bd(y):
        return jnp.concatenate([jnp.where(first, y, 0), jnp.where(first, 0, y)], axis=0)

    def pmm(x, y):
        return _dot(_bf(x), bd(_bf(y)))

    def pmm_hi(x, y):
        xh, xl = _split2(x)
        yh, yl = _split2(y)
        bh = bd(yh)
        return _dot(xh, bh) + _dot(xh, bd(yl)) + _dot(xl, bh)

    def pair_cols(a, c0, c1):
        return jnp.where(first, a[:, c0:c0 + 1], a[:, c1:c1 + 1])

    def chunk(ci, carry):
        rows = pl.ds(pl.multiple_of(ci * C, C), C)
        pairs = range(GDN_PAIRS)
        sls = [slice(p * LANES, (p + 1) * LANES) for p in pairs]
        gc = gcol_ref[rows, :]
        gr_all = grow_ref[ci]
        q = [q_ref[rows, sl] for sl in sls]
        k = [k_ref[rows, sl] for sl in sls]
        v = [v_ref[rows, sl] for sl in sls]
        g = [pair_cols(gc, 2 * p, 2 * p + 1) for p in pairs]
        beta = [pair_cols(gc, GDN_HEADS + 2 * p, GDN_HEADS + 2 * p + 1) for p in pairs]
        decay = [jnp.where(tril, jnp.exp(jnp.where(tril, g[p] - gr_all[p:p + 1, :], 0.0)), 0.0) for p in pairs]
        kb = [k[p] * beta[p] for p in pairs]
        kbd = [bd(_bf(k[p])) for p in pairs]
        lmat = [jnp.where(strict, _dot_nt(_bf(kb[p]), kbd[p]) * decay[p], 0.0) for p in pairs]
        a_intra = [jnp.where(tril, _dot_nt(_bf(q[p]), kbd[p]) * decay[p], 0.0) for p in pairs]
        n1 = [jnp.where(blk16, lmat[p], 0.0) for p in pairs]
        x = [eye - n1[p] for p in pairs]
        nk = n1
        for _ in range(3):
            nk = [pmm_hi(nk[p], nk[p]) for p in pairs]
            x = [x[p] + pmm_hi(x[p], nk[p]) for p in pairs]
        for off in ([jnp.where(blk32 & jnp.logical_not(blk16), lmat[p], 0.0) for p in pairs],
                    [jnp.where(blk32, 0.0, lmat[p]) for p in pairs]):
            xb = [pmm(x[p], off[p]) for p in pairs]
            x = [x[p] - pmm(xb[p], x[p]) for p in pairs]
        u = [pmm(x[p], v[p] * beta[p]) for p in pairs]
        w = [pmm(x[p], kb[p] * jnp.exp(g[p])) for p in pairs]
        state = [state_ref[p] for p in pairs]
        sb = [_bf(state[p]) for p in pairs]
        v_new = [u[p] - _dot(_bf(w[p]), sb[p]) for p in pairs]
        o = [_dot(_bf(q[p] * jnp.exp(g[p])), sb[p]) + pmm(a_intra[p], v_new[p]) for p in pairs]
        for p in pairs:
            g_last = g[p][C - 1:C, :]
            k_dec = k[p] * jnp.exp(g_last - g[p])
            scale = jnp.exp(jnp.where(first_rows, g_last[:, 0:1], g_last[:, GDN_DK:GDN_DK + 1]))
            state_ref[p] = state[p] * scale + _dot_tn(bd(_bf(k_dec)), bd(_bf(v_new[p])))
        for p in pairs:
            hi, lo = _split2(o[p] * o[p])
            ms = (_dot(hi, head_ones) + _dot(lo, head_ones)) * (1.0 / GDN_DV)
            on = o[p] * lax.rsqrt(ms + RMS_EPS) * on_ref[...]
            y_ref[rows, sls[p]] = _bf(on * _silu(gate_ref[rows, sls[p]]))
        return carry

    lax.fori_loop(0, n_chunks, chunk, 0)


def _gdn(gq, gk, gv, gate, gcol, growp, onorm2, bsz, seq):
    t = gq.shape[0]
    r = min(ROW_TILE, seq)
    nblk = seq // r
    n_chunks = r // GDN_CHUNK
    row = lambda n: pl.BlockSpec((r, n), lambda b, j: (b * nblk + j, 0))
    return pl.pallas_call(
        functools.partial(_gdn_kernel, n_chunks=n_chunks),
        grid=(bsz, nblk),
        in_specs=[row(GDN_HD), row(GDN_HD), row(GDN_HD), row(GDN_HD), row(LANES),
                  pl.BlockSpec((n_chunks, GDN_PAIRS, LANES), lambda b, j: (b * nblk + j, 0, 0)),
                  _const_spec(onorm2.shape)],
        out_specs=row(GDN_HD),
        out_shape=jax.ShapeDtypeStruct((t, GDN_HD), BF16),
        scratch_shapes=[pltpu.VMEM((GDN_PAIRS, LANES, LANES), F32)],
        compiler_params=_params(("parallel", "arbitrary")),
    )(gq, gk, gv, gate, gcol, growp, onorm2)


def _back_kernel(*refs, n_in):
    a_refs = refs[:n_in]
    (w_ref, x_ref, mod_ref, lng_ref, lnb_ref, wrh_ref, wrl_ref, br_ref,
     x1_ref, h2_ref, route_ref, counts_ref, base_ref) = refs[n_in:]
    off = 0
    y = None
    for a_ref in a_refs:
        kk = a_ref.shape[1]
        part = _dot(a_ref[...], w_ref[off:off + kk, :])
        y = part if y is None else y + part
        off += kk
    z = DEEPNORM_ALPHA * x_ref[...] + (1.0 + mod_ref[2:3, :]) * y
    x1 = _layer_norm(z, lng_ref[...], lnb_ref[...])
    x1_ref[...] = x1
    h2 = x1 * (1.0 + mod_ref[4:5, :]) + mod_ref[3:4, :]
    h2_ref[...] = h2

    hh, hl = _split2(h2)
    logits = _dot(hh, wrh_ref[...]) + _dot(hh, wrl_ref[...]) + _dot(hl, wrh_ref[...]) + br_ref[...]
    lane = lax.broadcasted_iota(jnp.int32, logits.shape, 1)
    lane_f = lane.astype(F32)
    neg = -jnp.inf
    gmask = (lane >= N_EXPERTS) & (lane < N_EXPERTS + N_GROUPS)
    gl = jnp.where(gmask, logits, neg)
    gmax = jnp.max(gl, -1, keepdims=True)
    g_top = 1.0 / jnp.sum(jnp.where(gmask, jnp.exp(gl - gmax), 0.0), -1, keepdims=True)
    g_lane = jnp.min(jnp.where(gl == gmax, lane_f, 1e9), -1, keepdims=True)
    g_idx = g_lane.astype(jnp.int32) - N_EXPERTS
    emask = (lane < N_EXPERTS) & ((lane // EXPERTS_PER_GROUP) == g_idx)
    el = jnp.where(emask, logits, neg)
    m1 = jnp.max(el, -1, keepdims=True)
    i1 = jnp.min(jnp.where(el == m1, lane_f, 1e9), -1, keepdims=True)
    el2 = jnp.where(lane_f == i1, neg, el)
    m2 = jnp.max(el2, -1, keepdims=True)
    i2 = jnp.min(jnp.where(el2 == m2, lane_f, 1e9), -1, keepdims=True)
    e2 = jnp.exp(m2 - m1)
    w1 = g_top / (1.0 + e2)
    w2 = g_top * e2 / (1.0 + e2)

    @pl.when(pl.program_id(0) == 0)
    def _():
        base_ref[...] = jnp.zeros(base_ref.shape, F32)

    tm = logits.shape[0]
    rr = lax.broadcasted_iota(jnp.int32, (tm, tm), 0)
    cc = lax.broadcasted_iota(jnp.int32, (tm, tm), 1)
    tri = jnp.where(cc <= rr, 1.0, 0.0).astype(BF16)
    oh1 = jnp.where(lane_f == i1, 1.0, 0.0)
    oh2 = jnp.where(lane_f == i2, 1.0, 0.0)
    cs1 = _dot(tri, _bf(oh1))
    cs2 = _dot(tri, _bf(oh2))
    base = base_ref[...]
    tot1 = cs1[tm - 1:tm, :]
    r1 = jnp.sum(oh1 * (base + cs1), -1, keepdims=True) - 1.0
    r2 = jnp.sum(oh2 * (base + tot1 + cs2), -1, keepdims=True) - 1.0
    total = base + tot1 + cs2[tm - 1:tm, :]
    base_ref[...] = total
    counts_ref[...] = total
    route = jnp.where(lane == 0, i1, jnp.where(lane == 1, i2, jnp.where(lane == 2, w1, jnp.where(lane == 3, w2, 0.0))))
    route_ref[...] = jnp.where(lane == 4, r1, jnp.where(lane == 5, r2, route))


def _mixer_back(acts, w_out, x2, mod_l, ln_g, ln_b, wr_hi, wr_lo, br, seq):
    t, d = x2.shape
    tm = min(ROW_TILE, seq)
    tiles_per_seq = seq // tm
    row = lambda n: pl.BlockSpec((tm, n), lambda i: (i, 0))
    return pl.pallas_call(
        functools.partial(_back_kernel, n_in=len(acts)),
        grid=(t // tm,),
        in_specs=[row(a.shape[1]) for a in acts] + [
            _const_spec(w_out.shape), row(d),
            pl.BlockSpec((None, N_MOD, d), lambda i: (i // tiles_per_seq, 0, 0)),
            _const_spec(ln_g.shape), _const_spec(ln_b.shape),
            _const_spec(wr_hi.shape), _const_spec(wr_lo.shape), _const_spec(br.shape)],
        out_specs=(row(d), row(d), row(LANES), _const_spec((1, LANES))),
        out_shape=(jax.ShapeDtypeStruct((t, d), F32), jax.ShapeDtypeStruct((t, d), F32),
                   jax.ShapeDtypeStruct((t, LANES), F32), jax.ShapeDtypeStruct((1, LANES), F32)),
        scratch_shapes=[pltpu.VMEM((1, LANES), F32)],
        compiler_params=_params(("arbitrary",)),
    )(*acts, w_out, x2, mod_l, ln_g, ln_b, wr_hi, wr_lo, br)


def _moe_kernel(be_ref, nb_ref, x_ref, wg_ref, wu_ref, wd_ref, y_ref, wg_s, wu_s, wd_s):
    i = pl.program_id(0)

    @pl.when(i < nb_ref[0])
    def _():
        prev = be_ref[jnp.maximum(i - 1, 0)]

        @pl.when((i == 0) | (be_ref[i] != prev))
        def _():
            wg_s[...] = _bf(wg_ref[...])
            wu_s[...] = _bf(wu_ref[...])
            wd_s[...] = _bf(wd_ref[...])

        x = _bf(x_ref[...])
        hid = _silu(_dot(x, wg_s[...])) * _dot(x, wu_s[...])
        y_ref[...] = _dot(_bf(hid), wd_s[...])

    @pl.when(i >= nb_ref[0])
    def _():
        y_ref[...] = jnp.zeros(y_ref.shape, F32)


def _moe_experts(block_expert, n_used, xb, w_gate, w_up, w_down, layer):
    n_slots, d = xb.shape
    ff = w_gate.shape[-1]
    n_blocks = n_slots // MOE_BLOCK
    grid_spec = pltpu.PrefetchScalarGridSpec(
        num_scalar_prefetch=2,
        grid=(n_blocks,),
        in_specs=[pl.BlockSpec((MOE_BLOCK, d), lambda i, be, nb: (jnp.minimum(i, nb[0] - 1), 0)),
                  pl.BlockSpec((None, None, d, ff), lambda i, be, nb: (layer, be[i], 0, 0)),
                  pl.BlockSpec((None, None, d, ff), lambda i, be, nb: (layer, be[i], 0, 0)),
                  pl.BlockSpec((None, None, ff, d), lambda i, be, nb: (layer, be[i], 0, 0))],
        out_specs=pl.BlockSpec((MOE_BLOCK, d), lambda i, be, nb: (i, 0)),
        scratch_shapes=[pltpu.VMEM((d, ff), BF16), pltpu.VMEM((d, ff), BF16), pltpu.VMEM((ff, d), BF16)],
    )
    return pl.pallas_call(
        _moe_kernel,
        grid_spec=grid_spec,
        out_shape=jax.ShapeDtypeStruct((n_slots, d), F32),
        compiler_params=_params(("arbitrary",)),
    )(block_expert, n_used, xb, w_gate, w_up, w_down)


def _slot_tables(route, counts_row, t):
    counts = counts_row[0, :N_EXPERTS].astype(jnp.int32)
    padded = (counts + MOE_BLOCK - 1) // MOE_BLOCK * MOE_BLOCK
    pend = jnp.cumsum(padded)
    pstart = pend - padded
    e = route[:, 0:2].astype(jnp.int32)
    rank = route[:, 4:6].astype(jnp.int32)
    slot = (jnp.take(pstart, e) + rank).reshape(2 * t)
    n_blocks = 2 * t // MOE_BLOCK + N_EXPERTS
    block_start = jnp.arange(n_blocks, dtype=jnp.int32) * MOE_BLOCK
    block_expert = jnp.minimum(jnp.sum((block_start[:, None] >= pend[None, :]).astype(jnp.int32), axis=1), N_EXPERTS - 1)
    n_used = (pend[-1] // MOE_BLOCK).astype(jnp.int32).reshape(1)
    n_slots = n_blocks * MOE_BLOCK
    tail = jnp.stack([pend[-1], (n_slots - pend[-1]) // (MOE_BLOCK // 2)])
    pad = jnp.concatenate([pstart + counts, padded - counts, tail]).astype(jnp.int32)
    return slot, block_expert, n_used, pad, n_slots


def _scatter_kernel(slot_ref, pad_ref, h_ref, xb_ref, stage_ref, zero_ref, sems, zsem, *, tm, n_steps):
    i = pl.program_id(0)

    def pad_copies(fn):
        def per_expert(e, carry):
            start = pad_ref[e]

            def body(j, c):
                fn(pltpu.make_async_copy(zero_ref.at[pl.ds(0, 1)], xb_ref.at[pl.ds(start + j, 1)], zsem))
                return c

            lax.fori_loop(0, pad_ref[N_EXPERTS + e], body, 0)
            return carry

        lax.fori_loop(0, N_EXPERTS, per_expert, 0)
        piece = zero_ref.shape[0]
        tail = pad_ref[2 * N_EXPERTS]

        def tail_body(j, c):
            pos = pl.multiple_of(tail + j * piece, piece)
            fn(pltpu.make_async_copy(zero_ref, xb_ref.at[pl.ds(pos, piece)], zsem))
            return c

        lax.fori_loop(0, pad_ref[2 * N_EXPERTS + 1], tail_body, 0)

    @pl.when(i == 0)
    def _():
        zero_ref[...] = jnp.zeros(zero_ref.shape, F32)
        pad_copies(lambda c: c.start())
        pad_copies(lambda c: c.wait())

    def start_rows(step):
        buf = step % 2

        def body(r, carry):
            tok = step * tm + r
            for choice in range(2):
                dst = slot_ref[2 * tok + choice]
                pltpu.make_async_copy(stage_ref.at[buf, pl.ds(r, 1)], xb_ref.at[pl.ds(dst, 1)], sems.at[buf]).start(priority=choice)
            return carry

        lax.fori_loop(0, tm, body, 0, unroll=8)

    def wait_rows(step):
        buf = step % 2
        for _ in range(2):
            pltpu.make_async_copy(stage_ref.at[buf], xb_ref.at[pl.ds(0, tm)], sems.at[buf]).wait()

    @pl.when(i >= 2)
    def _():
        wait_rows(i - 2)

    stage_ref[i % 2] = h_ref[...]
    start_rows(i)

    @pl.when(i == n_steps - 1)
    def _():
        @pl.when(i >= 1)
        def _():
            wait_rows(i - 1)

        wait_rows(i)


def _scatter_rows(slot, pad, h2, n_slots, seq):
    t, d = h2.shape
    tm = min(ROW_TILE, seq)
    n_steps = t // tm
    grid_spec = pltpu.PrefetchScalarGridSpec(
        num_scalar_prefetch=2,
        grid=(n_steps,),
        in_specs=[pl.BlockSpec((tm, d), lambda i, sl, pd: (i, 0))],
        out_specs=pl.BlockSpec(memory_space=pl.ANY),
        scratch_shapes=[pltpu.VMEM((2, tm, d), F32), pltpu.VMEM((MOE_BLOCK // 2, d), F32),
                        pltpu.SemaphoreType.DMA((2,)), pltpu.SemaphoreType.DMA(())],
    )
    return pl.pallas_call(
        functools.partial(_scatter_kernel, tm=tm, n_steps=n_steps),
        grid_spec=grid_spec,
        out_shape=jax.ShapeDtypeStruct((n_slots, d), F32),
        compiler_params=_params(("arbitrary",)),
    )(slot, pad, h2)


def _moe_back_kernel(slot_ref, x_ref, route_ref, mod_ref, lng_ref, lnb_ref, yb_ref, o_ref, ybuf, sems,
                     *, tm, n_steps):
    i = pl.program_id(0)

    def start_rows(step):
        buf = step % 2

        def body(r, carry):
            tok = step * tm + r
            for choice in range(2):
                src = slot_ref[2 * tok + choice]
                pltpu.make_async_copy(yb_ref.at[pl.ds(src, 1)], ybuf.at[buf, choice, pl.ds(r, 1)], sems.at[buf]).start(priority=choice)
            return carry

        lax.fori_loop(0, tm, body, 0, unroll=8)

    @pl.when(i == 0)
    def _():
        start_rows(0)

    @pl.when(i + 1 < n_steps)
    def _():
        start_rows(i + 1)

    buf = i % 2
    for choice in range(2):
        pltpu.make_async_copy(yb_ref.at[pl.ds(0, tm)], ybuf.at[buf, choice], sems.at[buf]).wait()
    rt = route_ref[...]
    y = ybuf[buf, 0] * rt[:, 2:3] + ybuf[buf, 1] * rt[:, 3:4]
    z = DEEPNORM_ALPHA * x_ref[...] + (1.0 + mod_ref[5:6, :]) * y
    o_ref[...] = _layer_norm(z, lng_ref[...], lnb_ref[...])


def _moe_back(slot, x1, yb, route, mod_l, ln_g, ln_b, seq):
    t, d = x1.shape
    tm = min(ROW_TILE, seq)
    tiles_per_seq = seq // tm
    n_steps = t // tm
    row = lambda n: pl.BlockSpec((tm, n), lambda i, sl: (i, 0))
    const = lambda shape: pl.BlockSpec(shape, lambda i, sl: (0,) * len(shape))
    grid_spec = pltpu.PrefetchScalarGridSpec(
        num_scalar_prefetch=1,
        grid=(n_steps,),
        in_specs=[row(d), row(LANES),
                  pl.BlockSpec((None, N_MOD, d), lambda i, sl: (i // tiles_per_seq, 0, 0)),
                  const(ln_g.shape), const(ln_b.shape), pl.BlockSpec(memory_space=pl.ANY)],
        out_specs=row(d),
        scratch_shapes=[pltpu.VMEM((2, 2, tm, d), F32), pltpu.SemaphoreType.DMA((2,))],
    )
    return pl.pallas_call(
        functools.partial(_moe_back_kernel, tm=tm, n_steps=n_steps),
        grid_spec=grid_spec,
        out_shape=jax.ShapeDtypeStruct((t, d), F32),
        compiler_params=_params(("arbitrary",)),
    )(slot, x1, route, mod_l, ln_g, ln_b, yb)


def _odd_front_kernel(x_ref, mod_ref, w_ref, tab_ref, q_ref, k_ref, v_ref, gate_ref):
    hb = _bf(x_ref[...] * (1.0 + mod_ref[1:2, :]) + mod_ref[0:1, :])
    c2 = tab_ref[:, 0:LANES]
    s2 = tab_ref[:, LANES:2 * LANES]
    nqk = RET_HEADS * RET_DK
    nv = RET_HEADS * RET_DV

    def rope(b):
        return b * c2 + pltpu.roll(b, RET_DK // 2, 1) * s2

    qa = _dot(hb, w_ref[:, 0:nqk])
    ka = _dot(hb, w_ref[:, nqk:2 * nqk])
    for h in range(RET_HEADS):
        sl = slice(h * LANES, (h + 1) * LANES)
        q_ref[:, sl] = _bf(rope(qa[:, sl]) * (RET_DK ** -0.5))
        k_ref[:, sl] = _bf(rope(ka[:, sl]))
    v_ref[...] = _bf(_dot(hb, w_ref[:, 2 * nqk:2 * nqk + nv]))
    gate_ref[...] = _silu(_dot(hb, w_ref[:, 2 * nqk + nv:2 * nqk + 2 * nv]))


def _odd_front(x2, mod_l, w_in, tabs, seq):
    t, d = x2.shape
    tm = min(ROW_TILE, seq)
    tiles_per_seq = seq // tm
    nqk = RET_HEADS * RET_DK
    nv = RET_HEADS * RET_DV
    row = lambda n: pl.BlockSpec((tm, n), lambda i: (i, 0))
    return pl.pallas_call(
        _odd_front_kernel,
        grid=(t // tm,),
        in_specs=[row(d), pl.BlockSpec((None, N_MOD, d), lambda i: (i // tiles_per_seq, 0, 0)),
                  _const_spec(w_in.shape), row(2 * LANES)],
        out_specs=(row(nqk), row(nqk), row(nv), row(nv)),
        out_shape=(jax.ShapeDtypeStruct((t, nqk), BF16), jax.ShapeDtypeStruct((t, nqk), BF16),
                   jax.ShapeDtypeStruct((t, nv), BF16), jax.ShapeDtypeStruct((t, nv), F32)),
        compiler_params=_params(("parallel",)),
    )(x2, mod_l, w_in, tabs)


def _ret_kernel(q_ref, k_ref, v_ref, gate_ref, lg_ref, gng_ref, gnb_ref, o_ref, state_ref, decay_ref, *, C):
    heads = range(RET_HEADS)
    lg = [lg_ref[h][:, 0:1] for h in heads]

    @pl.when(pl.program_id(1) == 0)
    def _():
        state_ref[...] = jnp.zeros(state_ref.shape, F32)
        ri = lax.broadcasted_iota(jnp.int32, (C, C), 0)
        ci = lax.broadcasted_iota(jnp.int32, (C, C), 1)
        diff = (ri - ci).astype(F32)
        causal = ri >= ci
        for h in heads:
            decay_ref[h] = jnp.where(causal, jnp.exp(jnp.where(causal, diff, 0.0) * lg[h]), 0.0)

    idx = lax.broadcasted_iota(jnp.int32, (C, 1), 0).astype(F32)
    qs = [q_ref[:, h * RET_DK:(h + 1) * RET_DK] for h in heads]
    ks = [k_ref[:, h * RET_DK:(h + 1) * RET_DK] for h in heads]
    vs = [v_ref[:, h * RET_DV:(h + 1) * RET_DV] for h in heads]
    states = [state_ref[h] for h in heads]
    scores = [_dot_nt(qs[h], ks[h]) for h in heads]
    cross = [_dot(qs[h], _bf(states[h])) for h in heads]
    kd = [_bf(ks[h].astype(F32) * jnp.exp((C - 1.0 - idx) * lg[h])) for h in heads]
    upd = [_dot_tn(kd[h], vs[h]) for h in heads]
    inner = [_dot(_bf(scores[h] * decay_ref[h]), vs[h]) for h in heads]
    for h in heads:
        state_ref[h] = states[h] * jnp.exp(C * lg[h]) + upd[h]
        o = inner[h] + cross[h] * jnp.exp((idx + 1.0) * lg[h])
        sl = slice(h * RET_DV, (h + 1) * RET_DV)
        o_ref[:, sl] = _bf(gate_ref[:, sl] * _layer_norm(o, gng_ref[h], gnb_ref[h]))


def _retention(q, k, v, gate, lg, gn_g, gn_b, bsz, seq):
    t = q.shape[0]
    C = min(RET_CHUNK, seq)
    nc = seq // C
    blk = lambda n: pl.BlockSpec((C, n), lambda b, j: (b * nc + j, 0))
    return pl.pallas_call(
        functools.partial(_ret_kernel, C=C),
        grid=(bsz, nc),
        in_specs=[blk(RET_HEADS * RET_DK), blk(RET_HEADS * RET_DK), blk(RET_HEADS * RET_DV), blk(RET_HEADS * RET_DV),
                  _const_spec(lg.shape), _const_spec(gn_g.shape), _const_spec(gn_b.shape)],
        out_specs=blk(RET_HEADS * RET_DV),
        out_shape=jax.ShapeDtypeStruct((t, RET_HEADS * RET_DV), BF16),
        scratch_shapes=[pltpu.VMEM((RET_HEADS, RET_DK, RET_DV), F32), pltpu.VMEM((RET_HEADS, C, C), F32)],
        compiler_params=_params(("parallel", "arbitrary")),
    )(q, k, v, gate, lg, gn_g, gn_b)


def _relayout_hy_w_in(w):
    d = w.shape[0]
    z = lambda n: jnp.zeros((d, n), w.dtype)
    c_rope = MLA_Q_RANK + MLA_KV_RANK
    c_gdn = c_rope + MLA_ROPE
    c_a = c_gdn + GDN_QKV
    c_gate = c_a + 2 * GDN_HEADS
    return _bf(jnp.concatenate([
        w[:, 0:c_rope], z(MLA_NOPE), w[:, c_rope:c_gdn], z(LANES - MLA_NOPE - MLA_ROPE),
        w[:, c_gdn:c_a], w[:, c_gate:], w[:, c_a:c_gate], z(LANES - 2 * GDN_HEADS)], axis=1))


def _relayout_w_uq(w):
    r = w.shape[0]
    wh = w.reshape(r, MLA_HEADS, MLA_NOPE + MLA_ROPE)
    pad = jnp.zeros((r, MLA_HEADS, LANES - MLA_NOPE - MLA_ROPE), w.dtype)
    return _bf(jnp.concatenate([wh, pad], axis=-1).reshape(r, MLA_HEADS * LANES))


def _relayout_w_ukv(w):
    r = w.shape[0]
    wh = w.reshape(r, MLA_HEADS, MLA_NOPE + MLA_V)
    pad = jnp.zeros((r, MLA_HEADS, LANES - MLA_NOPE), w.dtype)
    wk = jnp.concatenate([wh[:, :, :MLA_NOPE], pad], axis=-1).reshape(r, MLA_HEADS * LANES)
    wv = wh[:, :, MLA_NOPE:].reshape(r, MLA_HEADS * MLA_V).T
    return _bf(wk), _bf(wv)


def _rope_angles(positions, dim):
    inv_freq = ROPE_BASE ** (-jnp.arange(0, dim, 2, dtype=F32) / dim)
    ang = positions.astype(F32).reshape(-1)[:, None] * inv_freq
    return jnp.cos(ang), jnp.sin(ang)


def _mla_rope_tables(positions):
    cos, sin = _rope_angles(positions, MLA_ROPE)
    t = cos.shape[0]
    z = lambda n: jnp.zeros((t, n), F32)
    ct = jnp.concatenate([jnp.ones((t, MLA_NOPE), F32), cos, cos, z(32)], axis=1)
    s1 = jnp.concatenate([z(MLA_NOPE), -sin, z(48)], axis=1)
    s2 = jnp.concatenate([z(MLA_NOPE + 16), sin, z(32)], axis=1)
    return jnp.concatenate([ct, s1, s2], axis=1)


def _ret_rope_tables(positions):
    cos, sin = _rope_angles(positions, RET_DK)
    return jnp.concatenate([cos, cos, -sin, sin], axis=1)


def _router_weights(w_group, b_group, w_expert, b_expert):
    d = w_group.shape[0]
    pad = LANES - N_EXPERTS - N_GROUPS
    w = jnp.concatenate([w_expert, w_group, jnp.zeros((d, pad), F32)], axis=1)
    b = jnp.concatenate([b_expert, b_group, jnp.zeros((pad,), F32)]).reshape(1, LANES)
    hi = _bf(w)
    return hi, _bf(w - hi.astype(F32)), b


def _pad_lanes(v):
    return jnp.zeros((LANES,), F32).at[:v.shape[0]].set(v)


def kernel(x, c, positions, ada_w, ada_b, ln_mix_g, ln_mix_b, ln_ffn_g, ln_ffn_b, hy_w_in, mla_q_norm, mla_w_uq, mla_kv_norm, mla_w_ukv, gdn_conv_w, gdn_a_log, gdn_dt_bias, gdn_norm, hy_w_out, ret_w_in, ret_gn_g, ret_gn_b, ret_w_out, moe_w_group, moe_b_group, moe_w_expert, moe_b_expert, moe_w_gate, moe_w_up, moe_w_down):
    bsz, seq, d = x.shape
    t = bsz * seq
    mods = _ada_mod(c, ada_w, ada_b)
    mla_tabs = _mla_rope_tables(positions)
    ret_tabs = _ret_rope_tables(positions)
    log_gamma = jnp.log(1.0 - 2.0 ** (-5.0 - jnp.arange(RET_HEADS, dtype=F32)))
    lg = jnp.broadcast_to(log_gamma[:, None, None], (RET_HEADS, 1, LANES))
    xc = x.reshape(t, d)
    for layer in range(DEPTH):
        i = layer // 2
        mod_l = mods[layer]
        if layer % 2 == 0:
            wuk_p, wuv = _relayout_w_ukv(mla_w_ukv[i])
            gparam = jnp.stack([_pad_lanes(gdn_a_log[i]), _pad_lanes(gdn_dt_bias[i])])
            (qm, km, vm, gq, gk, gv, gate, gcol, grow) = _even_front(
                xc, mod_l, _relayout_hy_w_in(hy_w_in[i]), mla_q_norm[i].reshape(1, -1), _relayout_w_uq(mla_w_uq[i]),
                mla_kv_norm[i].reshape(1, -1), wuk_p, wuv, gdn_conv_w[i], gparam, mla_tabs, seq)
            o_mla = _attention(qm, km, vm, bsz, seq)
            n_ch = t // GDN_CHUNK
            growp = grow[:GDN_HEADS].reshape(GDN_PAIRS, 2, n_ch, GDN_CHUNK).transpose(2, 0, 1, 3).reshape(n_ch, GDN_PAIRS, LANES)
            onorm2 = jnp.concatenate([gdn_norm[i], gdn_norm[i]]).reshape(1, LANES)
            y_gdn = _gdn(gq, gk, gv, gate, gcol, growp, onorm2, bsz, seq)
            acts = [o_mla, y_gdn]
            w_out = _bf(hy_w_out[i])
        else:
            q, k, v, gate = _odd_front(xc, mod_l, _bf(ret_w_in[i]), ret_tabs, seq)
            og = _retention(q, k, v, gate, lg, ret_gn_g[i].reshape(RET_HEADS, 1, RET_DV),
                            ret_gn_b[i].reshape(RET_HEADS, 1, RET_DV), bsz, seq)
            acts = [og]
            w_out = _bf(ret_w_out[i])
        wr_hi, wr_lo, br = _router_weights(moe_w_group[layer], moe_b_group[layer], moe_w_expert[layer], moe_b_expert[layer])
        x1, h2, route, counts = _mixer_back(acts, w_out, xc, mod_l, ln_mix_g[layer].reshape(1, d), ln_mix_b[layer].reshape(1, d),
                                    wr_hi, wr_lo, br, seq)
        slot, block_expert, n_used, pad, n_slots = _slot_tables(route, counts, t)
        xb = _scatter_rows(slot, pad, h2, n_slots, seq)
        yb = _moe_experts(block_expert, n_used, xb, moe_w_gate, moe_w_up, moe_w_down, layer)
        xc = _moe_back(slot, x1, yb, route, mod_l, ln_ffn_g[layer].reshape(1, d), ln_ffn_b[layer].reshape(1, d), seq)
    return xc.reshape(bsz, seq, d)
```

```python
import functools

import jax
import jax.numpy as jnp
from jax import lax
from jax.experimental import pallas as pl
from jax.experimental.pallas import tpu as pltpu

F32 = jnp.float32
BF16 = jnp.bfloat16

DEPTH = 4
N_MOD = 6
LANES = 128

MLA_HEADS = 8
MLA_NOPE = 64
MLA_ROPE = 32
MLA_V = 64
MLA_Q_RANK = 256
MLA_KV_RANK = 128
MLA_SCALE = (MLA_NOPE + MLA_ROPE) ** -0.5
LOG2_E = 1.4426950408889634

GDN_HEADS = 8
GDN_DK = 64
GDN_DV = 64
GDN_CONV = 4
GDN_CHUNK = 64
GDN_HD = GDN_HEADS * GDN_DK
GDN_QKV = 3 * GDN_HD
GDN_PAIRS = GDN_HEADS // 2

RET_HEADS = 8
RET_DK = 128
RET_DV = 256
RET_CHUNK = 256

N_GROUPS = 4
EXPERTS_PER_GROUP = 8
N_EXPERTS = 32
EXPERT_FF = 512
MOE_BLOCK = 512

ROPE_BASE = 10000.0
LN_EPS = 1e-5
RMS_EPS = 1e-6
DEEPNORM_ALPHA = (2.0 * DEPTH) ** 0.25

ROW_TILE = 256
ATTN_TILE = 512
VMEM_LIMIT = 56 * 1024 * 1024

_C_MLA = 0
_C_QKV = 512
_C_GATE = 2048
_C_AB = 2560
_C_END = 2688


def _bf(x):
    return x.astype(BF16)


def _dot(a, b):
    return jnp.dot(a, b, preferred_element_type=F32)


def _dot_nt(a, b):
    return lax.dot_general(a, b, (((1,), (1,)), ((), ())), preferred_element_type=F32)


def _dot_tn(a, b):
    return lax.dot_general(a, b, (((0,), (0,)), ((), ())), preferred_element_type=F32)


def _split2(x):
    hi = _bf(x)
    return hi, _bf(x - hi.astype(F32))


def _split3(x):
    hi = _bf(x)
    r = x - hi.astype(F32)
    mid = _bf(r)
    return hi, mid, _bf(r - mid.astype(F32))


def _silu(x):
    return x * jax.nn.sigmoid(x)


def _layer_norm(z, g, b):
    mu = jnp.mean(z, -1, keepdims=True)
    zc = z - mu
    var = jnp.mean(zc * zc, -1, keepdims=True)
    return zc * lax.rsqrt(var + LN_EPS) * g + b


def _rms_norm(x, g):
    return x * lax.rsqrt(jnp.mean(x * x, -1, keepdims=True) + RMS_EPS) * g


def _params(sem):
    return pltpu.CompilerParams(dimension_semantics=sem, vmem_limit_bytes=VMEM_LIMIT)


def _const_spec(shape):
    nd = len(shape)
    return pl.BlockSpec(shape, lambda *_: (0,) * nd)


def _ada_kernel(c_ref, w_ref, b_ref, o_ref):
    c = c_ref[...]
    o_ref[...] = _dot(_bf(_silu(c)), _bf(w_ref[...])) + b_ref[...]


def _ada_mod(c, ada_w, ada_b):
    depth, d, n = ada_w.shape
    bsz = c.shape[0]
    rows = 16
    tn = 1536
    cp = jnp.zeros((rows, d), F32).at[:bsz].set(c)
    out = pl.pallas_call(
        _ada_kernel,
        grid=(depth, n // tn),
        in_specs=[pl.BlockSpec((rows, d), lambda l, j: (0, 0)),
                  pl.BlockSpec((None, d, tn), lambda l, j: (l, 0, j)),
                  pl.BlockSpec((None, 1, tn), lambda l, j: (l, 0, j))],
        out_specs=pl.BlockSpec((None, rows, tn), lambda l, j: (l, 0, j)),
        out_shape=jax.ShapeDtypeStruct((depth, rows, n), F32),
        compiler_params=_params(("parallel", "parallel")),
    )(cp, ada_w, ada_b.reshape(depth, 1, n))
    return out[:, :bsz].reshape(depth, bsz, N_MOD, d)


def _even_front_kernel(x_ref, mod_ref, w_ref, qn_ref, wuq_ref, kvn_ref, wuk_ref, wuv_ref,
                       cw_ref, gp_ref, tab_ref,
                       q_ref, k_ref, v_ref, gq_ref, gk_ref, gv_ref, gate_ref, gcol_ref, grow_ref,
                       xbuf, *, tm, tiles_per_seq):
    i = pl.program_id(0)
    hb = _bf(x_ref[...] * (1.0 + mod_ref[1:2, :]) + mod_ref[0:1, :])

    pm = _dot(hb, w_ref[:, _C_MLA:_C_QKV])
    ct = tab_ref[:, 0:LANES]
    s1 = tab_ref[:, LANES:2 * LANES]
    s2 = tab_ref[:, 2 * LANES:3 * LANES]

    def rope(b):
        return b * ct + pltpu.roll(b, LANES - 16, 1) * s1 + pltpu.roll(b, 16, 1) * s2

    qa = _dot(_bf(_rms_norm(pm[:, 0:MLA_Q_RANK], qn_ref[...])), wuq_ref[...])
    for h in range(MLA_HEADS):
        sl = slice(h * LANES, (h + 1) * LANES)
        q_ref[:, sl] = _bf(rope(qa[:, sl]) * (MLA_SCALE * LOG2_E))
    kvn = _bf(_rms_norm(pm[:, MLA_Q_RANK:MLA_Q_RANK + MLA_KV_RANK], kvn_ref[...]))
    ka = _dot(kvn, wuk_ref[...])
    kr = rope(pm[:, MLA_Q_RANK + MLA_KV_RANK:_C_QKV])
    for h in range(MLA_HEADS):
        sl = slice(h * LANES, (h + 1) * LANES)
        k_ref[:, sl] = _bf(ka[:, sl] + kr)
    v_ref[...] = _bf(_dot_nt(wuv_ref[...], kvn))

    @pl.when(i % tiles_per_seq == 0)
    def _():
        xbuf[0:8, :] = jnp.zeros((8, GDN_QKV), F32)

    xbuf[8:8 + tm, :] = _dot(hb, w_ref[:, _C_QKV:_C_GATE])
    cw = cw_ref[...]
    y = cw[0:1, :] * xbuf[5:5 + tm, :]
    for t in range(1, GDN_CONV):
        y = y + cw[t:t + 1, :] * xbuf[5 + t:5 + t + tm, :]
    xbuf[0:8, :] = xbuf[tm:tm + 8, :]
    y = _silu(y)

    r = lax.broadcasted_iota(jnp.int32, (GDN_HD, GDN_HD), 0) // GDN_DK
    c = lax.broadcasted_iota(jnp.int32, (GDN_HD, GDN_HD), 1) // GDN_DK
    head_ones = jnp.where(r == c, 1.0, 0.0).astype(BF16)

    def head_sumsq(z):
        hi, lo = _split2(z * z)
        return _dot(hi, head_ones) + _dot(lo, head_ones)

    qg = y[:, 0:GDN_HD]
    kg = y[:, GDN_HD:2 * GDN_HD]
    gq_ref[...] = qg * lax.rsqrt(head_sumsq(qg) + RMS_EPS) * (GDN_DK ** -0.5)
    gk_ref[...] = kg * lax.rsqrt(head_sumsq(kg) + RMS_EPS)
    gv_ref[...] = y[:, 2 * GDN_HD:3 * GDN_HD]
    gate_ref[...] = _dot(hb, w_ref[:, _C_GATE:_C_AB])

    ab = _dot(hb, w_ref[:, _C_AB:_C_END])
    lane = lax.broadcasted_iota(jnp.int32, (tm, LANES), 1)
    z = ab + gp_ref[1:2, :]
    softplus = jnp.maximum(z, 0.0) + jnp.log(1.0 + jnp.exp(-jnp.abs(z)))
    la = jnp.where(lane < GDN_HEADS, -jnp.exp(gp_ref[0:1, :]) * softplus, 0.0)
    rr = lax.broadcasted_iota(jnp.int32, (tm, tm), 0)
    cc = lax.broadcasted_iota(jnp.int32, (tm, tm), 1)
    tri = jnp.where((rr // GDN_CHUNK == cc // GDN_CHUNK) & (cc <= rr), 1.0, 0.0).astype(BF16)
    l0, l1, l2 = _split3(la)
    g = _dot(tri, l0) + _dot(tri, l1) + _dot(tri, l2)
    gcol = jnp.where(lane < GDN_HEADS, g, jax.nn.sigmoid(ab))
    gcol_ref[...] = gcol
    grow_ref[...] = gcol.T[0:16, :]


def _even_front(x2, mod_l, w_in_p, qn, wuq_p, kvn, wuk_p, wuv, conv_w, gparam, tabs, seq):
    t, d = x2.shape
    tm = min(ROW_TILE, seq)
    tiles_per_seq = seq // tm
    row = lambda n: pl.BlockSpec((tm, n), lambda i: (i, 0))
    out_shapes = (
        jax.ShapeDtypeStruct((t, MLA_HEADS * LANES), BF16),
        jax.ShapeDtypeStruct((t, MLA_HEADS * LANES), BF16),
        jax.ShapeDtypeStruct((MLA_HEADS * MLA_V, t), BF16),
        jax.ShapeDtypeStruct((t, GDN_HD), F32),
        jax.ShapeDtypeStruct((t, GDN_HD), F32),
        jax.ShapeDtypeStruct((t, GDN_HD), F32),
        jax.ShapeDtypeStruct((t, GDN_HD), F32),
        jax.ShapeDtypeStruct((t, LANES), F32),
        jax.ShapeDtypeStruct((16, t), F32),
    )
    return pl.pallas_call(
        functools.partial(_even_front_kernel, tm=tm, tiles_per_seq=tiles_per_seq),
        grid=(t // tm,),
        in_specs=[row(d),
                  pl.BlockSpec((None, N_MOD, d), lambda i: (i // tiles_per_seq, 0, 0)),
                  _const_spec(w_in_p.shape), _const_spec(qn.shape), _const_spec(wuq_p.shape),
                  _const_spec(kvn.shape), _const_spec(wuk_p.shape), _const_spec(wuv.shape),
                  _const_spec(conv_w.shape), _const_spec(gparam.shape), row(3 * LANES)],
        out_specs=(row(MLA_HEADS * LANES), row(MLA_HEADS * LANES),
                   pl.BlockSpec((MLA_HEADS * MLA_V, tm), lambda i: (0, i)),
                   row(GDN_HD), row(GDN_HD), row(GDN_HD), row(GDN_HD), row(LANES),
                   pl.BlockSpec((16, tm), lambda i: (0, i))),
        out_shape=out_shapes,
        scratch_shapes=[pltpu.VMEM((tm + 8, GDN_QKV), F32)],
        compiler_params=_params(("arbitrary",)),
    )(x2, mod_l, w_in_p, qn, wuq_p, kvn, wuk_p, wuv, conv_w, gparam, tabs)


def _attn_kernel(q_ref, k_ref, vt_ref, o_ref, m_ref, l_ref, acc_ref, *, tq):
    qi = pl.program_id(1)
    ki = pl.program_id(2)

    @pl.when(ki == 0)
    def _():
        m_ref[...] = jnp.full(m_ref.shape, -jnp.inf, F32)
        l_ref[...] = jnp.zeros(l_ref.shape, F32)
        acc_ref[...] = jnp.zeros(acc_ref.shape, F32)

    def scores(h):
        sl = slice(h * LANES, (h + 1) * LANES)
        return _dot_nt(k_ref[:, sl], q_ref[:, sl])

    def step(diagonal):
        if diagonal:
            key = lax.broadcasted_iota(jnp.int32, (tq, tq), 0)
            qry = lax.broadcasted_iota(jnp.int32, (tq, tq), 1)
            keep = key <= qry
        s_next = scores(0)
        for h in range(MLA_HEADS):
            s = s_next
            if h + 1 < MLA_HEADS:
                s_next = scores(h + 1)
            if diagonal:
                s = jnp.where(keep, s, -jnp.inf)
            rows = slice(h * MLA_V, (h + 1) * MLA_V)
            m_prev = m_ref[h:h + 1, :]
            m_new = jnp.maximum(m_prev, jnp.max(s, 0, keepdims=True))
            alpha = jnp.exp2(m_prev - m_new)
            p = jnp.exp2(s - m_new)
            l_ref[h:h + 1, :] = alpha * l_ref[h:h + 1, :] + jnp.sum(p, 0, keepdims=True)
            acc_ref[rows, :] = alpha * acc_ref[rows, :] + _dot(vt_ref[rows, :], _bf(p))
            m_ref[h:h + 1, :] = m_new

    @pl.when(ki < qi)
    def _():
        step(False)

    @pl.when(ki == qi)
    def _():
        step(True)
        for h in range(MLA_HEADS):
            rows = slice(h * MLA_V, (h + 1) * MLA_V)
            acc_ref[rows, :] = acc_ref[rows, :] / l_ref[h:h + 1, :]
        o_ref[...] = _bf(acc_ref[...].T)


def _attention(qm, km, vm, bsz, seq):
    t = qm.shape[0]
    tq = min(ATTN_TILE, seq)
    nq = seq // tq
    return pl.pallas_call(
        functools.partial(_attn_kernel, tq=tq),
        grid=(bsz, nq, nq),
        in_specs=[pl.BlockSpec((tq, MLA_HEADS * LANES), lambda b, qi, ki: (b * nq + qi, 0)),
                  pl.BlockSpec((tq, MLA_HEADS * LANES), lambda b, qi, ki: (b * nq + jnp.minimum(ki, qi), 0)),
                  pl.BlockSpec((MLA_HEADS * MLA_V, tq), lambda b, qi, ki: (0, b * nq + jnp.minimum(ki, qi)))],
        out_specs=pl.BlockSpec((tq, MLA_HEADS * MLA_V), lambda b, qi, ki: (b * nq + qi, 0)),
        out_shape=jax.ShapeDtypeStruct((t, MLA_HEADS * MLA_V), BF16),
        scratch_shapes=[pltpu.VMEM((MLA_HEADS, tq), F32),
                        pltpu.VMEM((MLA_HEADS, tq), F32),
                        pltpu.VMEM((MLA_HEADS * MLA_V, tq), F32)],
        compiler_params=_params(("parallel", "parallel", "arbitrary")),
    )(qm, km, vm)


def _gdn_kernel(q_ref, k_ref, v_ref, gate_ref, gcol_ref, grow_ref, on_ref, y_ref, state_ref, *, n_chunks):
    C = GDN_CHUNK

    @pl.when(pl.program_id(1) == 0)
    def _():
        state_ref[...] = jnp.zeros(state_ref.shape, F32)

    lane = lax.broadcasted_iota(jnp.int32, (C, LANES), 1)
    rowi = lax.broadcasted_iota(jnp.int32, (C, LANES), 0)
    first = lane < GDN_DK
    col = jnp.where(first, lane, lane - GDN_DK)
    tril = col <= rowi
    strict = col < rowi
    eye = jnp.where(col == rowi, 1.0, 0.0)
    blk16 = (col // 16) == (rowi // 16)
    blk32 = (col // 32) == (rowi // 32)
    r2 = lax.broadcasted_iota(jnp.int32, (LANES, LANES), 0)
    c2 = lax.broadcasted_iota(jnp.int32, (LANES, LANES), 1)
    first_rows = r2 < GDN_DK
    head_ones = jnp.where((r2 < GDN_DK) == (c2 < GDN_DK), 1.0, 0.0).astype(BF16)

    def bd(y):
        return jnp.concatenate([jnp.where(first, y, 0), jnp.where(first, 0, y)], axis=0)

    def pmm(x, y):
        return _dot(_bf(x), bd(_bf(y)))

    def pmm_hi(x, y):
        xh, xl = _split2(x)
        yh, yl = _split2(y)
        bh = bd(yh)
        return _dot(xh, bh) + _dot(xh, bd(yl)) + _dot(xl, bh)

    def pair_cols(a, c0, c1):
        return jnp.where(first, a[:, c0:c0 + 1], a[:, c1:c1 + 1])

    def chunk(ci, carry):
        rows = pl.ds(pl.multiple_of(ci * C, C), C)
        pairs = range(GDN_PAIRS)
        sls = [slice(p * LANES, (p + 1) * LANES) for p in pairs]
        gc = gcol_ref[rows, :]
        gr_all = grow_ref[ci]
        q = [q_ref[rows, sl] for sl in sls]
        k = [k_ref[rows, sl] for sl in sls]
        v = [v_ref[rows, sl] for sl in sls]
        g = [pair_cols(gc, 2 * p, 2 * p + 1) for p in pairs]
        beta = [pair_cols(gc, GDN_HEADS + 2 * p, GDN_HEADS + 2 * p + 1) for p in pairs]
        decay = [jnp.where(tril, jnp.exp(jnp.where(tril, g[p] - gr_all[p:p + 1, :], 0.0)), 0.0) for p in pairs]
        kb = [k[p] * beta[p] for p in pairs]
        kbd = [bd(_bf(k[p])) for p in pairs]
        lmat = [jnp.where(strict, _dot_nt(_bf(kb[p]), kbd[p]) * decay[p], 0.0) for p in pairs]
        a_intra = [jnp.where(tril, _dot_nt(_bf(q[p]), kbd[p]) * decay[p], 0.0) for p in pairs]
        n1 = [jnp.where(blk16, lmat[p], 0.0) for p in pairs]
        x = [eye - n1[p] for p in pairs]
        nk = n1
        for _ in range(3):
            nk = [pmm_hi(nk[p], nk[p]) for p in pairs]
            x = [x[p] + pmm_hi(x[p], nk[p]) for p in pairs]
        for off in ([jnp.where(blk32 & jnp.logical_not(blk16), lmat[p], 0.0) for p in pairs],
                    [jnp.where(blk32, 0.0, lmat[p]) for p in pairs]):
            xb = [pmm(x[p], off[p]) for p in pairs]
            x = [x[p] - pmm(xb[p], x[p]) for p in pairs]
        u = [pmm(x[p], v[p] * beta[p]) for p in pairs]
        w = [pmm(x[p], kb[p] * jnp.exp(g[p])) for p in pairs]
        state = [state_ref[p] for p in pairs]
        sb = [_bf(state[p]) for p in pairs]
        v_new = [u[p] - _dot(_bf(w[p]), sb[p]) for p in pairs]
        o = [_dot(_bf(q[p] * jnp.exp(g[p])), sb[p]) + pmm(a_intra[p], v_new[p]) for p in pairs]
        for p in pairs:
            g_last = g[p][C - 1:C, :]
            k_dec = k[p] * jnp.exp(g_last - g[p])
            scale = jnp.exp(jnp.where(first_rows, g_last[:, 0:1], g_last[:, GDN_DK:GDN_DK + 1]))
            state_ref[p] = state[p] * scale + _dot_tn(bd(_bf(k_dec)), bd(_bf(v_new[p])))
        for p in pairs:
            hi, lo = _split2(o[p] * o[p])
            ms = (_dot(hi, head_ones) + _dot(lo, head_ones)) * (1.0 / GDN_DV)
            on = o[p] * lax.rsqrt(ms + RMS_EPS) * on_ref[...]
            y_ref[rows, sls[p]] = _bf(on * _silu(gate_ref[rows, sls[p]]))
        return carry

    lax.fori_loop(0, n_chunks, chunk, 0)


def _gdn(gq, gk, gv, gate, gcol, growp, onorm2, bsz, seq):
    t = gq.shape[0]
    r = min(ROW_TILE, seq)
    nblk = seq // r
    n_chunks = r // GDN_CHUNK
    row = lambda n: pl.BlockSpec((r, n), lambda b, j: (b * nblk + j, 0))
    return pl.pallas_call(
        functools.partial(_gdn_kernel, n_chunks=n_chunks),
        grid=(bsz, nblk),
        in_specs=[row(GDN_HD), row(GDN_HD), row(GDN_HD), row(GDN_HD), row(LANES),
                  pl.BlockSpec((n_chunks, GDN_PAIRS, LANES), lambda b, j: (b * nblk + j, 0, 0)),
                  _const_spec(onorm2.shape)],
        out_specs=row(GDN_HD),
        out_shape=jax.ShapeDtypeStruct((t, GDN_HD), BF16),
        scratch_shapes=[pltpu.VMEM((GDN_PAIRS, LANES, LANES), F32)],
        compiler_params=_params(("parallel", "arbitrary")),
    )(gq, gk, gv, gate, gcol, growp, onorm2)


def _back_kernel(*refs, n_in):
    a_refs = refs[:n_in]
    (w_ref, x_ref, mod_ref, lng_ref, lnb_ref, wrh_ref, wrl_ref, br_ref,
     x1_ref, h2_ref, route_ref, counts_ref, base_ref) = refs[n_in:]
    off = 0
    y = None
    for a_ref in a_refs:
        kk = a_ref.shape[1]
        part = _dot(a_ref[...], w_ref[off:off + kk, :])
        y = part if y is None else y + part
        off += kk
    z = DEEPNORM_ALPHA * x_ref[...] + (1.0 + mod_ref[2:3, :]) * y
    x1 = _layer_norm(z, lng_ref[...], lnb_ref[...])
    x1_ref[...] = x1
    h2 = x1 * (1.0 + mod_ref[4:5, :]) + mod_ref[3:4, :]
    h2_ref[...] = h2

    hh, hl = _split2(h2)
    logits = _dot(hh, wrh_ref[...]) + _dot(hh, wrl_ref[...]) + _dot(hl, wrh_ref[...]) + br_ref[...]
    lane = lax.broadcasted_iota(jnp.int32, logits.shape, 1)
    lane_f = lane.astype(F32)
    neg = -jnp.inf
    gmask = (lane >= N_EXPERTS) & (lane < N_EXPERTS + N_GROUPS)
    gl = jnp.where(gmask, logits, neg)
    gmax = jnp.max(gl, -1, keepdims=True)
    g_top = 1.0 / jnp.sum(jnp.where(gmask, jnp.exp(gl - gmax), 0.0), -1, keepdims=True)
    g_lane = jnp.min(jnp.where(gl == gmax, lane_f, 1e9), -1, keepdims=True)
    g_idx = g_lane.astype(jnp.int32) - N_EXPERTS
    emask = (lane < N_EXPERTS) & ((lane // EXPERTS_PER_GROUP) == g_idx)
    el = jnp.where(emask, logits, neg)
    m1 = jnp.max(el, -1, keepdims=True)
    i1 = jnp.min(jnp.where(el == m1, lane_f, 1e9), -1, keepdims=True)
    el2 = jnp.where(lane_f == i1, neg, el)
    m2 = jnp.max(el2, -1, keepdims=True)
    i2 = jnp.min(jnp.where(el2 == m2, lane_f, 1e9), -1, keepdims=True)
    e2 = jnp.exp(m2 - m1)
    w1 = g_top / (1.0 + e2)
    w2 = g_top * e2 / (1.0 + e2)

    @pl.when(pl.program_id(0) == 0)
    def _():
        base_ref[...] = jnp.zeros(base_ref.shape, F32)

    tm = logits.shape[0]
    rr = lax.broadcasted_iota(jnp.int32, (tm, tm), 0)
    cc = lax.broadcasted_iota(jnp.int32, (tm, tm), 1)
    tri = jnp.where(cc <= rr, 1.0, 0.0).astype(BF16)
    oh1 = jnp.where(lane_f == i1, 1.0, 0.0)
    oh2 = jnp.where(lane_f == i2, 1.0, 0.0)
    cs1 = _dot(tri, _bf(oh1))
    cs2 = _dot(tri, _bf(oh2))
    base = base_ref[...]
    tot1 = cs1[tm - 1:tm, :]
    r1 = jnp.sum(oh1 * (base + cs1), -1, keepdims=True) - 1.0
    r2 = jnp.sum(oh2 * (base + tot1 + cs2), -1, keepdims=True) - 1.0
    total = base + tot1 + cs2[tm - 1:tm, :]
    base_ref[...] = total
    counts_ref[...] = total
    route = jnp.where(lane == 0, i1, jnp.where(lane == 1, i2, jnp.where(lane == 2, w1, jnp.where(lane == 3, w2, 0.0))))
    route_ref[...] = jnp.where(lane == 4, r1, jnp.where(lane == 5, r2, route))


def _mixer_back(acts, w_out, x2, mod_l, ln_g, ln_b, wr_hi, wr_lo, br, seq):
    t, d = x2.shape
    tm = min(ROW_TILE, seq)
    tiles_per_seq = seq // tm
    row = lambda n: pl.BlockSpec((tm, n), lambda i: (i, 0))
    return pl.pallas_call(
        functools.partial(_back_kernel, n_in=len(acts)),
        grid=(t // tm,),
        in_specs=[row(a.shape[1]) for a in acts] + [
            _const_spec(w_out.shape), row(d),
            pl.BlockSpec((None, N_MOD, d), lambda i: (i // tiles_per_seq, 0, 0)),
            _const_spec(ln_g.shape), _const_spec(ln_b.shape),
            _const_spec(wr_hi.shape), _const_spec(wr_lo.shape), _const_spec(br.shape)],
        out_specs=(row(d), row(d), row(LANES), _const_spec((1, LANES))),
        out_shape=(jax.ShapeDtypeStruct((t, d), F32), jax.ShapeDtypeStruct((t, d), F32),
                   jax.ShapeDtypeStruct((t, LANES), F32), jax.ShapeDtypeStruct((1, LANES), F32)),
        scratch_shapes=[pltpu.VMEM((1, LANES), F32)],
        compiler_params=_params(("arbitrary",)),
    )(*acts, w_out, x2, mod_l, ln_g, ln_b, wr_hi, wr_lo, br)


def _moe_kernel(be_ref, nb_ref, x_ref, wg_ref, wu_ref, wd_ref, y_ref, wg_s, wu_s, wd_s):
    i = pl.program_id(0)

    @pl.when(i < nb_ref[0])
    def _():
        prev = be_ref[jnp.maximum(i - 1, 0)]

        @pl.when((i == 0) | (be_ref[i] != prev))
        def _():
            wg_s[...] = _bf(wg_ref[...])
            wu_s[...] = _bf(wu_ref[...])
            wd_s[...] = _bf(wd_ref[...])

        x = _bf(x_ref[...])
        hid = _silu(_dot(x, wg_s[...])) * _dot(x, wu_s[...])
        y_ref[...] = _dot(_bf(hid), wd_s[...])

    @pl.when(i >= nb_ref[0])
    def _():
        y_ref[...] = jnp.zeros(y_ref.shape, F32)


def _moe_experts(block_expert, n_used, xb, w_gate, w_up, w_down, layer):
    n_slots, d = xb.shape
    ff = w_gate.shape[-1]
    n_blocks = n_slots // MOE_BLOCK
    grid_spec = pltpu.PrefetchScalarGridSpec(
        num_scalar_prefetch=2,
        grid=(n_blocks,),
        in_specs=[pl.BlockSpec((MOE_BLOCK, d), lambda i, be, nb: (jnp.minimum(i, nb[0] - 1), 0)),
                  pl.BlockSpec((None, None, d, ff), lambda i, be, nb: (layer, be[i], 0, 0)),
                  pl.BlockSpec((None, None, d, ff), lambda i, be, nb: (layer, be[i], 0, 0)),
                  pl.BlockSpec((None, None, ff, d), lambda i, be, nb: (layer, be[i], 0, 0))],
        out_specs=pl.BlockSpec((MOE_BLOCK, d), lambda i, be, nb: (i, 0)),
        scratch_shapes=[pltpu.VMEM((d, ff), BF16), pltpu.VMEM((d, ff), BF16), pltpu.VMEM((ff, d), BF16)],
    )
    return pl.pallas_call(
        _moe_kernel,
        grid_spec=grid_spec,
        out_shape=jax.ShapeDtypeStruct((n_slots, d), F32),
        compiler_params=_params(("arbitrary",)),
    )(block_expert, n_used, xb, w_gate, w_up, w_down)


def _slot_tables(route, counts_row, t):
    counts = counts_row[0, :N_EXPERTS].astype(jnp.int32)
    padded = (counts + MOE_BLOCK - 1) // MOE_BLOCK * MOE_BLOCK
    pend = jnp.cumsum(padded)
    pstart = pend - padded
    e = route[:, 0:2].astype(jnp.int32)
    rank = route[:, 4:6].astype(jnp.int32)
    slot = (jnp.take(pstart, e) + rank).reshape(2 * t)
    n_blocks = 2 * t // MOE_BLOCK + N_EXPERTS
    block_start = jnp.arange(n_blocks, dtype=jnp.int32) * MOE_BLOCK
    block_expert = jnp.minimum(jnp.sum((block_start[:, None] >= pend[None, :]).astype(jnp.int32), axis=1), N_EXPERTS - 1)
    n_used = (pend[-1] // MOE_BLOCK).astype(jnp.int32).reshape(1)
    n_slots = n_blocks * MOE_BLOCK
    tail = jnp.stack([pend[-1], (n_slots - pend[-1]) // (MOE_BLOCK // 2)])
    pad = jnp.concatenate([pstart + counts, padded - counts, tail]).astype(jnp.int32)
    return slot, block_expert, n_used, pad, n_slots


def _scatter_kernel(slot_ref, pad_ref, h_ref, xb_ref, stage_ref, zero_ref, sems, zsem, *, tm, n_steps):
    i = pl.program_id(0)

    def pad_copies(fn):
        def rows_at(pos, count, size):
            def body(j, c):
                at = pos + j * size
                if size > 1:
                    at = pl.multiple_of(at, size)
                fn(pltpu.make_async_copy(zero_ref.at[pl.ds(0, size)], xb_ref.at[pl.ds(at, size)], zsem))
                return c

            lax.fori_loop(0, count, body, 0)

        def per_expert(e, carry):
            start = pad_ref[e]
            length = pad_ref[N_EXPERTS + e]
            head = jnp.minimum((-start) & 7, length)
            rows_at(start, head, 1)
            rows_at(start + head, (length - head) >> 3, 8)
            rows_at(start + head + ((length - head) & ~7), (length - head) & 7, 1)
            return carry

        lax.fori_loop(0, N_EXPERTS, per_expert, 0)
        rows_at(pad_ref[2 * N_EXPERTS], pad_ref[2 * N_EXPERTS + 1], zero_ref.shape[0])

    @pl.when(i == 0)
    def _():
        zero_ref[...] = jnp.zeros(zero_ref.shape, F32)
        pad_copies(lambda c: c.start())
        pad_copies(lambda c: c.wait())

    def start_rows(step):
        buf = step % 2
        for r in range(tm):
            for choice in range(2):
                dst = slot_ref[2 * (step * tm + r) + choice]
                pltpu.make_async_copy(stage_ref.at[buf, pl.ds(r, 1)], xb_ref.at[pl.ds(dst, 1)], sems.at[buf]).start()

    def wait_rows(step):
        buf = step % 2
        for _ in range(2):
            pltpu.make_async_copy(stage_ref.at[buf], xb_ref.at[pl.ds(0, tm)], sems.at[buf]).wait()

    @pl.when(i >= 2)
    def _():
        wait_rows(i - 2)

    stage_ref[i % 2] = h_ref[...]
    start_rows(i)

    @pl.when(i == n_steps - 1)
    def _():
        @pl.when(i >= 1)
        def _():
            wait_rows(i - 1)

        wait_rows(i)


def _scatter_rows(slot, pad, h2, n_slots, seq):
    t, d = h2.shape
    tm = min(ROW_TILE, seq)
    n_steps = t // tm
    grid_spec = pltpu.PrefetchScalarGridSpec(
        num_scalar_prefetch=2,
        grid=(n_steps,),
        in_specs=[pl.BlockSpec((tm, d), lambda i, sl, pd: (i, 0))],
        out_specs=pl.BlockSpec(memory_space=pl.ANY),
        scratch_shapes=[pltpu.VMEM((2, tm, d), F32), pltpu.VMEM((MOE_BLOCK // 2, d), F32),
                        pltpu.SemaphoreType.DMA((2,)), pltpu.SemaphoreType.DMA(())],
    )
    return pl.pallas_call(
        functools.partial(_scatter_kernel, tm=tm, n_steps=n_steps),
        grid_spec=grid_spec,
        out_shape=jax.ShapeDtypeStruct((n_slots, d), F32),
        compiler_params=_params(("arbitrary",)),
    )(slot, pad, h2)


def _moe_back_kernel(slot_ref, x_ref, route_ref, mod_ref, lng_ref, lnb_ref, yb_ref, o_ref, ybuf, sems,
                     *, tm, n_steps):
    i = pl.program_id(0)

    def start_rows(tile, buf):
        for r in range(tm):
            for choice in range(2):
                src = slot_ref[2 * (tile * tm + r) + choice]
                pltpu.make_async_copy(yb_ref.at[pl.ds(src, 1)], ybuf.at[buf, choice, pl.ds(r, 1)], sems.at[buf]).start()

    def wait_rows(buf):
        for choice in range(2):
            pltpu.make_async_copy(yb_ref.at[pl.ds(0, tm)], ybuf.at[buf, choice], sems.at[buf]).wait()

    buf = i % 2

    @pl.when(i == 0)
    def _():
        start_rows(0, 0)

    wait_rows(buf)
    start_rows(jnp.minimum(i + 1, n_steps - 1), 1 - buf)
    rt = route_ref[...]
    y = ybuf[buf, 0] * rt[:, 2:3] + ybuf[buf, 1] * rt[:, 3:4]
    z = DEEPNORM_ALPHA * x_ref[...] + (1.0 + mod_ref[5:6, :]) * y
    o_ref[...] = _layer_norm(z, lng_ref[...], lnb_ref[...])

    @pl.when(i == n_steps - 1)
    def _():
        wait_rows(1 - buf)


def _moe_back(slot, x1, yb, route, mod_l, ln_g, ln_b, seq):
    t, d = x1.shape
    tm = min(ROW_TILE, seq)
    tiles_per_seq = seq // tm
    n_steps = t // tm
    row = lambda n: pl.BlockSpec((tm, n), lambda i, sl: (i, 0))
    const = lambda shape: pl.BlockSpec(shape, lambda i, sl: (0,) * len(shape))
    grid_spec = pltpu.PrefetchScalarGridSpec(
        num_scalar_prefetch=1,
        grid=(n_steps,),
        in_specs=[row(d), row(LANES),
                  pl.BlockSpec((None, N_MOD, d), lambda i, sl: (i // tiles_per_seq, 0, 0)),
                  const(ln_g.shape), const(ln_b.shape), pl.BlockSpec(memory_space=pl.ANY)],
        out_specs=row(d),
        scratch_shapes=[pltpu.VMEM((2, 2, tm, d), F32), pltpu.SemaphoreType.DMA((2,))],
    )
    return pl.pallas_call(
        functools.partial(_moe_back_kernel, tm=tm, n_steps=n_steps),
        grid_spec=grid_spec,
        out_shape=jax.ShapeDtypeStruct((t, d), F32),
        compiler_params=_params(("arbitrary",)),
    )(slot, x1, route, mod_l, ln_g, ln_b, yb)


def _odd_front_kernel(x_ref, mod_ref, w_ref, tab_ref, q_ref, k_ref, v_ref, gate_ref):
    hb = _bf(x_ref[...] * (1.0 + mod_ref[1:2, :]) + mod_ref[0:1, :])
    c2 = tab_ref[:, 0:LANES]
    s2 = tab_ref[:, LANES:2 * LANES]
    nqk = RET_HEADS * RET_DK
    nv = RET_HEADS * RET_DV

    def rope(b):
        return b * c2 + pltpu.roll(b, RET_DK // 2, 1) * s2

    qa = _dot(hb, w_ref[:, 0:nqk])
    ka = _dot(hb, w_ref[:, nqk:2 * nqk])
    for h in range(RET_HEADS):
        sl = slice(h * LANES, (h + 1) * LANES)
        q_ref[:, sl] = _bf(rope(qa[:, sl]) * (RET_DK ** -0.5))
        k_ref[:, sl] = _bf(rope(ka[:, sl]))
    v_ref[...] = _bf(_dot(hb, w_ref[:, 2 * nqk:2 * nqk + nv]))
    gate_ref[...] = _silu(_dot(hb, w_ref[:, 2 * nqk + nv:2 * nqk + 2 * nv]))


def _odd_front(x2, mod_l, w_in, tabs, seq):
    t, d = x2.shape
    tm = min(ROW_TILE, seq)
    tiles_per_seq = seq // tm
    nqk = RET_HEADS * RET_DK
    nv = RET_HEADS * RET_DV
    row = lambda n: pl.BlockSpec((tm, n), lambda i: (i, 0))
    return pl.pallas_call(
        _odd_front_kernel,
        grid=(t // tm,),
        in_specs=[row(d), pl.BlockSpec((None, N_MOD, d), lambda i: (i // tiles_per_seq, 0, 0)),
                  _const_spec(w_in.shape), row(2 * LANES)],
        out_specs=(row(nqk), row(nqk), row(nv), row(nv)),
        out_shape=(jax.ShapeDtypeStruct((t, nqk), BF16), jax.ShapeDtypeStruct((t, nqk), BF16),
                   jax.ShapeDtypeStruct((t, nv), BF16), jax.ShapeDtypeStruct((t, nv), F32)),
        compiler_params=_params(("parallel",)),
    )(x2, mod_l, w_in, tabs)


def _ret_kernel(q_ref, k_ref, v_ref, gate_ref, lg_ref, gng_ref, gnb_ref, o_ref, state_ref, decay_ref, *, C):
    heads = range(RET_HEADS)
    lg = [lg_ref[h][:, 0:1] for h in heads]

    @pl.when(pl.program_id(1) == 0)
    def _():
        state_ref[...] = jnp.zeros(state_ref.shape, F32)
        ri = lax.broadcasted_iota(jnp.int32, (C, C), 0)
        ci = lax.broadcasted_iota(jnp.int32, (C, C), 1)
        diff = (ri - ci).astype(F32)
        causal = ri >= ci
        for h in heads:
            decay_ref[h] = jnp.where(causal, jnp.exp(jnp.where(causal, diff, 0.0) * lg[h]), 0.0)

    idx = lax.broadcasted_iota(jnp.int32, (C, 1), 0).astype(F32)
    qs = [q_ref[:, h * RET_DK:(h + 1) * RET_DK] for h in heads]
    ks = [k_ref[:, h * RET_DK:(h + 1) * RET_DK] for h in heads]
    vs = [v_ref[:, h * RET_DV:(h + 1) * RET_DV] for h in heads]
    states = [state_ref[h] for h in heads]
    scores = [_dot_nt(qs[h], ks[h]) for h in heads]
    cross = [_dot(qs[h], _bf(states[h])) for h in heads]
    kd = [_bf(ks[h].astype(F32) * jnp.exp((C - 1.0 - idx) * lg[h])) for h in heads]
    upd = [_dot_tn(kd[h], vs[h]) for h in heads]
    inner = [_dot(_bf(scores[h] * decay_ref[h]), vs[h]) for h in heads]
    for h in heads:
        state_ref[h] = states[h] * jnp.exp(C * lg[h]) + upd[h]
        o = inner[h] + cross[h] * jnp.exp((idx + 1.0) * lg[h])
        sl = slice(h * RET_DV, (h + 1) * RET_DV)
        o_ref[:, sl] = _bf(gate_ref[:, sl] * _layer_norm(o, gng_ref[h], gnb_ref[h]))


def _retention(q, k, v, gate, lg, gn_g, gn_b, bsz, seq):
    t = q.shape[0]
    C = min(RET_CHUNK, seq)
    nc = seq // C
    blk = lambda n: pl.BlockSpec((C, n), lambda b, j: (b * nc + j, 0))
    return pl.pallas_call(
        functools.partial(_ret_kernel, C=C),
        grid=(bsz, nc),
        in_specs=[blk(RET_HEADS * RET_DK), blk(RET_HEADS * RET_DK), blk(RET_HEADS * RET_DV), blk(RET_HEADS * RET_DV),
                  _const_spec(lg.shape), _const_spec(gn_g.shape), _const_spec(gn_b.shape)],
        out_specs=blk(RET_HEADS * RET_DV),
        out_shape=jax.ShapeDtypeStruct((t, RET_HEADS * RET_DV), BF16),
        scratch_shapes=[pltpu.VMEM((RET_HEADS, RET_DK, RET_DV), F32), pltpu.VMEM((RET_HEADS, C, C), F32)],
        compiler_params=_params(("parallel", "arbitrary")),
    )(q, k, v, gate, lg, gn_g, gn_b)


def _relayout_hy_w_in(w):
    d = w.shape[0]
    z = lambda n: jnp.zeros((d, n), w.dtype)
    c_rope = MLA_Q_RANK + MLA_KV_RANK
    c_gdn = c_rope + MLA_ROPE
    c_a = c_gdn + GDN_QKV
    c_gate = c_a + 2 * GDN_HEADS
    return _bf(jnp.concatenate([
        w[:, 0:c_rope], z(MLA_NOPE), w[:, c_rope:c_gdn], z(LANES - MLA_NOPE - MLA_ROPE),
        w[:, c_gdn:c_a], w[:, c_gate:], w[:, c_a:c_gate], z(LANES - 2 * GDN_HEADS)], axis=1))


def _relayout_w_uq(w):
    r = w.shape[0]
    wh = w.reshape(r, MLA_HEADS, MLA_NOPE + MLA_ROPE)
    pad = jnp.zeros((r, MLA_HEADS, LANES - MLA_NOPE - MLA_ROPE), w.dtype)
    return _bf(jnp.concatenate([wh, pad], axis=-1).reshape(r, MLA_HEADS * LANES))


def _relayout_w_ukv(w):
    r = w.shape[0]
    wh = w.reshape(r, MLA_HEADS, MLA_NOPE + MLA_V)
    pad = jnp.zeros((r, MLA_HEADS, LANES - MLA_NOPE), w.dtype)
    wk = jnp.concatenate([wh[:, :, :MLA_NOPE], pad], axis=-1).reshape(r, MLA_HEADS * LANES)
    wv = wh[:, :, MLA_NOPE:].reshape(r, MLA_HEADS * MLA_V).T
    return _bf(wk), _bf(wv)


def _rope_angles(positions, dim):
    inv_freq = ROPE_BASE ** (-jnp.arange(0, dim, 2, dtype=F32) / dim)
    ang = positions.astype(F32).reshape(-1)[:, None] * inv_freq
    return jnp.cos(ang), jnp.sin(ang)


def _mla_rope_tables(positions):
    cos, sin = _rope_angles(positions, MLA_ROPE)
    t = cos.shape[0]
    z = lambda n: jnp.zeros((t, n), F32)
    ct = jnp.concatenate([jnp.ones((t, MLA_NOPE), F32), cos, cos, z(32)], axis=1)
    s1 = jnp.concatenate([z(MLA_NOPE), -sin, z(48)], axis=1)
    s2 = jnp.concatenate([z(MLA_NOPE + 16), sin, z(32)], axis=1)
    return jnp.concatenate([ct, s1, s2], axis=1)


def _ret_rope_tables(positions):
    cos, sin = _rope_angles(positions, RET_DK)
    return jnp.concatenate([cos, cos, -sin, sin], axis=1)


def _router_weights(w_group, b_group, w_expert, b_expert):
    d = w_group.shape[0]
    pad = LANES - N_EXPERTS - N_GROUPS
    w = jnp.concatenate([w_expert, w_group, jnp.zeros((d, pad), F32)], axis=1)
    b = jnp.concatenate([b_expert, b_group, jnp.zeros((pad,), F32)]).reshape(1, LANES)
    hi = _bf(w)
    return hi, _bf(w - hi.astype(F32)), b


def _pad_lanes(v):
    return jnp.zeros((LANES,), F32).at[:v.shape[0]].set(v)


def kernel(x, c, positions, ada_w, ada_b, ln_mix_g, ln_mix_b, ln_ffn_g, ln_ffn_b, hy_w_in, mla_q_norm, mla_w_uq, mla_kv_norm, mla_w_ukv, gdn_conv_w, gdn_a_log, gdn_dt_bias, gdn_norm, hy_w_out, ret_w_in, ret_gn_g, ret_gn_b, ret_w_out, moe_w_group, moe_b_group, moe_w_expert, moe_b_expert, moe_w_gate, moe_w_up, moe_w_down):
    bsz, seq, d = x.shape
    t = bsz * seq
    mods = _ada_mod(c, ada_w, ada_b)
    mla_tabs = _mla_rope_tables(positions)
    ret_tabs = _ret_rope_tables(positions)
    log_gamma = jnp.log(1.0 - 2.0 ** (-5.0 - jnp.arange(RET_HEADS, dtype=F32)))
    lg = jnp.broadcast_to(log_gamma[:, None, None], (RET_HEADS, 1, LANES))
    xc = x.reshape(t, d)
    for layer in range(DEPTH):
        i = layer // 2
        mod_l = mods[layer]
        if layer % 2 == 0:
            wuk_p, wuv = _relayout_w_ukv(mla_w_ukv[i])
            gparam = jnp.stack([_pad_lanes(gdn_a_log[i]), _pad_lanes(gdn_dt_bias[i])])
            (qm, km, vm, gq, gk, gv, gate, gcol, grow) = _even_front(
                xc, mod_l, _relayout_hy_w_in(hy_w_in[i]), mla_q_norm[i].reshape(1, -1), _relayout_w_uq(mla_w_uq[i]),
                mla_kv_norm[i].reshape(1, -1), wuk_p, wuv, gdn_conv_w[i], gparam, mla_tabs, seq)
            o_mla = _attention(qm, km, vm, bsz, seq)
            n_ch = t // GDN_CHUNK
            growp = grow[:GDN_HEADS].reshape(GDN_PAIRS, 2, n_ch, GDN_CHUNK).transpose(2, 0, 1, 3).reshape(n_ch, GDN_PAIRS, LANES)
            onorm2 = jnp.concatenate([gdn_norm[i], gdn_norm[i]]).reshape(1, LANES)
            y_gdn = _gdn(gq, gk, gv, gate, gcol, growp, onorm2, bsz, seq)
            acts = [o_mla, y_gdn]
            w_out = _bf(hy_w_out[i])
        else:
            q, k, v, gate = _odd_front(xc, mod_l, _bf(ret_w_in[i]), ret_tabs, seq)
            og = _retention(q, k, v, gate, lg, ret_gn_g[i].reshape(RET_HEADS, 1, RET_DV),
                            ret_gn_b[i].reshape(RET_HEADS, 1, RET_DV), bsz, seq)
            acts = [og]
            w_out = _bf(ret_w_out[i])
        wr_hi, wr_lo, br = _router_weights(moe_w_group[layer], moe_b_group[layer], moe_w_expert[layer], moe_b_expert[layer])
        x1, h2, route, counts = _mixer_back(acts, w_out, xc, mod_l, ln_mix_g[layer].reshape(1, d), ln_mix_b[layer].reshape(1, d),
                                    wr_hi, wr_lo, br, seq)
        slot, block_expert, n_used, pad, n_slots = _slot_tables(route, counts, t)
        xb = _scatter_rows(slot, pad, h2, n_slots, seq)
        yb = _moe_experts(block_expert, n_used, xb, moe_w_gate, moe_w_up, moe_w_down, layer)
        xc = _moe_back(slot, x1, yb, route, mod_l, ln_ffn_g[layer].reshape(1, d), ln_ffn_b[layer].reshape(1, d), seq)
    return xc.reshape(bsz, seq, d)
```

```python
import functools

import jax
import jax.numpy as jnp
import numpy as np
from jax import lax
from jax.experimental import pallas as pl
from jax.experimental.pallas import tpu as pltpu

F32 = jnp.float32
BF16 = jnp.bfloat16

DEPTH = 4
N_MOD = 6
LANES = 128

MLA_HEADS = 8
MLA_NOPE = 64
MLA_ROPE = 32
MLA_V = 64
MLA_Q_RANK = 256
MLA_KV_RANK = 128
MLA_SCALE = (MLA_NOPE + MLA_ROPE) ** -0.5
LOG2_E = 1.4426950408889634

GDN_HEADS = 8
GDN_DK = 64
GDN_DV = 64
GDN_CONV = 4
GDN_CHUNK = 64
GDN_HD = GDN_HEADS * GDN_DK
GDN_QKV = 3 * GDN_HD
GDN_PAIRS = GDN_HEADS // 2
GDN_GROUP = 4

RET_HEADS = 8
RET_DK = 128
RET_DV = 256
RET_CHUNK = 256

N_GROUPS = 4
EXPERTS_PER_GROUP = 8
N_EXPERTS = 32
EXPERT_FF = 512
MOE_BLOCK = 512

ROPE_BASE = 10000.0
LN_EPS = 1e-5
RMS_EPS = 1e-6
DEEPNORM_ALPHA = (2.0 * DEPTH) ** 0.25

ROW_TILE = 256
ATTN_TILE = 512
VMEM_LIMIT = 56 * 1024 * 1024

_C_MLA = 0
_C_QKV = 512
_C_GATE = 2048
_C_AB = 2560
_C_END = 2688


def _bf(x):
    return x.astype(BF16)


def _dot(a, b):
    return jnp.dot(a, b, preferred_element_type=F32)


def _dot_nt(a, b):
    return lax.dot_general(a, b, (((1,), (1,)), ((), ())), preferred_element_type=F32)


def _dot_tn(a, b):
    return lax.dot_general(a, b, (((0,), (0,)), ((), ())), preferred_element_type=F32)


def _split2(x):
    hi = _bf(x)
    return hi, _bf(x - hi.astype(F32))


def _split3(x):
    hi = _bf(x)
    r = x - hi.astype(F32)
    mid = _bf(r)
    return hi, mid, _bf(r - mid.astype(F32))


def _silu(x):
    return x * jax.nn.sigmoid(x)


def _layer_norm(z, g, b):
    mu = jnp.mean(z, -1, keepdims=True)
    zc = z - mu
    var = jnp.mean(zc * zc, -1, keepdims=True)
    return zc * lax.rsqrt(var + LN_EPS) * g + b


def _rms_norm(x, g):
    return x * lax.rsqrt(jnp.mean(x * x, -1, keepdims=True) + RMS_EPS) * g


def _params(sem):
    return pltpu.CompilerParams(dimension_semantics=sem, vmem_limit_bytes=VMEM_LIMIT)


def _const_spec(shape):
    nd = len(shape)
    return pl.BlockSpec(shape, lambda *_: (0,) * nd)


def _ada_kernel(c_ref, w_ref, b_ref, o_ref):
    c = c_ref[...]
    o_ref[...] = _dot(_bf(_silu(c)), _bf(w_ref[...])) + b_ref[...]


def _ada_mod(c, ada_w, ada_b):
    depth, d, n = ada_w.shape
    bsz = c.shape[0]
    rows = 16
    tn = 1536
    cp = jnp.zeros((rows, d), F32).at[:bsz].set(c)
    out = pl.pallas_call(
        _ada_kernel,
        grid=(depth, n // tn),
        in_specs=[pl.BlockSpec((rows, d), lambda l, j: (0, 0)),
                  pl.BlockSpec((None, d, tn), lambda l, j: (l, 0, j)),
                  pl.BlockSpec((None, 1, tn), lambda l, j: (l, 0, j))],
        out_specs=pl.BlockSpec((None, rows, tn), lambda l, j: (l, 0, j)),
        out_shape=jax.ShapeDtypeStruct((depth, rows, n), F32),
        compiler_params=_params(("parallel", "parallel")),
    )(cp, ada_w, ada_b.reshape(depth, 1, n))
    return out[:, :bsz].reshape(depth, bsz, N_MOD, d)


def _even_front_kernel(x_ref, mod_ref, w_ref, qn_ref, wuq_ref, kvn_ref, wuk_ref, wuv_ref,
                       cw_ref, gp_ref, tab_ref,
                       q_ref, k_ref, v_ref, gq_ref, gk_ref, gv_ref, gate_ref, gcol_ref, grow_ref,
                       xbuf, *, tm, tiles_per_seq):
    i = pl.program_id(0)
    hb = _bf(x_ref[...] * (1.0 + mod_ref[1:2, :]) + mod_ref[0:1, :])

    pm = _dot(hb, w_ref[:, _C_MLA:_C_QKV])
    ct = tab_ref[:, 0:LANES]
    s1 = tab_ref[:, LANES:2 * LANES]
    s2 = tab_ref[:, 2 * LANES:3 * LANES]

    def rope(b):
        return b * ct + pltpu.roll(b, LANES - 16, 1) * s1 + pltpu.roll(b, 16, 1) * s2

    qa = _dot(_bf(_rms_norm(pm[:, 0:MLA_Q_RANK], qn_ref[...])), wuq_ref[...])
    for h in range(MLA_HEADS):
        sl = slice(h * LANES, (h + 1) * LANES)
        q_ref[:, sl] = _bf(rope(qa[:, sl]) * (MLA_SCALE * LOG2_E))
    kvn = _bf(_rms_norm(pm[:, MLA_Q_RANK:MLA_Q_RANK + MLA_KV_RANK], kvn_ref[...]))
    ka = _dot(kvn, wuk_ref[...])
    kr = rope(pm[:, MLA_Q_RANK + MLA_KV_RANK:_C_QKV])
    for h in range(MLA_HEADS):
        sl = slice(h * LANES, (h + 1) * LANES)
        k_ref[:, sl] = _bf(ka[:, sl] + kr)
    v_ref[...] = _bf(_dot_nt(wuv_ref[...], kvn))

    @pl.when(i % tiles_per_seq == 0)
    def _():
        xbuf[0:8, :] = jnp.zeros((8, GDN_QKV), F32)

    xbuf[8:8 + tm, :] = _dot(hb, w_ref[:, _C_QKV:_C_GATE])
    cw = cw_ref[...]
    y = cw[0:1, :] * xbuf[5:5 + tm, :]
    for t in range(1, GDN_CONV):
        y = y + cw[t:t + 1, :] * xbuf[5 + t:5 + t + tm, :]
    xbuf[0:8, :] = xbuf[tm:tm + 8, :]
    y = _silu(y)

    r = lax.broadcasted_iota(jnp.int32, (GDN_HD, GDN_HD), 0) // GDN_DK
    c = lax.broadcasted_iota(jnp.int32, (GDN_HD, GDN_HD), 1) // GDN_DK
    head_ones = jnp.where(r == c, 1.0, 0.0).astype(BF16)

    def head_sumsq(z):
        return _dot(_bf(z * z), head_ones)

    qg = y[:, 0:GDN_HD]
    kg = y[:, GDN_HD:2 * GDN_HD]
    gq_ref[...] = qg * lax.rsqrt(head_sumsq(qg) + RMS_EPS) * (GDN_DK ** -0.5)
    gk_ref[...] = kg * lax.rsqrt(head_sumsq(kg) + RMS_EPS)
    gv_ref[...] = y[:, 2 * GDN_HD:3 * GDN_HD]
    gate_ref[...] = _dot(hb, w_ref[:, _C_GATE:_C_AB])

    ab = _dot(hb, w_ref[:, _C_AB:_C_END])
    lane = lax.broadcasted_iota(jnp.int32, (tm, LANES), 1)
    z = ab + gp_ref[1:2, :]
    softplus = jnp.maximum(z, 0.0) + jnp.log(1.0 + jnp.exp(-jnp.abs(z)))
    la = jnp.where(lane < GDN_HEADS, -jnp.exp(gp_ref[0:1, :]) * softplus, 0.0)
    rr = lax.broadcasted_iota(jnp.int32, (tm, tm), 0)
    cc = lax.broadcasted_iota(jnp.int32, (tm, tm), 1)
    tri = jnp.where((rr // GDN_CHUNK == cc // GDN_CHUNK) & (cc <= rr), 1.0, 0.0).astype(BF16)
    l0, l1, l2 = _split3(la)
    g = _dot(tri, l0) + _dot(tri, l1) + _dot(tri, l2)
    gcol = jnp.where(lane < GDN_HEADS, g, jax.nn.sigmoid(ab))
    gcol_ref[...] = gcol
    grow_ref[...] = gcol.T[0:16, :]


def _even_front(x2, mod_l, w_in_p, qn, wuq_p, kvn, wuk_p, wuv, conv_w, gparam, tabs, seq):
    t, d = x2.shape
    tm = min(ROW_TILE, seq)
    tiles_per_seq = seq // tm
    row = lambda n: pl.BlockSpec((tm, n), lambda i: (i, 0))
    out_shapes = (
        jax.ShapeDtypeStruct((t, MLA_HEADS * LANES), BF16),
        jax.ShapeDtypeStruct((t, MLA_HEADS * LANES), BF16),
        jax.ShapeDtypeStruct((MLA_HEADS * MLA_V, t), BF16),
        jax.ShapeDtypeStruct((t, GDN_HD), F32),
        jax.ShapeDtypeStruct((t, GDN_HD), F32),
        jax.ShapeDtypeStruct((t, GDN_HD), F32),
        jax.ShapeDtypeStruct((t, GDN_HD), F32),
        jax.ShapeDtypeStruct((t, LANES), F32),
        jax.ShapeDtypeStruct((16, t), F32),
    )
    return pl.pallas_call(
        functools.partial(_even_front_kernel, tm=tm, tiles_per_seq=tiles_per_seq),
        grid=(t // tm,),
        in_specs=[row(d),
                  pl.BlockSpec((None, N_MOD, d), lambda i: (i // tiles_per_seq, 0, 0)),
                  _const_spec(w_in_p.shape), _const_spec(qn.shape), _const_spec(wuq_p.shape),
                  _const_spec(kvn.shape), _const_spec(wuk_p.shape), _const_spec(wuv.shape),
                  _const_spec(conv_w.shape), _const_spec(gparam.shape), row(3 * LANES)],
        out_specs=(row(MLA_HEADS * LANES), row(MLA_HEADS * LANES),
                   pl.BlockSpec((MLA_HEADS * MLA_V, tm), lambda i: (0, i)),
                   row(GDN_HD), row(GDN_HD), row(GDN_HD), row(GDN_HD), row(LANES),
                   pl.BlockSpec((16, tm), lambda i: (0, i))),
        out_shape=out_shapes,
        scratch_shapes=[pltpu.VMEM((tm + 8, GDN_QKV), F32)],
        compiler_params=_params(("arbitrary",)),
    )(x2, mod_l, w_in_p, qn, wuq_p, kvn, wuk_p, wuv, conv_w, gparam, tabs)


def _attn_kernel(qi_ref, ki_ref, q_ref, k_ref, vt_ref, o_ref, m_ref, l_ref, acc_ref, *, tq):
    qi = qi_ref[pl.program_id(1)]
    ki = ki_ref[pl.program_id(1)]

    @pl.when(ki == 0)
    def _():
        m_ref[...] = jnp.full(m_ref.shape, -jnp.inf, F32)
        l_ref[...] = jnp.zeros(l_ref.shape, F32)
        acc_ref[...] = jnp.zeros(acc_ref.shape, F32)

    def scores(h):
        sl = slice(h * LANES, (h + 1) * LANES)
        return _dot_nt(k_ref[:, sl], q_ref[:, sl])

    def step(diagonal):
        if diagonal:
            key = lax.broadcasted_iota(jnp.int32, (tq, tq), 0)
            qry = lax.broadcasted_iota(jnp.int32, (tq, tq), 1)
            keep = key <= qry
        s_next = scores(0)
        for h in range(MLA_HEADS):
            s = s_next
            if h + 1 < MLA_HEADS:
                s_next = scores(h + 1)
            if diagonal:
                s = jnp.where(keep, s, -jnp.inf)
            rows = slice(h * MLA_V, (h + 1) * MLA_V)
            m_prev = m_ref[h:h + 1, :]
            m_new = jnp.maximum(m_prev, jnp.max(s, 0, keepdims=True))
            alpha = jnp.exp2(m_prev - m_new)
            p = jnp.exp2(s - m_new)
            l_ref[h:h + 1, :] = alpha * l_ref[h:h + 1, :] + jnp.sum(p, 0, keepdims=True)
            acc_ref[rows, :] = alpha * acc_ref[rows, :] + _dot(vt_ref[rows, :], _bf(p))
            m_ref[h:h + 1, :] = m_new

    @pl.when(ki < qi)
    def _():
        step(False)

    @pl.when(ki == qi)
    def _():
        step(True)
        for h in range(MLA_HEADS):
            rows = slice(h * MLA_V, (h + 1) * MLA_V)
            acc_ref[rows, :] = acc_ref[rows, :] / l_ref[h:h + 1, :]
        o_ref[...] = _bf(acc_ref[...].T)


def _attention(qm, km, vm, bsz, seq):
    t = qm.shape[0]
    tq = min(ATTN_TILE, seq)
    nq = seq // tq
    pairs = [(q, k) for q in range(nq) for k in range(q + 1)]
    qi_tab = jnp.asarray(np.array([p[0] for p in pairs], np.int32))
    ki_tab = jnp.asarray(np.array([p[1] for p in pairs], np.int32))
    grid_spec = pltpu.PrefetchScalarGridSpec(
        num_scalar_prefetch=2,
        grid=(bsz, len(pairs)),
        in_specs=[pl.BlockSpec((tq, MLA_HEADS * LANES), lambda b, s, qt, kt: (b * nq + qt[s], 0)),
                  pl.BlockSpec((tq, MLA_HEADS * LANES), lambda b, s, qt, kt: (b * nq + kt[s], 0)),
                  pl.BlockSpec((MLA_HEADS * MLA_V, tq), lambda b, s, qt, kt: (0, b * nq + kt[s]))],
        out_specs=pl.BlockSpec((tq, MLA_HEADS * MLA_V), lambda b, s, qt, kt: (b * nq + qt[s], 0)),
        scratch_shapes=[pltpu.VMEM((MLA_HEADS, tq), F32),
                        pltpu.VMEM((MLA_HEADS, tq), F32),
                        pltpu.VMEM((MLA_HEADS * MLA_V, tq), F32)],
    )
    return pl.pallas_call(
        functools.partial(_attn_kernel, tq=tq),
        grid_spec=grid_spec,
        out_shape=jax.ShapeDtypeStruct((t, MLA_HEADS * MLA_V), BF16),
        compiler_params=_params(("parallel", "arbitrary")),
    )(qi_tab, ki_tab, qm, km, vm)


def _gdn_kernel(q_ref, k_ref, v_ref, gate_ref, gcol_ref, grow_ref, on_ref, y_ref, state_ref, *, n_chunks):
    C = GDN_CHUNK

    @pl.when(pl.program_id(1) == 0)
    def _():
        state_ref[...] = jnp.zeros(state_ref.shape, F32)

    lane = lax.broadcasted_iota(jnp.int32, (C, LANES), 1)
    rowi = lax.broadcasted_iota(jnp.int32, (C, LANES), 0)
    first = lane < GDN_DK
    col = jnp.where(first, lane, lane - GDN_DK)
    tril = col <= rowi
    strict = col < rowi
    eye = jnp.where(col == rowi, 1.0, 0.0)
    blk16 = (col // 16) == (rowi // 16)
    blk32 = (col // 32) == (rowi // 32)
    r2 = lax.broadcasted_iota(jnp.int32, (LANES, LANES), 0)
    c2 = lax.broadcasted_iota(jnp.int32, (LANES, LANES), 1)
    first_rows = r2 < GDN_DK
    head_ones = jnp.where((r2 < GDN_DK) == (c2 < GDN_DK), 1.0, 0.0).astype(BF16)

    def bd(y):
        return jnp.concatenate([jnp.where(first, y, 0), jnp.where(first, 0, y)], axis=0)

    def pmm(x, y):
        return _dot(_bf(x), bd(_bf(y)))

    def pmm_hi(x, y):
        xh, xl = _split2(x)
        yh, yl = _split2(y)
        top = _dot(jnp.concatenate([xh, xl], axis=0), bd(yh))
        return top[:C] + top[C:] + _dot(xh, bd(yl))

    def pair_cols(a, c0, c1):
        return jnp.where(first, a[:, c0:c0 + 1], a[:, c1:c1 + 1])

    def chunk_group(gi, carry):
        pairs = range(GDN_PAIRS)
        items = [(c, p) for c in range(GDN_GROUP) for p in pairs]
        sls = [slice(p * LANES, (p + 1) * LANES) for p in pairs]
        rows = [pl.ds(pl.multiple_of((gi * GDN_GROUP + c) * C, C), C) for c in range(GDN_GROUP)]
        gc = [gcol_ref[rows[c], :] for c in range(GDN_GROUP)]
        gr = [grow_ref[gi * GDN_GROUP + c] for c in range(GDN_GROUP)]
        every = range(len(items))
        q = [q_ref[rows[c], sls[p]] for c, p in items]
        k = [k_ref[rows[c], sls[p]] for c, p in items]
        v = [v_ref[rows[c], sls[p]] for c, p in items]
        g = [pair_cols(gc[c], 2 * p, 2 * p + 1) for c, p in items]
        beta = [pair_cols(gc[c], GDN_HEADS + 2 * p, GDN_HEADS + 2 * p + 1) for c, p in items]
        decay = [jnp.where(tril, jnp.exp(jnp.where(tril, g[i] - gr[c][p:p + 1, :], 0.0)), 0.0)
                 for i, (c, p) in enumerate(items)]
        kb = [k[i] * beta[i] for i in every]
        kbd = [bd(_bf(k[i])) for i in every]
        lmat = [jnp.where(strict, _dot_nt(_bf(kb[i]), kbd[i]) * decay[i], 0.0) for i in every]
        a_intra = [jnp.where(tril, _dot_nt(_bf(q[i]), kbd[i]) * decay[i], 0.0) for i in every]
        n1 = [jnp.where(blk16, lmat[i], 0.0) for i in every]
        x = [eye - n1[i] for i in every]
        nk = n1
        for mm in (pmm_hi, pmm, pmm):
            nk = [mm(nk[i], nk[i]) for i in every]
            x = [x[i] + mm(x[i], nk[i]) for i in every]
        for off in ([jnp.where(blk32 & jnp.logical_not(blk16), lmat[i], 0.0) for i in every],
                    [jnp.where(blk32, 0.0, lmat[i]) for i in every]):
            xb = [pmm(x[i], off[i]) for i in every]
            x = [x[i] - pmm(xb[i], x[i]) for i in every]
        u = [pmm(x[i], v[i] * beta[i]) for i in every]
        w = [pmm(x[i], kb[i] * jnp.exp(g[i])) for i in every]
        for c in range(GDN_GROUP):
            idx = [c * GDN_PAIRS + p for p in pairs]
            state = [state_ref[p] for p in pairs]
            sb = [_bf(state[p]) for p in pairs]
            v_new = [u[i] - _dot(_bf(w[i]), sb[p]) for p, i in enumerate(idx)]
            o = [_dot(_bf(q[i] * jnp.exp(g[i])), sb[p]) + pmm(a_intra[i], v_new[p]) for p, i in enumerate(idx)]
            for p, i in enumerate(idx):
                g_last = g[i][C - 1:C, :]
                k_dec = k[i] * jnp.exp(g_last - g[i])
                scale = jnp.exp(jnp.where(first_rows, g_last[:, 0:1], g_last[:, GDN_DK:GDN_DK + 1]))
                state_ref[p] = state[p] * scale + _dot_tn(bd(_bf(k_dec)), bd(_bf(v_new[p])))
            for p in pairs:
                ms = _dot(_bf(o[p] * o[p]), head_ones) * (1.0 / GDN_DV)
                on = o[p] * lax.rsqrt(ms + RMS_EPS) * on_ref[...]
                y_ref[rows[c], sls[p]] = _bf(on * _silu(gate_ref[rows[c], sls[p]]))
        return carry

    lax.fori_loop(0, n_chunks // GDN_GROUP, chunk_group, 0)


def _gdn(gq, gk, gv, gate, gcol, growp, onorm2, bsz, seq):
    t = gq.shape[0]
    r = min(ROW_TILE, seq)
    nblk = seq // r
    n_chunks = r // GDN_CHUNK
    row = lambda n: pl.BlockSpec((r, n), lambda b, j: (b * nblk + j, 0))
    return pl.pallas_call(
        functools.partial(_gdn_kernel, n_chunks=n_chunks),
        grid=(bsz, nblk),
        in_specs=[row(GDN_HD), row(GDN_HD), row(GDN_HD), row(GDN_HD), row(LANES),
                  pl.BlockSpec((n_chunks, GDN_PAIRS, LANES), lambda b, j: (b * nblk + j, 0, 0)),
                  _const_spec(onorm2.shape)],
        out_specs=row(GDN_HD),
        out_shape=jax.ShapeDtypeStruct((t, GDN_HD), BF16),
        scratch_shapes=[pltpu.VMEM((GDN_PAIRS, LANES, LANES), F32)],
        compiler_params=_params(("parallel", "arbitrary")),
    )(gq, gk, gv, gate, gcol, growp, onorm2)


def _back_kernel(*refs, n_in):
    a_refs = refs[:n_in]
    (w_ref, x_ref, mod_ref, lng_ref, lnb_ref, wrh_ref, wrl_ref, br_ref,
     x1_ref, h2_ref, route_ref, counts_ref, base_ref) = refs[n_in:]
    off = 0
    y = None
    for a_ref in a_refs:
        kk = a_ref.shape[1]
        part = _dot(a_ref[...], w_ref[off:off + kk, :])
        y = part if y is None else y + part
        off += kk
    z = DEEPNORM_ALPHA * x_ref[...] + (1.0 + mod_ref[2:3, :]) * y
    x1 = _layer_norm(z, lng_ref[...], lnb_ref[...])
    x1_ref[...] = x1
    h2 = x1 * (1.0 + mod_ref[4:5, :]) + mod_ref[3:4, :]
    h2_ref[...] = h2

    hh, hl = _split2(h2)
    logits = _dot(hh, wrh_ref[...]) + _dot(hh, wrl_ref[...]) + _dot(hl, wrh_ref[...]) + br_ref[...]
    lane = lax.broadcasted_iota(jnp.int32, logits.shape, 1)
    lane_f = lane.astype(F32)
    neg = -jnp.inf
    gmask = (lane >= N_EXPERTS) & (lane < N_EXPERTS + N_GROUPS)
    gl = jnp.where(gmask, logits, neg)
    gmax = jnp.max(gl, -1, keepdims=True)
    g_top = 1.0 / jnp.sum(jnp.where(gmask, jnp.exp(gl - gmax), 0.0), -1, keepdims=True)
    g_lane = jnp.min(jnp.where(gl == gmax, lane_f, 1e9), -1, keepdims=True)
    g_idx = g_lane.astype(jnp.int32) - N_EXPERTS
    emask = (lane < N_EXPERTS) & ((lane // EXPERTS_PER_GROUP) == g_idx)
    el = jnp.where(emask, logits, neg)
    m1 = jnp.max(el, -1, keepdims=True)
    i1 = jnp.min(jnp.where(el == m1, lane_f, 1e9), -1, keepdims=True)
    el2 = jnp.where(lane_f == i1, neg, el)
    m2 = jnp.max(el2, -1, keepdims=True)
    i2 = jnp.min(jnp.where(el2 == m2, lane_f, 1e9), -1, keepdims=True)
    e2 = jnp.exp(m2 - m1)
    w1 = g_top / (1.0 + e2)
    w2 = g_top * e2 / (1.0 + e2)

    @pl.when(pl.program_id(0) == 0)
    def _():
        base_ref[...] = jnp.zeros(base_ref.shape, F32)

    tm = logits.shape[0]
    rr = lax.broadcasted_iota(jnp.int32, (tm, tm), 0)
    cc = lax.broadcasted_iota(jnp.int32, (tm, tm), 1)
    tri = jnp.where(cc <= rr, 1.0, 0.0).astype(BF16)
    oh1 = jnp.where(lane_f == i1, 1.0, 0.0)
    oh2 = jnp.where(lane_f == i2, 1.0, 0.0)
    cs1 = _dot(tri, _bf(oh1))
    cs2 = _dot(tri, _bf(oh2))
    base = base_ref[...]
    tot1 = cs1[tm - 1:tm, :]
    r1 = jnp.sum(oh1 * (base + cs1), -1, keepdims=True) - 1.0
    r2 = jnp.sum(oh2 * (base + tot1 + cs2), -1, keepdims=True) - 1.0
    total = base + tot1 + cs2[tm - 1:tm, :]
    base_ref[...] = total
    counts_ref[...] = total
    route = jnp.where(lane == 0, i1, jnp.where(lane == 1, i2, jnp.where(lane == 2, w1, jnp.where(lane == 3, w2, 0.0))))
    route_ref[...] = jnp.where(lane == 4, r1, jnp.where(lane == 5, r2, route))


def _mixer_back(acts, w_out, x2, mod_l, ln_g, ln_b, wr_hi, wr_lo, br, seq):
    t, d = x2.shape
    tm = min(ROW_TILE, seq)
    tiles_per_seq = seq // tm
    row = lambda n: pl.BlockSpec((tm, n), lambda i: (i, 0))
    return pl.pallas_call(
        functools.partial(_back_kernel, n_in=len(acts)),
        grid=(t // tm,),
        in_specs=[row(a.shape[1]) for a in acts] + [
            _const_spec(w_out.shape), row(d),
            pl.BlockSpec((None, N_MOD, d), lambda i: (i // tiles_per_seq, 0, 0)),
            _const_spec(ln_g.shape), _const_spec(ln_b.shape),
            _const_spec(wr_hi.shape), _const_spec(wr_lo.shape), _const_spec(br.shape)],
        out_specs=(row(d), row(d), row(LANES), _const_spec((1, LANES))),
        out_shape=(jax.ShapeDtypeStruct((t, d), F32), jax.ShapeDtypeStruct((t, d), F32),
                   jax.ShapeDtypeStruct((t, LANES), F32), jax.ShapeDtypeStruct((1, LANES), F32)),
        scratch_shapes=[pltpu.VMEM((1, LANES), F32)],
        compiler_params=_params(("arbitrary",)),
    )(*acts, w_out, x2, mod_l, ln_g, ln_b, wr_hi, wr_lo, br)


def _moe_kernel(be_ref, nb_ref, x_ref, wg_ref, wu_ref, wd_ref, y_ref, wg_s, wu_s, wd_s):
    i = pl.program_id(0)

    @pl.when(i < nb_ref[0])
    def _():
        prev = be_ref[jnp.maximum(i - 1, 0)]

        @pl.when((i == 0) | (be_ref[i] != prev))
        def _():
            wg_s[...] = _bf(wg_ref[...])
            wu_s[...] = _bf(wu_ref[...])
            wd_s[...] = _bf(wd_ref[...])

        x = _bf(x_ref[...])
        hid = _silu(_dot(x, wg_s[...])) * _dot(x, wu_s[...])
        y_ref[...] = _dot(_bf(hid), wd_s[...])

    @pl.when(i >= nb_ref[0])
    def _():
        y_ref[...] = jnp.zeros(y_ref.shape, F32)


def _moe_experts(block_expert, n_used, xb, w_gate, w_up, w_down, layer):
    n_slots, d = xb.shape
    ff = w_gate.shape[-1]
    n_blocks = n_slots // MOE_BLOCK
    grid_spec = pltpu.PrefetchScalarGridSpec(
        num_scalar_prefetch=2,
        grid=(n_blocks,),
        in_specs=[pl.BlockSpec((MOE_BLOCK, d), lambda i, be, nb: (jnp.minimum(i, nb[0] - 1), 0)),
                  pl.BlockSpec((None, None, d, ff), lambda i, be, nb: (layer, be[i], 0, 0)),
                  pl.BlockSpec((None, None, d, ff), lambda i, be, nb: (layer, be[i], 0, 0)),
                  pl.BlockSpec((None, None, ff, d), lambda i, be, nb: (layer, be[i], 0, 0))],
        out_specs=pl.BlockSpec((MOE_BLOCK, d), lambda i, be, nb: (i, 0)),
        scratch_shapes=[pltpu.VMEM((d, ff), BF16), pltpu.VMEM((d, ff), BF16), pltpu.VMEM((ff, d), BF16)],
    )
    return pl.pallas_call(
        _moe_kernel,
        grid_spec=grid_spec,
        out_shape=jax.ShapeDtypeStruct((n_slots, d), F32),
        compiler_params=_params(("arbitrary",)),
    )(block_expert, n_used, xb, w_gate, w_up, w_down)


def _slot_tables(route, counts_row, t):
    counts = counts_row[0, :N_EXPERTS].astype(jnp.int32)
    padded = (counts + MOE_BLOCK - 1) // MOE_BLOCK * MOE_BLOCK
    pend = jnp.cumsum(padded)
    pstart = pend - padded
    e = route[:, 0:2].astype(jnp.int32)
    rank = route[:, 4:6].astype(jnp.int32)
    slot = (jnp.take(pstart, e) + rank).reshape(2 * t)
    n_blocks = 2 * t // MOE_BLOCK + N_EXPERTS
    block_start = jnp.arange(n_blocks, dtype=jnp.int32) * MOE_BLOCK
    block_expert = jnp.minimum(jnp.sum((block_start[:, None] >= pend[None, :]).astype(jnp.int32), axis=1), N_EXPERTS - 1)
    n_used = (pend[-1] // MOE_BLOCK).astype(jnp.int32).reshape(1)
    n_slots = n_blocks * MOE_BLOCK
    tail = jnp.stack([pend[-1], (n_slots - pend[-1]) // (MOE_BLOCK // 2)])
    pad = jnp.concatenate([pstart + counts, padded - counts, tail]).astype(jnp.int32)
    return slot, block_expert, n_used, pad, n_slots


def _scatter_kernel(slot_ref, pad_ref, h_ref, xb_ref, stage_ref, zero_ref, sems, zsem, *, tm, n_steps):
    i = pl.program_id(0)

    def pad_copies(fn):
        def rows_at(pos, count, size):
            def body(j, c):
                at = pos + j * size
                if size > 1:
                    at = pl.multiple_of(at, size)
                fn(pltpu.make_async_copy(zero_ref.at[pl.ds(0, size)], xb_ref.at[pl.ds(at, size)], zsem))
                return c

            lax.fori_loop(0, count, body, 0)

        def per_expert(e, carry):
            start = pad_ref[e]
            length = pad_ref[N_EXPERTS + e]
            head = jnp.minimum((-start) & 7, length)
            rows_at(start, head, 1)
            rows_at(start + head, (length - head) >> 3, 8)
            rows_at(start + head + ((length - head) & ~7), (length - head) & 7, 1)
            return carry

        lax.fori_loop(0, N_EXPERTS, per_expert, 0)
        rows_at(pad_ref[2 * N_EXPERTS], pad_ref[2 * N_EXPERTS + 1], zero_ref.shape[0])

    @pl.when(i == 0)
    def _():
        zero_ref[...] = jnp.zeros(zero_ref.shape, F32)
        pad_copies(lambda c: c.start())
        pad_copies(lambda c: c.wait())

    def start_rows(step):
        buf = step % 2
        for r in range(tm):
            for choice in range(2):
                dst = slot_ref[2 * (step * tm + r) + choice]
                pltpu.make_async_copy(stage_ref.at[buf, pl.ds(r, 1)], xb_ref.at[pl.ds(dst, 1)], sems.at[buf]).start()

    def wait_rows(step):
        buf = step % 2
        for _ in range(2):
            pltpu.make_async_copy(stage_ref.at[buf], xb_ref.at[pl.ds(0, tm)], sems.at[buf]).wait()

    @pl.when(i >= 2)
    def _():
        wait_rows(i - 2)

    stage_ref[i % 2] = h_ref[...]
    start_rows(i)

    @pl.when(i == n_steps - 1)
    def _():
        @pl.when(i >= 1)
        def _():
            wait_rows(i - 1)

        wait_rows(i)


def _scatter_rows(slot, pad, h2, n_slots, seq):
    t, d = h2.shape
    tm = min(ROW_TILE, seq)
    n_steps = t // tm
    grid_spec = pltpu.PrefetchScalarGridSpec(
        num_scalar_prefetch=2,
        grid=(n_steps,),
        in_specs=[pl.BlockSpec((tm, d), lambda i, sl, pd: (i, 0))],
        out_specs=pl.BlockSpec(memory_space=pl.ANY),
        scratch_shapes=[pltpu.VMEM((2, tm, d), F32), pltpu.VMEM((MOE_BLOCK // 2, d), F32),
                        pltpu.SemaphoreType.DMA((2,)), pltpu.SemaphoreType.DMA(())],
    )
    return pl.pallas_call(
        functools.partial(_scatter_kernel, tm=tm, n_steps=n_steps),
        grid_spec=grid_spec,
        out_shape=jax.ShapeDtypeStruct((n_slots, d), F32),
        compiler_params=_params(("arbitrary",)),
    )(slot, pad, h2)


def _moe_back_kernel(slot_ref, x_ref, route_ref, mod_ref, lng_ref, lnb_ref, yb_ref, o_ref, ybuf, sems,
                     *, tm, n_steps):
    i = pl.program_id(0)

    def start_rows(tile, buf):
        for r in range(tm):
            for choice in range(2):
                src = slot_ref[2 * (tile * tm + r) + choice]
                pltpu.make_async_copy(yb_ref.at[pl.ds(src, 1)], ybuf.at[buf, choice, pl.ds(r, 1)], sems.at[buf]).start()

    def wait_rows(buf):
        for choice in range(2):
            pltpu.make_async_copy(yb_ref.at[pl.ds(0, tm)], ybuf.at[buf, choice], sems.at[buf]).wait()

    buf = i % 2

    @pl.when(i == 0)
    def _():
        start_rows(0, 0)

    wait_rows(buf)
    start_rows(jnp.minimum(i + 1, n_steps - 1), 1 - buf)
    rt = route_ref[...]
    y = ybuf[buf, 0] * rt[:, 2:3] + ybuf[buf, 1] * rt[:, 3:4]
    z = DEEPNORM_ALPHA * x_ref[...] + (1.0 + mod_ref[5:6, :]) * y
    o_ref[...] = _layer_norm(z, lng_ref[...], lnb_ref[...])

    @pl.when(i == n_steps - 1)
    def _():
        wait_rows(1 - buf)


def _moe_back(slot, x1, yb, route, mod_l, ln_g, ln_b, seq):
    t, d = x1.shape
    tm = min(ROW_TILE, seq)
    tiles_per_seq = seq // tm
    n_steps = t // tm
    row = lambda n: pl.BlockSpec((tm, n), lambda i, sl: (i, 0))
    const = lambda shape: pl.BlockSpec(shape, lambda i, sl: (0,) * len(shape))
    grid_spec = pltpu.PrefetchScalarGridSpec(
        num_scalar_prefetch=1,
        grid=(n_steps,),
        in_specs=[row(d), row(LANES),
                  pl.BlockSpec((None, N_MOD, d), lambda i, sl: (i // tiles_per_seq, 0, 0)),
                  const(ln_g.shape), const(ln_b.shape), pl.BlockSpec(memory_space=pl.ANY)],
        out_specs=row(d),
        scratch_shapes=[pltpu.VMEM((2, 2, tm, d), F32), pltpu.SemaphoreType.DMA((2,))],
    )
    return pl.pallas_call(
        functools.partial(_moe_back_kernel, tm=tm, n_steps=n_steps),
        grid_spec=grid_spec,
        out_shape=jax.ShapeDtypeStruct((t, d), F32),
        compiler_params=_params(("arbitrary",)),
    )(slot, x1, route, mod_l, ln_g, ln_b, yb)


def _odd_front_kernel(x_ref, mod_ref, w_ref, tab_ref, q_ref, k_ref, v_ref, gate_ref):
    hb = _bf(x_ref[...] * (1.0 + mod_ref[1:2, :]) + mod_ref[0:1, :])
    c2 = tab_ref[:, 0:LANES]
    s2 = tab_ref[:, LANES:2 * LANES]
    nqk = RET_HEADS * RET_DK
    nv = RET_HEADS * RET_DV

    def rope(b):
        return b * c2 + pltpu.roll(b, RET_DK // 2, 1) * s2

    qa = _dot(hb, w_ref[:, 0:nqk])
    ka = _dot(hb, w_ref[:, nqk:2 * nqk])
    for h in range(RET_HEADS):
        sl = slice(h * LANES, (h + 1) * LANES)
        q_ref[:, sl] = _bf(rope(qa[:, sl]) * (RET_DK ** -0.5))
        k_ref[:, sl] = _bf(rope(ka[:, sl]))
    v_ref[...] = _bf(_dot(hb, w_ref[:, 2 * nqk:2 * nqk + nv]))
    gate_ref[...] = _silu(_dot(hb, w_ref[:, 2 * nqk + nv:2 * nqk + 2 * nv]))


def _odd_front(x2, mod_l, w_in, tabs, seq):
    t, d = x2.shape
    tm = min(ROW_TILE, seq)
    tiles_per_seq = seq // tm
    nqk = RET_HEADS * RET_DK
    nv = RET_HEADS * RET_DV
    row = lambda n: pl.BlockSpec((tm, n), lambda i: (i, 0))
    return pl.pallas_call(
        _odd_front_kernel,
        grid=(t // tm,),
        in_specs=[row(d), pl.BlockSpec((None, N_MOD, d), lambda i: (i // tiles_per_seq, 0, 0)),
                  _const_spec(w_in.shape), row(2 * LANES)],
        out_specs=(row(nqk), row(nqk), row(nv), row(nv)),
        out_shape=(jax.ShapeDtypeStruct((t, nqk), BF16), jax.ShapeDtypeStruct((t, nqk), BF16),
                   jax.ShapeDtypeStruct((t, nv), BF16), jax.ShapeDtypeStruct((t, nv), F32)),
        compiler_params=_params(("parallel",)),
    )(x2, mod_l, w_in, tabs)


def _ret_kernel(q_ref, k_ref, v_ref, gate_ref, lg_ref, gng_ref, gnb_ref, o_ref, state_ref, decay_ref, *, C):
    heads = range(RET_HEADS)
    lg = [lg_ref[h][:, 0:1] for h in heads]

    @pl.when(pl.program_id(1) == 0)
    def _():
        state_ref[...] = jnp.zeros(state_ref.shape, F32)
        ri = lax.broadcasted_iota(jnp.int32, (C, C), 0)
        ci = lax.broadcasted_iota(jnp.int32, (C, C), 1)
        diff = (ri - ci).astype(F32)
        causal = ri >= ci
        for h in heads:
            decay_ref[h] = jnp.where(causal, jnp.exp(jnp.where(causal, diff, 0.0) * lg[h]), 0.0)

    idx = lax.broadcasted_iota(jnp.int32, (C, 1), 0).astype(F32)
    qs = [q_ref[:, h * RET_DK:(h + 1) * RET_DK] for h in heads]
    ks = [k_ref[:, h * RET_DK:(h + 1) * RET_DK] for h in heads]
    vs = [v_ref[:, h * RET_DV:(h + 1) * RET_DV] for h in heads]
    states = [state_ref[h] for h in heads]
    scores = [_dot_nt(qs[h], ks[h]) for h in heads]
    cross = [_dot(qs[h], _bf(states[h])) for h in heads]
    kd = [_bf(ks[h].astype(F32) * jnp.exp((C - 1.0 - idx) * lg[h])) for h in heads]
    upd = [_dot_tn(kd[h], vs[h]) for h in heads]
    inner = [_dot(_bf(scores[h] * decay_ref[h]), vs[h]) for h in heads]
    for h in heads:
        state_ref[h] = states[h] * jnp.exp(C * lg[h]) + upd[h]
        o = inner[h] + cross[h] * jnp.exp((idx + 1.0) * lg[h])
        sl = slice(h * RET_DV, (h + 1) * RET_DV)
        o_ref[:, sl] = _bf(gate_ref[:, sl] * _layer_norm(o, gng_ref[h], gnb_ref[h]))


def _retention(q, k, v, gate, lg, gn_g, gn_b, bsz, seq):
    t = q.shape[0]
    C = min(RET_CHUNK, seq)
    nc = seq // C
    blk = lambda n: pl.BlockSpec((C, n), lambda b, j: (b * nc + j, 0))
    return pl.pallas_call(
        functools.partial(_ret_kernel, C=C),
        grid=(bsz, nc),
        in_specs=[blk(RET_HEADS * RET_DK), blk(RET_HEADS * RET_DK), blk(RET_HEADS * RET_DV), blk(RET_HEADS * RET_DV),
                  _const_spec(lg.shape), _const_spec(gn_g.shape), _const_spec(gn_b.shape)],
        out_specs=blk(RET_HEADS * RET_DV),
        out_shape=jax.ShapeDtypeStruct((t, RET_HEADS * RET_DV), BF16),
        scratch_shapes=[pltpu.VMEM((RET_HEADS, RET_DK, RET_DV), F32), pltpu.VMEM((RET_HEADS, C, C), F32)],
        compiler_params=_params(("parallel", "arbitrary")),
    )(q, k, v, gate, lg, gn_g, gn_b)


def _relayout_hy_w_in(w):
    d = w.shape[0]
    z = lambda n: jnp.zeros((d, n), w.dtype)
    c_rope = MLA_Q_RANK + MLA_KV_RANK
    c_gdn = c_rope + MLA_ROPE
    c_a = c_gdn + GDN_QKV
    c_gate = c_a + 2 * GDN_HEADS
    return _bf(jnp.concatenate([
        w[:, 0:c_rope], z(MLA_NOPE), w[:, c_rope:c_gdn], z(LANES - MLA_NOPE - MLA_ROPE),
        w[:, c_gdn:c_a], w[:, c_gate:], w[:, c_a:c_gate], z(LANES - 2 * GDN_HEADS)], axis=1))


def _relayout_w_uq(w):
    r = w.shape[0]
    wh = w.reshape(r, MLA_HEADS, MLA_NOPE + MLA_ROPE)
    pad = jnp.zeros((r, MLA_HEADS, LANES - MLA_NOPE - MLA_ROPE), w.dtype)
    return _bf(jnp.concatenate([wh, pad], axis=-1).reshape(r, MLA_HEADS * LANES))


def _relayout_w_ukv(w):
    r = w.shape[0]
    wh = w.reshape(r, MLA_HEADS, MLA_NOPE + MLA_V)
    pad = jnp.zeros((r, MLA_HEADS, LANES - MLA_NOPE), w.dtype)
    wk = jnp.concatenate([wh[:, :, :MLA_NOPE], pad], axis=-1).reshape(r, MLA_HEADS * LANES)
    wv = wh[:, :, MLA_NOPE:].reshape(r, MLA_HEADS * MLA_V).T
    return _bf(wk), _bf(wv)


def _rope_angles(positions, dim):
    inv_freq = ROPE_BASE ** (-jnp.arange(0, dim, 2, dtype=F32) / dim)
    ang = positions.astype(F32).reshape(-1)[:, None] * inv_freq
    return jnp.cos(ang), jnp.sin(ang)


def _mla_rope_tables(positions):
    cos, sin = _rope_angles(positions, MLA_ROPE)
    t = cos.shape[0]
    z = lambda n: jnp.zeros((t, n), F32)
    ct = jnp.concatenate([jnp.ones((t, MLA_NOPE), F32), cos, cos, z(32)], axis=1)
    s1 = jnp.concatenate([z(MLA_NOPE), -sin, z(48)], axis=1)
    s2 = jnp.concatenate([z(MLA_NOPE + 16), sin, z(32)], axis=1)
    return jnp.concatenate([ct, s1, s2], axis=1)


def _ret_rope_tables(positions):
    cos, sin = _rope_angles(positions, RET_DK)
    return jnp.concatenate([cos, cos, -sin, sin], axis=1)


def _router_weights(w_group, b_group, w_expert, b_expert):
    d = w_group.shape[0]
    pad = LANES - N_EXPERTS - N_GROUPS
    w = jnp.concatenate([w_expert, w_group, jnp.zeros((d, pad), F32)], axis=1)
    b = jnp.concatenate([b_expert, b_group, jnp.zeros((pad,), F32)]).reshape(1, LANES)
    hi = _bf(w)
    return hi, _bf(w - hi.astype(F32)), b


def _pad_lanes(v):
    return jnp.zeros((LANES,), F32).at[:v.shape[0]].set(v)


def kernel(x, c, positions, ada_w, ada_b, ln_mix_g, ln_mix_b, ln_ffn_g, ln_ffn_b, hy_w_in, mla_q_norm, mla_w_uq, mla_kv_norm, mla_w_ukv, gdn_conv_w, gdn_a_log, gdn_dt_bias, gdn_norm, hy_w_out, ret_w_in, ret_gn_g, ret_gn_b, ret_w_out, moe_w_group, moe_b_group, moe_w_expert, moe_b_expert, moe_w_gate, moe_w_up, moe_w_down):
    bsz, seq, d = x.shape
    t = bsz * seq
    mods = _ada_mod(c, ada_w, ada_b)
    mla_tabs = _mla_rope_tables(positions)
    ret_tabs = _ret_rope_tables(positions)
    log_gamma = jnp.log(1.0 - 2.0 ** (-5.0 - jnp.arange(RET_HEADS, dtype=F32)))
    lg = jnp.broadcast_to(log_gamma[:, None, None], (RET_HEADS, 1, LANES))
    xc = x.reshape(t, d)
    for layer in range(DEPTH):
        i = layer // 2
        mod_l = mods[layer]
        if layer % 2 == 0:
            wuk_p, wuv = _relayout_w_ukv(mla_w_ukv[i])
            gparam = jnp.stack([_pad_lanes(gdn_a_log[i]), _pad_lanes(gdn_dt_bias[i])])
            (qm, km, vm, gq, gk, gv, gate, gcol, grow) = _even_front(
                xc, mod_l, _relayout_hy_w_in(hy_w_in[i]), mla_q_norm[i].reshape(1, -1), _relayout_w_uq(mla_w_uq[i]),
                mla_kv_norm[i].reshape(1, -1), wuk_p, wuv, gdn_conv_w[i], gparam, mla_tabs, seq)
            o_mla = _attention(qm, km, vm, bsz, seq)
            n_ch = t // GDN_CHUNK
            growp = grow[:GDN_HEADS].reshape(GDN_PAIRS, 2, n_ch, GDN_CHUNK).transpose(2, 0, 1, 3).reshape(n_ch, GDN_PAIRS, LANES)
            onorm2 = jnp.concatenate([gdn_norm[i], gdn_norm[i]]).reshape(1, LANES)
            y_gdn = _gdn(gq, gk, gv, gate, gcol, growp, onorm2, bsz, seq)
            acts = [o_mla, y_gdn]
            w_out = _bf(hy_w_out[i])
        else:
            q, k, v, gate = _odd_front(xc, mod_l, _bf(ret_w_in[i]), ret_tabs, seq)
            og = _retention(q, k, v, gate, lg, ret_gn_g[i].reshape(RET_HEADS, 1, RET_DV),
                            ret_gn_b[i].reshape(RET_HEADS, 1, RET_DV), bsz, seq)
            acts = [og]
            w_out = _bf(ret_w_out[i])
        wr_hi, wr_lo, br = _router_weights(moe_w_group[layer], moe_b_group[layer], moe_w_expert[layer], moe_b_expert[layer])
        x1, h2, route, counts = _mixer_back(acts, w_out, xc, mod_l, ln_mix_g[layer].reshape(1, d), ln_mix_b[layer].reshape(1, d),
                                    wr_hi, wr_lo, br, seq)
        slot, block_expert, n_used, pad, n_slots = _slot_tables(route, counts, t)
        xb = _scatter_rows(slot, pad, h2, n_slots, seq)
        yb = _moe_experts(block_expert, n_used, xb, moe_w_gate, moe_w_up, moe_w_down, layer)
        xc = _moe_back(slot, x1, yb, route, mod_l, ln_ffn_g[layer].reshape(1, d), ln_ffn_b[layer].reshape(1, d), seq)
    return xc.reshape(bsz, seq, d)
```

```python
import functools

import jax
import jax.numpy as jnp
import numpy as np
from jax import lax
from jax.experimental import pallas as pl
from jax.experimental.pallas import tpu as pltpu

F32 = jnp.float32
BF16 = jnp.bfloat16

DEPTH = 4
N_MOD = 6
LANES = 128

MLA_HEADS = 8
MLA_NOPE = 64
MLA_ROPE = 32
MLA_V = 64
MLA_Q_RANK = 256
MLA_KV_RANK = 128
MLA_SCALE = (MLA_NOPE + MLA_ROPE) ** -0.5
LOG2_E = 1.4426950408889634

GDN_HEADS = 8
GDN_DK = 64
GDN_DV = 64
GDN_CONV = 4
GDN_CHUNK = 64
GDN_HD = GDN_HEADS * GDN_DK
GDN_QKV = 3 * GDN_HD
GDN_PAIRS = GDN_HEADS // 2
GDN_GROUP = 4

RET_HEADS = 8
RET_DK = 128
RET_DV = 256
RET_CHUNK = 256

N_GROUPS = 4
EXPERTS_PER_GROUP = 8
N_EXPERTS = 32
EXPERT_FF = 512
MOE_BLOCK = 512

ROPE_BASE = 10000.0
LN_EPS = 1e-5
RMS_EPS = 1e-6
DEEPNORM_ALPHA = (2.0 * DEPTH) ** 0.25

ROW_TILE = 256
ATTN_TILE = 1024
VMEM_LIMIT = 56 * 1024 * 1024

_C_MLA = 0
_C_QKV = 512
_C_GATE = 2048
_C_AB = 2560
_C_END = 2688


def _bf(x):
    return x.astype(BF16)


def _dot(a, b):
    return jnp.dot(a, b, preferred_element_type=F32)


def _dot_nt(a, b):
    return lax.dot_general(a, b, (((1,), (1,)), ((), ())), preferred_element_type=F32)


def _dot_tn(a, b):
    return lax.dot_general(a, b, (((0,), (0,)), ((), ())), preferred_element_type=F32)


def _split2(x):
    hi = _bf(x)
    return hi, _bf(x - hi.astype(F32))


def _split3(x):
    hi = _bf(x)
    r = x - hi.astype(F32)
    mid = _bf(r)
    return hi, mid, _bf(r - mid.astype(F32))


def _silu(x):
    return x * jax.nn.sigmoid(x)


def _pack_bf16_pairs(x):
    n = x.shape[1] // 2
    bits = lambda v: lax.bitcast_convert_type(_bf(v).astype(F32), jnp.uint32)
    return (bits(x[:, n:]) & jnp.uint32(0xFFFF0000)) | (bits(x[:, :n]) >> 16)


def _unpack_bf16_pairs(w):
    lo = lax.bitcast_convert_type(w << 16, F32)
    hi = lax.bitcast_convert_type(w & jnp.uint32(0xFFFF0000), F32)
    return lo, hi


def _layer_norm(z, g, b):
    mu = jnp.mean(z, -1, keepdims=True)
    zc = z - mu
    var = jnp.mean(zc * zc, -1, keepdims=True)
    return zc * lax.rsqrt(var + LN_EPS) * g + b


def _rms_norm(x, g):
    return x * lax.rsqrt(jnp.mean(x * x, -1, keepdims=True) + RMS_EPS) * g


def _params(sem):
    return pltpu.CompilerParams(dimension_semantics=sem, vmem_limit_bytes=VMEM_LIMIT)


def _const_spec(shape):
    nd = len(shape)
    return pl.BlockSpec(shape, lambda *_: (0,) * nd)


def _ada_kernel(c_ref, w_ref, b_ref, o_ref):
    c = c_ref[...]
    o_ref[...] = _dot(_bf(_silu(c)), _bf(w_ref[...])) + b_ref[...]


def _ada_mod(c, ada_w, ada_b):
    depth, d, n = ada_w.shape
    bsz = c.shape[0]
    rows = 16
    tn = 1536
    cp = jnp.zeros((rows, d), F32).at[:bsz].set(c)
    out = pl.pallas_call(
        _ada_kernel,
        grid=(depth, n // tn),
        in_specs=[pl.BlockSpec((rows, d), lambda l, j: (0, 0)),
                  pl.BlockSpec((None, d, tn), lambda l, j: (l, 0, j)),
                  pl.BlockSpec((None, 1, tn), lambda l, j: (l, 0, j))],
        out_specs=pl.BlockSpec((None, rows, tn), lambda l, j: (l, 0, j)),
        out_shape=jax.ShapeDtypeStruct((depth, rows, n), F32),
        compiler_params=_params(("parallel", "parallel")),
    )(cp, ada_w, ada_b.reshape(depth, 1, n))
    return out[:, :bsz].reshape(depth, bsz, N_MOD, d)


def _even_front_kernel(x_ref, mod_ref, w_ref, qn_ref, wuq_ref, kvn_ref, wuk_ref, wuv_ref,
                       cw_ref, gp_ref, tab_ref,
                       q_ref, k_ref, v_ref, gq_ref, gk_ref, gv_ref, gate_ref, gcol_ref, grow_ref,
                       xbuf, *, tm, tiles_per_seq):
    i = pl.program_id(0)
    hb = _bf(x_ref[...] * (1.0 + mod_ref[1:2, :]) + mod_ref[0:1, :])

    pm = _dot(hb, w_ref[:, _C_MLA:_C_QKV])
    ct = tab_ref[:, 0:LANES]
    s1 = tab_ref[:, LANES:2 * LANES]
    s2 = tab_ref[:, 2 * LANES:3 * LANES]

    def rope(b):
        return b * ct + pltpu.roll(b, LANES - 16, 1) * s1 + pltpu.roll(b, 16, 1) * s2

    qa = _dot(_bf(_rms_norm(pm[:, 0:MLA_Q_RANK], qn_ref[...])), wuq_ref[...])
    for h in range(MLA_HEADS):
        sl = slice(h * LANES, (h + 1) * LANES)
        q_ref[:, sl] = _bf(rope(qa[:, sl]) * (MLA_SCALE * LOG2_E))
    kvn = _bf(_rms_norm(pm[:, MLA_Q_RANK:MLA_Q_RANK + MLA_KV_RANK], kvn_ref[...]))
    ka = _dot(kvn, wuk_ref[...])
    kr = rope(pm[:, MLA_Q_RANK + MLA_KV_RANK:_C_QKV])
    for h in range(MLA_HEADS):
        sl = slice(h * LANES, (h + 1) * LANES)
        k_ref[:, sl] = _bf(ka[:, sl] + kr)
    v_ref[...] = _bf(_dot_nt(wuv_ref[...], kvn))

    @pl.when(i % tiles_per_seq == 0)
    def _():
        xbuf[0:8, :] = jnp.zeros((8, GDN_QKV), F32)

    xbuf[8:8 + tm, :] = _dot(hb, w_ref[:, _C_QKV:_C_GATE])
    cw = cw_ref[...]
    y = cw[0:1, :] * xbuf[5:5 + tm, :]
    for t in range(1, GDN_CONV):
        y = y + cw[t:t + 1, :] * xbuf[5 + t:5 + t + tm, :]
    xbuf[0:8, :] = xbuf[tm:tm + 8, :]
    y = _silu(y)

    r = lax.broadcasted_iota(jnp.int32, (GDN_HD, GDN_HD), 0) // GDN_DK
    c = lax.broadcasted_iota(jnp.int32, (GDN_HD, GDN_HD), 1) // GDN_DK
    head_ones = jnp.where(r == c, 1.0, 0.0).astype(BF16)

    def head_sumsq(z):
        return _dot(_bf(z * z), head_ones)

    qg = y[:, 0:GDN_HD]
    kg = y[:, GDN_HD:2 * GDN_HD]
    gq_ref[...] = qg * lax.rsqrt(head_sumsq(qg) + RMS_EPS) * (GDN_DK ** -0.5)
    gk_ref[...] = kg * lax.rsqrt(head_sumsq(kg) + RMS_EPS)
    gv_ref[...] = y[:, 2 * GDN_HD:3 * GDN_HD]
    gate_ref[...] = _dot(hb, w_ref[:, _C_GATE:_C_AB])

    ab = _dot(hb, w_ref[:, _C_AB:_C_END])
    lane = lax.broadcasted_iota(jnp.int32, (tm, LANES), 1)
    z = ab + gp_ref[1:2, :]
    softplus = jnp.maximum(z, 0.0) + jnp.log(1.0 + jnp.exp(-jnp.abs(z)))
    la = jnp.where(lane < GDN_HEADS, -jnp.exp(gp_ref[0:1, :]) * softplus, 0.0)
    rr = lax.broadcasted_iota(jnp.int32, (tm, tm), 0)
    cc = lax.broadcasted_iota(jnp.int32, (tm, tm), 1)
    tri = jnp.where((rr // GDN_CHUNK == cc // GDN_CHUNK) & (cc <= rr), 1.0, 0.0).astype(BF16)
    l0, l1, l2 = _split3(la)
    g = _dot(tri, l0) + _dot(tri, l1) + _dot(tri, l2)
    gcol = jnp.where(lane < GDN_HEADS, g, jax.nn.sigmoid(ab))
    gcol_ref[...] = gcol
    grow_ref[...] = gcol.T[0:16, :]


def _even_front(x2, mod_l, w_in_p, qn, wuq_p, kvn, wuk_p, wuv, conv_w, gparam, tabs, seq):
    t, d = x2.shape
    tm = min(ROW_TILE, seq)
    tiles_per_seq = seq // tm
    row = lambda n: pl.BlockSpec((tm, n), lambda i: (i, 0))
    out_shapes = (
        jax.ShapeDtypeStruct((t, MLA_HEADS * LANES), BF16),
        jax.ShapeDtypeStruct((t, MLA_HEADS * LANES), BF16),
        jax.ShapeDtypeStruct((MLA_HEADS * MLA_V, t), BF16),
        jax.ShapeDtypeStruct((t, GDN_HD), F32),
        jax.ShapeDtypeStruct((t, GDN_HD), F32),
        jax.ShapeDtypeStruct((t, GDN_HD), F32),
        jax.ShapeDtypeStruct((t, GDN_HD), F32),
        jax.ShapeDtypeStruct((t, LANES), F32),
        jax.ShapeDtypeStruct((16, t), F32),
    )
    return pl.pallas_call(
        functools.partial(_even_front_kernel, tm=tm, tiles_per_seq=tiles_per_seq),
        grid=(t // tm,),
        in_specs=[row(d),
                  pl.BlockSpec((None, N_MOD, d), lambda i: (i // tiles_per_seq, 0, 0)),
                  _const_spec(w_in_p.shape), _const_spec(qn.shape), _const_spec(wuq_p.shape),
                  _const_spec(kvn.shape), _const_spec(wuk_p.shape), _const_spec(wuv.shape),
                  _const_spec(conv_w.shape), _const_spec(gparam.shape), row(3 * LANES)],
        out_specs=(row(MLA_HEADS * LANES), row(MLA_HEADS * LANES),
                   pl.BlockSpec((MLA_HEADS * MLA_V, tm), lambda i: (0, i)),
                   row(GDN_HD), row(GDN_HD), row(GDN_HD), row(GDN_HD), row(LANES),
                   pl.BlockSpec((16, tm), lambda i: (0, i))),
        out_shape=out_shapes,
        scratch_shapes=[pltpu.VMEM((tm + 8, GDN_QKV), F32)],
        compiler_params=_params(("arbitrary",)),
    )(x2, mod_l, w_in_p, qn, wuq_p, kvn, wuk_p, wuv, conv_w, gparam, tabs)


def _attn_kernel(qi_ref, ki_ref, q_ref, k_ref, vt_ref, o_ref, m_ref, l_ref, acc_ref, *, tq):
    qi = qi_ref[pl.program_id(1)]
    ki = ki_ref[pl.program_id(1)]

    @pl.when(ki == 0)
    def _():
        m_ref[...] = jnp.full(m_ref.shape, -jnp.inf, F32)
        l_ref[...] = jnp.zeros(l_ref.shape, F32)
        acc_ref[...] = jnp.zeros(acc_ref.shape, F32)

    def scores(h):
        sl = slice(h * LANES, (h + 1) * LANES)
        return _dot_nt(k_ref[:, sl], q_ref[:, sl])

    def step(diagonal):
        if diagonal:
            key = lax.broadcasted_iota(jnp.int32, (tq, tq), 0)
            qry = lax.broadcasted_iota(jnp.int32, (tq, tq), 1)
            keep = key <= qry
        s_next = scores(0)
        for h in range(MLA_HEADS):
            s = s_next
            if h + 1 < MLA_HEADS:
                s_next = scores(h + 1)
            if diagonal:
                s = jnp.where(keep, s, -jnp.inf)
            rows = slice(h * MLA_V, (h + 1) * MLA_V)
            m_prev = m_ref[h:h + 1, :]
            m_new = jnp.maximum(m_prev, jnp.max(s, 0, keepdims=True))
            alpha = jnp.exp2(m_prev - m_new)
            p = jnp.exp2(s - m_new)
            l_ref[h:h + 1, :] = alpha * l_ref[h:h + 1, :] + jnp.sum(p, 0, keepdims=True)
            acc_ref[rows, :] = alpha * acc_ref[rows, :] + _dot(vt_ref[rows, :], _bf(p))
            m_ref[h:h + 1, :] = m_new

    @pl.when(ki < qi)
    def _():
        step(False)

    @pl.when(ki == qi)
    def _():
        step(True)
        for h in range(MLA_HEADS):
            rows = slice(h * MLA_V, (h + 1) * MLA_V)
            acc_ref[rows, :] = acc_ref[rows, :] / l_ref[h:h + 1, :]
        o_ref[...] = _bf(acc_ref[...].T)


def _attention(qm, km, vm, bsz, seq):
    t = qm.shape[0]
    tq = min(ATTN_TILE, seq)
    nq = seq // tq
    pairs = [(q, k) for q in range(nq) for k in range(q + 1)]
    qi_tab = jnp.asarray(np.array([p[0] for p in pairs], np.int32))
    ki_tab = jnp.asarray(np.array([p[1] for p in pairs], np.int32))
    grid_spec = pltpu.PrefetchScalarGridSpec(
        num_scalar_prefetch=2,
        grid=(bsz, len(pairs)),
        in_specs=[pl.BlockSpec((tq, MLA_HEADS * LANES), lambda b, s, qt, kt: (b * nq + qt[s], 0)),
                  pl.BlockSpec((tq, MLA_HEADS * LANES), lambda b, s, qt, kt: (b * nq + kt[s], 0)),
                  pl.BlockSpec((MLA_HEADS * MLA_V, tq), lambda b, s, qt, kt: (0, b * nq + kt[s]))],
        out_specs=pl.BlockSpec((tq, MLA_HEADS * MLA_V), lambda b, s, qt, kt: (b * nq + qt[s], 0)),
        scratch_shapes=[pltpu.VMEM((MLA_HEADS, tq), F32),
                        pltpu.VMEM((MLA_HEADS, tq), F32),
                        pltpu.VMEM((MLA_HEADS * MLA_V, tq), F32)],
    )
    return pl.pallas_call(
        functools.partial(_attn_kernel, tq=tq),
        grid_spec=grid_spec,
        out_shape=jax.ShapeDtypeStruct((t, MLA_HEADS * MLA_V), BF16),
        compiler_params=_params(("parallel", "arbitrary")),
    )(qi_tab, ki_tab, qm, km, vm)


def _gdn_kernel(q_ref, k_ref, v_ref, gate_ref, gcol_ref, grow_ref, on_ref, y_ref, state_ref, *, n_chunks):
    C = GDN_CHUNK

    @pl.when(pl.program_id(1) == 0)
    def _():
        state_ref[...] = jnp.zeros(state_ref.shape, F32)

    lane = lax.broadcasted_iota(jnp.int32, (C, LANES), 1)
    rowi = lax.broadcasted_iota(jnp.int32, (C, LANES), 0)
    first = lane < GDN_DK
    col = jnp.where(first, lane, lane - GDN_DK)
    tril = col <= rowi
    strict = col < rowi
    eye = jnp.where(col == rowi, 1.0, 0.0)
    blk16 = (col // 16) == (rowi // 16)
    blk32 = (col // 32) == (rowi // 32)
    r2 = lax.broadcasted_iota(jnp.int32, (LANES, LANES), 0)
    c2 = lax.broadcasted_iota(jnp.int32, (LANES, LANES), 1)
    first_rows = r2 < GDN_DK
    head_ones = jnp.where((r2 < GDN_DK) == (c2 < GDN_DK), 1.0, 0.0).astype(BF16)

    def bd(y):
        return jnp.concatenate([jnp.where(first, y, 0), jnp.where(first, 0, y)], axis=0)

    def pmm(x, y):
        return _dot(_bf(x), bd(_bf(y)))

    def pmm_hi(x, y):
        xh, xl = _split2(x)
        yh, yl = _split2(y)
        top = _dot(jnp.concatenate([xh, xl], axis=0), bd(yh))
        return top[:C] + top[C:] + _dot(xh, bd(yl))

    def pair_cols(a, c0, c1):
        return jnp.where(first, a[:, c0:c0 + 1], a[:, c1:c1 + 1])

    def chunk_group(gi, carry):
        pairs = range(GDN_PAIRS)
        items = [(c, p) for c in range(GDN_GROUP) for p in pairs]
        sls = [slice(p * LANES, (p + 1) * LANES) for p in pairs]
        rows = [pl.ds(pl.multiple_of((gi * GDN_GROUP + c) * C, C), C) for c in range(GDN_GROUP)]
        gc = [gcol_ref[rows[c], :] for c in range(GDN_GROUP)]
        gr = [grow_ref[gi * GDN_GROUP + c] for c in range(GDN_GROUP)]
        every = range(len(items))
        q = [q_ref[rows[c], sls[p]] for c, p in items]
        k = [k_ref[rows[c], sls[p]] for c, p in items]
        v = [v_ref[rows[c], sls[p]] for c, p in items]
        g = [pair_cols(gc[c], 2 * p, 2 * p + 1) for c, p in items]
        beta = [pair_cols(gc[c], GDN_HEADS + 2 * p, GDN_HEADS + 2 * p + 1) for c, p in items]
        decay = [jnp.where(tril, jnp.exp(jnp.where(tril, g[i] - gr[c][p:p + 1, :], 0.0)), 0.0)
                 for i, (c, p) in enumerate(items)]
        kb = [k[i] * beta[i] for i in every]
        kbd = [bd(_bf(k[i])) for i in every]
        lmat = [jnp.where(strict, _dot_nt(_bf(kb[i]), kbd[i]) * decay[i], 0.0) for i in every]
        a_intra = [jnp.where(tril, _dot_nt(_bf(q[i]), kbd[i]) * decay[i], 0.0) for i in every]
        n1 = [jnp.where(blk16, lmat[i], 0.0) for i in every]
        x = [eye - n1[i] for i in every]
        nk = n1
        for mm in (pmm_hi, pmm, pmm):
            nk = [mm(nk[i], nk[i]) for i in every]
            x = [x[i] + mm(x[i], nk[i]) for i in every]
        for off in ([jnp.where(blk32 & jnp.logical_not(blk16), lmat[i], 0.0) for i in every],
                    [jnp.where(blk32, 0.0, lmat[i]) for i in every]):
            xb = [pmm(x[i], off[i]) for i in every]
            x = [x[i] - pmm(xb[i], x[i]) for i in every]
        u = [pmm(x[i], v[i] * beta[i]) for i in every]
        w = [pmm(x[i], kb[i] * jnp.exp(g[i])) for i in every]
        for c in range(GDN_GROUP):
            idx = [c * GDN_PAIRS + p for p in pairs]
            state = [state_ref[p] for p in pairs]
            sb = [_bf(state[p]) for p in pairs]
            v_new = [u[i] - _dot(_bf(w[i]), sb[p]) for p, i in enumerate(idx)]
            o = [_dot(_bf(q[i] * jnp.exp(g[i])), sb[p]) + pmm(a_intra[i], v_new[p]) for p, i in enumerate(idx)]
            for p, i in enumerate(idx):
                g_last = g[i][C - 1:C, :]
                k_dec = k[i] * jnp.exp(g_last - g[i])
                scale = jnp.exp(jnp.where(first_rows, g_last[:, 0:1], g_last[:, GDN_DK:GDN_DK + 1]))
                state_ref[p] = state[p] * scale + _dot_tn(bd(_bf(k_dec)), bd(_bf(v_new[p])))
            for p in pairs:
                ms = _dot(_bf(o[p] * o[p]), head_ones) * (1.0 / GDN_DV)
                on = o[p] * lax.rsqrt(ms + RMS_EPS) * on_ref[...]
                y_ref[rows[c], sls[p]] = _bf(on * _silu(gate_ref[rows[c], sls[p]]))
        return carry

    lax.fori_loop(0, n_chunks // GDN_GROUP, chunk_group, 0)


def _gdn(gq, gk, gv, gate, gcol, growp, onorm2, bsz, seq):
    t = gq.shape[0]
    r = min(ROW_TILE, seq)
    nblk = seq // r
    n_chunks = r // GDN_CHUNK
    row = lambda n: pl.BlockSpec((r, n), lambda b, j: (b * nblk + j, 0))
    return pl.pallas_call(
        functools.partial(_gdn_kernel, n_chunks=n_chunks),
        grid=(bsz, nblk),
        in_specs=[row(GDN_HD), row(GDN_HD), row(GDN_HD), row(GDN_HD), row(LANES),
                  pl.BlockSpec((n_chunks, GDN_PAIRS, LANES), lambda b, j: (b * nblk + j, 0, 0)),
                  _const_spec(onorm2.shape)],
        out_specs=row(GDN_HD),
        out_shape=jax.ShapeDtypeStruct((t, GDN_HD), BF16),
        scratch_shapes=[pltpu.VMEM((GDN_PAIRS, LANES, LANES), F32)],
        compiler_params=_params(("parallel", "arbitrary")),
    )(gq, gk, gv, gate, gcol, growp, onorm2)


def _back_kernel(*refs, n_in):
    a_refs = refs[:n_in]
    (w_ref, x_ref, mod_ref, lng_ref, lnb_ref, wrh_ref, wrl_ref, br_ref,
     x1_ref, h2_ref, route_ref, er_ref, counts_ref, base_ref) = refs[n_in:]
    off = 0
    y = None
    for a_ref in a_refs:
        kk = a_ref.shape[1]
        part = _dot(a_ref[...], w_ref[off:off + kk, :])
        y = part if y is None else y + part
        off += kk
    z = DEEPNORM_ALPHA * x_ref[...] + (1.0 + mod_ref[2:3, :]) * y
    x1 = _layer_norm(z, lng_ref[...], lnb_ref[...])
    x1_ref[...] = x1
    h2 = x1 * (1.0 + mod_ref[4:5, :]) + mod_ref[3:4, :]
    h2_ref[...] = h2

    hh, hl = _split2(h2)
    logits = _dot(hh, wrh_ref[...]) + _dot(hh, wrl_ref[...]) + _dot(hl, wrh_ref[...]) + br_ref[...]
    lane = lax.broadcasted_iota(jnp.int32, logits.shape, 1)
    lane_f = lane.astype(F32)
    neg = -jnp.inf
    gmask = (lane >= N_EXPERTS) & (lane < N_EXPERTS + N_GROUPS)
    gl = jnp.where(gmask, logits, neg)
    gmax = jnp.max(gl, -1, keepdims=True)
    g_top = 1.0 / jnp.sum(jnp.where(gmask, jnp.exp(gl - gmax), 0.0), -1, keepdims=True)
    g_lane = jnp.min(jnp.where(gl == gmax, lane_f, 1e9), -1, keepdims=True)
    g_idx = g_lane.astype(jnp.int32) - N_EXPERTS
    emask = (lane < N_EXPERTS) & ((lane // EXPERTS_PER_GROUP) == g_idx)
    el = jnp.where(emask, logits, neg)
    m1 = jnp.max(el, -1, keepdims=True)
    i1 = jnp.min(jnp.where(el == m1, lane_f, 1e9), -1, keepdims=True)
    el2 = jnp.where(lane_f == i1, neg, el)
    m2 = jnp.max(el2, -1, keepdims=True)
    i2 = jnp.min(jnp.where(el2 == m2, lane_f, 1e9), -1, keepdims=True)
    e2 = jnp.exp(m2 - m1)
    w1 = g_top / (1.0 + e2)
    w2 = g_top * e2 / (1.0 + e2)

    @pl.when(pl.program_id(0) == 0)
    def _():
        base_ref[...] = jnp.zeros(base_ref.shape, F32)

    tm = logits.shape[0]
    rr = lax.broadcasted_iota(jnp.int32, (tm, tm), 0)
    cc = lax.broadcasted_iota(jnp.int32, (tm, tm), 1)
    tri = jnp.where(cc <= rr, 1.0, 0.0).astype(BF16)
    oh1 = jnp.where(lane_f == i1, 1.0, 0.0)
    oh2 = jnp.where(lane_f == i2, 1.0, 0.0)
    cs1 = _dot(tri, _bf(oh1))
    cs2 = _dot(tri, _bf(oh2))
    base = base_ref[...]
    tot1 = cs1[tm - 1:tm, :]
    r1 = jnp.sum(oh1 * (base + cs1), -1, keepdims=True) - 1.0
    r2 = jnp.sum(oh2 * (base + tot1 + cs2), -1, keepdims=True) - 1.0
    total = base + tot1 + cs2[tm - 1:tm, :]
    base_ref[...] = total
    counts_ref[...] = total
    route = jnp.where(lane == 0, i1, jnp.where(lane == 1, i2, jnp.where(lane == 2, w1, jnp.where(lane == 3, w2, 0.0))))
    route_ref[...] = route
    col = lax.broadcasted_iota(jnp.int32, (tm, 4), 1)
    er_ref[...] = jnp.where(col == 0, i1, jnp.where(col == 1, i2, jnp.where(col == 2, r1, r2))).astype(jnp.int32)


def _mixer_back(acts, w_out, x2, mod_l, ln_g, ln_b, wr_hi, wr_lo, br, seq):
    t, d = x2.shape
    tm = min(ROW_TILE, seq)
    tiles_per_seq = seq // tm
    row = lambda n: pl.BlockSpec((tm, n), lambda i: (i, 0))
    return pl.pallas_call(
        functools.partial(_back_kernel, n_in=len(acts)),
        grid=(t // tm,),
        in_specs=[row(a.shape[1]) for a in acts] + [
            _const_spec(w_out.shape), row(d),
            pl.BlockSpec((None, N_MOD, d), lambda i: (i // tiles_per_seq, 0, 0)),
            _const_spec(ln_g.shape), _const_spec(ln_b.shape),
            _const_spec(wr_hi.shape), _const_spec(wr_lo.shape), _const_spec(br.shape)],
        out_specs=(row(d), row(d), row(LANES), row(4), _const_spec((1, LANES))),
        out_shape=(jax.ShapeDtypeStruct((t, d), F32), jax.ShapeDtypeStruct((t, d), F32),
                   jax.ShapeDtypeStruct((t, LANES), F32), jax.ShapeDtypeStruct((t, 4), jnp.int32),
                   jax.ShapeDtypeStruct((1, LANES), F32)),
        scratch_shapes=[pltpu.VMEM((1, LANES), F32)],
        compiler_params=_params(("arbitrary",)),
    )(*acts, w_out, x2, mod_l, ln_g, ln_b, wr_hi, wr_lo, br)


def _moe_kernel(be_ref, nb_ref, x_ref, wg_ref, wu_ref, wd_ref, y_ref, wg_s, wu_s, wd_s):
    i = pl.program_id(0)

    @pl.when(i < nb_ref[0])
    def _():
        prev = be_ref[jnp.maximum(i - 1, 0)]

        @pl.when((i == 0) | (be_ref[i] != prev))
        def _():
            wg_s[...] = _bf(wg_ref[...])
            wu_s[...] = _bf(wu_ref[...])
            wd_s[...] = _bf(wd_ref[...])

        lo, hi = _unpack_bf16_pairs(x_ref[...])
        x_lo, x_hi = _bf(lo), _bf(hi)
        half = x_lo.shape[1]
        gate = _dot(x_lo, wg_s[0:half, :]) + _dot(x_hi, wg_s[half:, :])
        up = _dot(x_lo, wu_s[0:half, :]) + _dot(x_hi, wu_s[half:, :])
        y_ref[...] = _pack_bf16_pairs(_dot(_bf(_silu(gate) * up), wd_s[...]))

    @pl.when(i >= nb_ref[0])
    def _():
        y_ref[...] = jnp.zeros(y_ref.shape, jnp.uint32)


def _moe_experts(block_expert, n_used, xb, w_gate, w_up, w_down, layer):
    n_slots = xb.shape[0]
    d, ff = w_gate.shape[-2:]
    n_blocks = n_slots // MOE_BLOCK
    grid_spec = pltpu.PrefetchScalarGridSpec(
        num_scalar_prefetch=2,
        grid=(n_blocks,),
        in_specs=[pl.BlockSpec((MOE_BLOCK, d // 2), lambda i, be, nb: (jnp.minimum(i, nb[0] - 1), 0)),
                  pl.BlockSpec((None, None, d, ff), lambda i, be, nb: (layer, be[i], 0, 0)),
                  pl.BlockSpec((None, None, d, ff), lambda i, be, nb: (layer, be[i], 0, 0)),
                  pl.BlockSpec((None, None, ff, d), lambda i, be, nb: (layer, be[i], 0, 0))],
        out_specs=pl.BlockSpec((MOE_BLOCK, d // 2), lambda i, be, nb: (i, 0)),
        scratch_shapes=[pltpu.VMEM((d, ff), BF16), pltpu.VMEM((d, ff), BF16), pltpu.VMEM((ff, d), BF16)],
    )
    return pl.pallas_call(
        _moe_kernel,
        grid_spec=grid_spec,
        out_shape=jax.ShapeDtypeStruct((n_slots, d // 2), jnp.uint32),
        compiler_params=_params(("arbitrary",)),
    )(block_expert, n_used, xb, w_gate, w_up, w_down)


def _slot_tables(er, counts_row, t):
    counts = counts_row[0, :N_EXPERTS].astype(jnp.int32)
    padded = (counts + MOE_BLOCK - 1) // MOE_BLOCK * MOE_BLOCK
    pend = jnp.cumsum(padded)
    pstart = pend - padded
    slot = (jnp.take(pstart, er[:, 0:2]) + er[:, 2:4]).reshape(2 * t)
    n_blocks = 2 * t // MOE_BLOCK + N_EXPERTS
    block_start = jnp.arange(n_blocks, dtype=jnp.int32) * MOE_BLOCK
    block_expert = jnp.minimum(jnp.sum((block_start[:, None] >= pend[None, :]).astype(jnp.int32), axis=1), N_EXPERTS - 1)
    n_used = (pend[-1] // MOE_BLOCK).astype(jnp.int32).reshape(1)
    n_slots = n_blocks * MOE_BLOCK
    tail = jnp.stack([pend[-1], (n_slots - pend[-1]) // (MOE_BLOCK // 2)])
    pad = jnp.concatenate([pstart + counts, padded - counts, tail]).astype(jnp.int32)
    return slot, block_expert, n_used, pad, n_slots


def _scatter_kernel(slot_ref, pad_ref, h_ref, xb_ref, stage_ref, zero_ref, sems, zsem, *, tm, n_steps):
    i = pl.program_id(0)

    def pad_copies(fn):
        def rows_at(pos, count, size):
            def body(j, c):
                at = pos + j * size
                if size > 1:
                    at = pl.multiple_of(at, size)
                fn(pltpu.make_async_copy(zero_ref.at[pl.ds(0, size)], xb_ref.at[pl.ds(at, size)], zsem))
                return c

            lax.fori_loop(0, count, body, 0)

        def per_expert(e, carry):
            start = pad_ref[e]
            length = pad_ref[N_EXPERTS + e]
            head = jnp.minimum((-start) & 7, length)
            rows_at(start, head, 1)
            rows_at(start + head, (length - head) >> 3, 8)
            rows_at(start + head + ((length - head) & ~7), (length - head) & 7, 1)
            return carry

        lax.fori_loop(0, N_EXPERTS, per_expert, 0)
        rows_at(pad_ref[2 * N_EXPERTS], pad_ref[2 * N_EXPERTS + 1], zero_ref.shape[0])

    @pl.when(i == 0)
    def _():
        zero_ref[...] = jnp.zeros(zero_ref.shape, jnp.uint32)
        pad_copies(lambda c: c.start())
        pad_copies(lambda c: c.wait())

    def start_rows(step):
        buf = step % 2
        for r in range(tm):
            for choice in range(2):
                dst = slot_ref[2 * (step * tm + r) + choice]
                pltpu.make_async_copy(stage_ref.at[buf, pl.ds(r, 1)], xb_ref.at[pl.ds(dst, 1)], sems.at[buf]).start()

    def wait_rows(step):
        buf = step % 2
        for _ in range(2):
            pltpu.make_async_copy(stage_ref.at[buf], xb_ref.at[pl.ds(0, tm)], sems.at[buf]).wait()

    @pl.when(i >= 2)
    def _():
        wait_rows(i - 2)

    stage_ref[i % 2] = _pack_bf16_pairs(h_ref[...])
    start_rows(i)

    @pl.when(i == n_steps - 1)
    def _():
        @pl.when(i >= 1)
        def _():
            wait_rows(i - 1)

        wait_rows(i)


def _scatter_rows(slot, pad, h2, n_slots, seq):
    t, d = h2.shape
    tm = min(ROW_TILE, seq)
    n_steps = t // tm
    grid_spec = pltpu.PrefetchScalarGridSpec(
        num_scalar_prefetch=2,
        grid=(n_steps,),
        in_specs=[pl.BlockSpec((tm, d), lambda i, sl, pd: (i, 0))],
        out_specs=pl.BlockSpec(memory_space=pl.ANY),
        scratch_shapes=[pltpu.VMEM((2, tm, d // 2), jnp.uint32), pltpu.VMEM((MOE_BLOCK // 2, d // 2), jnp.uint32),
                        pltpu.SemaphoreType.DMA((2,)), pltpu.SemaphoreType.DMA(())],
    )
    return pl.pallas_call(
        functools.partial(_scatter_kernel, tm=tm, n_steps=n_steps),
        grid_spec=grid_spec,
        out_shape=jax.ShapeDtypeStruct((n_slots, d // 2), jnp.uint32),
        compiler_params=_params(("arbitrary",)),
    )(slot, pad, h2)


def _moe_back_kernel(slot_ref, x_ref, route_ref, mod_ref, lng_ref, lnb_ref, yb_ref, o_ref, ybuf, sems,
                     *, tm, n_steps):
    i = pl.program_id(0)

    def start_rows(tile, buf):
        for r in range(tm):
            for choice in range(2):
                src = slot_ref[2 * (tile * tm + r) + choice]
                pltpu.make_async_copy(yb_ref.at[pl.ds(src, 1)], ybuf.at[buf, choice, pl.ds(r, 1)], sems.at[buf]).start()

    def wait_rows(buf):
        for choice in range(2):
            pltpu.make_async_copy(yb_ref.at[pl.ds(0, tm)], ybuf.at[buf, choice], sems.at[buf]).wait()

    buf = i % 2

    @pl.when(i == 0)
    def _():
        start_rows(0, 0)

    wait_rows(buf)
    start_rows(jnp.minimum(i + 1, n_steps - 1), 1 - buf)
    rt = route_ref[...]
    y0 = jnp.concatenate(_unpack_bf16_pairs(ybuf[buf, 0]), axis=1)
    y1 = jnp.concatenate(_unpack_bf16_pairs(ybuf[buf, 1]), axis=1)
    y = y0 * rt[:, 2:3] + y1 * rt[:, 3:4]
    z = DEEPNORM_ALPHA * x_ref[...] + (1.0 + mod_ref[5:6, :]) * y
    o_ref[...] = _layer_norm(z, lng_ref[...], lnb_ref[...])

    @pl.when(i == n_steps - 1)
    def _():
        wait_rows(1 - buf)


def _moe_back(slot, x1, yb, route, mod_l, ln_g, ln_b, seq):
    t, d = x1.shape
    tm = min(ROW_TILE, seq)
    tiles_per_seq = seq // tm
    n_steps = t // tm
    row = lambda n: pl.BlockSpec((tm, n), lambda i, sl: (i, 0))
    const = lambda shape: pl.BlockSpec(shape, lambda i, sl: (0,) * len(shape))
    grid_spec = pltpu.PrefetchScalarGridSpec(
        num_scalar_prefetch=1,
        grid=(n_steps,),
        in_specs=[row(d), row(LANES),
                  pl.BlockSpec((None, N_MOD, d), lambda i, sl: (i // tiles_per_seq, 0, 0)),
                  const(ln_g.shape), const(ln_b.shape), pl.BlockSpec(memory_space=pl.ANY)],
        out_specs=row(d),
        scratch_shapes=[pltpu.VMEM((2, 2, tm, d // 2), jnp.uint32), pltpu.SemaphoreType.DMA((2,))],
    )
    return pl.pallas_call(
        functools.partial(_moe_back_kernel, tm=tm, n_steps=n_steps),
        grid_spec=grid_spec,
        out_shape=jax.ShapeDtypeStruct((t, d), F32),
        compiler_params=_params(("arbitrary",)),
    )(slot, x1, route, mod_l, ln_g, ln_b, yb)


def _odd_front_kernel(x_ref, mod_ref, w_ref, tab_ref, q_ref, k_ref, v_ref, gate_ref):
    hb = _bf(x_ref[...] * (1.0 + mod_ref[1:2, :]) + mod_ref[0:1, :])
    c2 = tab_ref[:, 0:LANES]
    s2 = tab_ref[:, LANES:2 * LANES]
    nqk = RET_HEADS * RET_DK
    nv = RET_HEADS * RET_DV

    def rope(b):
        return b * c2 + pltpu.roll(b, RET_DK // 2, 1) * s2

    qa = _dot(hb, w_ref[:, 0:nqk])
    ka = _dot(hb, w_ref[:, nqk:2 * nqk])
    for h in range(RET_HEADS):
        sl = slice(h * LANES, (h + 1) * LANES)
        q_ref[:, sl] = _bf(rope(qa[:, sl]) * (RET_DK ** -0.5))
        k_ref[:, sl] = _bf(rope(ka[:, sl]))
    v_ref[...] = _bf(_dot(hb, w_ref[:, 2 * nqk:2 * nqk + nv]))
    gate_ref[...] = _silu(_dot(hb, w_ref[:, 2 * nqk + nv:2 * nqk + 2 * nv]))


def _odd_front(x2, mod_l, w_in, tabs, seq):
    t, d = x2.shape
    tm = min(ROW_TILE, seq)
    tiles_per_seq = seq // tm
    nqk = RET_HEADS * RET_DK
    nv = RET_HEADS * RET_DV
    row = lambda n: pl.BlockSpec((tm, n), lambda i: (i, 0))
    return pl.pallas_call(
        _odd_front_kernel,
        grid=(t // tm,),
        in_specs=[row(d), pl.BlockSpec((None, N_MOD, d), lambda i: (i // tiles_per_seq, 0, 0)),
                  _const_spec(w_in.shape), row(2 * LANES)],
        out_specs=(row(nqk), row(nqk), row(nv), row(nv)),
        out_shape=(jax.ShapeDtypeStruct((t, nqk), BF16), jax.ShapeDtypeStruct((t, nqk), BF16),
                   jax.ShapeDtypeStruct((t, nv), BF16), jax.ShapeDtypeStruct((t, nv), F32)),
        compiler_params=_params(("parallel",)),
    )(x2, mod_l, w_in, tabs)


def _ret_kernel(q_ref, k_ref, v_ref, gate_ref, lg_ref, gng_ref, gnb_ref, o_ref, state_ref, decay_ref, *, C):
    heads = range(RET_HEADS)
    lg = [lg_ref[h][:, 0:1] for h in heads]

    @pl.when(pl.program_id(1) == 0)
    def _():
        state_ref[...] = jnp.zeros(state_ref.shape, F32)
        ri = lax.broadcasted_iota(jnp.int32, (C, C), 0)
        ci = lax.broadcasted_iota(jnp.int32, (C, C), 1)
        diff = (ri - ci).astype(F32)
        causal = ri >= ci
        for h in heads:
            decay_ref[h] = jnp.where(causal, jnp.exp(jnp.where(causal, diff, 0.0) * lg[h]), 0.0)

    idx = lax.broadcasted_iota(jnp.int32, (C, 1), 0).astype(F32)
    qs = [q_ref[:, h * RET_DK:(h + 1) * RET_DK] for h in heads]
    ks = [k_ref[:, h * RET_DK:(h + 1) * RET_DK] for h in heads]
    vs = [v_ref[:, h * RET_DV:(h + 1) * RET_DV] for h in heads]
    states = [state_ref[h] for h in heads]
    scores = [_dot_nt(qs[h], ks[h]) for h in heads]
    cross = [_dot(qs[h], _bf(states[h])) for h in heads]
    kd = [_bf(ks[h].astype(F32) * jnp.exp((C - 1.0 - idx) * lg[h])) for h in heads]
    upd = [_dot_tn(kd[h], vs[h]) for h in heads]
    inner = [_dot(_bf(scores[h] * decay_ref[h]), vs[h]) for h in heads]
    for h in heads:
        state_ref[h] = states[h] * jnp.exp(C * lg[h]) + upd[h]
        o = inner[h] + cross[h] * jnp.exp((idx + 1.0) * lg[h])
        sl = slice(h * RET_DV, (h + 1) * RET_DV)
        o_ref[:, sl] = _bf(gate_ref[:, sl] * _layer_norm(o, gng_ref[h], gnb_ref[h]))


def _retention(q, k, v, gate, lg, gn_g, gn_b, bsz, seq):
    t = q.shape[0]
    C = min(RET_CHUNK, seq)
    nc = seq // C
    blk = lambda n: pl.BlockSpec((C, n), lambda b, j: (b * nc + j, 0))
    return pl.pallas_call(
        functools.partial(_ret_kernel, C=C),
        grid=(bsz, nc),
        in_specs=[blk(RET_HEADS * RET_DK), blk(RET_HEADS * RET_DK), blk(RET_HEADS * RET_DV), blk(RET_HEADS * RET_DV),
                  _const_spec(lg.shape), _const_spec(gn_g.shape), _const_spec(gn_b.shape)],
        out_specs=blk(RET_HEADS * RET_DV),
        out_shape=jax.ShapeDtypeStruct((t, RET_HEADS * RET_DV), BF16),
        scratch_shapes=[pltpu.VMEM((RET_HEADS, RET_DK, RET_DV), F32), pltpu.VMEM((RET_HEADS, C, C), F32)],
        compiler_params=_params(("parallel", "arbitrary")),
    )(q, k, v, gate, lg, gn_g, gn_b)


def _relayout_hy_w_in(w):
    d = w.shape[0]
    z = lambda n: jnp.zeros((d, n), w.dtype)
    c_rope = MLA_Q_RANK + MLA_KV_RANK
    c_gdn = c_rope + MLA_ROPE
    c_a = c_gdn + GDN_QKV
    c_gate = c_a + 2 * GDN_HEADS
    return _bf(jnp.concatenate([
        w[:, 0:c_rope], z(MLA_NOPE), w[:, c_rope:c_gdn], z(LANES - MLA_NOPE - MLA_ROPE),
        w[:, c_gdn:c_a], w[:, c_gate:], w[:, c_a:c_gate], z(LANES - 2 * GDN_HEADS)], axis=1))


def _relayout_w_uq(w):
    r = w.shape[0]
    wh = w.reshape(r, MLA_HEADS, MLA_NOPE + MLA_ROPE)
    pad = jnp.zeros((r, MLA_HEADS, LANES - MLA_NOPE - MLA_ROPE), w.dtype)
    return _bf(jnp.concatenate([wh, pad], axis=-1).reshape(r, MLA_HEADS * LANES))


def _relayout_w_ukv(w):
    r = w.shape[0]
    wh = w.reshape(r, MLA_HEADS, MLA_NOPE + MLA_V)
    pad = jnp.zeros((r, MLA_HEADS, LANES - MLA_NOPE), w.dtype)
    wk = jnp.concatenate([wh[:, :, :MLA_NOPE], pad], axis=-1).reshape(r, MLA_HEADS * LANES)
    wv = wh[:, :, MLA_NOPE:].reshape(r, MLA_HEADS * MLA_V).T
    return _bf(wk), _bf(wv)


def _rope_angles(positions, dim):
    inv_freq = ROPE_BASE ** (-jnp.arange(0, dim, 2, dtype=F32) / dim)
    ang = positions.astype(F32).reshape(-1)[:, None] * inv_freq
    return jnp.cos(ang), jnp.sin(ang)


def _mla_rope_tables(positions):
    cos, sin = _rope_angles(positions, MLA_ROPE)
    t = cos.shape[0]
    z = lambda n: jnp.zeros((t, n), F32)
    ct = jnp.concatenate([jnp.ones((t, MLA_NOPE), F32), cos, cos, z(32)], axis=1)
    s1 = jnp.concatenate([z(MLA_NOPE), -sin, z(48)], axis=1)
    s2 = jnp.concatenate([z(MLA_NOPE + 16), sin, z(32)], axis=1)
    return jnp.concatenate([ct, s1, s2], axis=1)


def _ret_rope_tables(positions):
    cos, sin = _rope_angles(positions, RET_DK)
    return jnp.concatenate([cos, cos, -sin, sin], axis=1)


def _router_weights(w_group, b_group, w_expert, b_expert):
    d = w_group.shape[0]
    pad = LANES - N_EXPERTS - N_GROUPS
    w = jnp.concatenate([w_expert, w_group, jnp.zeros((d, pad), F32)], axis=1)
    b = jnp.concatenate([b_expert, b_group, jnp.zeros((pad,), F32)]).reshape(1, LANES)
    hi = _bf(w)
    return hi, _bf(w - hi.astype(F32)), b


def _pad_lanes(v):
    return jnp.zeros((LANES,), F32).at[:v.shape[0]].set(v)


def kernel(x, c, positions, ada_w, ada_b, ln_mix_g, ln_mix_b, ln_ffn_g, ln_ffn_b, hy_w_in, mla_q_norm, mla_w_uq, mla_kv_norm, mla_w_ukv, gdn_conv_w, gdn_a_log, gdn_dt_bias, gdn_norm, hy_w_out, ret_w_in, ret_gn_g, ret_gn_b, ret_w_out, moe_w_group, moe_b_group, moe_w_expert, moe_b_expert, moe_w_gate, moe_w_up, moe_w_down):
    bsz, seq, d = x.shape
    t = bsz * seq
    mods = _ada_mod(c, ada_w, ada_b)
    mla_tabs = _mla_rope_tables(positions)
    ret_tabs = _ret_rope_tables(positions)
    log_gamma = jnp.log(1.0 - 2.0 ** (-5.0 - jnp.arange(RET_HEADS, dtype=F32)))
    lg = jnp.broadcast_to(log_gamma[:, None, None], (RET_HEADS, 1, LANES))
    xc = x.reshape(t, d)
    for layer in range(DEPTH):
        i = layer // 2
        mod_l = mods[layer]
        if layer % 2 == 0:
            wuk_p, wuv = _relayout_w_ukv(mla_w_ukv[i])
            gparam = jnp.stack([_pad_lanes(gdn_a_log[i]), _pad_lanes(gdn_dt_bias[i])])
            (qm, km, vm, gq, gk, gv, gate, gcol, grow) = _even_front(
                xc, mod_l, _relayout_hy_w_in(hy_w_in[i]), mla_q_norm[i].reshape(1, -1), _relayout_w_uq(mla_w_uq[i]),
                mla_kv_norm[i].reshape(1, -1), wuk_p, wuv, gdn_conv_w[i], gparam, mla_tabs, seq)
            o_mla = _attention(qm, km, vm, bsz, seq)
            n_ch = t // GDN_CHUNK
            growp = grow[:GDN_HEADS].reshape(GDN_PAIRS, 2, n_ch, GDN_CHUNK).transpose(2, 0, 1, 3).reshape(n_ch, GDN_PAIRS, LANES)
            onorm2 = jnp.concatenate([gdn_norm[i], gdn_norm[i]]).reshape(1, LANES)
            y_gdn = _gdn(gq, gk, gv, gate, gcol, growp, onorm2, bsz, seq)
            acts = [o_mla, y_gdn]
            w_out = _bf(hy_w_out[i])
        else:
            q, k, v, gate = _odd_front(xc, mod_l, _bf(ret_w_in[i]), ret_tabs, seq)
            og = _retention(q, k, v, gate, lg, ret_gn_g[i].reshape(RET_HEADS, 1, RET_DV),
                            ret_gn_b[i].reshape(RET_HEADS, 1, RET_DV), bsz, seq)
            acts = [og]
            w_out = _bf(ret_w_out[i])
        wr_hi, wr_lo, br = _router_weights(moe_w_group[layer], moe_b_group[layer], moe_w_expert[layer], moe_b_expert[layer])
        x1, h2, route, er, counts = _mixer_back(acts, w_out, xc, mod_l, ln_mix_g[layer].reshape(1, d), ln_mix_b[layer].reshape(1, d),
                                    wr_hi, wr_lo, br, seq)
        slot, block_expert, n_used, pad, n_slots = _slot_tables(er, counts, t)
        xb = _scatter_rows(slot, pad, h2, n_slots, seq)
        yb = _moe_experts(block_expert, n_used, xb, moe_w_gate, moe_w_up, moe_w_down, layer)
        xc = _moe_back(slot, x1, yb, route, mod_l, ln_ffn_g[layer].reshape(1, d), ln_ffn_b[layer].reshape(1, d), seq)
    return xc.reshape(bsz, seq, d)
```

```python
import functools

import jax
import jax.numpy as jnp
import numpy as np
from jax import lax
from jax.experimental import pallas as pl
from jax.experimental.pallas import tpu as pltpu

F32 = jnp.float32
BF16 = jnp.bfloat16

DEPTH = 4
N_MOD = 6
LANES = 128

MLA_HEADS = 8
MLA_NOPE = 64
MLA_ROPE = 32
MLA_V = 64
MLA_Q_RANK = 256
MLA_KV_RANK = 128
MLA_SCALE = (MLA_NOPE + MLA_ROPE) ** -0.5
LOG2_E = 1.4426950408889634

GDN_HEADS = 8
GDN_DK = 64
GDN_DV = 64
GDN_CONV = 4
GDN_CHUNK = 64
GDN_HD = GDN_HEADS * GDN_DK
GDN_QKV = 3 * GDN_HD
GDN_PAIRS = GDN_HEADS // 2
GDN_GROUP = 4

RET_HEADS = 8
RET_DK = 128
RET_DV = 256
RET_CHUNK = 256

N_GROUPS = 4
EXPERTS_PER_GROUP = 8
N_EXPERTS = 32
EXPERT_FF = 512
MOE_BLOCK = 512

ROPE_BASE = 10000.0
LN_EPS = 1e-5
RMS_EPS = 1e-6
DEEPNORM_ALPHA = (2.0 * DEPTH) ** 0.25

ROW_TILE = 256
ATTN_TILE = 1024
VMEM_LIMIT = 56 * 1024 * 1024

_C_MLA = 0
_C_QKV = 512
_C_GATE = 2048
_C_AB = 2560
_C_END = 2688


def _bf(x):
    return x.astype(BF16)


def _dot(a, b):
    return jnp.dot(a, b, preferred_element_type=F32)


def _dot_nt(a, b):
    return lax.dot_general(a, b, (((1,), (1,)), ((), ())), preferred_element_type=F32)


def _dot_tn(a, b):
    return lax.dot_general(a, b, (((0,), (0,)), ((), ())), preferred_element_type=F32)


def _split2(x):
    hi = _bf(x)
    return hi, _bf(x - hi.astype(F32))


def _split3(x):
    hi = _bf(x)
    r = x - hi.astype(F32)
    mid = _bf(r)
    return hi, mid, _bf(r - mid.astype(F32))


def _silu(x):
    return x * jax.nn.sigmoid(x)


def _pack_bf16_pairs(x):
    n = x.shape[1] // 2
    bits = lambda v: lax.bitcast_convert_type(_bf(v).astype(F32), jnp.uint32)
    return (bits(x[:, n:]) & jnp.uint32(0xFFFF0000)) | (bits(x[:, :n]) >> 16)


def _unpack_bf16_pairs(w):
    lo = lax.bitcast_convert_type(w << 16, F32)
    hi = lax.bitcast_convert_type(w & jnp.uint32(0xFFFF0000), F32)
    return lo, hi


def _layer_norm(z, g, b):
    mu = jnp.mean(z, -1, keepdims=True)
    zc = z - mu
    var = jnp.mean(zc * zc, -1, keepdims=True)
    return zc * lax.rsqrt(var + LN_EPS) * g + b


def _rms_norm(x, g):
    return x * lax.rsqrt(jnp.mean(x * x, -1, keepdims=True) + RMS_EPS) * g


def _params(sem):
    return pltpu.CompilerParams(dimension_semantics=sem, vmem_limit_bytes=VMEM_LIMIT)


def _const_spec(shape):
    nd = len(shape)
    return pl.BlockSpec(shape, lambda *_: (0,) * nd)


def _ada_kernel(c_ref, w_ref, b_ref, o_ref):
    c = c_ref[...]
    o_ref[...] = _dot(_bf(_silu(c)), _bf(w_ref[...])) + b_ref[...]


def _ada_mod(c, ada_w, ada_b):
    depth, d, n = ada_w.shape
    bsz = c.shape[0]
    rows = 16
    tn = 1536
    cp = jnp.zeros((rows, d), F32).at[:bsz].set(c)
    out = pl.pallas_call(
        _ada_kernel,
        grid=(depth, n // tn),
        in_specs=[pl.BlockSpec((rows, d), lambda l, j: (0, 0)),
                  pl.BlockSpec((None, d, tn), lambda l, j: (l, 0, j)),
                  pl.BlockSpec((None, 1, tn), lambda l, j: (l, 0, j))],
        out_specs=pl.BlockSpec((None, rows, tn), lambda l, j: (l, 0, j)),
        out_shape=jax.ShapeDtypeStruct((depth, rows, n), F32),
        compiler_params=_params(("parallel", "parallel")),
    )(cp, ada_w, ada_b.reshape(depth, 1, n))
    return out[:, :bsz].reshape(depth, bsz, N_MOD, d)


def _even_front_kernel(x_ref, mod_ref, w_ref, qn_ref, wuq_ref, kvn_ref, wuk_ref, wuv_ref,
                       cw_ref, gp_ref, tab_ref, ones_ref, tri_ref,
                       q_ref, k_ref, v_ref, gq_ref, gk_ref, gv_ref, gate_ref, gcol_ref, grow_ref,
                       xbuf, *, tm, tiles_per_seq):
    i = pl.program_id(0)
    hb = _bf(x_ref[...] * (1.0 + mod_ref[1:2, :]) + mod_ref[0:1, :])

    pm = _dot(hb, w_ref[:, _C_MLA:_C_QKV])
    ct = tab_ref[:, 0:LANES]
    s1 = tab_ref[:, LANES:2 * LANES]
    s2 = tab_ref[:, 2 * LANES:3 * LANES]

    def rope(b):
        return b * ct + pltpu.roll(b, LANES - 16, 1) * s1 + pltpu.roll(b, 16, 1) * s2

    qa = _dot(_bf(_rms_norm(pm[:, 0:MLA_Q_RANK], qn_ref[...])), wuq_ref[...])
    for h in range(MLA_HEADS):
        sl = slice(h * LANES, (h + 1) * LANES)
        q_ref[:, sl] = _bf(rope(qa[:, sl]) * (MLA_SCALE * LOG2_E))
    kvn = _bf(_rms_norm(pm[:, MLA_Q_RANK:MLA_Q_RANK + MLA_KV_RANK], kvn_ref[...]))
    ka = _dot(kvn, wuk_ref[...])
    kr = rope(pm[:, MLA_Q_RANK + MLA_KV_RANK:_C_QKV])
    for h in range(MLA_HEADS):
        sl = slice(h * LANES, (h + 1) * LANES)
        k_ref[:, sl] = _bf(ka[:, sl] + kr)
    v_ref[...] = _bf(_dot_nt(wuv_ref[...], kvn))

    @pl.when(i % tiles_per_seq == 0)
    def _():
        xbuf[0:8, :] = jnp.zeros((8, GDN_QKV), F32)

    xbuf[8:8 + tm, :] = _dot(hb, w_ref[:, _C_QKV:_C_GATE])
    cw = cw_ref[...]
    y = cw[0:1, :] * xbuf[5:5 + tm, :]
    for t in range(1, GDN_CONV):
        y = y + cw[t:t + 1, :] * xbuf[5 + t:5 + t + tm, :]
    xbuf[0:8, :] = xbuf[tm:tm + 8, :]
    y = _silu(y)

    def head_sumsq(z):
        return _dot(_bf(z * z), ones_ref[...])

    qg = y[:, 0:GDN_HD]
    kg = y[:, GDN_HD:2 * GDN_HD]
    gq_ref[...] = qg * lax.rsqrt(head_sumsq(qg) + RMS_EPS) * (GDN_DK ** -0.5)
    gk_ref[...] = kg * lax.rsqrt(head_sumsq(kg) + RMS_EPS)
    gv_ref[...] = y[:, 2 * GDN_HD:3 * GDN_HD]
    gate_ref[...] = _dot(hb, w_ref[:, _C_GATE:_C_AB])

    ab = _dot(hb, w_ref[:, _C_AB:_C_END])
    lane = lax.broadcasted_iota(jnp.int32, (tm, LANES), 1)
    z = ab + gp_ref[1:2, :]
    softplus = jnp.maximum(z, 0.0) + jnp.log(1.0 + jnp.exp(-jnp.abs(z)))
    la = jnp.where(lane < GDN_HEADS, -jnp.exp(gp_ref[0:1, :]) * softplus, 0.0)
    tri = tri_ref[...]
    l0, l1, l2 = _split3(la)
    g = _dot(tri, l0) + _dot(tri, l1) + _dot(tri, l2)
    gcol = jnp.where(lane < GDN_HEADS, g, jax.nn.sigmoid(ab))
    gcol_ref[...] = gcol
    grow_ref[...] = gcol.T[0:16, :]


def _even_front(x2, mod_l, w_in_p, qn, wuq_p, kvn, wuk_p, wuv, conv_w, gparam, tabs, seq):
    t, d = x2.shape
    tm = min(ROW_TILE, seq)
    tiles_per_seq = seq // tm
    row = lambda n: pl.BlockSpec((tm, n), lambda i: (i, 0))
    idx = np.arange(GDN_HD) // GDN_DK
    head_ones = jnp.asarray(idx[:, None] == idx[None, :], BF16)
    pos = np.arange(tm)
    chunk_tri = jnp.asarray((pos[:, None] // GDN_CHUNK == pos[None, :] // GDN_CHUNK) & (pos[None, :] <= pos[:, None]), BF16)
    out_shapes = (
        jax.ShapeDtypeStruct((t, MLA_HEADS * LANES), BF16),
        jax.ShapeDtypeStruct((t, MLA_HEADS * LANES), BF16),
        jax.ShapeDtypeStruct((MLA_HEADS * MLA_V, t), BF16),
        jax.ShapeDtypeStruct((t, GDN_HD), F32),
        jax.ShapeDtypeStruct((t, GDN_HD), F32),
        jax.ShapeDtypeStruct((t, GDN_HD), F32),
        jax.ShapeDtypeStruct((t, GDN_HD), F32),
        jax.ShapeDtypeStruct((t, LANES), F32),
        jax.ShapeDtypeStruct((16, t), F32),
    )
    return pl.pallas_call(
        functools.partial(_even_front_kernel, tm=tm, tiles_per_seq=tiles_per_seq),
        grid=(t // tm,),
        in_specs=[row(d),
                  pl.BlockSpec((None, N_MOD, d), lambda i: (i // tiles_per_seq, 0, 0)),
                  _const_spec(w_in_p.shape), _const_spec(qn.shape), _const_spec(wuq_p.shape),
                  _const_spec(kvn.shape), _const_spec(wuk_p.shape), _const_spec(wuv.shape),
                  _const_spec(conv_w.shape), _const_spec(gparam.shape), row(3 * LANES),
                  _const_spec(head_ones.shape), _const_spec(chunk_tri.shape)],
        out_specs=(row(MLA_HEADS * LANES), row(MLA_HEADS * LANES),
                   pl.BlockSpec((MLA_HEADS * MLA_V, tm), lambda i: (0, i)),
                   row(GDN_HD), row(GDN_HD), row(GDN_HD), row(GDN_HD), row(LANES),
                   pl.BlockSpec((16, tm), lambda i: (0, i))),
        out_shape=out_shapes,
        scratch_shapes=[pltpu.VMEM((tm + 8, GDN_QKV), F32)],
        compiler_params=_params(("arbitrary",)),
    )(x2, mod_l, w_in_p, qn, wuq_p, kvn, wuk_p, wuv, conv_w, gparam, tabs, head_ones, chunk_tri)


def _attn_kernel(qi_ref, ki_ref, q_ref, k_ref, vt_ref, o_ref, m_ref, l_ref, acc_ref, *, tq):
    qi = qi_ref[pl.program_id(1)]
    ki = ki_ref[pl.program_id(1)]

    @pl.when(ki == 0)
    def _():
        m_ref[...] = jnp.full(m_ref.shape, -jnp.inf, F32)
        l_ref[...] = jnp.zeros(l_ref.shape, F32)
        acc_ref[...] = jnp.zeros(acc_ref.shape, F32)

    def scores(h):
        sl = slice(h * LANES, (h + 1) * LANES)
        return _dot_nt(k_ref[:, sl], q_ref[:, sl])

    def step(diagonal):
        if diagonal:
            key = lax.broadcasted_iota(jnp.int32, (tq, tq), 0)
            qry = lax.broadcasted_iota(jnp.int32, (tq, tq), 1)
            keep = key <= qry
        s_next = scores(0)
        for h in range(MLA_HEADS):
            s = s_next
            if h + 1 < MLA_HEADS:
                s_next = scores(h + 1)
            if diagonal:
                s = jnp.where(keep, s, -jnp.inf)
            rows = slice(h * MLA_V, (h + 1) * MLA_V)
            m_prev = m_ref[h:h + 1, :]
            m_new = jnp.maximum(m_prev, jnp.max(s, 0, keepdims=True))
            alpha = jnp.exp2(m_prev - m_new)
            p = jnp.exp2(s - m_new)
            l_ref[h:h + 1, :] = alpha * l_ref[h:h + 1, :] + jnp.sum(p, 0, keepdims=True)
            acc_ref[rows, :] = alpha * acc_ref[rows, :] + _dot(vt_ref[rows, :], _bf(p))
            m_ref[h:h + 1, :] = m_new

    @pl.when(ki < qi)
    def _():
        step(False)

    @pl.when(ki == qi)
    def _():
        step(True)
        for h in range(MLA_HEADS):
            rows = slice(h * MLA_V, (h + 1) * MLA_V)
            acc_ref[rows, :] = acc_ref[rows, :] / l_ref[h:h + 1, :]
        o_ref[...] = _bf(acc_ref[...].T)


def _attention(qm, km, vm, bsz, seq):
    t = qm.shape[0]
    tq = min(ATTN_TILE, seq)
    nq = seq // tq
    pairs = [(q, k) for q in range(nq) for k in range(q + 1)]
    qi_tab = jnp.asarray(np.array([p[0] for p in pairs], np.int32))
    ki_tab = jnp.asarray(np.array([p[1] for p in pairs], np.int32))
    grid_spec = pltpu.PrefetchScalarGridSpec(
        num_scalar_prefetch=2,
        grid=(bsz, len(pairs)),
        in_specs=[pl.BlockSpec((tq, MLA_HEADS * LANES), lambda b, s, qt, kt: (b * nq + qt[s], 0)),
                  pl.BlockSpec((tq, MLA_HEADS * LANES), lambda b, s, qt, kt: (b * nq + kt[s], 0)),
                  pl.BlockSpec((MLA_HEADS * MLA_V, tq), lambda b, s, qt, kt: (0, b * nq + kt[s]))],
        out_specs=pl.BlockSpec((tq, MLA_HEADS * MLA_V), lambda b, s, qt, kt: (b * nq + qt[s], 0)),
        scratch_shapes=[pltpu.VMEM((MLA_HEADS, tq), F32),
                        pltpu.VMEM((MLA_HEADS, tq), F32),
                        pltpu.VMEM((MLA_HEADS * MLA_V, tq), F32)],
    )
    return pl.pallas_call(
        functools.partial(_attn_kernel, tq=tq),
        grid_spec=grid_spec,
        out_shape=jax.ShapeDtypeStruct((t, MLA_HEADS * MLA_V), BF16),
        compiler_params=_params(("parallel", "arbitrary")),
    )(qi_tab, ki_tab, qm, km, vm)


def _gdn_kernel(q_ref, k_ref, v_ref, gate_ref, gcol_ref, grow_ref, on_ref, y_ref, state_ref, *, n_chunks):
    C = GDN_CHUNK

    @pl.when(pl.program_id(1) == 0)
    def _():
        state_ref[...] = jnp.zeros(state_ref.shape, F32)

    lane = lax.broadcasted_iota(jnp.int32, (C, LANES), 1)
    rowi = lax.broadcasted_iota(jnp.int32, (C, LANES), 0)
    first = lane < GDN_DK
    col = jnp.where(first, lane, lane - GDN_DK)
    tril = col <= rowi
    strict = col < rowi
    eye = jnp.where(col == rowi, 1.0, 0.0)
    blk16 = (col // 16) == (rowi // 16)
    blk32 = (col // 32) == (rowi // 32)
    r2 = lax.broadcasted_iota(jnp.int32, (LANES, LANES), 0)
    c2 = lax.broadcasted_iota(jnp.int32, (LANES, LANES), 1)
    first_rows = r2 < GDN_DK
    head_ones = jnp.where((r2 < GDN_DK) == (c2 < GDN_DK), 1.0, 0.0).astype(BF16)

    def bd(y):
        return jnp.concatenate([jnp.where(first, y, 0), jnp.where(first, 0, y)], axis=0)

    def pmm(x, y):
        return _dot(_bf(x), bd(_bf(y)))

    def pmm_hi(x, y):
        xh, xl = _split2(x)
        yh, yl = _split2(y)
        top = _dot(jnp.concatenate([xh, xl], axis=0), bd(yh))
        return top[:C] + top[C:] + _dot(xh, bd(yl))

    def pair_cols(a, c0, c1):
        return jnp.where(first, a[:, c0:c0 + 1], a[:, c1:c1 + 1])

    def chunk_group(gi, carry):
        pairs = range(GDN_PAIRS)
        items = [(c, p) for c in range(GDN_GROUP) for p in pairs]
        sls = [slice(p * LANES, (p + 1) * LANES) for p in pairs]
        rows = [pl.ds(pl.multiple_of((gi * GDN_GROUP + c) * C, C), C) for c in range(GDN_GROUP)]
        gc = [gcol_ref[rows[c], :] for c in range(GDN_GROUP)]
        gr = [grow_ref[gi * GDN_GROUP + c] for c in range(GDN_GROUP)]
        every = range(len(items))
        q = [q_ref[rows[c], sls[p]] for c, p in items]
        k = [k_ref[rows[c], sls[p]] for c, p in items]
        v = [v_ref[rows[c], sls[p]] for c, p in items]
        g = [pair_cols(gc[c], 2 * p, 2 * p + 1) for c, p in items]
        beta = [pair_cols(gc[c], GDN_HEADS + 2 * p, GDN_HEADS + 2 * p + 1) for c, p in items]
        decay = [jnp.where(tril, jnp.exp(jnp.where(tril, g[i] - gr[c][p:p + 1, :], 0.0)), 0.0)
                 for i, (c, p) in enumerate(items)]
        kb = [k[i] * beta[i] for i in every]
        kbd = [bd(_bf(k[i])) for i in every]
        lmat = [jnp.where(strict, _dot_nt(_bf(kb[i]), kbd[i]) * decay[i], 0.0) for i in every]
        a_intra = [jnp.where(tril, _dot_nt(_bf(q[i]), kbd[i]) * decay[i], 0.0) for i in every]
        n1 = [jnp.where(blk16, lmat[i], 0.0) for i in every]
        x = [eye - n1[i] for i in every]
        nk = n1
        for mm in (pmm_hi, pmm, pmm):
            nk = [mm(nk[i], nk[i]) for i in every]
            x = [x[i] + mm(x[i], nk[i]) for i in every]
        for off in ([jnp.where(blk32 & jnp.logical_not(blk16), lmat[i], 0.0) for i in every],
                    [jnp.where(blk32, 0.0, lmat[i]) for i in every]):
            xb = [pmm(x[i], off[i]) for i in every]
            x = [x[i] - pmm(xb[i], x[i]) for i in every]
        u = [pmm(x[i], v[i] * beta[i]) for i in every]
        w = [pmm(x[i], kb[i] * jnp.exp(g[i])) for i in every]
        for c in range(GDN_GROUP):
            idx = [c * GDN_PAIRS + p for p in pairs]
            state = [state_ref[p] for p in pairs]
            sb = [_bf(state[p]) for p in pairs]
            v_new = [u[i] - _dot(_bf(w[i]), sb[p]) for p, i in enumerate(idx)]
            o = [_dot(_bf(q[i] * jnp.exp(g[i])), sb[p]) + pmm(a_intra[i], v_new[p]) for p, i in enumerate(idx)]
            for p, i in enumerate(idx):
                g_last = g[i][C - 1:C, :]
                k_dec = k[i] * jnp.exp(g_last - g[i])
                scale = jnp.exp(jnp.where(first_rows, g_last[:, 0:1], g_last[:, GDN_DK:GDN_DK + 1]))
                state_ref[p] = state[p] * scale + _dot_tn(bd(_bf(k_dec)), bd(_bf(v_new[p])))
            for p in pairs:
                ms = _dot(_bf(o[p] * o[p]), head_ones) * (1.0 / GDN_DV)
                on = o[p] * lax.rsqrt(ms + RMS_EPS) * on_ref[...]
                y_ref[rows[c], sls[p]] = _bf(on * _silu(gate_ref[rows[c], sls[p]]))
        return carry

    lax.fori_loop(0, n_chunks // GDN_GROUP, chunk_group, 0)


def _gdn(gq, gk, gv, gate, gcol, growp, onorm2, bsz, seq):
    t = gq.shape[0]
    r = min(ROW_TILE, seq)
    nblk = seq // r
    n_chunks = r // GDN_CHUNK
    row = lambda n: pl.BlockSpec((r, n), lambda b, j: (b * nblk + j, 0))
    return pl.pallas_call(
        functools.partial(_gdn_kernel, n_chunks=n_chunks),
        grid=(bsz, nblk),
        in_specs=[row(GDN_HD), row(GDN_HD), row(GDN_HD), row(GDN_HD), row(LANES),
                  pl.BlockSpec((n_chunks, GDN_PAIRS, LANES), lambda b, j: (b * nblk + j, 0, 0)),
                  _const_spec(onorm2.shape)],
        out_specs=row(GDN_HD),
        out_shape=jax.ShapeDtypeStruct((t, GDN_HD), BF16),
        scratch_shapes=[pltpu.VMEM((GDN_PAIRS, LANES, LANES), F32)],
        compiler_params=_params(("parallel", "arbitrary")),
    )(gq, gk, gv, gate, gcol, growp, onorm2)


def _back_kernel(*refs, n_in):
    a_refs = refs[:n_in]
    (w_ref, x_ref, mod_ref, lng_ref, lnb_ref, wrh_ref, wrl_ref, br_ref, tri_ref,
     x1_ref, h2_ref, route_ref, er_ref, counts_ref, base_ref) = refs[n_in:]
    off = 0
    y = None
    for a_ref in a_refs:
        kk = a_ref.shape[1]
        part = _dot(a_ref[...], w_ref[off:off + kk, :])
        y = part if y is None else y + part
        off += kk
    z = DEEPNORM_ALPHA * x_ref[...] + (1.0 + mod_ref[2:3, :]) * y
    x1 = _layer_norm(z, lng_ref[...], lnb_ref[...])
    x1_ref[...] = x1
    h2 = x1 * (1.0 + mod_ref[4:5, :]) + mod_ref[3:4, :]
    h2_ref[...] = h2

    hh, hl = _split2(h2)
    logits = _dot(hh, wrh_ref[...]) + _dot(hh, wrl_ref[...]) + _dot(hl, wrh_ref[...]) + br_ref[...]
    lane = lax.broadcasted_iota(jnp.int32, logits.shape, 1)
    lane_f = lane.astype(F32)
    neg = -jnp.inf
    gmask = (lane >= N_EXPERTS) & (lane < N_EXPERTS + N_GROUPS)
    gl = jnp.where(gmask, logits, neg)
    gmax = jnp.max(gl, -1, keepdims=True)
    g_top = 1.0 / jnp.sum(jnp.where(gmask, jnp.exp(gl - gmax), 0.0), -1, keepdims=True)
    g_lane = jnp.min(jnp.where(gl == gmax, lane_f, 1e9), -1, keepdims=True)
    g_idx = g_lane.astype(jnp.int32) - N_EXPERTS
    emask = (lane < N_EXPERTS) & ((lane // EXPERTS_PER_GROUP) == g_idx)
    el = jnp.where(emask, logits, neg)
    m1 = jnp.max(el, -1, keepdims=True)
    i1 = jnp.min(jnp.where(el == m1, lane_f, 1e9), -1, keepdims=True)
    el2 = jnp.where(lane_f == i1, neg, el)
    m2 = jnp.max(el2, -1, keepdims=True)
    i2 = jnp.min(jnp.where(el2 == m2, lane_f, 1e9), -1, keepdims=True)
    e2 = jnp.exp(m2 - m1)
    w1 = g_top / (1.0 + e2)
    w2 = g_top * e2 / (1.0 + e2)

    @pl.when(pl.program_id(0) == 0)
    def _():
        base_ref[...] = jnp.zeros(base_ref.shape, F32)

    tm = logits.shape[0]
    tri = tri_ref[...]
    oh1 = jnp.where(lane_f == i1, 1.0, 0.0)
    oh2 = jnp.where(lane_f == i2, 1.0, 0.0)
    cs1 = _dot(tri, _bf(oh1))
    cs2 = _dot(tri, _bf(oh2))
    base = base_ref[...]
    tot1 = cs1[tm - 1:tm, :]
    r1 = jnp.sum(oh1 * (base + cs1), -1, keepdims=True) - 1.0
    r2 = jnp.sum(oh2 * (base + tot1 + cs2), -1, keepdims=True) - 1.0
    total = base + tot1 + cs2[tm - 1:tm, :]
    base_ref[...] = total
    counts_ref[...] = total
    route = jnp.where(lane == 0, i1, jnp.where(lane == 1, i2, jnp.where(lane == 2, w1, jnp.where(lane == 3, w2, 0.0))))
    route = jnp.where(lane == 4, r1, jnp.where(lane == 5, r2, route))
    route_ref[...] = route
    er_ref[...] = route.T[0:8, :]


def _mixer_back(acts, w_out, x2, mod_l, ln_g, ln_b, wr_hi, wr_lo, br, seq):
    t, d = x2.shape
    tm = min(ROW_TILE, seq)
    tiles_per_seq = seq // tm
    row = lambda n: pl.BlockSpec((tm, n), lambda i: (i, 0))
    pos = np.arange(tm)
    tri = jnp.asarray(pos[None, :] <= pos[:, None], BF16)
    return pl.pallas_call(
        functools.partial(_back_kernel, n_in=len(acts)),
        grid=(t // tm,),
        in_specs=[row(a.shape[1]) for a in acts] + [
            _const_spec(w_out.shape), row(d),
            pl.BlockSpec((None, N_MOD, d), lambda i: (i // tiles_per_seq, 0, 0)),
            _const_spec(ln_g.shape), _const_spec(ln_b.shape),
            _const_spec(wr_hi.shape), _const_spec(wr_lo.shape), _const_spec(br.shape), _const_spec(tri.shape)],
        out_specs=(row(d), row(d), row(LANES), pl.BlockSpec((8, tm), lambda i: (0, i)), _const_spec((1, LANES))),
        out_shape=(jax.ShapeDtypeStruct((t, d), F32), jax.ShapeDtypeStruct((t, d), F32),
                   jax.ShapeDtypeStruct((t, LANES), F32), jax.ShapeDtypeStruct((8, t), F32),
                   jax.ShapeDtypeStruct((1, LANES), F32)),
        scratch_shapes=[pltpu.VMEM((1, LANES), F32)],
        compiler_params=_params(("arbitrary",)),
    )(*acts, w_out, x2, mod_l, ln_g, ln_b, wr_hi, wr_lo, br, tri)


def _moe_kernel(be_ref, nb_ref, x_ref, wg_ref, wu_ref, wd_ref, y_ref, wg_s, wu_s, wd_s):
    i = pl.program_id(0)

    @pl.when(i < nb_ref[0])
    def _():
        prev = be_ref[jnp.maximum(i - 1, 0)]

        @pl.when((i == 0) | (be_ref[i] != prev))
        def _():
            wg_s[...] = _bf(wg_ref[...])
            wu_s[...] = _bf(wu_ref[...])
            wd_s[...] = _bf(wd_ref[...])

        lo, hi = _unpack_bf16_pairs(x_ref[...])
        x_lo, x_hi = _bf(lo), _bf(hi)
        half = x_lo.shape[1]
        gate = _dot(x_lo, wg_s[0:half, :]) + _dot(x_hi, wg_s[half:, :])
        up = _dot(x_lo, wu_s[0:half, :]) + _dot(x_hi, wu_s[half:, :])
        y_ref[...] = _pack_bf16_pairs(_dot(_bf(_silu(gate) * up), wd_s[...]))

    @pl.when(i >= nb_ref[0])
    def _():
        y_ref[...] = jnp.zeros(y_ref.shape, jnp.uint32)


def _moe_experts(block_expert, n_used, xb, w_gate, w_up, w_down, layer):
    n_slots = xb.shape[0]
    d, ff = w_gate.shape[-2:]
    n_blocks = n_slots // MOE_BLOCK
    grid_spec = pltpu.PrefetchScalarGridSpec(
        num_scalar_prefetch=2,
        grid=(n_blocks,),
        in_specs=[pl.BlockSpec((MOE_BLOCK, d // 2), lambda i, be, nb: (jnp.minimum(i, nb[0] - 1), 0)),
                  pl.BlockSpec((None, None, d, ff), lambda i, be, nb: (layer, be[i], 0, 0)),
                  pl.BlockSpec((None, None, d, ff), lambda i, be, nb: (layer, be[i], 0, 0)),
                  pl.BlockSpec((None, None, ff, d), lambda i, be, nb: (layer, be[i], 0, 0))],
        out_specs=pl.BlockSpec((MOE_BLOCK, d // 2), lambda i, be, nb: (i, 0)),
        scratch_shapes=[pltpu.VMEM((d, ff), BF16), pltpu.VMEM((d, ff), BF16), pltpu.VMEM((ff, d), BF16)],
    )
    return pl.pallas_call(
        _moe_kernel,
        grid_spec=grid_spec,
        out_shape=jax.ShapeDtypeStruct((n_slots, d // 2), jnp.uint32),
        compiler_params=_params(("arbitrary",)),
    )(block_expert, n_used, xb, w_gate, w_up, w_down)


def _slot_tables(er, counts_row, t):
    counts = counts_row[0, :N_EXPERTS].astype(jnp.int32)
    padded = (counts + MOE_BLOCK - 1) // MOE_BLOCK * MOE_BLOCK
    pend = jnp.cumsum(padded)
    pstart = pend - padded
    slot = (jnp.take(pstart, er[0:2].astype(jnp.int32)) + er[4:6].astype(jnp.int32)).reshape(2 * t)
    n_blocks = 2 * t // MOE_BLOCK + N_EXPERTS
    block_start = jnp.arange(n_blocks, dtype=jnp.int32) * MOE_BLOCK
    block_expert = jnp.minimum(jnp.sum((block_start[:, None] >= pend[None, :]).astype(jnp.int32), axis=1), N_EXPERTS - 1)
    n_used = (pend[-1] // MOE_BLOCK).astype(jnp.int32).reshape(1)
    n_slots = n_blocks * MOE_BLOCK
    tail = jnp.stack([pend[-1], (n_slots - pend[-1]) // (MOE_BLOCK // 2)])
    pad = jnp.concatenate([pstart + counts, padded - counts, tail]).astype(jnp.int32)
    return slot, block_expert, n_used, pad, n_slots


def _scatter_kernel(slot_ref, pad_ref, h_ref, xb_ref, stage_ref, zero_ref, sems, zsem, *, tm, n_steps):
    i = pl.program_id(0)

    def pad_copies(fn):
        def rows_at(pos, count, size):
            def body(j, c):
                at = pos + j * size
                if size > 1:
                    at = pl.multiple_of(at, size)
                fn(pltpu.make_async_copy(zero_ref.at[pl.ds(0, size)], xb_ref.at[pl.ds(at, size)], zsem))
                return c

            lax.fori_loop(0, count, body, 0)

        def per_expert(e, carry):
            start = pad_ref[e]
            length = pad_ref[N_EXPERTS + e]
            head = jnp.minimum((-start) & 7, length)
            rows_at(start, head, 1)
            rows_at(start + head, (length - head) >> 3, 8)
            rows_at(start + head + ((length - head) & ~7), (length - head) & 7, 1)
            return carry

        lax.fori_loop(0, N_EXPERTS, per_expert, 0)
        rows_at(pad_ref[2 * N_EXPERTS], pad_ref[2 * N_EXPERTS + 1], zero_ref.shape[0])

    @pl.when(i == 0)
    def _():
        zero_ref[...] = jnp.zeros(zero_ref.shape, jnp.uint32)
        pad_copies(lambda c: c.start())
        pad_copies(lambda c: c.wait())

    def start_rows(step):
        buf = step % 2
        for r in range(tm):
            for choice in range(2):
                dst = slot_ref[choice * (n_steps * tm) + step * tm + r]
                pltpu.make_async_copy(stage_ref.at[buf, pl.ds(r, 1)], xb_ref.at[pl.ds(dst, 1)], sems.at[buf]).start()

    def wait_rows(step):
        buf = step % 2
        for _ in range(2):
            pltpu.make_async_copy(stage_ref.at[buf], xb_ref.at[pl.ds(0, tm)], sems.at[buf]).wait()

    @pl.when(i >= 2)
    def _():
        wait_rows(i - 2)

    stage_ref[i % 2] = _pack_bf16_pairs(h_ref[...])
    start_rows(i)

    @pl.when(i == n_steps - 1)
    def _():
        @pl.when(i >= 1)
        def _():
            wait_rows(i - 1)

        wait_rows(i)


def _scatter_rows(slot, pad, h2, n_slots, seq):
    t, d = h2.shape
    tm = min(ROW_TILE, seq)
    n_steps = t // tm
    grid_spec = pltpu.PrefetchScalarGridSpec(
        num_scalar_prefetch=2,
        grid=(n_steps,),
        in_specs=[pl.BlockSpec((tm, d), lambda i, sl, pd: (i, 0))],
        out_specs=pl.BlockSpec(memory_space=pl.ANY),
        scratch_shapes=[pltpu.VMEM((2, tm, d // 2), jnp.uint32), pltpu.VMEM((MOE_BLOCK // 2, d // 2), jnp.uint32),
                        pltpu.SemaphoreType.DMA((2,)), pltpu.SemaphoreType.DMA(())],
    )
    return pl.pallas_call(
        functools.partial(_scatter_kernel, tm=tm, n_steps=n_steps),
        grid_spec=grid_spec,
        out_shape=jax.ShapeDtypeStruct((n_slots, d // 2), jnp.uint32),
        compiler_params=_params(("arbitrary",)),
    )(slot, pad, h2)


def _moe_back_kernel(slot_ref, x_ref, route_ref, mod_ref, lng_ref, lnb_ref, yb_ref, o_ref, ybuf, sems,
                     *, tm, n_steps):
    i = pl.program_id(0)

    def start_rows(tile, buf):
        for r in range(tm):
            for choice in range(2):
                src = slot_ref[choice * (n_steps * tm) + tile * tm + r]
                pltpu.make_async_copy(yb_ref.at[pl.ds(src, 1)], ybuf.at[buf, choice, pl.ds(r, 1)], sems.at[buf]).start()

    def wait_rows(buf):
        for choice in range(2):
            pltpu.make_async_copy(yb_ref.at[pl.ds(0, tm)], ybuf.at[buf, choice], sems.at[buf]).wait()

    buf = i % 2

    @pl.when(i == 0)
    def _():
        start_rows(0, 0)

    wait_rows(buf)
    start_rows(jnp.minimum(i + 1, n_steps - 1), 1 - buf)
    rt = route_ref[...]
    y0 = jnp.concatenate(_unpack_bf16_pairs(ybuf[buf, 0]), axis=1)
    y1 = jnp.concatenate(_unpack_bf16_pairs(ybuf[buf, 1]), axis=1)
    y = y0 * rt[:, 2:3] + y1 * rt[:, 3:4]
    z = DEEPNORM_ALPHA * x_ref[...] + (1.0 + mod_ref[5:6, :]) * y
    o_ref[...] = _layer_norm(z, lng_ref[...], lnb_ref[...])

    @pl.when(i == n_steps - 1)
    def _():
        wait_rows(1 - buf)


def _moe_back(slot, x1, yb, route, mod_l, ln_g, ln_b, seq):
    t, d = x1.shape
    tm = min(ROW_TILE, seq)
    tiles_per_seq = seq // tm
    n_steps = t // tm
    row = lambda n: pl.BlockSpec((tm, n), lambda i, sl: (i, 0))
    const = lambda shape: pl.BlockSpec(shape, lambda i, sl: (0,) * len(shape))
    grid_spec = pltpu.PrefetchScalarGridSpec(
        num_scalar_prefetch=1,
        grid=(n_steps,),
        in_specs=[row(d), row(LANES),
                  pl.BlockSpec((None, N_MOD, d), lambda i, sl: (i // tiles_per_seq, 0, 0)),
                  const(ln_g.shape), const(ln_b.shape), pl.BlockSpec(memory_space=pl.ANY)],
        out_specs=row(d),
        scratch_shapes=[pltpu.VMEM((2, 2, tm, d // 2), jnp.uint32), pltpu.SemaphoreType.DMA((2,))],
    )
    return pl.pallas_call(
        functools.partial(_moe_back_kernel, tm=tm, n_steps=n_steps),
        grid_spec=grid_spec,
        out_shape=jax.ShapeDtypeStruct((t, d), F32),
        compiler_params=_params(("arbitrary",)),
    )(slot, x1, route, mod_l, ln_g, ln_b, yb)


def _odd_front_kernel(x_ref, mod_ref, w_ref, tab_ref, q_ref, k_ref, v_ref, gate_ref):
    hb = _bf(x_ref[...] * (1.0 + mod_ref[1:2, :]) + mod_ref[0:1, :])
    c2 = tab_ref[:, 0:LANES]
    s2 = tab_ref[:, LANES:2 * LANES]
    nqk = RET_HEADS * RET_DK
    nv = RET_HEADS * RET_DV

    def rope(b):
        return b * c2 + pltpu.roll(b, RET_DK // 2, 1) * s2

    qa = _dot(hb, w_ref[:, 0:nqk])
    ka = _dot(hb, w_ref[:, nqk:2 * nqk])
    for h in range(RET_HEADS):
        sl = slice(h * LANES, (h + 1) * LANES)
        q_ref[:, sl] = _bf(rope(qa[:, sl]) * (RET_DK ** -0.5))
        k_ref[:, sl] = _bf(rope(ka[:, sl]))
    v_ref[...] = _bf(_dot(hb, w_ref[:, 2 * nqk:2 * nqk + nv]))
    gate_ref[...] = _silu(_dot(hb, w_ref[:, 2 * nqk + nv:2 * nqk + 2 * nv]))


def _odd_front(x2, mod_l, w_in, tabs, seq):
    t, d = x2.shape
    tm = min(ROW_TILE, seq)
    tiles_per_seq = seq // tm
    nqk = RET_HEADS * RET_DK
    nv = RET_HEADS * RET_DV
    row = lambda n: pl.BlockSpec((tm, n), lambda i: (i, 0))
    return pl.pallas_call(
        _odd_front_kernel,
        grid=(t // tm,),
        in_specs=[row(d), pl.BlockSpec((None, N_MOD, d), lambda i: (i // tiles_per_seq, 0, 0)),
                  _const_spec(w_in.shape), row(2 * LANES)],
        out_specs=(row(nqk), row(nqk), row(nv), row(nv)),
        out_shape=(jax.ShapeDtypeStruct((t, nqk), BF16), jax.ShapeDtypeStruct((t, nqk), BF16),
                   jax.ShapeDtypeStruct((t, nv), BF16), jax.ShapeDtypeStruct((t, nv), F32)),
        compiler_params=_params(("parallel",)),
    )(x2, mod_l, w_in, tabs)


def _ret_kernel(q_ref, k_ref, v_ref, gate_ref, lg_ref, gng_ref, gnb_ref, o_ref, state_ref, decay_ref, *, C):
    heads = range(RET_HEADS)
    lg = [lg_ref[h][:, 0:1] for h in heads]

    @pl.when(pl.program_id(1) == 0)
    def _():
        state_ref[...] = jnp.zeros(state_ref.shape, F32)
        ri = lax.broadcasted_iota(jnp.int32, (C, C), 0)
        ci = lax.broadcasted_iota(jnp.int32, (C, C), 1)
        diff = (ri - ci).astype(F32)
        causal = ri >= ci
        for h in heads:
            decay_ref[h] = jnp.where(causal, jnp.exp(jnp.where(causal, diff, 0.0) * lg[h]), 0.0)

    idx = lax.broadcasted_iota(jnp.int32, (C, 1), 0).astype(F32)
    qs = [q_ref[:, h * RET_DK:(h + 1) * RET_DK] for h in heads]
    ks = [k_ref[:, h * RET_DK:(h + 1) * RET_DK] for h in heads]
    vs = [v_ref[:, h * RET_DV:(h + 1) * RET_DV] for h in heads]
    states = [state_ref[h] for h in heads]
    scores = [_dot_nt(qs[h], ks[h]) for h in heads]
    cross = [_dot(qs[h], _bf(states[h])) for h in heads]
    kd = [_bf(ks[h].astype(F32) * jnp.exp((C - 1.0 - idx) * lg[h])) for h in heads]
    upd = [_dot_tn(kd[h], vs[h]) for h in heads]
    inner = [_dot(_bf(scores[h] * decay_ref[h]), vs[h]) for h in heads]
    for h in heads:
        state_ref[h] = states[h] * jnp.exp(C * lg[h]) + upd[h]
        o = inner[h] + cross[h] * jnp.exp((idx + 1.0) * lg[h])
        sl = slice(h * RET_DV, (h + 1) * RET_DV)
        o_ref[:, sl] = _bf(gate_ref[:, sl] * _layer_norm(o, gng_ref[h], gnb_ref[h]))


def _retention(q, k, v, gate, lg, gn_g, gn_b, bsz, seq):
    t = q.shape[0]
    C = min(RET_CHUNK, seq)
    nc = seq // C
    blk = lambda n: pl.BlockSpec((C, n), lambda b, j: (b * nc + j, 0))
    return pl.pallas_call(
        functools.partial(_ret_kernel, C=C),
        grid=(bsz, nc),
        in_specs=[blk(RET_HEADS * RET_DK), blk(RET_HEADS * RET_DK), blk(RET_HEADS * RET_DV), blk(RET_HEADS * RET_DV),
                  _const_spec(lg.shape), _const_spec(gn_g.shape), _const_spec(gn_b.shape)],
        out_specs=blk(RET_HEADS * RET_DV),
        out_shape=jax.ShapeDtypeStruct((t, RET_HEADS * RET_DV), BF16),
        scratch_shapes=[pltpu.VMEM((RET_HEADS, RET_DK, RET_DV), F32), pltpu.VMEM((RET_HEADS, C, C), F32)],
        compiler_params=_params(("parallel", "arbitrary")),
    )(q, k, v, gate, lg, gn_g, gn_b)


def _relayout_hy_w_in(w):
    d = w.shape[0]
    z = lambda n: jnp.zeros((d, n), w.dtype)
    c_rope = MLA_Q_RANK + MLA_KV_RANK
    c_gdn = c_rope + MLA_ROPE
    c_a = c_gdn + GDN_QKV
    c_gate = c_a + 2 * GDN_HEADS
    return _bf(jnp.concatenate([
        w[:, 0:c_rope], z(MLA_NOPE), w[:, c_rope:c_gdn], z(LANES - MLA_NOPE - MLA_ROPE),
        w[:, c_gdn:c_a], w[:, c_gate:], w[:, c_a:c_gate], z(LANES - 2 * GDN_HEADS)], axis=1))


def _relayout_w_uq(w):
    r = w.shape[0]
    wh = w.reshape(r, MLA_HEADS, MLA_NOPE + MLA_ROPE)
    pad = jnp.zeros((r, MLA_HEADS, LANES - MLA_NOPE - MLA_ROPE), w.dtype)
    return _bf(jnp.concatenate([wh, pad], axis=-1).reshape(r, MLA_HEADS * LANES))


def _relayout_w_ukv(w):
    r = w.shape[0]
    wh = w.reshape(r, MLA_HEADS, MLA_NOPE + MLA_V)
    pad = jnp.zeros((r, MLA_HEADS, LANES - MLA_NOPE), w.dtype)
    wk = jnp.concatenate([wh[:, :, :MLA_NOPE], pad], axis=-1).reshape(r, MLA_HEADS * LANES)
    wv = wh[:, :, MLA_NOPE:].reshape(r, MLA_HEADS * MLA_V).T
    return _bf(wk), _bf(wv)


def _rope_angles(positions, dim):
    inv_freq = ROPE_BASE ** (-jnp.arange(0, dim, 2, dtype=F32) / dim)
    ang = positions.astype(F32).reshape(-1)[:, None] * inv_freq
    return jnp.cos(ang), jnp.sin(ang)


def _expand_lanes(compact, placement):
    return jnp.dot(compact, jnp.asarray(placement), precision=lax.Precision.HIGHEST)


def _mla_rope_tables(positions):
    cos, sin = _rope_angles(positions, MLA_ROPE)
    half = MLA_ROPE // 2
    place = np.zeros((2 * half, 3 * LANES), np.float32)
    for j in range(half):
        place[j, MLA_NOPE + j] = place[j, MLA_NOPE + half + j] = 1.0
        place[half + j, LANES + MLA_NOPE + j] = -1.0
        place[half + j, 2 * LANES + MLA_NOPE + half + j] = 1.0
    ones = np.zeros((1, 3 * LANES), np.float32)
    ones[0, :MLA_NOPE] = 1.0
    return _expand_lanes(jnp.concatenate([cos, sin], axis=1), place) + jnp.asarray(ones)


def _ret_rope_tables(positions):
    cos, sin = _rope_angles(positions, RET_DK)
    half = RET_DK // 2
    place = np.zeros((2 * half, 2 * LANES), np.float32)
    for j in range(half):
        place[j, j] = place[j, half + j] = 1.0
        place[half + j, LANES + j] = -1.0
        place[half + j, LANES + half + j] = 1.0
    return _expand_lanes(jnp.concatenate([cos, sin], axis=1), place)


def _router_weights(w_group, b_group, w_expert, b_expert):
    d = w_group.shape[0]
    pad = LANES - N_EXPERTS - N_GROUPS
    w = jnp.concatenate([w_expert, w_group, jnp.zeros((d, pad), F32)], axis=1)
    b = jnp.concatenate([b_expert, b_group, jnp.zeros((pad,), F32)]).reshape(1, LANES)
    hi = _bf(w)
    return hi, _bf(w - hi.astype(F32)), b


def _pad_lanes(v):
    return jnp.zeros((LANES,), F32).at[:v.shape[0]].set(v)


def kernel(x, c, positions, ada_w, ada_b, ln_mix_g, ln_mix_b, ln_ffn_g, ln_ffn_b, hy_w_in, mla_q_norm, mla_w_uq, mla_kv_norm, mla_w_ukv, gdn_conv_w, gdn_a_log, gdn_dt_bias, gdn_norm, hy_w_out, ret_w_in, ret_gn_g, ret_gn_b, ret_w_out, moe_w_group, moe_b_group, moe_w_expert, moe_b_expert, moe_w_gate, moe_w_up, moe_w_down):
    bsz, seq, d = x.shape
    t = bsz * seq
    mods = _ada_mod(c, ada_w, ada_b)
    mla_tabs = _mla_rope_tables(positions)
    ret_tabs = _ret_rope_tables(positions)
    log_gamma = jnp.log(1.0 - 2.0 ** (-5.0 - jnp.arange(RET_HEADS, dtype=F32)))
    lg = jnp.broadcast_to(log_gamma[:, None, None], (RET_HEADS, 1, LANES))
    xc = x.reshape(t, d)
    for layer in range(DEPTH):
        i = layer // 2
        mod_l = mods[layer]
        if layer % 2 == 0:
            wuk_p, wuv = _relayout_w_ukv(mla_w_ukv[i])
            gparam = jnp.stack([_pad_lanes(gdn_a_log[i]), _pad_lanes(gdn_dt_bias[i])])
            (qm, km, vm, gq, gk, gv, gate, gcol, grow) = _even_front(
                xc, mod_l, _relayout_hy_w_in(hy_w_in[i]), mla_q_norm[i].reshape(1, -1), _relayout_w_uq(mla_w_uq[i]),
                mla_kv_norm[i].reshape(1, -1), wuk_p, wuv, gdn_conv_w[i], gparam, mla_tabs, seq)
            o_mla = _attention(qm, km, vm, bsz, seq)
            n_ch = t // GDN_CHUNK
            growp = grow[:GDN_HEADS].reshape(GDN_PAIRS, 2, n_ch, GDN_CHUNK).transpose(2, 0, 1, 3).reshape(n_ch, GDN_PAIRS, LANES)
            onorm2 = jnp.concatenate([gdn_norm[i], gdn_norm[i]]).reshape(1, LANES)
            y_gdn = _gdn(gq, gk, gv, gate, gcol, growp, onorm2, bsz, seq)
            acts = [o_mla, y_gdn]
            w_out = _bf(hy_w_out[i])
        else:
            q, k, v, gate = _odd_front(xc, mod_l, _bf(ret_w_in[i]), ret_tabs, seq)
            og = _retention(q, k, v, gate, lg, ret_gn_g[i].reshape(RET_HEADS, 1, RET_DV),
                            ret_gn_b[i].reshape(RET_HEADS, 1, RET_DV), bsz, seq)
            acts = [og]
            w_out = _bf(ret_w_out[i])
        wr_hi, wr_lo, br = _router_weights(moe_w_group[layer], moe_b_group[layer], moe_w_expert[layer], moe_b_expert[layer])
        x1, h2, route, er, counts = _mixer_back(acts, w_out, xc, mod_l, ln_mix_g[layer].reshape(1, d), ln_mix_b[layer].reshape(1, d),
                                    wr_hi, wr_lo, br, seq)
        slot, block_expert, n_used, pad, n_slots = _slot_tables(er, counts, t)
        xb = _scatter_rows(slot, pad, h2, n_slots, seq)
        yb = _moe_experts(block_expert, n_used, xb, moe_w_gate, moe_w_up, moe_w_down, layer)
        xc = _moe_back(slot, x1, yb, route, mod_l, ln_ffn_g[layer].reshape(1, d), ln_ffn_b[layer].reshape(1, d), seq)
    return xc.reshape(bsz, seq, d)
```

```python
import functools

import jax
import jax.numpy as jnp
import numpy as np
from jax import lax
from jax.experimental import pallas as pl
from jax.experimental.pallas import tpu as pltpu

F32 = jnp.float32
BF16 = jnp.bfloat16

DEPTH = 4
N_MOD = 6
LANES = 128

MLA_HEADS = 8
MLA_NOPE = 64
MLA_ROPE = 32
MLA_V = 64
MLA_Q_RANK = 256
MLA_KV_RANK = 128
MLA_SCALE = (MLA_NOPE + MLA_ROPE) ** -0.5
LOG2_E = 1.4426950408889634

GDN_HEADS = 8
GDN_DK = 64
GDN_DV = 64
GDN_CONV = 4
GDN_CHUNK = 64
GDN_HD = GDN_HEADS * GDN_DK
GDN_QKV = 3 * GDN_HD
GDN_PAIRS = GDN_HEADS // 2
GDN_GROUP = 4

RET_HEADS = 8
RET_DK = 128
RET_DV = 256
RET_CHUNK = 256

N_GROUPS = 4
EXPERTS_PER_GROUP = 8
N_EXPERTS = 32
EXPERT_FF = 512
MOE_BLOCK = 512

ROPE_BASE = 10000.0
LN_EPS = 1e-5
RMS_EPS = 1e-6
DEEPNORM_ALPHA = (2.0 * DEPTH) ** 0.25

ROW_TILE = 256
ATTN_TILE = 1024
VMEM_LIMIT = 56 * 1024 * 1024

_C_MLA = 0
_C_QKV = 512
_C_GATE = 2048
_C_AB = 2560
_C_END = 2688


def _bf(x):
    return x.astype(BF16)


def _dot(a, b):
    return jnp.dot(a, b, preferred_element_type=F32)


def _dot_nt(a, b):
    return lax.dot_general(a, b, (((1,), (1,)), ((), ())), preferred_element_type=F32)


def _dot_tn(a, b):
    return lax.dot_general(a, b, (((0,), (0,)), ((), ())), preferred_element_type=F32)


def _split2(x):
    hi = _bf(x)
    return hi, _bf(x - hi.astype(F32))


def _split3(x):
    hi = _bf(x)
    r = x - hi.astype(F32)
    mid = _bf(r)
    return hi, mid, _bf(r - mid.astype(F32))


def _silu(x):
    return x * jax.nn.sigmoid(x)


def _pack_bf16_pairs(x):
    n = x.shape[1] // 2
    bits = lambda v: lax.bitcast_convert_type(_bf(v).astype(F32), jnp.uint32)
    return (bits(x[:, n:]) & jnp.uint32(0xFFFF0000)) | (bits(x[:, :n]) >> 16)


def _unpack_bf16_pairs(w):
    lo = lax.bitcast_convert_type(w << 16, F32)
    hi = lax.bitcast_convert_type(w & jnp.uint32(0xFFFF0000), F32)
    return lo, hi


def _layer_norm(z, g, b):
    mu = jnp.mean(z, -1, keepdims=True)
    zc = z - mu
    var = jnp.mean(zc * zc, -1, keepdims=True)
    return zc * lax.rsqrt(var + LN_EPS) * g + b


def _rms_norm(x, g):
    return x * lax.rsqrt(jnp.mean(x * x, -1, keepdims=True) + RMS_EPS) * g


def _params(sem):
    return pltpu.CompilerParams(dimension_semantics=sem, vmem_limit_bytes=VMEM_LIMIT)


def _const_spec(shape):
    nd = len(shape)
    return pl.BlockSpec(shape, lambda *_: (0,) * nd)


def _ada_kernel(c_ref, w_ref, b_ref, o_ref):
    c = c_ref[...]
    o_ref[...] = _dot(_bf(_silu(c)), _bf(w_ref[...])) + b_ref[...]


def _ada_mod(c, ada_w, ada_b):
    depth, d, n = ada_w.shape
    bsz = c.shape[0]
    rows = 16
    tn = 1536
    cp = jnp.zeros((rows, d), F32).at[:bsz].set(c)
    out = pl.pallas_call(
        _ada_kernel,
        grid=(depth, n // tn),
        in_specs=[pl.BlockSpec((rows, d), lambda l, j: (0, 0)),
                  pl.BlockSpec((None, d, tn), lambda l, j: (l, 0, j)),
                  pl.BlockSpec((None, 1, tn), lambda l, j: (l, 0, j))],
        out_specs=pl.BlockSpec((None, rows, tn), lambda l, j: (l, 0, j)),
        out_shape=jax.ShapeDtypeStruct((depth, rows, n), F32),
        compiler_params=_params(("parallel", "parallel")),
    )(cp, ada_w, ada_b.reshape(depth, 1, n))
    return out[:, :bsz].reshape(depth, bsz, N_MOD, d)


def _even_front_kernel(x_ref, mod_ref, w_ref, qn_ref, wuq_ref, kvn_ref, wuk_ref, wuv_ref,
                       cw_ref, gp_ref, tab_ref, ones_ref, tri_ref,
                       q_ref, k_ref, v_ref, gq_ref, gk_ref, gv_ref, gate_ref, gcol_ref, grow_ref,
                       xbuf, *, tm, tiles_per_seq):
    i = pl.program_id(0)
    hb = _bf(x_ref[...] * (1.0 + mod_ref[1:2, :]) + mod_ref[0:1, :])

    pm = _dot(hb, w_ref[:, _C_MLA:_C_QKV])
    ct = tab_ref[:, 0:LANES]
    s1 = tab_ref[:, LANES:2 * LANES]
    s2 = tab_ref[:, 2 * LANES:3 * LANES]

    def rope(b):
        return b * ct + pltpu.roll(b, LANES - 16, 1) * s1 + pltpu.roll(b, 16, 1) * s2

    qa = _dot(_bf(_rms_norm(pm[:, 0:MLA_Q_RANK], qn_ref[...])), wuq_ref[...])
    for h in range(MLA_HEADS):
        sl = slice(h * LANES, (h + 1) * LANES)
        q_ref[:, sl] = _bf(rope(qa[:, sl]) * (MLA_SCALE * LOG2_E))
    kvn = _bf(_rms_norm(pm[:, MLA_Q_RANK:MLA_Q_RANK + MLA_KV_RANK], kvn_ref[...]))
    ka = _dot(kvn, wuk_ref[...])
    kr = rope(pm[:, MLA_Q_RANK + MLA_KV_RANK:_C_QKV])
    for h in range(MLA_HEADS):
        sl = slice(h * LANES, (h + 1) * LANES)
        k_ref[:, sl] = _bf(ka[:, sl] + kr)
    v_ref[...] = _bf(_dot_nt(wuv_ref[...], kvn))

    @pl.when(i % tiles_per_seq == 0)
    def _():
        xbuf[0:8, :] = jnp.zeros((8, GDN_QKV), F32)

    xbuf[8:8 + tm, :] = _dot(hb, w_ref[:, _C_QKV:_C_GATE])
    cw = cw_ref[...]
    y = cw[0:1, :] * xbuf[5:5 + tm, :]
    for t in range(1, GDN_CONV):
        y = y + cw[t:t + 1, :] * xbuf[5 + t:5 + t + tm, :]
    xbuf[0:8, :] = xbuf[tm:tm + 8, :]
    y = _silu(y)

    def head_sumsq(z):
        return _dot(_bf(z * z), ones_ref[...])

    qg = y[:, 0:GDN_HD]
    kg = y[:, GDN_HD:2 * GDN_HD]
    gq_ref[...] = qg * lax.rsqrt(head_sumsq(qg) + RMS_EPS) * (GDN_DK ** -0.5)
    gk_ref[...] = kg * lax.rsqrt(head_sumsq(kg) + RMS_EPS)
    gv_ref[...] = y[:, 2 * GDN_HD:3 * GDN_HD]
    gate_ref[...] = _dot(hb, w_ref[:, _C_GATE:_C_AB])

    ab = _dot(hb, w_ref[:, _C_AB:_C_END])
    lane = lax.broadcasted_iota(jnp.int32, (tm, LANES), 1)
    z = ab + gp_ref[1:2, :]
    softplus = jnp.maximum(z, 0.0) + jnp.log(1.0 + jnp.exp(-jnp.abs(z)))
    la = jnp.where(lane < GDN_HEADS, -jnp.exp(gp_ref[0:1, :]) * softplus, 0.0)
    tri = tri_ref[...]
    l0, l1, l2 = _split3(la)
    g = _dot(tri, l0) + _dot(tri, l1) + _dot(tri, l2)
    gcol = jnp.where(lane < GDN_HEADS, g, jax.nn.sigmoid(ab))
    gcol_ref[...] = gcol
    grow_ref[...] = gcol.T[0:16, :]


def _even_front(x2, mod_l, w_in_p, qn, wuq_p, kvn, wuk_p, wuv, conv_w, gparam, tabs, seq):
    t, d = x2.shape
    tm = min(ROW_TILE, seq)
    tiles_per_seq = seq // tm
    row = lambda n: pl.BlockSpec((tm, n), lambda i: (i, 0))
    idx = np.arange(GDN_HD) // GDN_DK
    head_ones = jnp.asarray(idx[:, None] == idx[None, :], BF16)
    pos = np.arange(tm)
    chunk_tri = jnp.asarray((pos[:, None] // GDN_CHUNK == pos[None, :] // GDN_CHUNK) & (pos[None, :] <= pos[:, None]), BF16)
    out_shapes = (
        jax.ShapeDtypeStruct((t, MLA_HEADS * LANES), BF16),
        jax.ShapeDtypeStruct((t, MLA_HEADS * LANES), BF16),
        jax.ShapeDtypeStruct((MLA_HEADS * MLA_V, t), BF16),
        jax.ShapeDtypeStruct((t, GDN_HD), F32),
        jax.ShapeDtypeStruct((t, GDN_HD), F32),
        jax.ShapeDtypeStruct((t, GDN_HD), F32),
        jax.ShapeDtypeStruct((t, GDN_HD), F32),
        jax.ShapeDtypeStruct((t, LANES), F32),
        jax.ShapeDtypeStruct((16, t), F32),
    )
    return pl.pallas_call(
        functools.partial(_even_front_kernel, tm=tm, tiles_per_seq=tiles_per_seq),
        grid=(t // tm,),
        in_specs=[row(d),
                  pl.BlockSpec((None, N_MOD, d), lambda i: (i // tiles_per_seq, 0, 0)),
                  _const_spec(w_in_p.shape), _const_spec(qn.shape), _const_spec(wuq_p.shape),
                  _const_spec(kvn.shape), _const_spec(wuk_p.shape), _const_spec(wuv.shape),
                  _const_spec(conv_w.shape), _const_spec(gparam.shape), row(3 * LANES),
                  _const_spec(head_ones.shape), _const_spec(chunk_tri.shape)],
        out_specs=(row(MLA_HEADS * LANES), row(MLA_HEADS * LANES),
                   pl.BlockSpec((MLA_HEADS * MLA_V, tm), lambda i: (0, i)),
                   row(GDN_HD), row(GDN_HD), row(GDN_HD), row(GDN_HD), row(LANES),
                   pl.BlockSpec((16, tm), lambda i: (0, i))),
        out_shape=out_shapes,
        scratch_shapes=[pltpu.VMEM((tm + 8, GDN_QKV), F32)],
        compiler_params=_params(("arbitrary",)),
    )(x2, mod_l, w_in_p, qn, wuq_p, kvn, wuk_p, wuv, conv_w, gparam, tabs, head_ones, chunk_tri)


def _attn_kernel(qi_ref, ki_ref, q_ref, k_ref, vt_ref, o_ref, m_ref, l_ref, acc_ref, *, tq):
    qi = qi_ref[pl.program_id(1)]
    ki = ki_ref[pl.program_id(1)]

    @pl.when(ki == 0)
    def _():
        m_ref[...] = jnp.full(m_ref.shape, -jnp.inf, F32)
        l_ref[...] = jnp.zeros(l_ref.shape, F32)
        acc_ref[...] = jnp.zeros(acc_ref.shape, F32)

    def scores(h):
        sl = slice(h * LANES, (h + 1) * LANES)
        return _dot_nt(k_ref[:, sl], q_ref[:, sl])

    def step(diagonal):
        if diagonal:
            key = lax.broadcasted_iota(jnp.int32, (tq, tq), 0)
            qry = lax.broadcasted_iota(jnp.int32, (tq, tq), 1)
            keep = key <= qry
        s_next = scores(0)
        for h in range(MLA_HEADS):
            s = s_next
            if h + 1 < MLA_HEADS:
                s_next = scores(h + 1)
            if diagonal:
                s = jnp.where(keep, s, -jnp.inf)
            rows = slice(h * MLA_V, (h + 1) * MLA_V)
            m_prev = m_ref[h:h + 1, :]
            m_new = jnp.maximum(m_prev, jnp.max(s, 0, keepdims=True))
            alpha = jnp.exp2(m_prev - m_new)
            p = jnp.exp2(s - m_new)
            l_ref[h:h + 1, :] = alpha * l_ref[h:h + 1, :] + jnp.sum(p, 0, keepdims=True)
            acc_ref[rows, :] = alpha * acc_ref[rows, :] + _dot(vt_ref[rows, :], _bf(p))
            m_ref[h:h + 1, :] = m_new

    @pl.when(ki < qi)
    def _():
        step(False)

    @pl.when(ki == qi)
    def _():
        step(True)
        for h in range(MLA_HEADS):
            rows = slice(h * MLA_V, (h + 1) * MLA_V)
            acc_ref[rows, :] = acc_ref[rows, :] / l_ref[h:h + 1, :]
        o_ref[...] = _bf(acc_ref[...].T)


def _attention(qm, km, vm, bsz, seq):
    t = qm.shape[0]
    tq = min(ATTN_TILE, seq)
    nq = seq // tq
    pairs = [(q, k) for q in range(nq) for k in range(q + 1)]
    qi_tab = jnp.asarray(np.array([p[0] for p in pairs], np.int32))
    ki_tab = jnp.asarray(np.array([p[1] for p in pairs], np.int32))
    grid_spec = pltpu.PrefetchScalarGridSpec(
        num_scalar_prefetch=2,
        grid=(bsz, len(pairs)),
        in_specs=[pl.BlockSpec((tq, MLA_HEADS * LANES), lambda b, s, qt, kt: (b * nq + qt[s], 0)),
                  pl.BlockSpec((tq, MLA_HEADS * LANES), lambda b, s, qt, kt: (b * nq + kt[s], 0)),
                  pl.BlockSpec((MLA_HEADS * MLA_V, tq), lambda b, s, qt, kt: (0, b * nq + kt[s]))],
        out_specs=pl.BlockSpec((tq, MLA_HEADS * MLA_V), lambda b, s, qt, kt: (b * nq + qt[s], 0)),
        scratch_shapes=[pltpu.VMEM((MLA_HEADS, tq), F32),
                        pltpu.VMEM((MLA_HEADS, tq), F32),
                        pltpu.VMEM((MLA_HEADS * MLA_V, tq), F32)],
    )
    return pl.pallas_call(
        functools.partial(_attn_kernel, tq=tq),
        grid_spec=grid_spec,
        out_shape=jax.ShapeDtypeStruct((t, MLA_HEADS * MLA_V), BF16),
        compiler_params=_params(("parallel", "arbitrary")),
    )(qi_tab, ki_tab, qm, km, vm)


def _gdn_kernel(q_ref, k_ref, v_ref, gate_ref, gcol_ref, grow_ref, on_ref, y_ref, state_ref, *, n_chunks):
    C = GDN_CHUNK

    @pl.when(pl.program_id(1) == 0)
    def _():
        state_ref[...] = jnp.zeros(state_ref.shape, F32)

    lane = lax.broadcasted_iota(jnp.int32, (C, LANES), 1)
    rowi = lax.broadcasted_iota(jnp.int32, (C, LANES), 0)
    first = lane < GDN_DK
    col = jnp.where(first, lane, lane - GDN_DK)
    tril = col <= rowi
    strict = col < rowi
    eye = jnp.where(col == rowi, 1.0, 0.0)
    blk16 = (col // 16) == (rowi // 16)
    blk32 = (col // 32) == (rowi // 32)
    r2 = lax.broadcasted_iota(jnp.int32, (LANES, LANES), 0)
    c2 = lax.broadcasted_iota(jnp.int32, (LANES, LANES), 1)
    first_rows = r2 < GDN_DK
    head_ones = jnp.where((r2 < GDN_DK) == (c2 < GDN_DK), 1.0, 0.0).astype(BF16)

    def bd(y):
        return jnp.concatenate([jnp.where(first, y, 0), jnp.where(first, 0, y)], axis=0)

    def pmm(x, y):
        return _dot(_bf(x), bd(_bf(y)))

    def pmm_hi(x, y):
        xh, xl = _split2(x)
        yh, yl = _split2(y)
        top = _dot(jnp.concatenate([xh, xl], axis=0), bd(yh))
        return top[:C] + top[C:] + _dot(xh, bd(yl))

    def pair_cols(a, c0, c1):
        return jnp.where(first, a[:, c0:c0 + 1], a[:, c1:c1 + 1])

    def chunk_group(gi, carry):
        pairs = range(GDN_PAIRS)
        items = [(c, p) for c in range(GDN_GROUP) for p in pairs]
        sls = [slice(p * LANES, (p + 1) * LANES) for p in pairs]
        rows = [pl.ds(pl.multiple_of((gi * GDN_GROUP + c) * C, C), C) for c in range(GDN_GROUP)]
        gc = [gcol_ref[rows[c], :] for c in range(GDN_GROUP)]
        gr = [grow_ref[gi * GDN_GROUP + c] for c in range(GDN_GROUP)]
        every = range(len(items))
        q = [q_ref[rows[c], sls[p]] for c, p in items]
        k = [k_ref[rows[c], sls[p]] for c, p in items]
        v = [v_ref[rows[c], sls[p]] for c, p in items]
        g = [pair_cols(gc[c], 2 * p, 2 * p + 1) for c, p in items]
        beta = [pair_cols(gc[c], GDN_HEADS + 2 * p, GDN_HEADS + 2 * p + 1) for c, p in items]
        decay = [jnp.where(tril, jnp.exp(jnp.where(tril, g[i] - gr[c][p:p + 1, :], 0.0)), 0.0)
                 for i, (c, p) in enumerate(items)]
        kb = [k[i] * beta[i] for i in every]
        kbd = [bd(_bf(k[i])) for i in every]
        lmat = [jnp.where(strict, _dot_nt(_bf(kb[i]), kbd[i]) * decay[i], 0.0) for i in every]
        a_intra = [jnp.where(tril, _dot_nt(_bf(q[i]), kbd[i]) * decay[i], 0.0) for i in every]
        n1 = [jnp.where(blk16, lmat[i], 0.0) for i in every]
        x = [eye - n1[i] for i in every]
        nk = n1
        for mm in (pmm_hi, pmm, pmm):
            nk = [mm(nk[i], nk[i]) for i in every]
            x = [x[i] + mm(x[i], nk[i]) for i in every]
        for off in ([jnp.where(blk32 & jnp.logical_not(blk16), lmat[i], 0.0) for i in every],
                    [jnp.where(blk32, 0.0, lmat[i]) for i in every]):
            xb = [pmm(x[i], off[i]) for i in every]
            x = [x[i] - pmm(xb[i], x[i]) for i in every]
        u = [pmm(x[i], v[i] * beta[i]) for i in every]
        w = [pmm(x[i], kb[i] * jnp.exp(g[i])) for i in every]
        for c in range(GDN_GROUP):
            idx = [c * GDN_PAIRS + p for p in pairs]
            state = [state_ref[p] for p in pairs]
            sb = [_bf(state[p]) for p in pairs]
            v_new = [u[i] - _dot(_bf(w[i]), sb[p]) for p, i in enumerate(idx)]
            o = [_dot(_bf(q[i] * jnp.exp(g[i])), sb[p]) + pmm(a_intra[i], v_new[p]) for p, i in enumerate(idx)]
            for p, i in enumerate(idx):
                g_last = g[i][C - 1:C, :]
                k_dec = k[i] * jnp.exp(g_last - g[i])
                scale = jnp.exp(jnp.where(first_rows, g_last[:, 0:1], g_last[:, GDN_DK:GDN_DK + 1]))
                state_ref[p] = state[p] * scale + _dot_tn(bd(_bf(k_dec)), bd(_bf(v_new[p])))
            for p in pairs:
                ms = _dot(_bf(o[p] * o[p]), head_ones) * (1.0 / GDN_DV)
                on = o[p] * lax.rsqrt(ms + RMS_EPS) * on_ref[...]
                y_ref[rows[c], sls[p]] = _bf(on * _silu(gate_ref[rows[c], sls[p]]))
        return carry

    lax.fori_loop(0, n_chunks // GDN_GROUP, chunk_group, 0)


def _gdn(gq, gk, gv, gate, gcol, growp, onorm2, bsz, seq):
    t = gq.shape[0]
    r = min(ROW_TILE, seq)
    nblk = seq // r
    n_chunks = r // GDN_CHUNK
    row = lambda n: pl.BlockSpec((r, n), lambda b, j: (b * nblk + j, 0))
    return pl.pallas_call(
        functools.partial(_gdn_kernel, n_chunks=n_chunks),
        grid=(bsz, nblk),
        in_specs=[row(GDN_HD), row(GDN_HD), row(GDN_HD), row(GDN_HD), row(LANES),
                  pl.BlockSpec((n_chunks, GDN_PAIRS, LANES), lambda b, j: (b * nblk + j, 0, 0)),
                  _const_spec(onorm2.shape)],
        out_specs=row(GDN_HD),
        out_shape=jax.ShapeDtypeStruct((t, GDN_HD), BF16),
        scratch_shapes=[pltpu.VMEM((GDN_PAIRS, LANES, LANES), F32)],
        compiler_params=_params(("parallel", "arbitrary")),
    )(gq, gk, gv, gate, gcol, growp, onorm2)


def _back_kernel(*refs, n_in):
    a_refs = refs[:n_in]
    (w_ref, x_ref, mod_ref, lng_ref, lnb_ref, wrh_ref, wrl_ref, br_ref, tri_ref,
     x1_ref, h2_ref, route_ref, er_ref, counts_ref, base_ref) = refs[n_in:]
    off = 0
    y = None
    for a_ref in a_refs:
        kk = a_ref.shape[1]
        part = _dot(a_ref[...], w_ref[off:off + kk, :])
        y = part if y is None else y + part
        off += kk
    z = DEEPNORM_ALPHA * x_ref[...] + (1.0 + mod_ref[2:3, :]) * y
    x1 = _layer_norm(z, lng_ref[...], lnb_ref[...])
    x1_ref[...] = x1
    h2 = x1 * (1.0 + mod_ref[4:5, :]) + mod_ref[3:4, :]
    h2_ref[...] = h2

    hh, hl = _split2(h2)
    logits = _dot(hh, wrh_ref[...]) + _dot(hh, wrl_ref[...]) + _dot(hl, wrh_ref[...]) + br_ref[...]
    lane = lax.broadcasted_iota(jnp.int32, logits.shape, 1)
    lane_f = lane.astype(F32)
    neg = -jnp.inf
    gmask = (lane >= N_EXPERTS) & (lane < N_EXPERTS + N_GROUPS)
    gl = jnp.where(gmask, logits, neg)
    gmax = jnp.max(gl, -1, keepdims=True)
    g_top = 1.0 / jnp.sum(jnp.where(gmask, jnp.exp(gl - gmax), 0.0), -1, keepdims=True)
    g_lane = jnp.min(jnp.where(gl == gmax, lane_f, 1e9), -1, keepdims=True)
    g_idx = g_lane.astype(jnp.int32) - N_EXPERTS
    emask = (lane < N_EXPERTS) & ((lane // EXPERTS_PER_GROUP) == g_idx)
    el = jnp.where(emask, logits, neg)
    m1 = jnp.max(el, -1, keepdims=True)
    i1 = jnp.min(jnp.where(el == m1, lane_f, 1e9), -1, keepdims=True)
    el2 = jnp.where(lane_f == i1, neg, el)
    m2 = jnp.max(el2, -1, keepdims=True)
    i2 = jnp.min(jnp.where(el2 == m2, lane_f, 1e9), -1, keepdims=True)
    e2 = jnp.exp(m2 - m1)
    w1 = g_top / (1.0 + e2)
    w2 = g_top * e2 / (1.0 + e2)

    @pl.when(pl.program_id(0) == 0)
    def _():
        base_ref[...] = jnp.zeros(base_ref.shape, F32)

    tm = logits.shape[0]
    tri = tri_ref[...]
    oh1 = jnp.where(lane_f == i1, 1.0, 0.0)
    oh2 = jnp.where(lane_f == i2, 1.0, 0.0)
    cs1 = _dot(tri, _bf(oh1))
    cs2 = _dot(tri, _bf(oh2))
    base = base_ref[...]
    tot1 = cs1[tm - 1:tm, :]
    r1 = jnp.sum(oh1 * (base + cs1), -1, keepdims=True) - 1.0
    r2 = jnp.sum(oh2 * (base + tot1 + cs2), -1, keepdims=True) - 1.0
    total = base + tot1 + cs2[tm - 1:tm, :]
    base_ref[...] = total
    counts_ref[...] = total
    route = jnp.where(lane == 0, i1, jnp.where(lane == 1, i2, jnp.where(lane == 2, w1, jnp.where(lane == 3, w2, 0.0))))
    route = jnp.where(lane == 4, r1, jnp.where(lane == 5, r2, route))
    route_ref[...] = route
    er_ref[...] = route.T[0:8, :]


def _mixer_back(acts, w_out, x2, mod_l, ln_g, ln_b, wr_hi, wr_lo, br, seq):
    t, d = x2.shape
    tm = min(ROW_TILE, seq)
    tiles_per_seq = seq // tm
    row = lambda n: pl.BlockSpec((tm, n), lambda i: (i, 0))
    pos = np.arange(tm)
    tri = jnp.asarray(pos[None, :] <= pos[:, None], BF16)
    return pl.pallas_call(
        functools.partial(_back_kernel, n_in=len(acts)),
        grid=(t // tm,),
        in_specs=[row(a.shape[1]) for a in acts] + [
            _const_spec(w_out.shape), row(d),
            pl.BlockSpec((None, N_MOD, d), lambda i: (i // tiles_per_seq, 0, 0)),
            _const_spec(ln_g.shape), _const_spec(ln_b.shape),
            _const_spec(wr_hi.shape), _const_spec(wr_lo.shape), _const_spec(br.shape), _const_spec(tri.shape)],
        out_specs=(row(d), row(d), row(LANES), pl.BlockSpec((8, tm), lambda i: (0, i)), _const_spec((1, LANES))),
        out_shape=(jax.ShapeDtypeStruct((t, d), F32), jax.ShapeDtypeStruct((t, d), F32),
                   jax.ShapeDtypeStruct((t, LANES), F32), jax.ShapeDtypeStruct((8, t), F32),
                   jax.ShapeDtypeStruct((1, LANES), F32)),
        scratch_shapes=[pltpu.VMEM((1, LANES), F32)],
        compiler_params=_params(("arbitrary",)),
    )(*acts, w_out, x2, mod_l, ln_g, ln_b, wr_hi, wr_lo, br, tri)


def _moe_kernel(be_ref, nb_ref, x_ref, wg_ref, wu_ref, wd_ref, y_ref, wg_s, wu_s, wd_s):
    i = pl.program_id(0)

    @pl.when(i < nb_ref[0])
    def _():
        prev = be_ref[jnp.maximum(i - 1, 0)]

        @pl.when((i == 0) | (be_ref[i] != prev))
        def _():
            wg_s[...] = _bf(wg_ref[...])
            wu_s[...] = _bf(wu_ref[...])
            wd_s[...] = _bf(wd_ref[...])

        lo, hi = _unpack_bf16_pairs(x_ref[...])
        x_lo, x_hi = _bf(lo), _bf(hi)
        half = x_lo.shape[1]
        gate = _dot(x_lo, wg_s[0:half, :]) + _dot(x_hi, wg_s[half:, :])
        up = _dot(x_lo, wu_s[0:half, :]) + _dot(x_hi, wu_s[half:, :])
        y_ref[...] = _pack_bf16_pairs(_dot(_bf(_silu(gate) * up), wd_s[...]))

    @pl.when(i >= nb_ref[0])
    def _():
        y_ref[...] = jnp.zeros(y_ref.shape, jnp.uint32)


def _moe_experts(block_expert, n_used, xb, w_gate, w_up, w_down, layer):
    n_slots = xb.shape[0]
    d, ff = w_gate.shape[-2:]
    n_blocks = n_slots // MOE_BLOCK
    grid_spec = pltpu.PrefetchScalarGridSpec(
        num_scalar_prefetch=2,
        grid=(n_blocks,),
        in_specs=[pl.BlockSpec((MOE_BLOCK, d // 2), lambda i, be, nb: (jnp.minimum(i, nb[0] - 1), 0)),
                  pl.BlockSpec((None, None, d, ff), lambda i, be, nb: (layer, be[i], 0, 0)),
                  pl.BlockSpec((None, None, d, ff), lambda i, be, nb: (layer, be[i], 0, 0)),
                  pl.BlockSpec((None, None, ff, d), lambda i, be, nb: (layer, be[i], 0, 0))],
        out_specs=pl.BlockSpec((MOE_BLOCK, d // 2), lambda i, be, nb: (i, 0)),
        scratch_shapes=[pltpu.VMEM((d, ff), BF16), pltpu.VMEM((d, ff), BF16), pltpu.VMEM((ff, d), BF16)],
    )
    return pl.pallas_call(
        _moe_kernel,
        grid_spec=grid_spec,
        out_shape=jax.ShapeDtypeStruct((n_slots, d // 2), jnp.uint32),
        compiler_params=_params(("arbitrary",)),
    )(block_expert, n_used, xb, w_gate, w_up, w_down)


def _slot_tables(er, counts_row, t):
    counts = counts_row[0, :N_EXPERTS].astype(jnp.int32)
    padded = (counts + MOE_BLOCK - 1) // MOE_BLOCK * MOE_BLOCK
    pend = jnp.cumsum(padded)
    pstart = pend - padded
    e = er[0:2].astype(jnp.int32)
    starts = jnp.sum(jnp.where(e[:, :, None] == jnp.arange(N_EXPERTS, dtype=jnp.int32), pstart, 0), axis=-1)
    slot = (starts + er[4:6].astype(jnp.int32)).reshape(2 * t)
    n_blocks = 2 * t // MOE_BLOCK + N_EXPERTS
    block_start = jnp.arange(n_blocks, dtype=jnp.int32) * MOE_BLOCK
    block_expert = jnp.minimum(jnp.sum((block_start[:, None] >= pend[None, :]).astype(jnp.int32), axis=1), N_EXPERTS - 1)
    n_used = (pend[-1] // MOE_BLOCK).astype(jnp.int32).reshape(1)
    n_slots = n_blocks * MOE_BLOCK
    tail = jnp.stack([pend[-1], (n_slots - pend[-1]) // (MOE_BLOCK // 2)])
    pad = jnp.concatenate([pstart + counts, padded - counts, tail]).astype(jnp.int32)
    return slot, block_expert, n_used, pad, n_slots


def _scatter_kernel(slot_ref, pad_ref, h_ref, xb_ref, stage_ref, zero_ref, sems, zsem, *, tm, n_steps):
    i = pl.program_id(0)

    def pad_copies(fn):
        def rows_at(pos, count, size):
            def body(j, c):
                at = pos + j * size
                if size > 1:
                    at = pl.multiple_of(at, size)
                fn(pltpu.make_async_copy(zero_ref.at[pl.ds(0, size)], xb_ref.at[pl.ds(at, size)], zsem))
                return c

            lax.fori_loop(0, count, body, 0)

        def per_expert(e, carry):
            start = pad_ref[e]
            length = pad_ref[N_EXPERTS + e]
            head = jnp.minimum((-start) & 7, length)
            rows_at(start, head, 1)
            rows_at(start + head, (length - head) >> 3, 8)
            rows_at(start + head + ((length - head) & ~7), (length - head) & 7, 1)
            return carry

        lax.fori_loop(0, N_EXPERTS, per_expert, 0)
        rows_at(pad_ref[2 * N_EXPERTS], pad_ref[2 * N_EXPERTS + 1], zero_ref.shape[0])

    @pl.when(i == 0)
    def _():
        zero_ref[...] = jnp.zeros(zero_ref.shape, jnp.uint32)
        pad_copies(lambda c: c.start())
        pad_copies(lambda c: c.wait())

    def start_rows(buf):
        for r in range(tm):
            for choice in range(2):
                dst = slot_ref[choice * (n_steps * tm) + i * tm + r]
                pltpu.make_async_copy(stage_ref.at[buf, pl.ds(r, 1)], xb_ref.at[pl.ds(dst, 1)], sems.at[buf]).start()

    def wait_rows(buf):
        for _ in range(2):
            pltpu.make_async_copy(stage_ref.at[buf], xb_ref.at[pl.ds(0, tm)], sems.at[buf]).wait()

    for buf in range(2):
        @pl.when(i % 2 == buf)
        def _(buf=buf):
            @pl.when(i >= 2)
            def _():
                wait_rows(buf)

            stage_ref[buf] = _pack_bf16_pairs(h_ref[...])
            start_rows(buf)

            @pl.when(i == n_steps - 1)
            def _():
                @pl.when(i >= 1)
                def _():
                    wait_rows(1 - buf)

                wait_rows(buf)


def _scatter_rows(slot, pad, h2, n_slots, seq):
    t, d = h2.shape
    tm = min(ROW_TILE, seq)
    n_steps = t // tm
    grid_spec = pltpu.PrefetchScalarGridSpec(
        num_scalar_prefetch=2,
        grid=(n_steps,),
        in_specs=[pl.BlockSpec((tm, d), lambda i, sl, pd: (i, 0))],
        out_specs=pl.BlockSpec(memory_space=pl.ANY),
        scratch_shapes=[pltpu.VMEM((2, tm, d // 2), jnp.uint32), pltpu.VMEM((MOE_BLOCK // 2, d // 2), jnp.uint32),
                        pltpu.SemaphoreType.DMA((2,)), pltpu.SemaphoreType.DMA(())],
    )
    return pl.pallas_call(
        functools.partial(_scatter_kernel, tm=tm, n_steps=n_steps),
        grid_spec=grid_spec,
        out_shape=jax.ShapeDtypeStruct((n_slots, d // 2), jnp.uint32),
        compiler_params=_params(("arbitrary",)),
    )(slot, pad, h2)


def _moe_back_kernel(slot_ref, x_ref, route_ref, mod_ref, lng_ref, lnb_ref, yb_ref, o_ref, ybuf, sems,
                     *, tm, n_steps):
    i = pl.program_id(0)

    def start_rows(tile, buf):
        for r in range(tm):
            for choice in range(2):
                src = slot_ref[choice * (n_steps * tm) + tile * tm + r]
                pltpu.make_async_copy(yb_ref.at[pl.ds(src, 1)], ybuf.at[buf, choice, pl.ds(r, 1)], sems.at[buf]).start()

    def wait_rows(buf):
        for choice in range(2):
            pltpu.make_async_copy(yb_ref.at[pl.ds(0, tm)], ybuf.at[buf, choice], sems.at[buf]).wait()

    @pl.when(i == 0)
    def _():
        start_rows(0, 0)

    for buf in range(2):
        @pl.when(i % 2 == buf)
        def _(buf=buf):
            wait_rows(buf)
            start_rows(jnp.minimum(i + 1, n_steps - 1), 1 - buf)
            rt = route_ref[...]
            y0 = jnp.concatenate(_unpack_bf16_pairs(ybuf[buf, 0]), axis=1)
            y1 = jnp.concatenate(_unpack_bf16_pairs(ybuf[buf, 1]), axis=1)
            y = y0 * rt[:, 2:3] + y1 * rt[:, 3:4]
            z = DEEPNORM_ALPHA * x_ref[...] + (1.0 + mod_ref[5:6, :]) * y
            o_ref[...] = _layer_norm(z, lng_ref[...], lnb_ref[...])

            @pl.when(i == n_steps - 1)
            def _():
                wait_rows(1 - buf)


def _moe_back(slot, x1, yb, route, mod_l, ln_g, ln_b, seq):
    t, d = x1.shape
    tm = min(ROW_TILE, seq)
    tiles_per_seq = seq // tm
    n_steps = t // tm
    row = lambda n: pl.BlockSpec((tm, n), lambda i, sl: (i, 0))
    const = lambda shape: pl.BlockSpec(shape, lambda i, sl: (0,) * len(shape))
    grid_spec = pltpu.PrefetchScalarGridSpec(
        num_scalar_prefetch=1,
        grid=(n_steps,),
        in_specs=[row(d), row(LANES),
                  pl.BlockSpec((None, N_MOD, d), lambda i, sl: (i // tiles_per_seq, 0, 0)),
                  const(ln_g.shape), const(ln_b.shape), pl.BlockSpec(memory_space=pl.ANY)],
        out_specs=row(d),
        scratch_shapes=[pltpu.VMEM((2, 2, tm, d // 2), jnp.uint32), pltpu.SemaphoreType.DMA((2,))],
    )
    return pl.pallas_call(
        functools.partial(_moe_back_kernel, tm=tm, n_steps=n_steps),
        grid_spec=grid_spec,
        out_shape=jax.ShapeDtypeStruct((t, d), F32),
        compiler_params=_params(("arbitrary",)),
    )(slot, x1, route, mod_l, ln_g, ln_b, yb)


def _odd_front_kernel(x_ref, mod_ref, w_ref, tab_ref, q_ref, k_ref, v_ref, gate_ref):
    hb = _bf(x_ref[...] * (1.0 + mod_ref[1:2, :]) + mod_ref[0:1, :])
    c2 = tab_ref[:, 0:LANES]
    s2 = tab_ref[:, LANES:2 * LANES]
    nqk = RET_HEADS * RET_DK
    nv = RET_HEADS * RET_DV

    def rope(b):
        return b * c2 + pltpu.roll(b, RET_DK // 2, 1) * s2

    qa = _dot(hb, w_ref[:, 0:nqk])
    ka = _dot(hb, w_ref[:, nqk:2 * nqk])
    for h in range(RET_HEADS):
        sl = slice(h * LANES, (h + 1) * LANES)
        q_ref[:, sl] = _bf(rope(qa[:, sl]) * (RET_DK ** -0.5))
        k_ref[:, sl] = _bf(rope(ka[:, sl]))
    v_ref[...] = _bf(_dot(hb, w_ref[:, 2 * nqk:2 * nqk + nv]))
    gate_ref[...] = _silu(_dot(hb, w_ref[:, 2 * nqk + nv:2 * nqk + 2 * nv]))


def _odd_front(x2, mod_l, w_in, tabs, seq):
    t, d = x2.shape
    tm = min(ROW_TILE, seq)
    tiles_per_seq = seq // tm
    nqk = RET_HEADS * RET_DK
    nv = RET_HEADS * RET_DV
    row = lambda n: pl.BlockSpec((tm, n), lambda i: (i, 0))
    return pl.pallas_call(
        _odd_front_kernel,
        grid=(t // tm,),
        in_specs=[row(d), pl.BlockSpec((None, N_MOD, d), lambda i: (i // tiles_per_seq, 0, 0)),
                  _const_spec(w_in.shape), row(2 * LANES)],
        out_specs=(row(nqk), row(nqk), row(nv), row(nv)),
        out_shape=(jax.ShapeDtypeStruct((t, nqk), BF16), jax.ShapeDtypeStruct((t, nqk), BF16),
                   jax.ShapeDtypeStruct((t, nv), BF16), jax.ShapeDtypeStruct((t, nv), F32)),
        compiler_params=_params(("parallel",)),
    )(x2, mod_l, w_in, tabs)


def _ret_kernel(q_ref, k_ref, v_ref, gate_ref, lg_ref, gng_ref, gnb_ref, o_ref, state_ref, decay_ref, *, C):
    heads = range(RET_HEADS)
    lg = [lg_ref[h][:, 0:1] for h in heads]

    @pl.when(pl.program_id(1) == 0)
    def _():
        state_ref[...] = jnp.zeros(state_ref.shape, F32)
        ri = lax.broadcasted_iota(jnp.int32, (C, C), 0)
        ci = lax.broadcasted_iota(jnp.int32, (C, C), 1)
        diff = (ri - ci).astype(F32)
        causal = ri >= ci
        for h in heads:
            decay_ref[h] = jnp.where(causal, jnp.exp(jnp.where(causal, diff, 0.0) * lg[h]), 0.0)

    idx = lax.broadcasted_iota(jnp.int32, (C, 1), 0).astype(F32)
    qs = [q_ref[:, h * RET_DK:(h + 1) * RET_DK] for h in heads]
    ks = [k_ref[:, h * RET_DK:(h + 1) * RET_DK] for h in heads]
    vs = [v_ref[:, h * RET_DV:(h + 1) * RET_DV] for h in heads]
    states = [state_ref[h] for h in heads]
    scores = [_dot_nt(qs[h], ks[h]) for h in heads]
    cross = [_dot(qs[h], _bf(states[h])) for h in heads]
    kd = [_bf(ks[h].astype(F32) * jnp.exp((C - 1.0 - idx) * lg[h])) for h in heads]
    upd = [_dot_tn(kd[h], vs[h]) for h in heads]
    inner = [_dot(_bf(scores[h] * decay_ref[h]), vs[h]) for h in heads]
    for h in heads:
        state_ref[h] = states[h] * jnp.exp(C * lg[h]) + upd[h]
        o = inner[h] + cross[h] * jnp.exp((idx + 1.0) * lg[h])
        sl = slice(h * RET_DV, (h + 1) * RET_DV)
        o_ref[:, sl] = _bf(gate_ref[:, sl] * _layer_norm(o, gng_ref[h], gnb_ref[h]))


def _retention(q, k, v, gate, lg, gn_g, gn_b, bsz, seq):
    t = q.shape[0]
    C = min(RET_CHUNK, seq)
    nc = seq // C
    blk = lambda n: pl.BlockSpec((C, n), lambda b, j: (b * nc + j, 0))
    return pl.pallas_call(
        functools.partial(_ret_kernel, C=C),
        grid=(bsz, nc),
        in_specs=[blk(RET_HEADS * RET_DK), blk(RET_HEADS * RET_DK), blk(RET_HEADS * RET_DV), blk(RET_HEADS * RET_DV),
                  _const_spec(lg.shape), _const_spec(gn_g.shape), _const_spec(gn_b.shape)],
        out_specs=blk(RET_HEADS * RET_DV),
        out_shape=jax.ShapeDtypeStruct((t, RET_HEADS * RET_DV), BF16),
        scratch_shapes=[pltpu.VMEM((RET_HEADS, RET_DK, RET_DV), F32), pltpu.VMEM((RET_HEADS, C, C), F32)],
        compiler_params=_params(("parallel", "arbitrary")),
    )(q, k, v, gate, lg, gn_g, gn_b)


def _relayout_hy_w_in(w):
    d = w.shape[0]
    z = lambda n: jnp.zeros((d, n), w.dtype)
    c_rope = MLA_Q_RANK + MLA_KV_RANK
    c_gdn = c_rope + MLA_ROPE
    c_a = c_gdn + GDN_QKV
    c_gate = c_a + 2 * GDN_HEADS
    return _bf(jnp.concatenate([
        w[:, 0:c_rope], z(MLA_NOPE), w[:, c_rope:c_gdn], z(LANES - MLA_NOPE - MLA_ROPE),
        w[:, c_gdn:c_a], w[:, c_gate:], w[:, c_a:c_gate], z(LANES - 2 * GDN_HEADS)], axis=1))


def _relayout_w_uq(w):
    r = w.shape[0]
    wh = w.reshape(r, MLA_HEADS, MLA_NOPE + MLA_ROPE)
    pad = jnp.zeros((r, MLA_HEADS, LANES - MLA_NOPE - MLA_ROPE), w.dtype)
    return _bf(jnp.concatenate([wh, pad], axis=-1).reshape(r, MLA_HEADS * LANES))


def _relayout_w_ukv(w):
    r = w.shape[0]
    wh = w.reshape(r, MLA_HEADS, MLA_NOPE + MLA_V)
    pad = jnp.zeros((r, MLA_HEADS, LANES - MLA_NOPE), w.dtype)
    wk = jnp.concatenate([wh[:, :, :MLA_NOPE], pad], axis=-1).reshape(r, MLA_HEADS * LANES)
    wv = wh[:, :, MLA_NOPE:].reshape(r, MLA_HEADS * MLA_V).T
    return _bf(wk), _bf(wv)


def _rope_angles(positions, dim):
    inv_freq = ROPE_BASE ** (-jnp.arange(0, dim, 2, dtype=F32) / dim)
    ang = positions.astype(F32).reshape(-1)[:, None] * inv_freq
    return jnp.cos(ang), jnp.sin(ang)


def _expand_lanes(compact, placement):
    return jnp.dot(compact, jnp.asarray(placement), precision=lax.Precision.HIGHEST)


def _mla_rope_tables(positions):
    cos, sin = _rope_angles(positions, MLA_ROPE)
    half = MLA_ROPE // 2
    place = np.zeros((2 * half, 3 * LANES), np.float32)
    for j in range(half):
        place[j, MLA_NOPE + j] = place[j, MLA_NOPE + half + j] = 1.0
        place[half + j, LANES + MLA_NOPE + j] = -1.0
        place[half + j, 2 * LANES + MLA_NOPE + half + j] = 1.0
    ones = np.zeros((1, 3 * LANES), np.float32)
    ones[0, :MLA_NOPE] = 1.0
    return _expand_lanes(jnp.concatenate([cos, sin], axis=1), place) + jnp.asarray(ones)


def _ret_rope_tables(positions):
    cos, sin = _rope_angles(positions, RET_DK)
    half = RET_DK // 2
    place = np.zeros((2 * half, 2 * LANES), np.float32)
    for j in range(half):
        place[j, j] = place[j, half + j] = 1.0
        place[half + j, LANES + j] = -1.0
        place[half + j, LANES + half + j] = 1.0
    return _expand_lanes(jnp.concatenate([cos, sin], axis=1), place)


def _router_weights(w_group, b_group, w_expert, b_expert):
    d = w_group.shape[0]
    pad = LANES - N_EXPERTS - N_GROUPS
    w = jnp.concatenate([w_expert, w_group, jnp.zeros((d, pad), F32)], axis=1)
    b = jnp.concatenate([b_expert, b_group, jnp.zeros((pad,), F32)]).reshape(1, LANES)
    hi = _bf(w)
    return hi, _bf(w - hi.astype(F32)), b


def _pad_lanes(v):
    return jnp.zeros((LANES,), F32).at[:v.shape[0]].set(v)


def kernel(x, c, positions, ada_w, ada_b, ln_mix_g, ln_mix_b, ln_ffn_g, ln_ffn_b, hy_w_in, mla_q_norm, mla_w_uq, mla_kv_norm, mla_w_ukv, gdn_conv_w, gdn_a_log, gdn_dt_bias, gdn_norm, hy_w_out, ret_w_in, ret_gn_g, ret_gn_b, ret_w_out, moe_w_group, moe_b_group, moe_w_expert, moe_b_expert, moe_w_gate, moe_w_up, moe_w_down):
    bsz, seq, d = x.shape
    t = bsz * seq
    mods = _ada_mod(c, ada_w, ada_b)
    mla_tabs = _mla_rope_tables(positions)
    ret_tabs = _ret_rope_tables(positions)
    log_gamma = jnp.log(1.0 - 2.0 ** (-5.0 - jnp.arange(RET_HEADS, dtype=F32)))
    lg = jnp.broadcast_to(log_gamma[:, None, None], (RET_HEADS, 1, LANES))
    xc = x.reshape(t, d)
    for layer in range(DEPTH):
        i = layer // 2
        mod_l = mods[layer]
        if layer % 2 == 0:
            wuk_p, wuv = _relayout_w_ukv(mla_w_ukv[i])
            gparam = jnp.stack([_pad_lanes(gdn_a_log[i]), _pad_lanes(gdn_dt_bias[i])])
            (qm, km, vm, gq, gk, gv, gate, gcol, grow) = _even_front(
                xc, mod_l, _relayout_hy_w_in(hy_w_in[i]), mla_q_norm[i].reshape(1, -1), _relayout_w_uq(mla_w_uq[i]),
                mla_kv_norm[i].reshape(1, -1), wuk_p, wuv, gdn_conv_w[i], gparam, mla_tabs, seq)
            o_mla = _attention(qm, km, vm, bsz, seq)
            n_ch = t // GDN_CHUNK
            growp = grow[:GDN_HEADS].reshape(GDN_PAIRS, 2, n_ch, GDN_CHUNK).transpose(2, 0, 1, 3).reshape(n_ch, GDN_PAIRS, LANES)
            onorm2 = jnp.concatenate([gdn_norm[i], gdn_norm[i]]).reshape(1, LANES)
            y_gdn = _gdn(gq, gk, gv, gate, gcol, growp, onorm2, bsz, seq)
            acts = [o_mla, y_gdn]
            w_out = _bf(hy_w_out[i])
        else:
            q, k, v, gate = _odd_front(xc, mod_l, _bf(ret_w_in[i]), ret_tabs, seq)
            og = _retention(q, k, v, gate, lg, ret_gn_g[i].reshape(RET_HEADS, 1, RET_DV),
                            ret_gn_b[i].reshape(RET_HEADS, 1, RET_DV), bsz, seq)
            acts = [og]
            w_out = _bf(ret_w_out[i])
        wr_hi, wr_lo, br = _router_weights(moe_w_group[layer], moe_b_group[layer], moe_w_expert[layer], moe_b_expert[layer])
        x1, h2, route, er, counts = _mixer_back(acts, w_out, xc, mod_l, ln_mix_g[layer].reshape(1, d), ln_mix_b[layer].reshape(1, d),
                                    wr_hi, wr_lo, br, seq)
        slot, block_expert, n_used, pad, n_slots = _slot_tables(er, counts, t)
        xb = _scatter_rows(slot, pad, h2, n_slots, seq)
        yb = _moe_experts(block_expert, n_used, xb, moe_w_gate, moe_w_up, moe_w_down, layer)
        xc = _moe_back(slot, x1, yb, route, mod_l, ln_ffn_g[layer].reshape(1, d), ln_ffn_b[layer].reshape(1, d), seq)
    return xc.reshape(bsz, seq, d)
```

```python
import functools

import jax
import jax.numpy as jnp
import numpy as np
from jax import lax
from jax.experimental import pallas as pl
from jax.experimental.pallas import tpu as pltpu

F32 = jnp.float32
BF16 = jnp.bfloat16

DEPTH = 4
N_MOD = 6
LANES = 128

MLA_HEADS = 8
MLA_NOPE = 64
MLA_ROPE = 32
MLA_V = 64
MLA_Q_RANK = 256
MLA_KV_RANK = 128
MLA_SCALE = (MLA_NOPE + MLA_ROPE) ** -0.5
LOG2_E = 1.4426950408889634

GDN_HEADS = 8
GDN_DK = 64
GDN_DV = 64
GDN_CONV = 4
GDN_CHUNK = 64
GDN_HD = GDN_HEADS * GDN_DK
GDN_QKV = 3 * GDN_HD
GDN_PAIRS = GDN_HEADS // 2
GDN_GROUP = 4

RET_HEADS = 8
RET_DK = 128
RET_DV = 256
RET_CHUNK = 256

N_GROUPS = 4
EXPERTS_PER_GROUP = 8
N_EXPERTS = 32
EXPERT_FF = 512
MOE_BLOCK = 512

ROPE_BASE = 10000.0
LN_EPS = 1e-5
RMS_EPS = 1e-6
DEEPNORM_ALPHA = (2.0 * DEPTH) ** 0.25

ROW_TILE = 256
ATTN_TILE = 1024
VMEM_LIMIT = 56 * 1024 * 1024

_C_MLA = 0
_C_QKV = 512
_C_GATE = 2048
_C_AB = 2560
_C_END = 2688


def _bf(x):
    return x.astype(BF16)


def _dot(a, b):
    return jnp.dot(a, b, preferred_element_type=F32)


def _dot_nt(a, b):
    return lax.dot_general(a, b, (((1,), (1,)), ((), ())), preferred_element_type=F32)


def _dot_tn(a, b):
    return lax.dot_general(a, b, (((0,), (0,)), ((), ())), preferred_element_type=F32)


def _split2(x):
    hi = _bf(x)
    return hi, _bf(x - hi.astype(F32))


def _split3(x):
    hi = _bf(x)
    r = x - hi.astype(F32)
    mid = _bf(r)
    return hi, mid, _bf(r - mid.astype(F32))


def _silu(x):
    return x * jax.nn.sigmoid(x)


def _pack_bf16_pairs(x):
    n = x.shape[1] // 2
    bits = lambda v: lax.bitcast_convert_type(_bf(v).astype(F32), jnp.uint32)
    return (bits(x[:, n:]) & jnp.uint32(0xFFFF0000)) | (bits(x[:, :n]) >> 16)


def _unpack_bf16_pairs(w):
    lo = lax.bitcast_convert_type(w << 16, F32)
    hi = lax.bitcast_convert_type(w & jnp.uint32(0xFFFF0000), F32)
    return lo, hi


def _layer_norm(z, g, b):
    mu = jnp.mean(z, -1, keepdims=True)
    zc = z - mu
    var = jnp.mean(zc * zc, -1, keepdims=True)
    return zc * lax.rsqrt(var + LN_EPS) * g + b


def _rms_norm(x, g):
    return x * lax.rsqrt(jnp.mean(x * x, -1, keepdims=True) + RMS_EPS) * g


def _params(sem):
    return pltpu.CompilerParams(dimension_semantics=sem, vmem_limit_bytes=VMEM_LIMIT)


def _const_spec(shape):
    nd = len(shape)
    return pl.BlockSpec(shape, lambda *_: (0,) * nd)


def _ada_kernel(c_ref, w_ref, b_ref, o_ref):
    c = c_ref[...]
    o_ref[...] = _dot(_bf(_silu(c)), _bf(w_ref[...])) + b_ref[...]


def _ada_mod(c, ada_w, ada_b):
    depth, d, n = ada_w.shape
    bsz = c.shape[0]
    rows = 16
    tn = 1536
    cp = jnp.zeros((rows, d), F32).at[:bsz].set(c)
    out = pl.pallas_call(
        _ada_kernel,
        grid=(depth, n // tn),
        in_specs=[pl.BlockSpec((rows, d), lambda l, j: (0, 0)),
                  pl.BlockSpec((None, d, tn), lambda l, j: (l, 0, j)),
                  pl.BlockSpec((None, 1, tn), lambda l, j: (l, 0, j))],
        out_specs=pl.BlockSpec((None, rows, tn), lambda l, j: (l, 0, j)),
        out_shape=jax.ShapeDtypeStruct((depth, rows, n), F32),
        compiler_params=_params(("parallel", "parallel")),
    )(cp, ada_w, ada_b.reshape(depth, 1, n))
    return out[:, :bsz].reshape(depth, bsz, N_MOD, d)


def _even_front_kernel(x_ref, mod_ref, w_ref, qn_ref, wuq_ref, kvn_ref, wuk_ref, wuv_ref,
                       cw_ref, gp_ref, tab_ref, ones_ref, tri_ref,
                       q_ref, k_ref, v_ref, gq_ref, gk_ref, gv_ref, gate_ref, gcol_ref, grow_ref,
                       xbuf, *, tm, tiles_per_seq):
    i = pl.program_id(0)
    hb = _bf(x_ref[...] * (1.0 + mod_ref[1:2, :]) + mod_ref[0:1, :])

    pm = _dot(hb, w_ref[:, _C_MLA:_C_QKV])
    ct = tab_ref[:, 0:LANES]
    s1 = tab_ref[:, LANES:2 * LANES]
    s2 = tab_ref[:, 2 * LANES:3 * LANES]

    def rope(b):
        return b * ct + pltpu.roll(b, LANES - 16, 1) * s1 + pltpu.roll(b, 16, 1) * s2

    qa = _dot(_bf(_rms_norm(pm[:, 0:MLA_Q_RANK], qn_ref[...])), wuq_ref[...])
    for h in range(MLA_HEADS):
        sl = slice(h * LANES, (h + 1) * LANES)
        q_ref[:, sl] = _bf(rope(qa[:, sl]) * (MLA_SCALE * LOG2_E))
    kvn = _bf(_rms_norm(pm[:, MLA_Q_RANK:MLA_Q_RANK + MLA_KV_RANK], kvn_ref[...]))
    ka = _dot(kvn, wuk_ref[...])
    kr = rope(pm[:, MLA_Q_RANK + MLA_KV_RANK:_C_QKV])
    for h in range(MLA_HEADS):
        sl = slice(h * LANES, (h + 1) * LANES)
        k_ref[:, sl] = _bf(ka[:, sl] + kr)
    v_ref[...] = _bf(_dot_nt(wuv_ref[...], kvn))

    @pl.when(i % tiles_per_seq == 0)
    def _():
        xbuf[0:8, :] = jnp.zeros((8, GDN_QKV), F32)

    xbuf[8:8 + tm, :] = _dot(hb, w_ref[:, _C_QKV:_C_GATE])
    cw = cw_ref[...]
    y = cw[0:1, :] * xbuf[5:5 + tm, :]
    for t in range(1, GDN_CONV):
        y = y + cw[t:t + 1, :] * xbuf[5 + t:5 + t + tm, :]
    xbuf[0:8, :] = xbuf[tm:tm + 8, :]
    y = _silu(y)

    def head_sumsq(z):
        return _dot(_bf(z * z), ones_ref[...])

    qg = y[:, 0:GDN_HD]
    kg = y[:, GDN_HD:2 * GDN_HD]
    gq_ref[...] = qg * lax.rsqrt(head_sumsq(qg) + RMS_EPS) * (GDN_DK ** -0.5)
    gk_ref[...] = kg * lax.rsqrt(head_sumsq(kg) + RMS_EPS)
    gv_ref[...] = y[:, 2 * GDN_HD:3 * GDN_HD]
    gate_ref[...] = _dot(hb, w_ref[:, _C_GATE:_C_AB])

    ab = _dot(hb, w_ref[:, _C_AB:_C_END])
    lane = lax.broadcasted_iota(jnp.int32, (tm, LANES), 1)
    z = ab + gp_ref[1:2, :]
    softplus = jnp.maximum(z, 0.0) + jnp.log(1.0 + jnp.exp(-jnp.abs(z)))
    la = jnp.where(lane < GDN_HEADS, -jnp.exp(gp_ref[0:1, :]) * softplus, 0.0)
    tri = tri_ref[...]
    l0, l1, l2 = _split3(la)
    g = _dot(tri, l0) + _dot(tri, l1) + _dot(tri, l2)
    gcol = jnp.where(lane < GDN_HEADS, g, jax.nn.sigmoid(ab))
    gcol_ref[...] = gcol
    grow_ref[...] = gcol.T[0:16, :]


def _even_front(x2, mod_l, w_in_p, qn, wuq_p, kvn, wuk_p, wuv, conv_w, gparam, tabs, seq):
    t, d = x2.shape
    tm = min(ROW_TILE, seq)
    tiles_per_seq = seq // tm
    row = lambda n: pl.BlockSpec((tm, n), lambda i: (i, 0))
    idx = np.arange(GDN_HD) // GDN_DK
    head_ones = jnp.asarray(idx[:, None] == idx[None, :], BF16)
    pos = np.arange(tm)
    chunk_tri = jnp.asarray((pos[:, None] // GDN_CHUNK == pos[None, :] // GDN_CHUNK) & (pos[None, :] <= pos[:, None]), BF16)
    out_shapes = (
        jax.ShapeDtypeStruct((t, MLA_HEADS * LANES), BF16),
        jax.ShapeDtypeStruct((t, MLA_HEADS * LANES), BF16),
        jax.ShapeDtypeStruct((MLA_HEADS * MLA_V, t), BF16),
        jax.ShapeDtypeStruct((t, GDN_HD), F32),
        jax.ShapeDtypeStruct((t, GDN_HD), F32),
        jax.ShapeDtypeStruct((t, GDN_HD), F32),
        jax.ShapeDtypeStruct((t, GDN_HD), F32),
        jax.ShapeDtypeStruct((t, LANES), F32),
        jax.ShapeDtypeStruct((16, t), F32),
    )
    return pl.pallas_call(
        functools.partial(_even_front_kernel, tm=tm, tiles_per_seq=tiles_per_seq),
        grid=(t // tm,),
        in_specs=[row(d),
                  pl.BlockSpec((None, N_MOD, d), lambda i: (i // tiles_per_seq, 0, 0)),
                  _const_spec(w_in_p.shape), _const_spec(qn.shape), _const_spec(wuq_p.shape),
                  _const_spec(kvn.shape), _const_spec(wuk_p.shape), _const_spec(wuv.shape),
                  _const_spec(conv_w.shape), _const_spec(gparam.shape), row(3 * LANES),
                  _const_spec(head_ones.shape), _const_spec(chunk_tri.shape)],
        out_specs=(row(MLA_HEADS * LANES), row(MLA_HEADS * LANES),
                   pl.BlockSpec((MLA_HEADS * MLA_V, tm), lambda i: (0, i)),
                   row(GDN_HD), row(GDN_HD), row(GDN_HD), row(GDN_HD), row(LANES),
                   pl.BlockSpec((16, tm), lambda i: (0, i))),
        out_shape=out_shapes,
        scratch_shapes=[pltpu.VMEM((tm + 8, GDN_QKV), F32)],
        compiler_params=_params(("arbitrary",)),
    )(x2, mod_l, w_in_p, qn, wuq_p, kvn, wuk_p, wuv, conv_w, gparam, tabs, head_ones, chunk_tri)


def _attn_kernel(qi_ref, ki_ref, q_ref, k_ref, vt_ref, o_ref, m_ref, l_ref, acc_ref, *, tq):
    qi = qi_ref[pl.program_id(1)]
    ki = ki_ref[pl.program_id(1)]

    @pl.when(ki == 0)
    def _():
        m_ref[...] = jnp.full(m_ref.shape, -jnp.inf, F32)
        l_ref[...] = jnp.zeros(l_ref.shape, F32)
        acc_ref[...] = jnp.zeros(acc_ref.shape, F32)

    def scores(h):
        sl = slice(h * LANES, (h + 1) * LANES)
        return _dot_nt(k_ref[:, sl], q_ref[:, sl])

    def step(diagonal):
        if diagonal:
            key = lax.broadcasted_iota(jnp.int32, (tq, tq), 0)
            qry = lax.broadcasted_iota(jnp.int32, (tq, tq), 1)
            keep = key <= qry
        s_next = scores(0)
        for h in range(MLA_HEADS):
            s = s_next
            if h + 1 < MLA_HEADS:
                s_next = scores(h + 1)
            if diagonal:
                s = jnp.where(keep, s, -jnp.inf)
            rows = slice(h * MLA_V, (h + 1) * MLA_V)
            m_prev = m_ref[h:h + 1, :]
            m_new = jnp.maximum(m_prev, jnp.max(s, 0, keepdims=True))
            alpha = jnp.exp2(m_prev - m_new)
            p = jnp.exp2(s - m_new)
            l_ref[h:h + 1, :] = alpha * l_ref[h:h + 1, :] + jnp.sum(p, 0, keepdims=True)
            acc_ref[rows, :] = alpha * acc_ref[rows, :] + _dot(vt_ref[rows, :], _bf(p))
            m_ref[h:h + 1, :] = m_new

    @pl.when(ki < qi)
    def _():
        step(False)

    @pl.when(ki == qi)
    def _():
        step(True)
        for h in range(MLA_HEADS):
            rows = slice(h * MLA_V, (h + 1) * MLA_V)
            acc_ref[rows, :] = acc_ref[rows, :] / l_ref[h:h + 1, :]
        o_ref[...] = _bf(acc_ref[...].T)


def _attention(qm, km, vm, bsz, seq):
    t = qm.shape[0]
    tq = min(ATTN_TILE, seq)
    nq = seq // tq
    pairs = [(q, k) for q in range(nq) for k in range(q + 1)]
    qi_tab = jnp.asarray(np.array([p[0] for p in pairs], np.int32))
    ki_tab = jnp.asarray(np.array([p[1] for p in pairs], np.int32))
    grid_spec = pltpu.PrefetchScalarGridSpec(
        num_scalar_prefetch=2,
        grid=(bsz, len(pairs)),
        in_specs=[pl.BlockSpec((tq, MLA_HEADS * LANES), lambda b, s, qt, kt: (b * nq + qt[s], 0)),
                  pl.BlockSpec((tq, MLA_HEADS * LANES), lambda b, s, qt, kt: (b * nq + kt[s], 0)),
                  pl.BlockSpec((MLA_HEADS * MLA_V, tq), lambda b, s, qt, kt: (0, b * nq + kt[s]))],
        out_specs=pl.BlockSpec((tq, MLA_HEADS * MLA_V), lambda b, s, qt, kt: (b * nq + qt[s], 0)),
        scratch_shapes=[pltpu.VMEM((MLA_HEADS, tq), F32),
                        pltpu.VMEM((MLA_HEADS, tq), F32),
                        pltpu.VMEM((MLA_HEADS * MLA_V, tq), F32)],
    )
    return pl.pallas_call(
        functools.partial(_attn_kernel, tq=tq),
        grid_spec=grid_spec,
        out_shape=jax.ShapeDtypeStruct((t, MLA_HEADS * MLA_V), BF16),
        compiler_params=_params(("parallel", "arbitrary")),
    )(qi_tab, ki_tab, qm, km, vm)


def _gdn_kernel(q_ref, k_ref, v_ref, gate_ref, gcol_ref, grow_ref, on_ref, y_ref, state_ref, *, n_chunks):
    C = GDN_CHUNK

    @pl.when(pl.program_id(1) == 0)
    def _():
        state_ref[...] = jnp.zeros(state_ref.shape, F32)

    lane = lax.broadcasted_iota(jnp.int32, (C, LANES), 1)
    rowi = lax.broadcasted_iota(jnp.int32, (C, LANES), 0)
    first = lane < GDN_DK
    col = jnp.where(first, lane, lane - GDN_DK)
    tril = col <= rowi
    strict = col < rowi
    eye = jnp.where(col == rowi, 1.0, 0.0)
    blk16 = (col // 16) == (rowi // 16)
    blk32 = (col // 32) == (rowi // 32)
    r2 = lax.broadcasted_iota(jnp.int32, (LANES, LANES), 0)
    c2 = lax.broadcasted_iota(jnp.int32, (LANES, LANES), 1)
    first_rows = r2 < GDN_DK
    head_ones = jnp.where((r2 < GDN_DK) == (c2 < GDN_DK), 1.0, 0.0).astype(BF16)

    def bd(y):
        return jnp.concatenate([jnp.where(first, y, 0), jnp.where(first, 0, y)], axis=0)

    def pmm(x, y):
        return _dot(_bf(x), bd(_bf(y)))

    def pmm_hi(x, y):
        xh, xl = _split2(x)
        yh, yl = _split2(y)
        top = _dot(jnp.concatenate([xh, xl], axis=0), bd(yh))
        return top[:C] + top[C:] + _dot(xh, bd(yl))

    def pair_cols(a, c0, c1):
        return jnp.where(first, a[:, c0:c0 + 1], a[:, c1:c1 + 1])

    def chunk_group(gi, carry):
        pairs = range(GDN_PAIRS)
        items = [(c, p) for c in range(GDN_GROUP) for p in pairs]
        sls = [slice(p * LANES, (p + 1) * LANES) for p in pairs]
        rows = [pl.ds(pl.multiple_of((gi * GDN_GROUP + c) * C, C), C) for c in range(GDN_GROUP)]
        gc = [gcol_ref[rows[c], :] for c in range(GDN_GROUP)]
        gr = [grow_ref[gi * GDN_GROUP + c] for c in range(GDN_GROUP)]
        every = range(len(items))
        q = [q_ref[rows[c], sls[p]] for c, p in items]
        k = [k_ref[rows[c], sls[p]] for c, p in items]
        v = [v_ref[rows[c], sls[p]] for c, p in items]
        g = [pair_cols(gc[c], 2 * p, 2 * p + 1) for c, p in items]
        beta = [pair_cols(gc[c], GDN_HEADS + 2 * p, GDN_HEADS + 2 * p + 1) for c, p in items]
        decay = [jnp.where(tril, jnp.exp(jnp.where(tril, g[i] - gr[c][p:p + 1, :], 0.0)), 0.0)
                 for i, (c, p) in enumerate(items)]
        kb = [k[i] * beta[i] for i in every]
        kbd = [bd(_bf(k[i])) for i in every]
        lmat = [jnp.where(strict, _dot_nt(_bf(kb[i]), kbd[i]) * decay[i], 0.0) for i in every]
        a_intra = [jnp.where(tril, _dot_nt(_bf(q[i]), kbd[i]) * decay[i], 0.0) for i in every]
        n1 = [jnp.where(blk16, lmat[i], 0.0) for i in every]
        x = [eye - n1[i] for i in every]
        nk = n1
        for mm in (pmm_hi, pmm, pmm):
            nk = [mm(nk[i], nk[i]) for i in every]
            x = [x[i] + mm(x[i], nk[i]) for i in every]
        for off in ([jnp.where(blk32 & jnp.logical_not(blk16), lmat[i], 0.0) for i in every],
                    [jnp.where(blk32, 0.0, lmat[i]) for i in every]):
            xb = [pmm(x[i], off[i]) for i in every]
            x = [x[i] - pmm(xb[i], x[i]) for i in every]
        u = [pmm(x[i], v[i] * beta[i]) for i in every]
        w = [pmm(x[i], kb[i] * jnp.exp(g[i])) for i in every]
        for c in range(GDN_GROUP):
            idx = [c * GDN_PAIRS + p for p in pairs]
            state = [state_ref[p] for p in pairs]
            sb = [_bf(state[p]) for p in pairs]
            v_new = [u[i] - _dot(_bf(w[i]), sb[p]) for p, i in enumerate(idx)]
            o = [_dot(_bf(q[i] * jnp.exp(g[i])), sb[p]) + pmm(a_intra[i], v_new[p]) for p, i in enumerate(idx)]
            for p, i in enumerate(idx):
                g_last = g[i][C - 1:C, :]
                k_dec = k[i] * jnp.exp(g_last - g[i])
                scale = jnp.exp(jnp.where(first_rows, g_last[:, 0:1], g_last[:, GDN_DK:GDN_DK + 1]))
                state_ref[p] = state[p] * scale + _dot_tn(bd(_bf(k_dec)), bd(_bf(v_new[p])))
            for p in pairs:
                ms = _dot(_bf(o[p] * o[p]), head_ones) * (1.0 / GDN_DV)
                on = o[p] * lax.rsqrt(ms + RMS_EPS) * on_ref[...]
                y_ref[rows[c], sls[p]] = _bf(on * _silu(gate_ref[rows[c], sls[p]]))
        return carry

    lax.fori_loop(0, n_chunks // GDN_GROUP, chunk_group, 0)


def _gdn(gq, gk, gv, gate, gcol, growp, onorm2, bsz, seq):
    t = gq.shape[0]
    r = min(ROW_TILE, seq)
    nblk = seq // r
    n_chunks = r // GDN_CHUNK
    row = lambda n: pl.BlockSpec((r, n), lambda b, j: (b * nblk + j, 0))
    return pl.pallas_call(
        functools.partial(_gdn_kernel, n_chunks=n_chunks),
        grid=(bsz, nblk),
        in_specs=[row(GDN_HD), row(GDN_HD), row(GDN_HD), row(GDN_HD), row(LANES),
                  pl.BlockSpec((n_chunks, GDN_PAIRS, LANES), lambda b, j: (b * nblk + j, 0, 0)),
                  _const_spec(onorm2.shape)],
        out_specs=row(GDN_HD),
        out_shape=jax.ShapeDtypeStruct((t, GDN_HD), BF16),
        scratch_shapes=[pltpu.VMEM((GDN_PAIRS, LANES, LANES), F32)],
        compiler_params=_params(("parallel", "arbitrary")),
    )(gq, gk, gv, gate, gcol, growp, onorm2)


def _back_kernel(*refs, n_in):
    a_refs = refs[:n_in]
    (w_ref, x_ref, mod_ref, lng_ref, lnb_ref, wrh_ref, wrl_ref, br_ref, tri_ref,
     x1_ref, h2_ref, route_ref, er_ref, counts_ref, base_ref) = refs[n_in:]
    off = 0
    y = None
    for a_ref in a_refs:
        kk = a_ref.shape[1]
        part = _dot(a_ref[...], w_ref[off:off + kk, :])
        y = part if y is None else y + part
        off += kk
    z = DEEPNORM_ALPHA * x_ref[...] + (1.0 + mod_ref[2:3, :]) * y
    x1 = _layer_norm(z, lng_ref[...], lnb_ref[...])
    x1_ref[...] = x1
    h2 = x1 * (1.0 + mod_ref[4:5, :]) + mod_ref[3:4, :]
    h2_ref[...] = h2

    hh, hl = _split2(h2)
    logits = _dot(hh, wrh_ref[...]) + _dot(hh, wrl_ref[...]) + _dot(hl, wrh_ref[...]) + br_ref[...]
    lane = lax.broadcasted_iota(jnp.int32, logits.shape, 1)
    lane_f = lane.astype(F32)
    neg = -jnp.inf
    gmask = (lane >= N_EXPERTS) & (lane < N_EXPERTS + N_GROUPS)
    gl = jnp.where(gmask, logits, neg)
    gmax = jnp.max(gl, -1, keepdims=True)
    g_top = 1.0 / jnp.sum(jnp.where(gmask, jnp.exp(gl - gmax), 0.0), -1, keepdims=True)
    g_lane = jnp.min(jnp.where(gl == gmax, lane_f, 1e9), -1, keepdims=True)
    g_idx = g_lane.astype(jnp.int32) - N_EXPERTS
    emask = (lane < N_EXPERTS) & ((lane // EXPERTS_PER_GROUP) == g_idx)
    el = jnp.where(emask, logits, neg)
    m1 = jnp.max(el, -1, keepdims=True)
    i1 = jnp.min(jnp.where(el == m1, lane_f, 1e9), -1, keepdims=True)
    el2 = jnp.where(lane_f == i1, neg, el)
    m2 = jnp.max(el2, -1, keepdims=True)
    i2 = jnp.min(jnp.where(el2 == m2, lane_f, 1e9), -1, keepdims=True)
    e2 = jnp.exp(m2 - m1)
    w1 = g_top / (1.0 + e2)
    w2 = g_top * e2 / (1.0 + e2)

    @pl.when(pl.program_id(0) == 0)
    def _():
        base_ref[...] = jnp.zeros(base_ref.shape, F32)

    tm = logits.shape[0]
    tri = tri_ref[...]
    oh1 = jnp.where(lane_f == i1, 1.0, 0.0)
    oh2 = jnp.where(lane_f == i2, 1.0, 0.0)
    cs1 = _dot(tri, _bf(oh1))
    cs2 = _dot(tri, _bf(oh2))
    base = base_ref[...]
    tot1 = cs1[tm - 1:tm, :]
    r1 = jnp.sum(oh1 * (base + cs1), -1, keepdims=True) - 1.0
    r2 = jnp.sum(oh2 * (base + tot1 + cs2), -1, keepdims=True) - 1.0
    total = base + tot1 + cs2[tm - 1:tm, :]
    base_ref[...] = total
    counts_ref[...] = total
    route = jnp.where(lane == 0, i1, jnp.where(lane == 1, i2, jnp.where(lane == 2, w1, jnp.where(lane == 3, w2, 0.0))))
    route = jnp.where(lane == 4, r1, jnp.where(lane == 5, r2, route))
    route_ref[...] = route
    er_ref[...] = route.T[0:8, :]


def _mixer_back(acts, w_out, x2, mod_l, ln_g, ln_b, wr_hi, wr_lo, br, seq):
    t, d = x2.shape
    tm = min(ROW_TILE, seq)
    tiles_per_seq = seq // tm
    row = lambda n: pl.BlockSpec((tm, n), lambda i: (i, 0))
    pos = np.arange(tm)
    tri = jnp.asarray(pos[None, :] <= pos[:, None], BF16)
    return pl.pallas_call(
        functools.partial(_back_kernel, n_in=len(acts)),
        grid=(t // tm,),
        in_specs=[row(a.shape[1]) for a in acts] + [
            _const_spec(w_out.shape), row(d),
            pl.BlockSpec((None, N_MOD, d), lambda i: (i // tiles_per_seq, 0, 0)),
            _const_spec(ln_g.shape), _const_spec(ln_b.shape),
            _const_spec(wr_hi.shape), _const_spec(wr_lo.shape), _const_spec(br.shape), _const_spec(tri.shape)],
        out_specs=(row(d), row(d), row(LANES), pl.BlockSpec((8, tm), lambda i: (0, i)), _const_spec((1, LANES))),
        out_shape=(jax.ShapeDtypeStruct((t, d), F32), jax.ShapeDtypeStruct((t, d), F32),
                   jax.ShapeDtypeStruct((t, LANES), F32), jax.ShapeDtypeStruct((8, t), F32),
                   jax.ShapeDtypeStruct((1, LANES), F32)),
        scratch_shapes=[pltpu.VMEM((1, LANES), F32)],
        compiler_params=_params(("arbitrary",)),
    )(*acts, w_out, x2, mod_l, ln_g, ln_b, wr_hi, wr_lo, br, tri)


def _moe_kernel(be_ref, nb_ref, x_ref, wg_ref, wu_ref, wd_ref, y_ref, wg_s, wu_s, wd_s):
    i = pl.program_id(0)

    @pl.when(i < nb_ref[0])
    def _():
        prev = be_ref[jnp.maximum(i - 1, 0)]

        @pl.when((i == 0) | (be_ref[i] != prev))
        def _():
            wg_s[...] = _bf(wg_ref[...])
            wu_s[...] = _bf(wu_ref[...])
            wd_s[...] = _bf(wd_ref[...])

        lo, hi = _unpack_bf16_pairs(x_ref[...])
        x_lo, x_hi = _bf(lo), _bf(hi)
        half = x_lo.shape[1]
        gate = _dot(x_lo, wg_s[0:half, :]) + _dot(x_hi, wg_s[half:, :])
        up = _dot(x_lo, wu_s[0:half, :]) + _dot(x_hi, wu_s[half:, :])
        y_ref[...] = _pack_bf16_pairs(_dot(_bf(_silu(gate) * up), wd_s[...]))

    @pl.when(i >= nb_ref[0])
    def _():
        y_ref[...] = jnp.zeros(y_ref.shape, jnp.uint32)


def _moe_experts(block_expert, n_used, xb, w_gate, w_up, w_down, layer):
    n_slots = xb.shape[0]
    d, ff = w_gate.shape[-2:]
    n_blocks = n_slots // MOE_BLOCK
    grid_spec = pltpu.PrefetchScalarGridSpec(
        num_scalar_prefetch=2,
        grid=(n_blocks,),
        in_specs=[pl.BlockSpec((MOE_BLOCK, d // 2), lambda i, be, nb: (jnp.minimum(i, nb[0] - 1), 0)),
                  pl.BlockSpec((None, None, d, ff), lambda i, be, nb: (layer, be[i], 0, 0)),
                  pl.BlockSpec((None, None, d, ff), lambda i, be, nb: (layer, be[i], 0, 0)),
                  pl.BlockSpec((None, None, ff, d), lambda i, be, nb: (layer, be[i], 0, 0))],
        out_specs=pl.BlockSpec((MOE_BLOCK, d // 2), lambda i, be, nb: (i, 0)),
        scratch_shapes=[pltpu.VMEM((d, ff), BF16), pltpu.VMEM((d, ff), BF16), pltpu.VMEM((ff, d), BF16)],
    )
    return pl.pallas_call(
        _moe_kernel,
        grid_spec=grid_spec,
        out_shape=jax.ShapeDtypeStruct((n_slots, d // 2), jnp.uint32),
        compiler_params=_params(("arbitrary",)),
    )(block_expert, n_used, xb, w_gate, w_up, w_down)


def _slot_tables(er, counts_row, t):
    counts = counts_row[0, :N_EXPERTS].astype(jnp.int32)
    padded = (counts + MOE_BLOCK - 1) // MOE_BLOCK * MOE_BLOCK
    pend = jnp.cumsum(padded)
    pstart = pend - padded
    e = er[0:2].astype(jnp.int32)
    starts = jnp.sum(jnp.where(e[:, :, None] == jnp.arange(N_EXPERTS, dtype=jnp.int32), pstart, 0), axis=-1)
    slot = (starts + er[4:6].astype(jnp.int32)).reshape(2 * t)
    n_blocks = 2 * t // MOE_BLOCK + N_EXPERTS
    block_start = jnp.arange(n_blocks, dtype=jnp.int32) * MOE_BLOCK
    block_expert = jnp.minimum(jnp.sum((block_start[:, None] >= pend[None, :]).astype(jnp.int32), axis=1), N_EXPERTS - 1)
    n_used = (pend[-1] // MOE_BLOCK).astype(jnp.int32).reshape(1)
    n_slots = n_blocks * MOE_BLOCK
    tail = jnp.stack([pend[-1], (n_slots - pend[-1]) // (MOE_BLOCK // 2)])
    pad = jnp.concatenate([pstart + counts, padded - counts, tail]).astype(jnp.int32)
    return slot, block_expert, n_used, pad, n_slots


def _scatter_kernel(slot_ref, pad_ref, h_ref, xb_ref, stage_ref, zero_ref, sems, zsem, *, tm, n_steps):
    i = pl.program_id(0)

    def pad_copies(fn):
        def rows_at(pos, count, size):
            def body(j, c):
                at = pos + j * size
                if size > 1:
                    at = pl.multiple_of(at, size)
                fn(pltpu.make_async_copy(zero_ref.at[pl.ds(0, size)], xb_ref.at[pl.ds(at, size)], zsem))
                return c

            lax.fori_loop(0, count, body, 0)

        def per_expert(e, carry):
            start = pad_ref[e]
            length = pad_ref[N_EXPERTS + e]
            head = jnp.minimum((-start) & 7, length)
            rows_at(start, head, 1)
            rows_at(start + head, (length - head) >> 3, 8)
            rows_at(start + head + ((length - head) & ~7), (length - head) & 7, 1)
            return carry

        lax.fori_loop(0, N_EXPERTS, per_expert, 0)
        rows_at(pad_ref[2 * N_EXPERTS], pad_ref[2 * N_EXPERTS + 1], zero_ref.shape[0])

    @pl.when(i == 0)
    def _():
        zero_ref[...] = jnp.zeros(zero_ref.shape, jnp.uint32)
        pad_copies(lambda c: c.start())
        pad_copies(lambda c: c.wait())

    def start_rows(buf):
        for r in range(tm):
            for choice in range(2):
                dst = slot_ref[choice * (n_steps * tm) + i * tm + r]
                pltpu.make_async_copy(stage_ref.at[buf, pl.ds(r, 1)], xb_ref.at[pl.ds(dst, 1)], sems.at[buf]).start(priority=choice)

    def wait_rows(buf):
        for _ in range(2):
            pltpu.make_async_copy(stage_ref.at[buf], xb_ref.at[pl.ds(0, tm)], sems.at[buf]).wait()

    for buf in range(2):
        @pl.when(i % 2 == buf)
        def _(buf=buf):
            @pl.when(i >= 2)
            def _():
                wait_rows(buf)

            stage_ref[buf] = _pack_bf16_pairs(h_ref[...])
            start_rows(buf)

            @pl.when(i == n_steps - 1)
            def _():
                @pl.when(i >= 1)
                def _():
                    wait_rows(1 - buf)

                wait_rows(buf)


def _scatter_rows(slot, pad, h2, n_slots, seq):
    t, d = h2.shape
    tm = min(ROW_TILE, seq)
    n_steps = t // tm
    grid_spec = pltpu.PrefetchScalarGridSpec(
        num_scalar_prefetch=2,
        grid=(n_steps,),
        in_specs=[pl.BlockSpec((tm, d), lambda i, sl, pd: (i, 0))],
        out_specs=pl.BlockSpec(memory_space=pl.ANY),
        scratch_shapes=[pltpu.VMEM((2, tm, d // 2), jnp.uint32), pltpu.VMEM((MOE_BLOCK // 2, d // 2), jnp.uint32),
                        pltpu.SemaphoreType.DMA((2,)), pltpu.SemaphoreType.DMA(())],
    )
    return pl.pallas_call(
        functools.partial(_scatter_kernel, tm=tm, n_steps=n_steps),
        grid_spec=grid_spec,
        out_shape=jax.ShapeDtypeStruct((n_slots, d // 2), jnp.uint32),
        compiler_params=_params(("arbitrary",)),
    )(slot, pad, h2)


def _moe_back_kernel(slot_ref, x_ref, route_ref, mod_ref, lng_ref, lnb_ref, yb_ref, o_ref, ybuf, sems,
                     *, tm, n_steps):
    i = pl.program_id(0)

    def start_rows(tile, buf):
        for r in range(tm):
            for choice in range(2):
                src = slot_ref[choice * (n_steps * tm) + tile * tm + r]
                pltpu.make_async_copy(yb_ref.at[pl.ds(src, 1)], ybuf.at[buf, choice, pl.ds(r, 1)], sems.at[buf]).start(priority=choice)

    def wait_rows(buf):
        for choice in range(2):
            pltpu.make_async_copy(yb_ref.at[pl.ds(0, tm)], ybuf.at[buf, choice], sems.at[buf]).wait()

    @pl.when(i == 0)
    def _():
        start_rows(0, 0)

    for buf in range(2):
        @pl.when(i % 2 == buf)
        def _(buf=buf):
            wait_rows(buf)
            start_rows(jnp.minimum(i + 1, n_steps - 1), 1 - buf)
            rt = route_ref[...]
            y0 = jnp.concatenate(_unpack_bf16_pairs(ybuf[buf, 0]), axis=1)
            y1 = jnp.concatenate(_unpack_bf16_pairs(ybuf[buf, 1]), axis=1)
            y = y0 * rt[:, 2:3] + y1 * rt[:, 3:4]
            z = DEEPNORM_ALPHA * x_ref[...] + (1.0 + mod_ref[5:6, :]) * y
            o_ref[...] = _layer_norm(z, lng_ref[...], lnb_ref[...])

            @pl.when(i == n_steps - 1)
            def _():
                wait_rows(1 - buf)


def _moe_back(slot, x1, yb, route, mod_l, ln_g, ln_b, seq):
    t, d = x1.shape
    tm = min(ROW_TILE, seq)
    tiles_per_seq = seq // tm
    n_steps = t // tm
    row = lambda n: pl.BlockSpec((tm, n), lambda i, sl: (i, 0))
    const = lambda shape: pl.BlockSpec(shape, lambda i, sl: (0,) * len(shape))
    grid_spec = pltpu.PrefetchScalarGridSpec(
        num_scalar_prefetch=1,
        grid=(n_steps,),
        in_specs=[row(d), row(LANES),
                  pl.BlockSpec((None, N_MOD, d), lambda i, sl: (i // tiles_per_seq, 0, 0)),
                  const(ln_g.shape), const(ln_b.shape), pl.BlockSpec(memory_space=pl.ANY)],
        out_specs=row(d),
        scratch_shapes=[pltpu.VMEM((2, 2, tm, d // 2), jnp.uint32), pltpu.SemaphoreType.DMA((2,))],
    )
    return pl.pallas_call(
        functools.partial(_moe_back_kernel, tm=tm, n_steps=n_steps),
        grid_spec=grid_spec,
        out_shape=jax.ShapeDtypeStruct((t, d), F32),
        compiler_params=_params(("arbitrary",)),
    )(slot, x1, route, mod_l, ln_g, ln_b, yb)


def _odd_front_kernel(x_ref, mod_ref, w_ref, tab_ref, q_ref, k_ref, v_ref, gate_ref):
    hb = _bf(x_ref[...] * (1.0 + mod_ref[1:2, :]) + mod_ref[0:1, :])
    c2 = tab_ref[:, 0:LANES]
    s2 = tab_ref[:, LANES:2 * LANES]
    nqk = RET_HEADS * RET_DK
    nv = RET_HEADS * RET_DV

    def rope(b):
        return b * c2 + pltpu.roll(b, RET_DK // 2, 1) * s2

    qa = _dot(hb, w_ref[:, 0:nqk])
    ka = _dot(hb, w_ref[:, nqk:2 * nqk])
    for h in range(RET_HEADS):
        sl = slice(h * LANES, (h + 1) * LANES)
        q_ref[:, sl] = _bf(rope(qa[:, sl]) * (RET_DK ** -0.5))
        k_ref[:, sl] = _bf(rope(ka[:, sl]))
    v_ref[...] = _bf(_dot(hb, w_ref[:, 2 * nqk:2 * nqk + nv]))
    gate_ref[...] = _silu(_dot(hb, w_ref[:, 2 * nqk + nv:2 * nqk + 2 * nv]))


def _odd_front(x2, mod_l, w_in, tabs, seq):
    t, d = x2.shape
    tm = min(ROW_TILE, seq)
    tiles_per_seq = seq // tm
    nqk = RET_HEADS * RET_DK
    nv = RET_HEADS * RET_DV
    row = lambda n: pl.BlockSpec((tm, n), lambda i: (i, 0))
    return pl.pallas_call(
        _odd_front_kernel,
        grid=(t // tm,),
        in_specs=[row(d), pl.BlockSpec((None, N_MOD, d), lambda i: (i // tiles_per_seq, 0, 0)),
                  _const_spec(w_in.shape), row(2 * LANES)],
        out_specs=(row(nqk), row(nqk), row(nv), row(nv)),
        out_shape=(jax.ShapeDtypeStruct((t, nqk), BF16), jax.ShapeDtypeStruct((t, nqk), BF16),
                   jax.ShapeDtypeStruct((t, nv), BF16), jax.ShapeDtypeStruct((t, nv), F32)),
        compiler_params=_params(("parallel",)),
    )(x2, mod_l, w_in, tabs)


def _ret_kernel(q_ref, k_ref, v_ref, gate_ref, lg_ref, gng_ref, gnb_ref, o_ref, state_ref, decay_ref, *, C):
    heads = range(RET_HEADS)
    lg = [lg_ref[h][:, 0:1] for h in heads]

    @pl.when(pl.program_id(1) == 0)
    def _():
        state_ref[...] = jnp.zeros(state_ref.shape, F32)
        ri = lax.broadcasted_iota(jnp.int32, (C, C), 0)
        ci = lax.broadcasted_iota(jnp.int32, (C, C), 1)
        diff = (ri - ci).astype(F32)
        causal = ri >= ci
        for h in heads:
            decay_ref[h] = jnp.where(causal, jnp.exp(jnp.where(causal, diff, 0.0) * lg[h]), 0.0)

    idx = lax.broadcasted_iota(jnp.int32, (C, 1), 0).astype(F32)
    qs = [q_ref[:, h * RET_DK:(h + 1) * RET_DK] for h in heads]
    ks = [k_ref[:, h * RET_DK:(h + 1) * RET_DK] for h in heads]
    vs = [v_ref[:, h * RET_DV:(h + 1) * RET_DV] for h in heads]
    states = [state_ref[h] for h in heads]
    scores = [_dot_nt(qs[h], ks[h]) for h in heads]
    cross = [_dot(qs[h], _bf(states[h])) for h in heads]
    kd = [_bf(ks[h].astype(F32) * jnp.exp((C - 1.0 - idx) * lg[h])) for h in heads]
    upd = [_dot_tn(kd[h], vs[h]) for h in heads]
    inner = [_dot(_bf(scores[h] * decay_ref[h]), vs[h]) for h in heads]
    for h in heads:
        state_ref[h] = states[h] * jnp.exp(C * lg[h]) + upd[h]
        o = inner[h] + cross[h] * jnp.exp((idx + 1.0) * lg[h])
        sl = slice(h * RET_DV, (h + 1) * RET_DV)
        o_ref[:, sl] = _bf(gate_ref[:, sl] * _layer_norm(o, gng_ref[h], gnb_ref[h]))


def _retention(q, k, v, gate, lg, gn_g, gn_b, bsz, seq):
    t = q.shape[0]
    C = min(RET_CHUNK, seq)
    nc = seq // C
    blk = lambda n: pl.BlockSpec((C, n), lambda b, j: (b * nc + j, 0))
    return pl.pallas_call(
        functools.partial(_ret_kernel, C=C),
        grid=(bsz, nc),
        in_specs=[blk(RET_HEADS * RET_DK), blk(RET_HEADS * RET_DK), blk(RET_HEADS * RET_DV), blk(RET_HEADS * RET_DV),
                  _const_spec(lg.shape), _const_spec(gn_g.shape), _const_spec(gn_b.shape)],
        out_specs=blk(RET_HEADS * RET_DV),
        out_shape=jax.ShapeDtypeStruct((t, RET_HEADS * RET_DV), BF16),
        scratch_shapes=[pltpu.VMEM((RET_HEADS, RET_DK, RET_DV), F32), pltpu.VMEM((RET_HEADS, C, C), F32)],
        compiler_params=_params(("parallel", "arbitrary")),
    )(q, k, v, gate, lg, gn_g, gn_b)


def _relayout_hy_w_in(w):
    d = w.shape[0]
    z = lambda n: jnp.zeros((d, n), w.dtype)
    c_rope = MLA_Q_RANK + MLA_KV_RANK
    c_gdn = c_rope + MLA_ROPE
    c_a = c_gdn + GDN_QKV
    c_gate = c_a + 2 * GDN_HEADS
    return _bf(jnp.concatenate([
        w[:, 0:c_rope], z(MLA_NOPE), w[:, c_rope:c_gdn], z(LANES - MLA_NOPE - MLA_ROPE),
        w[:, c_gdn:c_a], w[:, c_gate:], w[:, c_a:c_gate], z(LANES - 2 * GDN_HEADS)], axis=1))


def _relayout_w_uq(w):
    r = w.shape[0]
    wh = w.reshape(r, MLA_HEADS, MLA_NOPE + MLA_ROPE)
    pad = jnp.zeros((r, MLA_HEADS, LANES - MLA_NOPE - MLA_ROPE), w.dtype)
    return _bf(jnp.concatenate([wh, pad], axis=-1).reshape(r, MLA_HEADS * LANES))


def _relayout_w_ukv(w):
    r = w.shape[0]
    wh = w.reshape(r, MLA_HEADS, MLA_NOPE + MLA_V)
    pad = jnp.zeros((r, MLA_HEADS, LANES - MLA_NOPE), w.dtype)
    wk = jnp.concatenate([wh[:, :, :MLA_NOPE], pad], axis=-1).reshape(r, MLA_HEADS * LANES)
    wv = wh[:, :, MLA_NOPE:].reshape(r, MLA_HEADS * MLA_V).T
    return _bf(wk), _bf(wv)


def _rope_angles(positions, dim):
    inv_freq = ROPE_BASE ** (-jnp.arange(0, dim, 2, dtype=F32) / dim)
    ang = positions.astype(F32).reshape(-1)[:, None] * inv_freq
    return jnp.cos(ang), jnp.sin(ang)


def _expand_lanes(compact, placement):
    return jnp.dot(compact, jnp.asarray(placement), precision=lax.Precision.HIGHEST)


def _mla_rope_tables(positions):
    cos, sin = _rope_angles(positions, MLA_ROPE)
    half = MLA_ROPE // 2
    place = np.zeros((2 * half, 3 * LANES), np.float32)
    for j in range(half):
        place[j, MLA_NOPE + j] = place[j, MLA_NOPE + half + j] = 1.0
        place[half + j, LANES + MLA_NOPE + j] = -1.0
        place[half + j, 2 * LANES + MLA_NOPE + half + j] = 1.0
    ones = np.zeros((1, 3 * LANES), np.float32)
    ones[0, :MLA_NOPE] = 1.0
    return _expand_lanes(jnp.concatenate([cos, sin], axis=1), place) + jnp.asarray(ones)


def _ret_rope_tables(positions):
    cos, sin = _rope_angles(positions, RET_DK)
    half = RET_DK // 2
    place = np.zeros((2 * half, 2 * LANES), np.float32)
    for j in range(half):
        place[j, j] = place[j, half + j] = 1.0
        place[half + j, LANES + j] = -1.0
        place[half + j, LANES + half + j] = 1.0
    return _expand_lanes(jnp.concatenate([cos, sin], axis=1), place)


def _router_weights(w_group, b_group, w_expert, b_expert):
    d = w_group.shape[0]
    pad = LANES - N_EXPERTS - N_GROUPS
    w = jnp.concatenate([w_expert, w_group, jnp.zeros((d, pad), F32)], axis=1)
    b = jnp.concatenate([b_expert, b_group, jnp.zeros((pad,), F32)]).reshape(1, LANES)
    hi = _bf(w)
    return hi, _bf(w - hi.astype(F32)), b


def _pad_lanes(v):
    return jnp.zeros((LANES,), F32).at[:v.shape[0]].set(v)


def kernel(x, c, positions, ada_w, ada_b, ln_mix_g, ln_mix_b, ln_ffn_g, ln_ffn_b, hy_w_in, mla_q_norm, mla_w_uq, mla_kv_norm, mla_w_ukv, gdn_conv_w, gdn_a_log, gdn_dt_bias, gdn_norm, hy_w_out, ret_w_in, ret_gn_g, ret_gn_b, ret_w_out, moe_w_group, moe_b_group, moe_w_expert, moe_b_expert, moe_w_gate, moe_w_up, moe_w_down):
    bsz, seq, d = x.shape
    t = bsz * seq
    mods = _ada_mod(c, ada_w, ada_b)
    mla_tabs = _mla_rope_tables(positions)
    ret_tabs = _ret_rope_tables(positions)
    log_gamma = jnp.log(1.0 - 2.0 ** (-5.0 - jnp.arange(RET_HEADS, dtype=F32)))
    lg = jnp.broadcast_to(log_gamma[:, None, None], (RET_HEADS, 1, LANES))
    xc = x.reshape(t, d)
    for layer in range(DEPTH):
        i = layer // 2
        mod_l = mods[layer]
        if layer % 2 == 0:
            wuk_p, wuv = _relayout_w_ukv(mla_w_ukv[i])
            gparam = jnp.stack([_pad_lanes(gdn_a_log[i]), _pad_lanes(gdn_dt_bias[i])])
            (qm, km, vm, gq, gk, gv, gate, gcol, grow) = _even_front(
                xc, mod_l, _relayout_hy_w_in(hy_w_in[i]), mla_q_norm[i].reshape(1, -1), _relayout_w_uq(mla_w_uq[i]),
                mla_kv_norm[i].reshape(1, -1), wuk_p, wuv, gdn_conv_w[i], gparam, mla_tabs, seq)
            o_mla = _attention(qm, km, vm, bsz, seq)
            n_ch = t // GDN_CHUNK
            growp = grow[:GDN_HEADS].reshape(GDN_PAIRS, 2, n_ch, GDN_CHUNK).transpose(2, 0, 1, 3).reshape(n_ch, GDN_PAIRS, LANES)
            onorm2 = jnp.concatenate([gdn_norm[i], gdn_norm[i]]).reshape(1, LANES)
            y_gdn = _gdn(gq, gk, gv, gate, gcol, growp, onorm2, bsz, seq)
            acts = [o_mla, y_gdn]
            w_out = _bf(hy_w_out[i])
        else:
            q, k, v, gate = _odd_front(xc, mod_l, _bf(ret_w_in[i]), ret_tabs, seq)
            og = _retention(q, k, v, gate, lg, ret_gn_g[i].reshape(RET_HEADS, 1, RET_DV),
                            ret_gn_b[i].reshape(RET_HEADS, 1, RET_DV), bsz, seq)
            acts = [og]
            w_out = _bf(ret_w_out[i])
        wr_hi, wr_lo, br = _router_weights(moe_w_group[layer], moe_b_group[layer], moe_w_expert[layer], moe_b_expert[layer])
        x1, h2, route, er, counts = _mixer_back(acts, w_out, xc, mod_l, ln_mix_g[layer].reshape(1, d), ln_mix_b[layer].reshape(1, d),
                                    wr_hi, wr_lo, br, seq)
        slot, block_expert, n_used, pad, n_slots = _slot_tables(er, counts, t)
        xb = _scatter_rows(slot, pad, h2, n_slots, seq)
        yb = _moe_experts(block_expert, n_used, xb, moe_w_gate, moe_w_up, moe_w_down, layer)
        xc = _moe_back(slot, x1, yb, route, mod_l, ln_ffn_g[layer].reshape(1, d), ln_ffn_b[layer].reshape(1, d), seq)
    return xc.reshape(bsz, seq, d)
```

```python
import functools

import jax
import jax.numpy as jnp
import numpy as np
from jax import lax
from jax.experimental import pallas as pl
from jax.experimental.pallas import tpu as pltpu

F32 = jnp.float32
BF16 = jnp.bfloat16

DEPTH = 4
N_MOD = 6
LANES = 128

MLA_HEADS = 8
MLA_NOPE = 64
MLA_ROPE = 32
MLA_V = 64
MLA_Q_RANK = 256
MLA_KV_RANK = 128
MLA_SCALE = (MLA_NOPE + MLA_ROPE) ** -0.5
LOG2_E = 1.4426950408889634

GDN_HEADS = 8
GDN_DK = 64
GDN_DV = 64
GDN_CONV = 4
GDN_CHUNK = 64
GDN_HD = GDN_HEADS * GDN_DK
GDN_QKV = 3 * GDN_HD
GDN_PAIRS = GDN_HEADS // 2
GDN_GROUP = 4

RET_HEADS = 8
RET_DK = 128
RET_DV = 256
RET_CHUNK = 256

N_GROUPS = 4
EXPERTS_PER_GROUP = 8
N_EXPERTS = 32
EXPERT_FF = 512
MOE_BLOCK = 512

ROPE_BASE = 10000.0
LN_EPS = 1e-5
RMS_EPS = 1e-6
DEEPNORM_ALPHA = (2.0 * DEPTH) ** 0.25

ROW_TILE = 256
ATTN_TILE = 1024
VMEM_LIMIT = 56 * 1024 * 1024

_C_MLA = 0
_C_QKV = 512
_C_GATE = 2048
_C_AB = 2560
_C_END = 2688


def _bf(x):
    return x.astype(BF16)


def _dot(a, b):
    return jnp.dot(a, b, preferred_element_type=F32)


def _dot_nt(a, b):
    return lax.dot_general(a, b, (((1,), (1,)), ((), ())), preferred_element_type=F32)


def _dot_tn(a, b):
    return lax.dot_general(a, b, (((0,), (0,)), ((), ())), preferred_element_type=F32)


def _split2(x):
    hi = _bf(x)
    return hi, _bf(x - hi.astype(F32))


def _split3(x):
    hi = _bf(x)
    r = x - hi.astype(F32)
    mid = _bf(r)
    return hi, mid, _bf(r - mid.astype(F32))


def _silu(x):
    return x * jax.nn.sigmoid(x)


def _pack_bf16_pairs(x):
    n = x.shape[1] // 2
    bits = lambda v: lax.bitcast_convert_type(_bf(v).astype(F32), jnp.uint32)
    return (bits(x[:, n:]) & jnp.uint32(0xFFFF0000)) | (bits(x[:, :n]) >> 16)


def _unpack_bf16_pairs(w):
    lo = lax.bitcast_convert_type(w << 16, F32)
    hi = lax.bitcast_convert_type(w & jnp.uint32(0xFFFF0000), F32)
    return lo, hi


def _layer_norm(z, g, b):
    mu = jnp.mean(z, -1, keepdims=True)
    zc = z - mu
    var = jnp.mean(zc * zc, -1, keepdims=True)
    return zc * lax.rsqrt(var + LN_EPS) * g + b


def _rms_norm(x, g):
    return x * lax.rsqrt(jnp.mean(x * x, -1, keepdims=True) + RMS_EPS) * g


def _params(sem):
    return pltpu.CompilerParams(dimension_semantics=sem, vmem_limit_bytes=VMEM_LIMIT)


def _const_spec(shape):
    nd = len(shape)
    return pl.BlockSpec(shape, lambda *_: (0,) * nd)


def _ada_kernel(c_ref, w_ref, b_ref, o_ref):
    c = c_ref[...]
    o_ref[...] = _dot(_bf(_silu(c)), _bf(w_ref[...])) + b_ref[...]


def _ada_mod(c, ada_w, ada_b):
    depth, d, n = ada_w.shape
    bsz = c.shape[0]
    rows = 16
    tn = 1536
    cp = jnp.zeros((rows, d), F32).at[:bsz].set(c)
    out = pl.pallas_call(
        _ada_kernel,
        grid=(depth, n // tn),
        in_specs=[pl.BlockSpec((rows, d), lambda l, j: (0, 0)),
                  pl.BlockSpec((None, d, tn), lambda l, j: (l, 0, j)),
                  pl.BlockSpec((None, 1, tn), lambda l, j: (l, 0, j))],
        out_specs=pl.BlockSpec((None, rows, tn), lambda l, j: (l, 0, j)),
        out_shape=jax.ShapeDtypeStruct((depth, rows, n), F32),
        compiler_params=_params(("parallel", "parallel")),
    )(cp, ada_w, ada_b.reshape(depth, 1, n))
    return out[:, :bsz].reshape(depth, bsz, N_MOD, d)


def _even_front_kernel(*refs, fused, tm, tiles_per_seq, n_steps):
    n_head = N_COMBINE_IN if fused else 1
    (mod_ref, w_ref, qn_ref, wuq_ref, kvn_ref, wuk_ref, wuv_ref, cw_ref, gp_ref, tab_ref, ones_ref, tri_ref,
     q_ref, k_ref, v_ref, gq_ref, gk_ref, gv_ref, gate_ref, gcol_ref, grow_ref) = refs[n_head:n_head + 21]
    if fused:
        xo_ref, xbuf, ybuf, sems = refs[n_head + 21:]
        x, drain = _fused_combine(*refs[:n_head], xo_ref, ybuf, sems, tm, n_steps)
    else:
        (xbuf,) = refs[n_head + 21:]
        x = refs[0][...]
    i = pl.program_id(0)
    hb = _bf(x * (1.0 + mod_ref[1:2, :]) + mod_ref[0:1, :])

    pm = _dot(hb, w_ref[:, _C_MLA:_C_QKV])
    ct = tab_ref[:, 0:LANES]
    s1 = tab_ref[:, LANES:2 * LANES]
    s2 = tab_ref[:, 2 * LANES:3 * LANES]

    def rope(b):
        return b * ct + pltpu.roll(b, LANES - 16, 1) * s1 + pltpu.roll(b, 16, 1) * s2

    qa = _dot(_bf(_rms_norm(pm[:, 0:MLA_Q_RANK], qn_ref[...])), wuq_ref[...])
    for h in range(MLA_HEADS):
        sl = slice(h * LANES, (h + 1) * LANES)
        q_ref[:, sl] = _bf(rope(qa[:, sl]) * (MLA_SCALE * LOG2_E))
    kvn = _bf(_rms_norm(pm[:, MLA_Q_RANK:MLA_Q_RANK + MLA_KV_RANK], kvn_ref[...]))
    ka = _dot(kvn, wuk_ref[...])
    kr = rope(pm[:, MLA_Q_RANK + MLA_KV_RANK:_C_QKV])
    for h in range(MLA_HEADS):
        sl = slice(h * LANES, (h + 1) * LANES)
        k_ref[:, sl] = _bf(ka[:, sl] + kr)
    v_ref[...] = _bf(_dot_nt(wuv_ref[...], kvn))

    @pl.when(i % tiles_per_seq == 0)
    def _():
        xbuf[0:8, :] = jnp.zeros((8, GDN_QKV), F32)

    xbuf[8:8 + tm, :] = _dot(hb, w_ref[:, _C_QKV:_C_GATE])
    cw = cw_ref[...]
    y = cw[0:1, :] * xbuf[5:5 + tm, :]
    for t in range(1, GDN_CONV):
        y = y + cw[t:t + 1, :] * xbuf[5 + t:5 + t + tm, :]
    xbuf[0:8, :] = xbuf[tm:tm + 8, :]
    y = _silu(y)

    def head_sumsq(z):
        return _dot(_bf(z * z), ones_ref[...])

    qg = y[:, 0:GDN_HD]
    kg = y[:, GDN_HD:2 * GDN_HD]
    gq_ref[...] = qg * lax.rsqrt(head_sumsq(qg) + RMS_EPS) * (GDN_DK ** -0.5)
    gk_ref[...] = kg * lax.rsqrt(head_sumsq(kg) + RMS_EPS)
    gv_ref[...] = y[:, 2 * GDN_HD:3 * GDN_HD]
    gate_ref[...] = _dot(hb, w_ref[:, _C_GATE:_C_AB])

    ab = _dot(hb, w_ref[:, _C_AB:_C_END])
    lane = lax.broadcasted_iota(jnp.int32, (tm, LANES), 1)
    z = ab + gp_ref[1:2, :]
    softplus = jnp.maximum(z, 0.0) + jnp.log(1.0 + jnp.exp(-jnp.abs(z)))
    la = jnp.where(lane < GDN_HEADS, -jnp.exp(gp_ref[0:1, :]) * softplus, 0.0)
    tri = tri_ref[...]
    l0, l1, l2 = _split3(la)
    g = _dot(tri, l0) + _dot(tri, l1) + _dot(tri, l2)
    gcol = jnp.where(lane < GDN_HEADS, g, jax.nn.sigmoid(ab))
    gcol_ref[...] = gcol
    grow_ref[...] = gcol.T[0:16, :]
    if fused:
        drain()


def _even_front(src, mod_l, w_in_p, qn, wuq_p, kvn, wuk_p, wuv, conv_w, gparam, tabs, seq):
    fused = isinstance(src, tuple)
    t, d = (src[1] if fused else src).shape
    tm = min(ROW_TILE, seq)
    tiles_per_seq = seq // tm
    n_steps = t // tm
    row = lambda n: pl.BlockSpec((tm, n), lambda i, *_: (i, 0))
    idx = np.arange(GDN_HD) // GDN_DK
    head_ones = jnp.asarray(idx[:, None] == idx[None, :], BF16)
    pos = np.arange(tm)
    chunk_tri = jnp.asarray((pos[:, None] // GDN_CHUNK == pos[None, :] // GDN_CHUNK) & (pos[None, :] <= pos[:, None]), BF16)
    out_shapes = (
        jax.ShapeDtypeStruct((t, MLA_HEADS * LANES), BF16),
        jax.ShapeDtypeStruct((t, MLA_HEADS * LANES), BF16),
        jax.ShapeDtypeStruct((MLA_HEADS * MLA_V, t), BF16),
        jax.ShapeDtypeStruct((t, GDN_HD), F32),
        jax.ShapeDtypeStruct((t, GDN_HD), F32),
        jax.ShapeDtypeStruct((t, GDN_HD), F32),
        jax.ShapeDtypeStruct((t, GDN_HD), F32),
        jax.ShapeDtypeStruct((t, LANES), F32),
        jax.ShapeDtypeStruct((16, t), F32),
    )
    if fused:
        slot, head_operands, head_specs = _combine_specs(src, tm, tiles_per_seq, d)
        prefetch = [slot]
    else:
        head_operands, head_specs, prefetch = [src], [row(d)], []
    in_specs = head_specs + [
        pl.BlockSpec((None, N_MOD, d), lambda i, *_: (i // tiles_per_seq, 0, 0)),
        _const_spec(w_in_p.shape), _const_spec(qn.shape), _const_spec(wuq_p.shape),
        _const_spec(kvn.shape), _const_spec(wuk_p.shape), _const_spec(wuv.shape),
        _const_spec(conv_w.shape), _const_spec(gparam.shape), row(3 * LANES),
        _const_spec(head_ones.shape), _const_spec(chunk_tri.shape)]
    out_specs = (row(MLA_HEADS * LANES), row(MLA_HEADS * LANES),
                 pl.BlockSpec((MLA_HEADS * MLA_V, tm), lambda i, *_: (0, i)),
                 row(GDN_HD), row(GDN_HD), row(GDN_HD), row(GDN_HD), row(LANES),
                 pl.BlockSpec((16, tm), lambda i, *_: (0, i)))
    scratch = [pltpu.VMEM((tm + 8, GDN_QKV), F32)]
    if fused:
        out_specs += (row(d),)
        out_shapes += (jax.ShapeDtypeStruct((t, d), F32),)
        scratch += [pltpu.VMEM((2, 2, tm, d // 2), jnp.uint32), pltpu.SemaphoreType.DMA((2,))]
    grid_spec = pltpu.PrefetchScalarGridSpec(
        num_scalar_prefetch=len(prefetch), grid=(n_steps,), in_specs=in_specs, out_specs=out_specs,
        scratch_shapes=scratch)
    return pl.pallas_call(
        functools.partial(_even_front_kernel, fused=fused, tm=tm, tiles_per_seq=tiles_per_seq, n_steps=n_steps),
        grid_spec=grid_spec,
        out_shape=out_shapes,
        compiler_params=_params(("arbitrary",)),
    )(*prefetch, *head_operands, mod_l, w_in_p, qn, wuq_p, kvn, wuk_p, wuv, conv_w, gparam, tabs, head_ones, chunk_tri)


def _attn_kernel(qi_ref, ki_ref, q_ref, k_ref, vt_ref, o_ref, m_ref, l_ref, acc_ref, *, tq):
    qi = qi_ref[pl.program_id(1)]
    ki = ki_ref[pl.program_id(1)]

    @pl.when(ki == 0)
    def _():
        m_ref[...] = jnp.full(m_ref.shape, -jnp.inf, F32)
        l_ref[...] = jnp.zeros(l_ref.shape, F32)
        acc_ref[...] = jnp.zeros(acc_ref.shape, F32)

    def scores(h):
        sl = slice(h * LANES, (h + 1) * LANES)
        return _dot_nt(k_ref[:, sl], q_ref[:, sl])

    def step(diagonal):
        if diagonal:
            key = lax.broadcasted_iota(jnp.int32, (tq, tq), 0)
            qry = lax.broadcasted_iota(jnp.int32, (tq, tq), 1)
            keep = key <= qry
        s_next = scores(0)
        for h in range(MLA_HEADS):
            s = s_next
            if h + 1 < MLA_HEADS:
                s_next = scores(h + 1)
            if diagonal:
                s = jnp.where(keep, s, -jnp.inf)
            rows = slice(h * MLA_V, (h + 1) * MLA_V)
            m_prev = m_ref[h:h + 1, :]
            m_new = jnp.maximum(m_prev, jnp.max(s, 0, keepdims=True))
            alpha = jnp.exp2(m_prev - m_new)
            p = jnp.exp2(s - m_new)
            l_ref[h:h + 1, :] = alpha * l_ref[h:h + 1, :] + jnp.sum(p, 0, keepdims=True)
            acc_ref[rows, :] = alpha * acc_ref[rows, :] + _dot(vt_ref[rows, :], _bf(p))
            m_ref[h:h + 1, :] = m_new

    @pl.when(ki < qi)
    def _():
        step(False)

    @pl.when(ki == qi)
    def _():
        step(True)
        for h in range(MLA_HEADS):
            rows = slice(h * MLA_V, (h + 1) * MLA_V)
            acc_ref[rows, :] = acc_ref[rows, :] / l_ref[h:h + 1, :]
        o_ref[...] = _bf(acc_ref[...].T)


def _attention(qm, km, vm, bsz, seq):
    t = qm.shape[0]
    tq = min(ATTN_TILE, seq)
    nq = seq // tq
    pairs = [(q, k) for q in range(nq) for k in range(q + 1)]
    qi_tab = jnp.asarray(np.array([p[0] for p in pairs], np.int32))
    ki_tab = jnp.asarray(np.array([p[1] for p in pairs], np.int32))
    grid_spec = pltpu.PrefetchScalarGridSpec(
        num_scalar_prefetch=2,
        grid=(bsz, len(pairs)),
        in_specs=[pl.BlockSpec((tq, MLA_HEADS * LANES), lambda b, s, qt, kt: (b * nq + qt[s], 0)),
                  pl.BlockSpec((tq, MLA_HEADS * LANES), lambda b, s, qt, kt: (b * nq + kt[s], 0)),
                  pl.BlockSpec((MLA_HEADS * MLA_V, tq), lambda b, s, qt, kt: (0, b * nq + kt[s]))],
        out_specs=pl.BlockSpec((tq, MLA_HEADS * MLA_V), lambda b, s, qt, kt: (b * nq + qt[s], 0)),
        scratch_shapes=[pltpu.VMEM((MLA_HEADS, tq), F32),
                        pltpu.VMEM((MLA_HEADS, tq), F32),
                        pltpu.VMEM((MLA_HEADS * MLA_V, tq), F32)],
    )
    return pl.pallas_call(
        functools.partial(_attn_kernel, tq=tq),
        grid_spec=grid_spec,
        out_shape=jax.ShapeDtypeStruct((t, MLA_HEADS * MLA_V), BF16),
        compiler_params=_params(("parallel", "arbitrary")),
    )(qi_tab, ki_tab, qm, km, vm)


def _gdn_kernel(q_ref, k_ref, v_ref, gate_ref, gcol_ref, grow_ref, on_ref, y_ref, state_ref, *, n_chunks):
    C = GDN_CHUNK

    @pl.when(pl.program_id(1) == 0)
    def _():
        state_ref[...] = jnp.zeros(state_ref.shape, F32)

    lane = lax.broadcasted_iota(jnp.int32, (C, LANES), 1)
    rowi = lax.broadcasted_iota(jnp.int32, (C, LANES), 0)
    first = lane < GDN_DK
    col = jnp.where(first, lane, lane - GDN_DK)
    tril = col <= rowi
    strict = col < rowi
    eye = jnp.where(col == rowi, 1.0, 0.0)
    blk16 = (col // 16) == (rowi // 16)
    blk32 = (col // 32) == (rowi // 32)
    r2 = lax.broadcasted_iota(jnp.int32, (LANES, LANES), 0)
    c2 = lax.broadcasted_iota(jnp.int32, (LANES, LANES), 1)
    first_rows = r2 < GDN_DK
    head_ones = jnp.where((r2 < GDN_DK) == (c2 < GDN_DK), 1.0, 0.0).astype(BF16)

    def bd(y):
        return jnp.concatenate([jnp.where(first, y, 0), jnp.where(first, 0, y)], axis=0)

    def pmm(x, y):
        return _dot(_bf(x), bd(_bf(y)))

    def pmm_hi(x, y):
        xh, xl = _split2(x)
        yh, yl = _split2(y)
        top = _dot(jnp.concatenate([xh, xl], axis=0), bd(yh))
        return top[:C] + top[C:] + _dot(xh, bd(yl))

    def pair_cols(a, c0, c1):
        return jnp.where(first, a[:, c0:c0 + 1], a[:, c1:c1 + 1])

    def chunk_group(gi, carry):
        pairs = range(GDN_PAIRS)
        items = [(c, p) for c in range(GDN_GROUP) for p in pairs]
        sls = [slice(p * LANES, (p + 1) * LANES) for p in pairs]
        rows = [pl.ds(pl.multiple_of((gi * GDN_GROUP + c) * C, C), C) for c in range(GDN_GROUP)]
        gc = [gcol_ref[rows[c], :] for c in range(GDN_GROUP)]
        gr = [grow_ref[gi * GDN_GROUP + c] for c in range(GDN_GROUP)]
        every = range(len(items))
        q = [q_ref[rows[c], sls[p]] for c, p in items]
        k = [k_ref[rows[c], sls[p]] for c, p in items]
        v = [v_ref[rows[c], sls[p]] for c, p in items]
        g = [pair_cols(gc[c], 2 * p, 2 * p + 1) for c, p in items]
        beta = [pair_cols(gc[c], GDN_HEADS + 2 * p, GDN_HEADS + 2 * p + 1) for c, p in items]
        decay = [jnp.where(tril, jnp.exp(jnp.where(tril, g[i] - gr[c][p:p + 1, :], 0.0)), 0.0)
                 for i, (c, p) in enumerate(items)]
        kb = [k[i] * beta[i] for i in every]
        kbd = [bd(_bf(k[i])) for i in every]
        lmat = [jnp.where(strict, _dot_nt(_bf(kb[i]), kbd[i]) * decay[i], 0.0) for i in every]
        a_intra = [jnp.where(tril, _dot_nt(_bf(q[i]), kbd[i]) * decay[i], 0.0) for i in every]
        n1 = [jnp.where(blk16, lmat[i], 0.0) for i in every]
        x = [eye - n1[i] for i in every]
        nk = n1
        for mm in (pmm_hi, pmm, pmm):
            nk = [mm(nk[i], nk[i]) for i in every]
            x = [x[i] + mm(x[i], nk[i]) for i in every]
        for off in ([jnp.where(blk32 & jnp.logical_not(blk16), lmat[i], 0.0) for i in every],
                    [jnp.where(blk32, 0.0, lmat[i]) for i in every]):
            xb = [pmm(x[i], off[i]) for i in every]
            x = [x[i] - pmm(xb[i], x[i]) for i in every]
        u = [pmm(x[i], v[i] * beta[i]) for i in every]
        w = [pmm(x[i], kb[i] * jnp.exp(g[i])) for i in every]
        for c in range(GDN_GROUP):
            idx = [c * GDN_PAIRS + p for p in pairs]
            state = [state_ref[p] for p in pairs]
            sb = [_bf(state[p]) for p in pairs]
            v_new = [u[i] - _dot(_bf(w[i]), sb[p]) for p, i in enumerate(idx)]
            o = [_dot(_bf(q[i] * jnp.exp(g[i])), sb[p]) + pmm(a_intra[i], v_new[p]) for p, i in enumerate(idx)]
            for p, i in enumerate(idx):
                g_last = g[i][C - 1:C, :]
                k_dec = k[i] * jnp.exp(g_last - g[i])
                scale = jnp.exp(jnp.where(first_rows, g_last[:, 0:1], g_last[:, GDN_DK:GDN_DK + 1]))
                state_ref[p] = state[p] * scale + _dot_tn(bd(_bf(k_dec)), bd(_bf(v_new[p])))
            for p in pairs:
                ms = _dot(_bf(o[p] * o[p]), head_ones) * (1.0 / GDN_DV)
                on = o[p] * lax.rsqrt(ms + RMS_EPS) * on_ref[...]
                y_ref[rows[c], sls[p]] = _bf(on * _silu(gate_ref[rows[c], sls[p]]))
        return carry

    lax.fori_loop(0, n_chunks // GDN_GROUP, chunk_group, 0)


def _gdn(gq, gk, gv, gate, gcol, growp, onorm2, bsz, seq):
    t = gq.shape[0]
    r = min(ROW_TILE, seq)
    nblk = seq // r
    n_chunks = r // GDN_CHUNK
    row = lambda n: pl.BlockSpec((r, n), lambda b, j: (b * nblk + j, 0))
    return pl.pallas_call(
        functools.partial(_gdn_kernel, n_chunks=n_chunks),
        grid=(bsz, nblk),
        in_specs=[row(GDN_HD), row(GDN_HD), row(GDN_HD), row(GDN_HD), row(LANES),
                  pl.BlockSpec((n_chunks, GDN_PAIRS, LANES), lambda b, j: (b * nblk + j, 0, 0)),
                  _const_spec(onorm2.shape)],
        out_specs=row(GDN_HD),
        out_shape=jax.ShapeDtypeStruct((t, GDN_HD), BF16),
        scratch_shapes=[pltpu.VMEM((GDN_PAIRS, LANES, LANES), F32)],
        compiler_params=_params(("parallel", "arbitrary")),
    )(gq, gk, gv, gate, gcol, growp, onorm2)


def _back_kernel(*refs, n_in):
    a_refs = refs[:n_in]
    (w_ref, x_ref, mod_ref, lng_ref, lnb_ref, wrh_ref, wrl_ref, br_ref, tri_ref,
     x1_ref, h2_ref, route_ref, er_ref, counts_ref, base_ref) = refs[n_in:]
    off = 0
    y = None
    for a_ref in a_refs:
        kk = a_ref.shape[1]
        part = _dot(a_ref[...], w_ref[off:off + kk, :])
        y = part if y is None else y + part
        off += kk
    z = DEEPNORM_ALPHA * x_ref[...] + (1.0 + mod_ref[2:3, :]) * y
    x1 = _layer_norm(z, lng_ref[...], lnb_ref[...])
    x1_ref[...] = x1
    h2 = x1 * (1.0 + mod_ref[4:5, :]) + mod_ref[3:4, :]
    h2_ref[...] = h2

    hh, hl = _split2(h2)
    logits = _dot(hh, wrh_ref[...]) + _dot(hh, wrl_ref[...]) + _dot(hl, wrh_ref[...]) + br_ref[...]
    lane = lax.broadcasted_iota(jnp.int32, logits.shape, 1)
    lane_f = lane.astype(F32)
    neg = -jnp.inf
    gmask = (lane >= N_EXPERTS) & (lane < N_EXPERTS + N_GROUPS)
    gl = jnp.where(gmask, logits, neg)
    gmax = jnp.max(gl, -1, keepdims=True)
    g_top = 1.0 / jnp.sum(jnp.where(gmask, jnp.exp(gl - gmax), 0.0), -1, keepdims=True)
    g_lane = jnp.min(jnp.where(gl == gmax, lane_f, 1e9), -1, keepdims=True)
    g_idx = g_lane.astype(jnp.int32) - N_EXPERTS
    emask = (lane < N_EXPERTS) & ((lane // EXPERTS_PER_GROUP) == g_idx)
    el = jnp.where(emask, logits, neg)
    m1 = jnp.max(el, -1, keepdims=True)
    i1 = jnp.min(jnp.where(el == m1, lane_f, 1e9), -1, keepdims=True)
    el2 = jnp.where(lane_f == i1, neg, el)
    m2 = jnp.max(el2, -1, keepdims=True)
    i2 = jnp.min(jnp.where(el2 == m2, lane_f, 1e9), -1, keepdims=True)
    e2 = jnp.exp(m2 - m1)
    w1 = g_top / (1.0 + e2)
    w2 = g_top * e2 / (1.0 + e2)

    @pl.when(pl.program_id(0) == 0)
    def _():
        base_ref[...] = jnp.zeros(base_ref.shape, F32)

    tm = logits.shape[0]
    tri = tri_ref[...]
    oh1 = jnp.where(lane_f == i1, 1.0, 0.0)
    oh2 = jnp.where(lane_f == i2, 1.0, 0.0)
    cs1 = _dot(tri, _bf(oh1))
    cs2 = _dot(tri, _bf(oh2))
    base = base_ref[...]
    tot1 = cs1[tm - 1:tm, :]
    r1 = jnp.sum(oh1 * (base + cs1), -1, keepdims=True) - 1.0
    r2 = jnp.sum(oh2 * (base + tot1 + cs2), -1, keepdims=True) - 1.0
    total = base + tot1 + cs2[tm - 1:tm, :]
    base_ref[...] = total
    counts_ref[...] = total
    route = jnp.where(lane == 0, i1, jnp.where(lane == 1, i2, jnp.where(lane == 2, w1, jnp.where(lane == 3, w2, 0.0))))
    route = jnp.where(lane == 4, r1, jnp.where(lane == 5, r2, route))
    route_ref[...] = route
    er_ref[...] = route.T[0:8, :]


def _mixer_back(acts, w_out, x2, mod_l, ln_g, ln_b, wr_hi, wr_lo, br, seq):
    t, d = x2.shape
    tm = min(ROW_TILE, seq)
    tiles_per_seq = seq // tm
    row = lambda n: pl.BlockSpec((tm, n), lambda i: (i, 0))
    pos = np.arange(tm)
    tri = jnp.asarray(pos[None, :] <= pos[:, None], BF16)
    return pl.pallas_call(
        functools.partial(_back_kernel, n_in=len(acts)),
        grid=(t // tm,),
        in_specs=[row(a.shape[1]) for a in acts] + [
            _const_spec(w_out.shape), row(d),
            pl.BlockSpec((None, N_MOD, d), lambda i: (i // tiles_per_seq, 0, 0)),
            _const_spec(ln_g.shape), _const_spec(ln_b.shape),
            _const_spec(wr_hi.shape), _const_spec(wr_lo.shape), _const_spec(br.shape), _const_spec(tri.shape)],
        out_specs=(row(d), row(d), row(LANES), pl.BlockSpec((8, tm), lambda i: (0, i)), _const_spec((1, LANES))),
        out_shape=(jax.ShapeDtypeStruct((t, d), F32), jax.ShapeDtypeStruct((t, d), F32),
                   jax.ShapeDtypeStruct((t, LANES), F32), jax.ShapeDtypeStruct((8, t), F32),
                   jax.ShapeDtypeStruct((1, LANES), F32)),
        scratch_shapes=[pltpu.VMEM((1, LANES), F32)],
        compiler_params=_params(("arbitrary",)),
    )(*acts, w_out, x2, mod_l, ln_g, ln_b, wr_hi, wr_lo, br, tri)


def _moe_kernel(be_ref, nb_ref, x_ref, wg_ref, wu_ref, wd_ref, y_ref, wg_s, wu_s, wd_s):
    i = pl.program_id(0)

    @pl.when(i < nb_ref[0])
    def _():
        prev = be_ref[jnp.maximum(i - 1, 0)]

        @pl.when((i == 0) | (be_ref[i] != prev))
        def _():
            wg_s[...] = _bf(wg_ref[...])
            wu_s[...] = _bf(wu_ref[...])
            wd_s[...] = _bf(wd_ref[...])

        lo, hi = _unpack_bf16_pairs(x_ref[...])
        x_lo, x_hi = _bf(lo), _bf(hi)
        half = x_lo.shape[1]
        gate = _dot(x_lo, wg_s[0:half, :]) + _dot(x_hi, wg_s[half:, :])
        up = _dot(x_lo, wu_s[0:half, :]) + _dot(x_hi, wu_s[half:, :])
        y_ref[...] = _pack_bf16_pairs(_dot(_bf(_silu(gate) * up), wd_s[...]))

    @pl.when(i >= nb_ref[0])
    def _():
        y_ref[...] = jnp.zeros(y_ref.shape, jnp.uint32)


def _moe_experts(block_expert, n_used, xb, w_gate, w_up, w_down, layer):
    n_slots = xb.shape[0]
    d, ff = w_gate.shape[-2:]
    n_blocks = n_slots // MOE_BLOCK
    grid_spec = pltpu.PrefetchScalarGridSpec(
        num_scalar_prefetch=2,
        grid=(n_blocks,),
        in_specs=[pl.BlockSpec((MOE_BLOCK, d // 2), lambda i, be, nb: (jnp.minimum(i, nb[0] - 1), 0)),
                  pl.BlockSpec((None, None, d, ff), lambda i, be, nb: (layer, be[i], 0, 0)),
                  pl.BlockSpec((None, None, d, ff), lambda i, be, nb: (layer, be[i], 0, 0)),
                  pl.BlockSpec((None, None, ff, d), lambda i, be, nb: (layer, be[i], 0, 0))],
        out_specs=pl.BlockSpec((MOE_BLOCK, d // 2), lambda i, be, nb: (i, 0)),
        scratch_shapes=[pltpu.VMEM((d, ff), BF16), pltpu.VMEM((d, ff), BF16), pltpu.VMEM((ff, d), BF16)],
    )
    return pl.pallas_call(
        _moe_kernel,
        grid_spec=grid_spec,
        out_shape=jax.ShapeDtypeStruct((n_slots, d // 2), jnp.uint32),
        compiler_params=_params(("arbitrary",)),
    )(block_expert, n_used, xb, w_gate, w_up, w_down)


def _slot_tables(er, counts_row, t):
    counts = counts_row[0, :N_EXPERTS].astype(jnp.int32)
    padded = (counts + MOE_BLOCK - 1) // MOE_BLOCK * MOE_BLOCK
    pend = jnp.cumsum(padded)
    pstart = pend - padded
    e = er[0:2].astype(jnp.int32)
    starts = jnp.sum(jnp.where(e[:, :, None] == jnp.arange(N_EXPERTS, dtype=jnp.int32), pstart, 0), axis=-1)
    slot = (starts + er[4:6].astype(jnp.int32)).reshape(2 * t)
    n_blocks = 2 * t // MOE_BLOCK + N_EXPERTS
    block_start = jnp.arange(n_blocks, dtype=jnp.int32) * MOE_BLOCK
    block_expert = jnp.minimum(jnp.sum((block_start[:, None] >= pend[None, :]).astype(jnp.int32), axis=1), N_EXPERTS - 1)
    n_used = (pend[-1] // MOE_BLOCK).astype(jnp.int32).reshape(1)
    n_slots = n_blocks * MOE_BLOCK
    tail = jnp.stack([pend[-1], (n_slots - pend[-1]) // (MOE_BLOCK // 2)])
    pad = jnp.concatenate([pstart + counts, padded - counts, tail]).astype(jnp.int32)
    return slot, block_expert, n_used, pad, n_slots


def _scatter_kernel(slot_ref, pad_ref, h_ref, xb_ref, stage_ref, zero_ref, sems, zsem, *, tm, n_steps):
    i = pl.program_id(0)

    def pad_copies(fn):
        def rows_at(pos, count, size):
            def body(j, c):
                at = pos + j * size
                if size > 1:
                    at = pl.multiple_of(at, size)
                fn(pltpu.make_async_copy(zero_ref.at[pl.ds(0, size)], xb_ref.at[pl.ds(at, size)], zsem))
                return c

            lax.fori_loop(0, count, body, 0)

        def per_expert(e, carry):
            start = pad_ref[e]
            length = pad_ref[N_EXPERTS + e]
            head = jnp.minimum((-start) & 7, length)
            rows_at(start, head, 1)
            rows_at(start + head, (length - head) >> 3, 8)
            rows_at(start + head + ((length - head) & ~7), (length - head) & 7, 1)
            return carry

        lax.fori_loop(0, N_EXPERTS, per_expert, 0)
        rows_at(pad_ref[2 * N_EXPERTS], pad_ref[2 * N_EXPERTS + 1], zero_ref.shape[0])

    @pl.when(i == 0)
    def _():
        zero_ref[...] = jnp.zeros(zero_ref.shape, jnp.uint32)
        pad_copies(lambda c: c.start())
        pad_copies(lambda c: c.wait())

    def start_rows(buf):
        for r in range(tm):
            for choice in range(2):
                dst = slot_ref[choice * (n_steps * tm) + i * tm + r]
                pltpu.make_async_copy(stage_ref.at[buf, pl.ds(r, 1)], xb_ref.at[pl.ds(dst, 1)], sems.at[buf]).start(priority=choice)

    def wait_rows(buf):
        for _ in range(2):
            pltpu.make_async_copy(stage_ref.at[buf], xb_ref.at[pl.ds(0, tm)], sems.at[buf]).wait()

    for buf in range(2):
        @pl.when(i % 2 == buf)
        def _(buf=buf):
            @pl.when(i >= 2)
            def _():
                wait_rows(buf)

            stage_ref[buf] = _pack_bf16_pairs(h_ref[...])
            start_rows(buf)

            @pl.when(i == n_steps - 1)
            def _():
                @pl.when(i >= 1)
                def _():
                    wait_rows(1 - buf)

                wait_rows(buf)


def _scatter_rows(slot, pad, h2, n_slots, seq):
    t, d = h2.shape
    tm = min(ROW_TILE, seq)
    n_steps = t // tm
    grid_spec = pltpu.PrefetchScalarGridSpec(
        num_scalar_prefetch=2,
        grid=(n_steps,),
        in_specs=[pl.BlockSpec((tm, d), lambda i, sl, pd: (i, 0))],
        out_specs=pl.BlockSpec(memory_space=pl.ANY),
        scratch_shapes=[pltpu.VMEM((2, tm, d // 2), jnp.uint32), pltpu.VMEM((MOE_BLOCK // 2, d // 2), jnp.uint32),
                        pltpu.SemaphoreType.DMA((2,)), pltpu.SemaphoreType.DMA(())],
    )
    return pl.pallas_call(
        functools.partial(_scatter_kernel, tm=tm, n_steps=n_steps),
        grid_spec=grid_spec,
        out_shape=jax.ShapeDtypeStruct((n_slots, d // 2), jnp.uint32),
        compiler_params=_params(("arbitrary",)),
    )(slot, pad, h2)


def _moe_back_kernel(slot_ref, x_ref, route_ref, mod_ref, lng_ref, lnb_ref, yb_ref, o_ref, ybuf, sems,
                     *, tm, n_steps):
    i = pl.program_id(0)

    def start_rows(tile, buf):
        for r in range(tm):
            for choice in range(2):
                src = slot_ref[choice * (n_steps * tm) + tile * tm + r]
                pltpu.make_async_copy(yb_ref.at[pl.ds(src, 1)], ybuf.at[buf, choice, pl.ds(r, 1)], sems.at[buf]).start(priority=choice)

    def wait_rows(buf):
        for choice in range(2):
            pltpu.make_async_copy(yb_ref.at[pl.ds(0, tm)], ybuf.at[buf, choice], sems.at[buf]).wait()

    @pl.when(i == 0)
    def _():
        start_rows(0, 0)

    for buf in range(2):
        @pl.when(i % 2 == buf)
        def _(buf=buf):
            wait_rows(buf)
            start_rows(jnp.minimum(i + 1, n_steps - 1), 1 - buf)
            rt = route_ref[...]
            y0 = jnp.concatenate(_unpack_bf16_pairs(ybuf[buf, 0]), axis=1)
            y1 = jnp.concatenate(_unpack_bf16_pairs(ybuf[buf, 1]), axis=1)
            y = y0 * rt[:, 2:3] + y1 * rt[:, 3:4]
            z = DEEPNORM_ALPHA * x_ref[...] + (1.0 + mod_ref[5:6, :]) * y
            o_ref[...] = _layer_norm(z, lng_ref[...], lnb_ref[...])

            @pl.when(i == n_steps - 1)
            def _():
                wait_rows(1 - buf)


def _moe_back(slot, x1, yb, route, mod_l, ln_g, ln_b, seq):
    t, d = x1.shape
    tm = min(ROW_TILE, seq)
    tiles_per_seq = seq // tm
    n_steps = t // tm
    row = lambda n: pl.BlockSpec((tm, n), lambda i, sl: (i, 0))
    const = lambda shape: pl.BlockSpec(shape, lambda i, sl: (0,) * len(shape))
    grid_spec = pltpu.PrefetchScalarGridSpec(
        num_scalar_prefetch=1,
        grid=(n_steps,),
        in_specs=[row(d), row(LANES),
                  pl.BlockSpec((None, N_MOD, d), lambda i, sl: (i // tiles_per_seq, 0, 0)),
                  const(ln_g.shape), const(ln_b.shape), pl.BlockSpec(memory_space=pl.ANY)],
        out_specs=row(d),
        scratch_shapes=[pltpu.VMEM((2, 2, tm, d // 2), jnp.uint32), pltpu.SemaphoreType.DMA((2,))],
    )
    return pl.pallas_call(
        functools.partial(_moe_back_kernel, tm=tm, n_steps=n_steps),
        grid_spec=grid_spec,
        out_shape=jax.ShapeDtypeStruct((t, d), F32),
        compiler_params=_params(("arbitrary",)),
    )(slot, x1, route, mod_l, ln_g, ln_b, yb)


N_COMBINE_IN = 7


def _fused_combine(slot_ref, x1_ref, route_ref, modp_ref, lng_ref, lnb_ref, yb_ref, xo_ref, ybuf, sems, tm, n_steps):
    i = pl.program_id(0)
    buf = i % 2

    def start_rows(tile, into):
        for r in range(tm):
            for choice in range(2):
                src = slot_ref[choice * (n_steps * tm) + tile * tm + r]
                pltpu.make_async_copy(yb_ref.at[pl.ds(src, 1)], ybuf.at[into, choice, pl.ds(r, 1)],
                                      sems.at[into]).start(priority=choice)

    def wait_rows(into):
        for choice in range(2):
            pltpu.make_async_copy(yb_ref.at[pl.ds(0, tm)], ybuf.at[into, choice], sems.at[into]).wait()

    @pl.when(i == 0)
    def _():
        start_rows(0, 0)

    wait_rows(buf)
    rt = route_ref[...]
    y0 = jnp.concatenate(_unpack_bf16_pairs(ybuf[buf, 0]), axis=1)
    y1 = jnp.concatenate(_unpack_bf16_pairs(ybuf[buf, 1]), axis=1)
    y = y0 * rt[:, 2:3] + y1 * rt[:, 3:4]
    z = DEEPNORM_ALPHA * x1_ref[...] + (1.0 + modp_ref[5:6, :]) * y
    x = _layer_norm(z, lng_ref[...], lnb_ref[...])
    xo_ref[...] = x
    start_rows(jnp.minimum(i + 1, n_steps - 1), 1 - buf)

    def drain():
        @pl.when(i == n_steps - 1)
        def _():
            wait_rows(1 - buf)

    return x, drain


def _combine_specs(prev, tm, tiles_per_seq, d):
    slot, x1, yb, route, mod_prev, ln_g, ln_b = prev
    row = lambda n: pl.BlockSpec((tm, n), lambda i, *_: (i, 0))
    const = lambda shape: pl.BlockSpec(shape, lambda i, *_: (0,) * len(shape))
    specs = [row(d), row(LANES), pl.BlockSpec((None, N_MOD, d), lambda i, *_: (i // tiles_per_seq, 0, 0)),
             const(ln_g.shape), const(ln_b.shape), pl.BlockSpec(memory_space=pl.ANY)]
    return slot, [x1, route, mod_prev, ln_g, ln_b, yb], specs


def _odd_front_kernel(*refs, tm, n_steps):
    (mod_ref, w_ref, tab_ref, q_ref, k_ref, v_ref, gate_ref, xo_ref, ybuf, sems) = refs[N_COMBINE_IN:]
    x, drain = _fused_combine(*refs[:N_COMBINE_IN], xo_ref, ybuf, sems, tm, n_steps)
    hb = _bf(x * (1.0 + mod_ref[1:2, :]) + mod_ref[0:1, :])
    c2 = tab_ref[:, 0:LANES]
    s2 = tab_ref[:, LANES:2 * LANES]
    nqk = RET_HEADS * RET_DK
    nv = RET_HEADS * RET_DV

    def rope(b):
        return b * c2 + pltpu.roll(b, RET_DK // 2, 1) * s2

    qa = _dot(hb, w_ref[:, 0:nqk])
    ka = _dot(hb, w_ref[:, nqk:2 * nqk])
    for h in range(RET_HEADS):
        sl = slice(h * LANES, (h + 1) * LANES)
        q_ref[:, sl] = _bf(rope(qa[:, sl]) * (RET_DK ** -0.5))
        k_ref[:, sl] = _bf(rope(ka[:, sl]))
    v_ref[...] = _bf(_dot(hb, w_ref[:, 2 * nqk:2 * nqk + nv]))
    gate_ref[...] = _silu(_dot(hb, w_ref[:, 2 * nqk + nv:2 * nqk + 2 * nv]))
    drain()


def _odd_front(prev, mod_l, w_in, tabs, seq):
    t, d = prev[1].shape
    tm = min(ROW_TILE, seq)
    tiles_per_seq = seq // tm
    n_steps = t // tm
    nqk = RET_HEADS * RET_DK
    nv = RET_HEADS * RET_DV
    row = lambda n: pl.BlockSpec((tm, n), lambda i, *_: (i, 0))
    const = lambda shape: pl.BlockSpec(shape, lambda i, *_: (0,) * len(shape))
    slot, operands, specs = _combine_specs(prev, tm, tiles_per_seq, d)
    grid_spec = pltpu.PrefetchScalarGridSpec(
        num_scalar_prefetch=1,
        grid=(n_steps,),
        in_specs=specs + [pl.BlockSpec((None, N_MOD, d), lambda i, *_: (i // tiles_per_seq, 0, 0)),
                          const(w_in.shape), row(2 * LANES)],
        out_specs=(row(nqk), row(nqk), row(nv), row(nv), row(d)),
        scratch_shapes=[pltpu.VMEM((2, 2, tm, d // 2), jnp.uint32), pltpu.SemaphoreType.DMA((2,))],
    )
    return pl.pallas_call(
        functools.partial(_odd_front_kernel, tm=tm, n_steps=n_steps),
        grid_spec=grid_spec,
        out_shape=(jax.ShapeDtypeStruct((t, nqk), BF16), jax.ShapeDtypeStruct((t, nqk), BF16),
                   jax.ShapeDtypeStruct((t, nv), BF16), jax.ShapeDtypeStruct((t, nv), F32),
                   jax.ShapeDtypeStruct((t, d), F32)),
        compiler_params=_params(("arbitrary",)),
    )(slot, *operands, mod_l, w_in, tabs)


def _ret_kernel(q_ref, k_ref, v_ref, gate_ref, lg_ref, gng_ref, gnb_ref, o_ref, state_ref, decay_ref, *, C):
    heads = range(RET_HEADS)
    lg = [lg_ref[h][:, 0:1] for h in heads]

    @pl.when(pl.program_id(1) == 0)
    def _():
        state_ref[...] = jnp.zeros(state_ref.shape, F32)
        ri = lax.broadcasted_iota(jnp.int32, (C, C), 0)
        ci = lax.broadcasted_iota(jnp.int32, (C, C), 1)
        diff = (ri - ci).astype(F32)
        causal = ri >= ci
        for h in heads:
            decay_ref[h] = jnp.where(causal, jnp.exp(jnp.where(causal, diff, 0.0) * lg[h]), 0.0)

    idx = lax.broadcasted_iota(jnp.int32, (C, 1), 0).astype(F32)
    qs = [q_ref[:, h * RET_DK:(h + 1) * RET_DK] for h in heads]
    ks = [k_ref[:, h * RET_DK:(h + 1) * RET_DK] for h in heads]
    vs = [v_ref[:, h * RET_DV:(h + 1) * RET_DV] for h in heads]
    states = [state_ref[h] for h in heads]
    scores = [_dot_nt(qs[h], ks[h]) for h in heads]
    cross = [_dot(qs[h], _bf(states[h])) for h in heads]
    kd = [_bf(ks[h].astype(F32) * jnp.exp((C - 1.0 - idx) * lg[h])) for h in heads]
    upd = [_dot_tn(kd[h], vs[h]) for h in heads]
    inner = [_dot(_bf(scores[h] * decay_ref[h]), vs[h]) for h in heads]
    for h in heads:
        state_ref[h] = states[h] * jnp.exp(C * lg[h]) + upd[h]
        o = inner[h] + cross[h] * jnp.exp((idx + 1.0) * lg[h])
        sl = slice(h * RET_DV, (h + 1) * RET_DV)
        o_ref[:, sl] = _bf(gate_ref[:, sl] * _layer_norm(o, gng_ref[h], gnb_ref[h]))


def _retention(q, k, v, gate, lg, gn_g, gn_b, bsz, seq):
    t = q.shape[0]
    C = min(RET_CHUNK, seq)
    nc = seq // C
    blk = lambda n: pl.BlockSpec((C, n), lambda b, j: (b * nc + j, 0))
    return pl.pallas_call(
        functools.partial(_ret_kernel, C=C),
        grid=(bsz, nc),
        in_specs=[blk(RET_HEADS * RET_DK), blk(RET_HEADS * RET_DK), blk(RET_HEADS * RET_DV), blk(RET_HEADS * RET_DV),
                  _const_spec(lg.shape), _const_spec(gn_g.shape), _const_spec(gn_b.shape)],
        out_specs=blk(RET_HEADS * RET_DV),
        out_shape=jax.ShapeDtypeStruct((t, RET_HEADS * RET_DV), BF16),
        scratch_shapes=[pltpu.VMEM((RET_HEADS, RET_DK, RET_DV), F32), pltpu.VMEM((RET_HEADS, C, C), F32)],
        compiler_params=_params(("parallel", "arbitrary")),
    )(q, k, v, gate, lg, gn_g, gn_b)


def _relayout_hy_w_in(w):
    d = w.shape[0]
    z = lambda n: jnp.zeros((d, n), w.dtype)
    c_rope = MLA_Q_RANK + MLA_KV_RANK
    c_gdn = c_rope + MLA_ROPE
    c_a = c_gdn + GDN_QKV
    c_gate = c_a + 2 * GDN_HEADS
    return _bf(jnp.concatenate([
        w[:, 0:c_rope], z(MLA_NOPE), w[:, c_rope:c_gdn], z(LANES - MLA_NOPE - MLA_ROPE),
        w[:, c_gdn:c_a], w[:, c_gate:], w[:, c_a:c_gate], z(LANES - 2 * GDN_HEADS)], axis=1))


def _relayout_w_uq(w):
    r = w.shape[0]
    wh = w.reshape(r, MLA_HEADS, MLA_NOPE + MLA_ROPE)
    pad = jnp.zeros((r, MLA_HEADS, LANES - MLA_NOPE - MLA_ROPE), w.dtype)
    return _bf(jnp.concatenate([wh, pad], axis=-1).reshape(r, MLA_HEADS * LANES))


def _relayout_w_ukv(w):
    r = w.shape[0]
    wh = w.reshape(r, MLA_HEADS, MLA_NOPE + MLA_V)
    pad = jnp.zeros((r, MLA_HEADS, LANES - MLA_NOPE), w.dtype)
    wk = jnp.concatenate([wh[:, :, :MLA_NOPE], pad], axis=-1).reshape(r, MLA_HEADS * LANES)
    wv = wh[:, :, MLA_NOPE:].reshape(r, MLA_HEADS * MLA_V).T
    return _bf(wk), _bf(wv)


def _rope_angles(positions, dim):
    inv_freq = ROPE_BASE ** (-jnp.arange(0, dim, 2, dtype=F32) / dim)
    ang = positions.astype(F32).reshape(-1)[:, None] * inv_freq
    return jnp.cos(ang), jnp.sin(ang)


def _expand_lanes(compact, placement):
    return jnp.dot(compact, jnp.asarray(placement), precision=lax.Precision.HIGHEST)


def _mla_rope_tables(positions):
    cos, sin = _rope_angles(positions, MLA_ROPE)
    half = MLA_ROPE // 2
    place = np.zeros((2 * half, 3 * LANES), np.float32)
    for j in range(half):
        place[j, MLA_NOPE + j] = place[j, MLA_NOPE + half + j] = 1.0
        place[half + j, LANES + MLA_NOPE + j] = -1.0
        place[half + j, 2 * LANES + MLA_NOPE + half + j] = 1.0
    ones = np.zeros((1, 3 * LANES), np.float32)
    ones[0, :MLA_NOPE] = 1.0
    return _expand_lanes(jnp.concatenate([cos, sin], axis=1), place) + jnp.asarray(ones)


def _ret_rope_tables(positions):
    cos, sin = _rope_angles(positions, RET_DK)
    half = RET_DK // 2
    place = np.zeros((2 * half, 2 * LANES), np.float32)
    for j in range(half):
        place[j, j] = place[j, half + j] = 1.0
        place[half + j, LANES + j] = -1.0
        place[half + j, LANES + half + j] = 1.0
    return _expand_lanes(jnp.concatenate([cos, sin], axis=1), place)


def _router_weights(w_group, b_group, w_expert, b_expert):
    d = w_group.shape[0]
    pad = LANES - N_EXPERTS - N_GROUPS
    w = jnp.concatenate([w_expert, w_group, jnp.zeros((d, pad), F32)], axis=1)
    b = jnp.concatenate([b_expert, b_group, jnp.zeros((pad,), F32)]).reshape(1, LANES)
    hi = _bf(w)
    return hi, _bf(w - hi.astype(F32)), b


def _pad_lanes(v):
    return jnp.zeros((LANES,), F32).at[:v.shape[0]].set(v)


def kernel(x, c, positions, ada_w, ada_b, ln_mix_g, ln_mix_b, ln_ffn_g, ln_ffn_b, hy_w_in, mla_q_norm, mla_w_uq, mla_kv_norm, mla_w_ukv, gdn_conv_w, gdn_a_log, gdn_dt_bias, gdn_norm, hy_w_out, ret_w_in, ret_gn_g, ret_gn_b, ret_w_out, moe_w_group, moe_b_group, moe_w_expert, moe_b_expert, moe_w_gate, moe_w_up, moe_w_down):
    bsz, seq, d = x.shape
    t = bsz * seq
    mods = _ada_mod(c, ada_w, ada_b)
    mla_tabs = _mla_rope_tables(positions)
    ret_tabs = _ret_rope_tables(positions)
    log_gamma = jnp.log(1.0 - 2.0 ** (-5.0 - jnp.arange(RET_HEADS, dtype=F32)))
    lg = jnp.broadcast_to(log_gamma[:, None, None], (RET_HEADS, 1, LANES))
    xc = x.reshape(t, d)
    for layer in range(DEPTH):
        i = layer // 2
        mod_l = mods[layer]
        if layer % 2 == 0:
            wuk_p, wuv = _relayout_w_ukv(mla_w_ukv[i])
            gparam = jnp.stack([_pad_lanes(gdn_a_log[i]), _pad_lanes(gdn_dt_bias[i])])
            front = _even_front(
                xc if layer == 0 else pending, mod_l, _relayout_hy_w_in(hy_w_in[i]), mla_q_norm[i].reshape(1, -1), _relayout_w_uq(mla_w_uq[i]),
                mla_kv_norm[i].reshape(1, -1), wuk_p, wuv, gdn_conv_w[i], gparam, mla_tabs, seq)
            (qm, km, vm, gq, gk, gv, gate, gcol, grow) = front[:9]
            if layer > 0:
                xc = front[9]
            o_mla = _attention(qm, km, vm, bsz, seq)
            n_ch = t // GDN_CHUNK
            growp = grow[:GDN_HEADS].reshape(GDN_PAIRS, 2, n_ch, GDN_CHUNK).transpose(2, 0, 1, 3).reshape(n_ch, GDN_PAIRS, LANES)
            onorm2 = jnp.concatenate([gdn_norm[i], gdn_norm[i]]).reshape(1, LANES)
            y_gdn = _gdn(gq, gk, gv, gate, gcol, growp, onorm2, bsz, seq)
            acts = [o_mla, y_gdn]
            w_out = _bf(hy_w_out[i])
        else:
            q, k, v, gate, xc = _odd_front(pending, mod_l, _bf(ret_w_in[i]), ret_tabs, seq)
            og = _retention(q, k, v, gate, lg, ret_gn_g[i].reshape(RET_HEADS, 1, RET_DV),
                            ret_gn_b[i].reshape(RET_HEADS, 1, RET_DV), bsz, seq)
            acts = [og]
            w_out = _bf(ret_w_out[i])
        wr_hi, wr_lo, br = _router_weights(moe_w_group[layer], moe_b_group[layer], moe_w_expert[layer], moe_b_expert[layer])
        x1, h2, route, er, counts = _mixer_back(acts, w_out, xc, mod_l, ln_mix_g[layer].reshape(1, d), ln_mix_b[layer].reshape(1, d),
                                    wr_hi, wr_lo, br, seq)
        slot, block_expert, n_used, pad, n_slots = _slot_tables(er, counts, t)
        xb = _scatter_rows(slot, pad, h2, n_slots, seq)
        yb = _moe_experts(block_expert, n_used, xb, moe_w_gate, moe_w_up, moe_w_down, layer)
        pending = (slot, x1, yb, route, mod_l, ln_ffn_g[layer].reshape(1, d), ln_ffn_b[layer].reshape(1, d))
        if layer == DEPTH - 1:
            xc = _moe_back(*pending, seq)
    return xc.reshape(bsz, seq, d)
```

```python
import functools

import jax
import jax.numpy as jnp
import numpy as np
from jax import lax
from jax.experimental import pallas as pl
from jax.experimental.pallas import tpu as pltpu

F32 = jnp.float32
BF16 = jnp.bfloat16

DEPTH = 4
N_MOD = 6
LANES = 128

MLA_HEADS = 8
MLA_NOPE = 64
MLA_ROPE = 32
MLA_V = 64
MLA_Q_RANK = 256
MLA_KV_RANK = 128
MLA_SCALE = (MLA_NOPE + MLA_ROPE) ** -0.5
LOG2_E = 1.4426950408889634

GDN_HEADS = 8
GDN_DK = 64
GDN_DV = 64
GDN_CONV = 4
GDN_CHUNK = 64
GDN_HD = GDN_HEADS * GDN_DK
GDN_QKV = 3 * GDN_HD
GDN_PAIRS = GDN_HEADS // 2
GDN_GROUP = 4

RET_HEADS = 8
RET_DK = 128
RET_DV = 256
RET_CHUNK = 256

N_GROUPS = 4
EXPERTS_PER_GROUP = 8
N_EXPERTS = 32
EXPERT_FF = 512
MOE_BLOCK = 512

ROPE_BASE = 10000.0
LN_EPS = 1e-5
RMS_EPS = 1e-6
DEEPNORM_ALPHA = (2.0 * DEPTH) ** 0.25

ROW_TILE = 256
WIDE_ROW_TILE = 512
ATTN_TILE = 1024
VMEM_LIMIT = 56 * 1024 * 1024

_C_MLA = 0
_C_QKV = 512
_C_GATE = 2048
_C_AB = 2560
_C_END = 2688


def _bf(x):
    return x.astype(BF16)


def _dot(a, b):
    return jnp.dot(a, b, preferred_element_type=F32)


def _dot_nt(a, b):
    return lax.dot_general(a, b, (((1,), (1,)), ((), ())), preferred_element_type=F32)


def _dot_tn(a, b):
    return lax.dot_general(a, b, (((0,), (0,)), ((), ())), preferred_element_type=F32)


def _split2(x):
    hi = _bf(x)
    return hi, _bf(x - hi.astype(F32))


def _split3(x):
    hi = _bf(x)
    r = x - hi.astype(F32)
    mid = _bf(r)
    return hi, mid, _bf(r - mid.astype(F32))


def _silu(x):
    return x * jax.nn.sigmoid(x)


def _pack_bf16_pairs(x):
    n = x.shape[1] // 2
    bits = lambda v: lax.bitcast_convert_type(_bf(v).astype(F32), jnp.uint32)
    return (bits(x[:, n:]) & jnp.uint32(0xFFFF0000)) | (bits(x[:, :n]) >> 16)


def _unpack_bf16_pairs(w):
    lo = lax.bitcast_convert_type(w << 16, F32)
    hi = lax.bitcast_convert_type(w & jnp.uint32(0xFFFF0000), F32)
    return lo, hi


def _layer_norm(z, g, b):
    mu = jnp.mean(z, -1, keepdims=True)
    zc = z - mu
    var = jnp.mean(zc * zc, -1, keepdims=True)
    return zc * lax.rsqrt(var + LN_EPS) * g + b


def _rms_norm(x, g):
    return x * lax.rsqrt(jnp.mean(x * x, -1, keepdims=True) + RMS_EPS) * g


def _params(sem):
    return pltpu.CompilerParams(dimension_semantics=sem, vmem_limit_bytes=VMEM_LIMIT)


def _const_spec(shape):
    nd = len(shape)
    return pl.BlockSpec(shape, lambda *_: (0,) * nd)


def _ada_kernel(c_ref, w_ref, b_ref, o_ref):
    c = c_ref[...]
    o_ref[...] = _dot(_bf(_silu(c)), _bf(w_ref[...])) + b_ref[...]


def _ada_mod(c, ada_w, ada_b):
    depth, d, n = ada_w.shape
    bsz = c.shape[0]
    rows = 16
    tn = 1536
    cp = jnp.zeros((rows, d), F32).at[:bsz].set(c)
    out = pl.pallas_call(
        _ada_kernel,
        grid=(depth, n // tn),
        in_specs=[pl.BlockSpec((rows, d), lambda l, j: (0, 0)),
                  pl.BlockSpec((None, d, tn), lambda l, j: (l, 0, j)),
                  pl.BlockSpec((None, 1, tn), lambda l, j: (l, 0, j))],
        out_specs=pl.BlockSpec((None, rows, tn), lambda l, j: (l, 0, j)),
        out_shape=jax.ShapeDtypeStruct((depth, rows, n), F32),
        compiler_params=_params(("parallel", "parallel")),
    )(cp, ada_w, ada_b.reshape(depth, 1, n))
    return out[:, :bsz].reshape(depth, bsz, N_MOD, d)


def _even_front_kernel(*refs, fused, tm, tiles_per_seq, n_steps):
    n_head = N_COMBINE_IN if fused else 1
    (mod_ref, w_ref, qn_ref, wuq_ref, kvn_ref, wuk_ref, wuv_ref, cw_ref, gp_ref, tab_ref, ones_ref, tri_ref,
     q_ref, k_ref, v_ref, gq_ref, gk_ref, gv_ref, gate_ref, gcol_ref, grow_ref) = refs[n_head:n_head + 21]
    if fused:
        xo_ref, xbuf, ybuf, sems = refs[n_head + 21:]
        x, drain = _fused_combine(*refs[:n_head], xo_ref, ybuf, sems, tm, n_steps)
    else:
        (xbuf,) = refs[n_head + 21:]
        x = refs[0][...]
    i = pl.program_id(0)
    hb = _bf(x * (1.0 + mod_ref[1:2, :]) + mod_ref[0:1, :])

    pm = _dot(hb, w_ref[:, _C_MLA:_C_QKV])
    ct = tab_ref[:, 0:LANES]
    s1 = tab_ref[:, LANES:2 * LANES]
    s2 = tab_ref[:, 2 * LANES:3 * LANES]

    def rope(b):
        return b * ct + pltpu.roll(b, LANES - 16, 1) * s1 + pltpu.roll(b, 16, 1) * s2

    qa = _dot(_bf(_rms_norm(pm[:, 0:MLA_Q_RANK], qn_ref[...])), wuq_ref[...])
    for h in range(MLA_HEADS):
        sl = slice(h * LANES, (h + 1) * LANES)
        q_ref[:, sl] = _bf(rope(qa[:, sl]) * (MLA_SCALE * LOG2_E))
    kvn = _bf(_rms_norm(pm[:, MLA_Q_RANK:MLA_Q_RANK + MLA_KV_RANK], kvn_ref[...]))
    ka = _dot(kvn, wuk_ref[...])
    kr = rope(pm[:, MLA_Q_RANK + MLA_KV_RANK:_C_QKV])
    for h in range(MLA_HEADS):
        sl = slice(h * LANES, (h + 1) * LANES)
        k_ref[:, sl] = _bf(ka[:, sl] + kr)
    v_ref[...] = _bf(_dot_nt(wuv_ref[...], kvn))

    @pl.when(i % tiles_per_seq == 0)
    def _():
        xbuf[0:8, :] = jnp.zeros((8, GDN_QKV), F32)

    xbuf[8:8 + tm, :] = _dot(hb, w_ref[:, _C_QKV:_C_GATE])
    cw = cw_ref[...]
    y = cw[0:1, :] * xbuf[5:5 + tm, :]
    for t in range(1, GDN_CONV):
        y = y + cw[t:t + 1, :] * xbuf[5 + t:5 + t + tm, :]
    xbuf[0:8, :] = xbuf[tm:tm + 8, :]
    y = _silu(y)

    def head_sumsq(z):
        return _dot(_bf(z * z), ones_ref[...])

    qg = y[:, 0:GDN_HD]
    kg = y[:, GDN_HD:2 * GDN_HD]
    gq_ref[...] = qg * lax.rsqrt(head_sumsq(qg) + RMS_EPS) * (GDN_DK ** -0.5)
    gk_ref[...] = kg * lax.rsqrt(head_sumsq(kg) + RMS_EPS)
    gv_ref[...] = y[:, 2 * GDN_HD:3 * GDN_HD]
    gate_ref[...] = _dot(hb, w_ref[:, _C_GATE:_C_AB])

    ab = _dot(hb, w_ref[:, _C_AB:_C_END])
    lane = lax.broadcasted_iota(jnp.int32, (tm, LANES), 1)
    z = ab + gp_ref[1:2, :]
    softplus = jnp.maximum(z, 0.0) + jnp.log(1.0 + jnp.exp(-jnp.abs(z)))
    la = jnp.where(lane < GDN_HEADS, -jnp.exp(gp_ref[0:1, :]) * softplus, 0.0)
    tri = tri_ref[...]
    l0, l1, l2 = _split3(la)
    g = _dot(tri, l0) + _dot(tri, l1) + _dot(tri, l2)
    gcol = jnp.where(lane < GDN_HEADS, g, jax.nn.sigmoid(ab))
    gcol_ref[...] = gcol
    grow_ref[...] = gcol.T[0:16, :]
    if fused:
        drain()


def _even_front(src, mod_l, w_in_p, qn, wuq_p, kvn, wuk_p, wuv, conv_w, gparam, tabs, seq):
    fused = isinstance(src, tuple)
    t, d = (src[1] if fused else src).shape
    tm = min(ROW_TILE, seq)
    tiles_per_seq = seq // tm
    n_steps = t // tm
    row = lambda n: pl.BlockSpec((tm, n), lambda i, *_: (i, 0))
    idx = np.arange(GDN_HD) // GDN_DK
    head_ones = jnp.asarray(idx[:, None] == idx[None, :], BF16)
    pos = np.arange(tm)
    chunk_tri = jnp.asarray((pos[:, None] // GDN_CHUNK == pos[None, :] // GDN_CHUNK) & (pos[None, :] <= pos[:, None]), BF16)
    out_shapes = (
        jax.ShapeDtypeStruct((t, MLA_HEADS * LANES), BF16),
        jax.ShapeDtypeStruct((t, MLA_HEADS * LANES), BF16),
        jax.ShapeDtypeStruct((MLA_HEADS * MLA_V, t), BF16),
        jax.ShapeDtypeStruct((t, GDN_HD), F32),
        jax.ShapeDtypeStruct((t, GDN_HD), F32),
        jax.ShapeDtypeStruct((t, GDN_HD), F32),
        jax.ShapeDtypeStruct((t, GDN_HD), F32),
        jax.ShapeDtypeStruct((t, LANES), F32),
        jax.ShapeDtypeStruct((16, t), F32),
    )
    if fused:
        slot, head_operands, head_specs = _combine_specs(src, tm, tiles_per_seq, d)
        prefetch = [slot]
    else:
        head_operands, head_specs, prefetch = [src], [row(d)], []
    in_specs = head_specs + [
        pl.BlockSpec((None, N_MOD, d), lambda i, *_: (i // tiles_per_seq, 0, 0)),
        _const_spec(w_in_p.shape), _const_spec(qn.shape), _const_spec(wuq_p.shape),
        _const_spec(kvn.shape), _const_spec(wuk_p.shape), _const_spec(wuv.shape),
        _const_spec(conv_w.shape), _const_spec(gparam.shape), row(3 * LANES),
        _const_spec(head_ones.shape), _const_spec(chunk_tri.shape)]
    out_specs = (row(MLA_HEADS * LANES), row(MLA_HEADS * LANES),
                 pl.BlockSpec((MLA_HEADS * MLA_V, tm), lambda i, *_: (0, i)),
                 row(GDN_HD), row(GDN_HD), row(GDN_HD), row(GDN_HD), row(LANES),
                 pl.BlockSpec((16, tm), lambda i, *_: (0, i)))
    scratch = [pltpu.VMEM((tm + 8, GDN_QKV), F32)]
    if fused:
        out_specs += (row(d),)
        out_shapes += (jax.ShapeDtypeStruct((t, d), F32),)
        scratch += [pltpu.VMEM((2, 2, tm, d // 2), jnp.uint32), pltpu.SemaphoreType.DMA((2,))]
    grid_spec = pltpu.PrefetchScalarGridSpec(
        num_scalar_prefetch=len(prefetch), grid=(n_steps,), in_specs=in_specs, out_specs=out_specs,
        scratch_shapes=scratch)
    return pl.pallas_call(
        functools.partial(_even_front_kernel, fused=fused, tm=tm, tiles_per_seq=tiles_per_seq, n_steps=n_steps),
        grid_spec=grid_spec,
        out_shape=out_shapes,
        compiler_params=_params(("arbitrary",)),
    )(*prefetch, *head_operands, mod_l, w_in_p, qn, wuq_p, kvn, wuk_p, wuv, conv_w, gparam, tabs, head_ones, chunk_tri)


def _attn_kernel(qi_ref, ki_ref, q_ref, k_ref, vt_ref, o_ref, m_ref, l_ref, acc_ref, *, tq):
    qi = qi_ref[pl.program_id(1)]
    ki = ki_ref[pl.program_id(1)]

    @pl.when(ki == 0)
    def _():
        m_ref[...] = jnp.full(m_ref.shape, -jnp.inf, F32)
        l_ref[...] = jnp.zeros(l_ref.shape, F32)
        acc_ref[...] = jnp.zeros(acc_ref.shape, F32)

    def scores(h):
        sl = slice(h * LANES, (h + 1) * LANES)
        return _dot_nt(k_ref[:, sl], q_ref[:, sl])

    def step(diagonal):
        if diagonal:
            key = lax.broadcasted_iota(jnp.int32, (tq, tq), 0)
            qry = lax.broadcasted_iota(jnp.int32, (tq, tq), 1)
            keep = key <= qry
        s_next = scores(0)
        for h in range(MLA_HEADS):
            s = s_next
            if h + 1 < MLA_HEADS:
                s_next = scores(h + 1)
            if diagonal:
                s = jnp.where(keep, s, -jnp.inf)
            rows = slice(h * MLA_V, (h + 1) * MLA_V)
            m_prev = m_ref[h:h + 1, :]
            m_new = jnp.maximum(m_prev, jnp.max(s, 0, keepdims=True))
            alpha = jnp.exp2(m_prev - m_new)
            p = jnp.exp2(s - m_new)
            l_ref[h:h + 1, :] = alpha * l_ref[h:h + 1, :] + jnp.sum(p, 0, keepdims=True)
            acc_ref[rows, :] = alpha * acc_ref[rows, :] + _dot(vt_ref[rows, :], _bf(p))
            m_ref[h:h + 1, :] = m_new

    @pl.when(ki < qi)
    def _():
        step(False)

    @pl.when(ki == qi)
    def _():
        step(True)
        for h in range(MLA_HEADS):
            rows = slice(h * MLA_V, (h + 1) * MLA_V)
            acc_ref[rows, :] = acc_ref[rows, :] / l_ref[h:h + 1, :]
        o_ref[...] = _bf(acc_ref[...].T)


def _attention(qm, km, vm, bsz, seq):
    t = qm.shape[0]
    tq = min(ATTN_TILE, seq)
    nq = seq // tq
    pairs = [(q, k) for q in range(nq) for k in range(q + 1)]
    qi_tab = jnp.asarray(np.array([p[0] for p in pairs], np.int32))
    ki_tab = jnp.asarray(np.array([p[1] for p in pairs], np.int32))
    grid_spec = pltpu.PrefetchScalarGridSpec(
        num_scalar_prefetch=2,
        grid=(bsz, len(pairs)),
        in_specs=[pl.BlockSpec((tq, MLA_HEADS * LANES), lambda b, s, qt, kt: (b * nq + qt[s], 0)),
                  pl.BlockSpec((tq, MLA_HEADS * LANES), lambda b, s, qt, kt: (b * nq + kt[s], 0)),
                  pl.BlockSpec((MLA_HEADS * MLA_V, tq), lambda b, s, qt, kt: (0, b * nq + kt[s]))],
        out_specs=pl.BlockSpec((tq, MLA_HEADS * MLA_V), lambda b, s, qt, kt: (b * nq + qt[s], 0)),
        scratch_shapes=[pltpu.VMEM((MLA_HEADS, tq), F32),
                        pltpu.VMEM((MLA_HEADS, tq), F32),
                        pltpu.VMEM((MLA_HEADS * MLA_V, tq), F32)],
    )
    return pl.pallas_call(
        functools.partial(_attn_kernel, tq=tq),
        grid_spec=grid_spec,
        out_shape=jax.ShapeDtypeStruct((t, MLA_HEADS * MLA_V), BF16),
        compiler_params=_params(("parallel", "arbitrary")),
    )(qi_tab, ki_tab, qm, km, vm)


def _gdn_kernel(q_ref, k_ref, v_ref, gate_ref, gcol_ref, grow_ref, on_ref, y_ref, state_ref, *, n_chunks):
    C = GDN_CHUNK

    @pl.when(pl.program_id(1) == 0)
    def _():
        state_ref[...] = jnp.zeros(state_ref.shape, F32)

    lane = lax.broadcasted_iota(jnp.int32, (C, LANES), 1)
    rowi = lax.broadcasted_iota(jnp.int32, (C, LANES), 0)
    first = lane < GDN_DK
    col = jnp.where(first, lane, lane - GDN_DK)
    tril = col <= rowi
    strict = col < rowi
    eye = jnp.where(col == rowi, 1.0, 0.0)
    blk16 = (col // 16) == (rowi // 16)
    blk32 = (col // 32) == (rowi // 32)
    r2 = lax.broadcasted_iota(jnp.int32, (LANES, LANES), 0)
    c2 = lax.broadcasted_iota(jnp.int32, (LANES, LANES), 1)
    first_rows = r2 < GDN_DK
    head_ones = jnp.where((r2 < GDN_DK) == (c2 < GDN_DK), 1.0, 0.0).astype(BF16)

    def bd(y):
        return jnp.concatenate([jnp.where(first, y, 0), jnp.where(first, 0, y)], axis=0)

    def pmm(x, y):
        return _dot(_bf(x), bd(_bf(y)))

    def pmm_hi(x, y):
        xh, xl = _split2(x)
        yh, yl = _split2(y)
        top = _dot(jnp.concatenate([xh, xl], axis=0), bd(yh))
        return top[:C] + top[C:] + _dot(xh, bd(yl))

    def pair_cols(a, c0, c1):
        return jnp.where(first, a[:, c0:c0 + 1], a[:, c1:c1 + 1])

    def chunk_group(gi, carry):
        pairs = range(GDN_PAIRS)
        items = [(c, p) for c in range(GDN_GROUP) for p in pairs]
        sls = [slice(p * LANES, (p + 1) * LANES) for p in pairs]
        rows = [pl.ds(pl.multiple_of((gi * GDN_GROUP + c) * C, C), C) for c in range(GDN_GROUP)]
        gc = [gcol_ref[rows[c], :] for c in range(GDN_GROUP)]
        gr = [grow_ref[gi * GDN_GROUP + c] for c in range(GDN_GROUP)]
        every = range(len(items))
        q = [q_ref[rows[c], sls[p]] for c, p in items]
        k = [k_ref[rows[c], sls[p]] for c, p in items]
        v = [v_ref[rows[c], sls[p]] for c, p in items]
        g = [pair_cols(gc[c], 2 * p, 2 * p + 1) for c, p in items]
        beta = [pair_cols(gc[c], GDN_HEADS + 2 * p, GDN_HEADS + 2 * p + 1) for c, p in items]
        decay = [jnp.where(tril, jnp.exp(jnp.where(tril, g[i] - gr[c][p:p + 1, :], 0.0)), 0.0)
                 for i, (c, p) in enumerate(items)]
        kb = [k[i] * beta[i] for i in every]
        kbd = [bd(_bf(k[i])) for i in every]
        lmat = [jnp.where(strict, _dot_nt(_bf(kb[i]), kbd[i]) * decay[i], 0.0) for i in every]
        a_intra = [jnp.where(tril, _dot_nt(_bf(q[i]), kbd[i]) * decay[i], 0.0) for i in every]
        n1 = [jnp.where(blk16, lmat[i], 0.0) for i in every]
        x = [eye - n1[i] for i in every]
        nk = n1
        for mm in (pmm_hi, pmm, pmm):
            nk = [mm(nk[i], nk[i]) for i in every]
            x = [x[i] + mm(x[i], nk[i]) for i in every]
        for off in ([jnp.where(blk32 & jnp.logical_not(blk16), lmat[i], 0.0) for i in every],
                    [jnp.where(blk32, 0.0, lmat[i]) for i in every]):
            xb = [pmm(x[i], off[i]) for i in every]
            x = [x[i] - pmm(xb[i], x[i]) for i in every]
        u = [pmm(x[i], v[i] * beta[i]) for i in every]
        w = [pmm(x[i], kb[i] * jnp.exp(g[i])) for i in every]
        for c in range(GDN_GROUP):
            idx = [c * GDN_PAIRS + p for p in pairs]
            state = [state_ref[p] for p in pairs]
            sb = [_bf(state[p]) for p in pairs]
            v_new = [u[i] - _dot(_bf(w[i]), sb[p]) for p, i in enumerate(idx)]
            o = [_dot(_bf(q[i] * jnp.exp(g[i])), sb[p]) + pmm(a_intra[i], v_new[p]) for p, i in enumerate(idx)]
            for p, i in enumerate(idx):
                g_last = g[i][C - 1:C, :]
                k_dec = k[i] * jnp.exp(g_last - g[i])
                scale = jnp.exp(jnp.where(first_rows, g_last[:, 0:1], g_last[:, GDN_DK:GDN_DK + 1]))
                state_ref[p] = state[p] * scale + _dot_tn(bd(_bf(k_dec)), bd(_bf(v_new[p])))
            for p in pairs:
                ms = _dot(_bf(o[p] * o[p]), head_ones) * (1.0 / GDN_DV)
                on = o[p] * lax.rsqrt(ms + RMS_EPS) * on_ref[...]
                y_ref[rows[c], sls[p]] = _bf(on * _silu(gate_ref[rows[c], sls[p]]))
        return carry

    lax.fori_loop(0, n_chunks // GDN_GROUP, chunk_group, 0)


def _gdn(gq, gk, gv, gate, gcol, growp, onorm2, bsz, seq):
    t = gq.shape[0]
    r = min(WIDE_ROW_TILE, seq)
    nblk = seq // r
    n_chunks = r // GDN_CHUNK
    row = lambda n: pl.BlockSpec((r, n), lambda b, j: (b * nblk + j, 0))
    return pl.pallas_call(
        functools.partial(_gdn_kernel, n_chunks=n_chunks),
        grid=(bsz, nblk),
        in_specs=[row(GDN_HD), row(GDN_HD), row(GDN_HD), row(GDN_HD), row(LANES),
                  pl.BlockSpec((n_chunks, GDN_PAIRS, LANES), lambda b, j: (b * nblk + j, 0, 0)),
                  _const_spec(onorm2.shape)],
        out_specs=row(GDN_HD),
        out_shape=jax.ShapeDtypeStruct((t, GDN_HD), BF16),
        scratch_shapes=[pltpu.VMEM((GDN_PAIRS, LANES, LANES), F32)],
        compiler_params=_params(("parallel", "arbitrary")),
    )(gq, gk, gv, gate, gcol, growp, onorm2)


def _back_kernel(*refs, n_in):
    a_refs = refs[:n_in]
    (w_ref, x_ref, mod_ref, lng_ref, lnb_ref, wrh_ref, wrl_ref, br_ref, tri_ref,
     x1_ref, h2_ref, route_ref, er_ref, counts_ref, base_ref) = refs[n_in:]
    off = 0
    y = None
    for a_ref in a_refs:
        kk = a_ref.shape[1]
        part = _dot(a_ref[...], w_ref[off:off + kk, :])
        y = part if y is None else y + part
        off += kk
    z = DEEPNORM_ALPHA * x_ref[...] + (1.0 + mod_ref[2:3, :]) * y
    x1 = _layer_norm(z, lng_ref[...], lnb_ref[...])
    x1_ref[...] = x1
    h2 = x1 * (1.0 + mod_ref[4:5, :]) + mod_ref[3:4, :]
    h2_ref[...] = h2

    hh, hl = _split2(h2)
    logits = _dot(hh, wrh_ref[...]) + _dot(hh, wrl_ref[...]) + _dot(hl, wrh_ref[...]) + br_ref[...]
    lane = lax.broadcasted_iota(jnp.int32, logits.shape, 1)
    lane_f = lane.astype(F32)
    neg = -jnp.inf
    gmask = (lane >= N_EXPERTS) & (lane < N_EXPERTS + N_GROUPS)
    gl = jnp.where(gmask, logits, neg)
    gmax = jnp.max(gl, -1, keepdims=True)
    g_top = 1.0 / jnp.sum(jnp.where(gmask, jnp.exp(gl - gmax), 0.0), -1, keepdims=True)
    g_lane = jnp.min(jnp.where(gl == gmax, lane_f, 1e9), -1, keepdims=True)
    g_idx = g_lane.astype(jnp.int32) - N_EXPERTS
    emask = (lane < N_EXPERTS) & ((lane // EXPERTS_PER_GROUP) == g_idx)
    el = jnp.where(emask, logits, neg)
    m1 = jnp.max(el, -1, keepdims=True)
    i1 = jnp.min(jnp.where(el == m1, lane_f, 1e9), -1, keepdims=True)
    el2 = jnp.where(lane_f == i1, neg, el)
    m2 = jnp.max(el2, -1, keepdims=True)
    i2 = jnp.min(jnp.where(el2 == m2, lane_f, 1e9), -1, keepdims=True)
    e2 = jnp.exp(m2 - m1)
    w1 = g_top / (1.0 + e2)
    w2 = g_top * e2 / (1.0 + e2)

    @pl.when(pl.program_id(0) == 0)
    def _():
        base_ref[...] = jnp.zeros(base_ref.shape, F32)

    tm = logits.shape[0]
    tri = tri_ref[...]
    oh1 = jnp.where(lane_f == i1, 1.0, 0.0)
    oh2 = jnp.where(lane_f == i2, 1.0, 0.0)
    cs1 = _dot(tri, _bf(oh1))
    cs2 = _dot(tri, _bf(oh2))
    base = base_ref[...]
    tot1 = cs1[tm - 1:tm, :]
    r1 = jnp.sum(oh1 * (base + cs1), -1, keepdims=True) - 1.0
    r2 = jnp.sum(oh2 * (base + tot1 + cs2), -1, keepdims=True) - 1.0
    total = base + tot1 + cs2[tm - 1:tm, :]
    base_ref[...] = total
    counts_ref[...] = total
    route = jnp.where(lane == 0, i1, jnp.where(lane == 1, i2, jnp.where(lane == 2, w1, jnp.where(lane == 3, w2, 0.0))))
    route = jnp.where(lane == 4, r1, jnp.where(lane == 5, r2, route))
    route_ref[...] = route
    er_ref[...] = route.T[0:8, :]


def _mixer_back(acts, w_out, x2, mod_l, ln_g, ln_b, wr_hi, wr_lo, br, seq):
    t, d = x2.shape
    tm = min(WIDE_ROW_TILE, seq)
    tiles_per_seq = seq // tm
    row = lambda n: pl.BlockSpec((tm, n), lambda i: (i, 0))
    pos = np.arange(tm)
    tri = jnp.asarray(pos[None, :] <= pos[:, None], BF16)
    return pl.pallas_call(
        functools.partial(_back_kernel, n_in=len(acts)),
        grid=(t // tm,),
        in_specs=[row(a.shape[1]) for a in acts] + [
            _const_spec(w_out.shape), row(d),
            pl.BlockSpec((None, N_MOD, d), lambda i: (i // tiles_per_seq, 0, 0)),
            _const_spec(ln_g.shape), _const_spec(ln_b.shape),
            _const_spec(wr_hi.shape), _const_spec(wr_lo.shape), _const_spec(br.shape), _const_spec(tri.shape)],
        out_specs=(row(d), row(d), row(LANES), pl.BlockSpec((8, tm), lambda i: (0, i)), _const_spec((1, LANES))),
        out_shape=(jax.ShapeDtypeStruct((t, d), F32), jax.ShapeDtypeStruct((t, d), F32),
                   jax.ShapeDtypeStruct((t, LANES), F32), jax.ShapeDtypeStruct((8, t), F32),
                   jax.ShapeDtypeStruct((1, LANES), F32)),
        scratch_shapes=[pltpu.VMEM((1, LANES), F32)],
        compiler_params=_params(("arbitrary",)),
    )(*acts, w_out, x2, mod_l, ln_g, ln_b, wr_hi, wr_lo, br, tri)


def _moe_kernel(be_ref, nb_ref, x_ref, wg_ref, wu_ref, wd_ref, y_ref, wg_s, wu_s, wd_s):
    i = pl.program_id(0)

    @pl.when(i < nb_ref[0])
    def _():
        prev = be_ref[jnp.maximum(i - 1, 0)]

        @pl.when((i == 0) | (be_ref[i] != prev))
        def _():
            wg_s[...] = _bf(wg_ref[...])
            wu_s[...] = _bf(wu_ref[...])
            wd_s[...] = _bf(wd_ref[...])

        lo, hi = _unpack_bf16_pairs(x_ref[...])
        x_lo, x_hi = _bf(lo), _bf(hi)
        half = x_lo.shape[1]
        gate = _dot(x_lo, wg_s[0:half, :]) + _dot(x_hi, wg_s[half:, :])
        up = _dot(x_lo, wu_s[0:half, :]) + _dot(x_hi, wu_s[half:, :])
        y_ref[...] = _pack_bf16_pairs(_dot(_bf(_silu(gate) * up), wd_s[...]))

    @pl.when(i >= nb_ref[0])
    def _():
        y_ref[...] = jnp.zeros(y_ref.shape, jnp.uint32)


def _moe_experts(block_expert, n_used, xb, w_gate, w_up, w_down, layer):
    n_slots = xb.shape[0]
    d, ff = w_gate.shape[-2:]
    n_blocks = n_slots // MOE_BLOCK
    grid_spec = pltpu.PrefetchScalarGridSpec(
        num_scalar_prefetch=2,
        grid=(n_blocks,),
        in_specs=[pl.BlockSpec((MOE_BLOCK, d // 2), lambda i, be, nb: (jnp.minimum(i, nb[0] - 1), 0)),
                  pl.BlockSpec((None, None, d, ff), lambda i, be, nb: (layer, be[i], 0, 0)),
                  pl.BlockSpec((None, None, d, ff), lambda i, be, nb: (layer, be[i], 0, 0)),
                  pl.BlockSpec((None, None, ff, d), lambda i, be, nb: (layer, be[i], 0, 0))],
        out_specs=pl.BlockSpec((MOE_BLOCK, d // 2), lambda i, be, nb: (i, 0)),
        scratch_shapes=[pltpu.VMEM((d, ff), BF16), pltpu.VMEM((d, ff), BF16), pltpu.VMEM((ff, d), BF16)],
    )
    return pl.pallas_call(
        _moe_kernel,
        grid_spec=grid_spec,
        out_shape=jax.ShapeDtypeStruct((n_slots, d // 2), jnp.uint32),
        compiler_params=_params(("arbitrary",)),
    )(block_expert, n_used, xb, w_gate, w_up, w_down)


def _slot_tables(er, counts_row, t):
    counts = counts_row[0, :N_EXPERTS].astype(jnp.int32)
    padded = (counts + MOE_BLOCK - 1) // MOE_BLOCK * MOE_BLOCK
    pend = jnp.cumsum(padded)
    pstart = pend - padded
    e = er[0:2].astype(jnp.int32)
    starts = jnp.sum(jnp.where(e[:, :, None] == jnp.arange(N_EXPERTS, dtype=jnp.int32), pstart, 0), axis=-1)
    slot = (starts + er[4:6].astype(jnp.int32)).reshape(2 * t)
    n_blocks = 2 * t // MOE_BLOCK + N_EXPERTS
    block_start = jnp.arange(n_blocks, dtype=jnp.int32) * MOE_BLOCK
    block_expert = jnp.minimum(jnp.sum((block_start[:, None] >= pend[None, :]).astype(jnp.int32), axis=1), N_EXPERTS - 1)
    n_used = (pend[-1] // MOE_BLOCK).astype(jnp.int32).reshape(1)
    n_slots = n_blocks * MOE_BLOCK
    tail = jnp.stack([pend[-1], (n_slots - pend[-1]) // (MOE_BLOCK // 2)])
    pad = jnp.concatenate([pstart + counts, padded - counts, tail]).astype(jnp.int32)
    return slot, block_expert, n_used, pad, n_slots


def _scatter_kernel(slot_ref, pad_ref, h_ref, xb_ref, stage_ref, zero_ref, sems, zsem, *, tm, n_steps):
    i = pl.program_id(0)

    def pad_copies(fn):
        def rows_at(pos, count, size):
            def body(j, c):
                at = pos + j * size
                if size > 1:
                    at = pl.multiple_of(at, size)
                fn(pltpu.make_async_copy(zero_ref.at[pl.ds(0, size)], xb_ref.at[pl.ds(at, size)], zsem))
                return c

            lax.fori_loop(0, count, body, 0)

        def per_expert(e, carry):
            start = pad_ref[e]
            length = pad_ref[N_EXPERTS + e]
            head = jnp.minimum((-start) & 7, length)
            rows_at(start, head, 1)
            rows_at(start + head, (length - head) >> 3, 8)
            rows_at(start + head + ((length - head) & ~7), (length - head) & 7, 1)
            return carry

        lax.fori_loop(0, N_EXPERTS, per_expert, 0)
        rows_at(pad_ref[2 * N_EXPERTS], pad_ref[2 * N_EXPERTS + 1], zero_ref.shape[0])

    @pl.when(i == 0)
    def _():
        zero_ref[...] = jnp.zeros(zero_ref.shape, jnp.uint32)
        pad_copies(lambda c: c.start())
        pad_copies(lambda c: c.wait())

    def start_rows(buf):
        for r in range(tm):
            for choice in range(2):
                dst = slot_ref[choice * (n_steps * tm) + i * tm + r]
                pltpu.make_async_copy(stage_ref.at[buf, pl.ds(r, 1)], xb_ref.at[pl.ds(dst, 1)], sems.at[buf]).start(priority=choice)

    def wait_rows(buf):
        for _ in range(2):
            pltpu.make_async_copy(stage_ref.at[buf], xb_ref.at[pl.ds(0, tm)], sems.at[buf]).wait()

    for buf in range(2):
        @pl.when(i % 2 == buf)
        def _(buf=buf):
            @pl.when(i >= 2)
            def _():
                wait_rows(buf)

            stage_ref[buf] = _pack_bf16_pairs(h_ref[...])
            start_rows(buf)

            @pl.when(i == n_steps - 1)
            def _():
                @pl.when(i >= 1)
                def _():
                    wait_rows(1 - buf)

                wait_rows(buf)


def _scatter_rows(slot, pad, h2, n_slots, seq):
    t, d = h2.shape
    tm = min(ROW_TILE, seq)
    n_steps = t // tm
    grid_spec = pltpu.PrefetchScalarGridSpec(
        num_scalar_prefetch=2,
        grid=(n_steps,),
        in_specs=[pl.BlockSpec((tm, d), lambda i, sl, pd: (i, 0))],
        out_specs=pl.BlockSpec(memory_space=pl.ANY),
        scratch_shapes=[pltpu.VMEM((2, tm, d // 2), jnp.uint32), pltpu.VMEM((MOE_BLOCK // 2, d // 2), jnp.uint32),
                        pltpu.SemaphoreType.DMA((2,)), pltpu.SemaphoreType.DMA(())],
    )
    return pl.pallas_call(
        functools.partial(_scatter_kernel, tm=tm, n_steps=n_steps),
        grid_spec=grid_spec,
        out_shape=jax.ShapeDtypeStruct((n_slots, d // 2), jnp.uint32),
        compiler_params=_params(("arbitrary",)),
    )(slot, pad, h2)


def _moe_back_kernel(slot_ref, x_ref, route_ref, mod_ref, lng_ref, lnb_ref, yb_ref, o_ref, ybuf, sems,
                     *, tm, n_steps):
    i = pl.program_id(0)

    def start_rows(tile, buf):
        for r in range(tm):
            for choice in range(2):
                src = slot_ref[choice * (n_steps * tm) + tile * tm + r]
                pltpu.make_async_copy(yb_ref.at[pl.ds(src, 1)], ybuf.at[buf, choice, pl.ds(r, 1)], sems.at[buf]).start(priority=choice)

    def wait_rows(buf):
        for choice in range(2):
            pltpu.make_async_copy(yb_ref.at[pl.ds(0, tm)], ybuf.at[buf, choice], sems.at[buf]).wait()

    @pl.when(i == 0)
    def _():
        start_rows(0, 0)

    for buf in range(2):
        @pl.when(i % 2 == buf)
        def _(buf=buf):
            wait_rows(buf)
            start_rows(jnp.minimum(i + 1, n_steps - 1), 1 - buf)
            rt = route_ref[...]
            y0 = jnp.concatenate(_unpack_bf16_pairs(ybuf[buf, 0]), axis=1)
            y1 = jnp.concatenate(_unpack_bf16_pairs(ybuf[buf, 1]), axis=1)
            y = y0 * rt[:, 2:3] + y1 * rt[:, 3:4]
            z = DEEPNORM_ALPHA * x_ref[...] + (1.0 + mod_ref[5:6, :]) * y
            o_ref[...] = _layer_norm(z, lng_ref[...], lnb_ref[...])

            @pl.when(i == n_steps - 1)
            def _():
                wait_rows(1 - buf)


def _moe_back(slot, x1, yb, route, mod_l, ln_g, ln_b, seq):
    t, d = x1.shape
    tm = min(ROW_TILE, seq)
    tiles_per_seq = seq // tm
    n_steps = t // tm
    row = lambda n: pl.BlockSpec((tm, n), lambda i, sl: (i, 0))
    const = lambda shape: pl.BlockSpec(shape, lambda i, sl: (0,) * len(shape))
    grid_spec = pltpu.PrefetchScalarGridSpec(
        num_scalar_prefetch=1,
        grid=(n_steps,),
        in_specs=[row(d), row(LANES),
                  pl.BlockSpec((None, N_MOD, d), lambda i, sl: (i // tiles_per_seq, 0, 0)),
                  const(ln_g.shape), const(ln_b.shape), pl.BlockSpec(memory_space=pl.ANY)],
        out_specs=row(d),
        scratch_shapes=[pltpu.VMEM((2, 2, tm, d // 2), jnp.uint32), pltpu.SemaphoreType.DMA((2,))],
    )
    return pl.pallas_call(
        functools.partial(_moe_back_kernel, tm=tm, n_steps=n_steps),
        grid_spec=grid_spec,
        out_shape=jax.ShapeDtypeStruct((t, d), F32),
        compiler_params=_params(("arbitrary",)),
    )(slot, x1, route, mod_l, ln_g, ln_b, yb)


N_COMBINE_IN = 7


def _fused_combine(slot_ref, x1_ref, route_ref, modp_ref, lng_ref, lnb_ref, yb_ref, xo_ref, ybuf, sems, tm, n_steps):
    i = pl.program_id(0)
    buf = i % 2

    def start_rows(tile, into):
        for r in range(tm):
            for choice in range(2):
                src = slot_ref[choice * (n_steps * tm) + tile * tm + r]
                pltpu.make_async_copy(yb_ref.at[pl.ds(src, 1)], ybuf.at[into, choice, pl.ds(r, 1)],
                                      sems.at[into]).start(priority=choice)

    def wait_rows(into):
        for choice in range(2):
            pltpu.make_async_copy(yb_ref.at[pl.ds(0, tm)], ybuf.at[into, choice], sems.at[into]).wait()

    @pl.when(i == 0)
    def _():
        start_rows(0, 0)

    wait_rows(buf)
    rt = route_ref[...]
    y0 = jnp.concatenate(_unpack_bf16_pairs(ybuf[buf, 0]), axis=1)
    y1 = jnp.concatenate(_unpack_bf16_pairs(ybuf[buf, 1]), axis=1)
    y = y0 * rt[:, 2:3] + y1 * rt[:, 3:4]
    z = DEEPNORM_ALPHA * x1_ref[...] + (1.0 + modp_ref[5:6, :]) * y
    x = _layer_norm(z, lng_ref[...], lnb_ref[...])
    xo_ref[...] = x
    start_rows(jnp.minimum(i + 1, n_steps - 1), 1 - buf)

    def drain():
        @pl.when(i == n_steps - 1)
        def _():
            wait_rows(1 - buf)

    return x, drain


def _combine_specs(prev, tm, tiles_per_seq, d):
    slot, x1, yb, route, mod_prev, ln_g, ln_b = prev
    row = lambda n: pl.BlockSpec((tm, n), lambda i, *_: (i, 0))
    const = lambda shape: pl.BlockSpec(shape, lambda i, *_: (0,) * len(shape))
    specs = [row(d), row(LANES), pl.BlockSpec((None, N_MOD, d), lambda i, *_: (i // tiles_per_seq, 0, 0)),
             const(ln_g.shape), const(ln_b.shape), pl.BlockSpec(memory_space=pl.ANY)]
    return slot, [x1, route, mod_prev, ln_g, ln_b, yb], specs


def _odd_front_kernel(*refs, tm, n_steps):
    (mod_ref, w_ref, tab_ref, q_ref, k_ref, v_ref, gate_ref, xo_ref, ybuf, sems) = refs[N_COMBINE_IN:]
    x, drain = _fused_combine(*refs[:N_COMBINE_IN], xo_ref, ybuf, sems, tm, n_steps)
    hb = _bf(x * (1.0 + mod_ref[1:2, :]) + mod_ref[0:1, :])
    c2 = tab_ref[:, 0:LANES]
    s2 = tab_ref[:, LANES:2 * LANES]
    nqk = RET_HEADS * RET_DK
    nv = RET_HEADS * RET_DV

    def rope(b):
        return b * c2 + pltpu.roll(b, RET_DK // 2, 1) * s2

    qa = _dot(hb, w_ref[:, 0:nqk])
    ka = _dot(hb, w_ref[:, nqk:2 * nqk])
    for h in range(RET_HEADS):
        sl = slice(h * LANES, (h + 1) * LANES)
        q_ref[:, sl] = _bf(rope(qa[:, sl]) * (RET_DK ** -0.5))
        k_ref[:, sl] = _bf(rope(ka[:, sl]))
    v_ref[...] = _bf(_dot(hb, w_ref[:, 2 * nqk:2 * nqk + nv]))
    gate_ref[...] = _silu(_dot(hb, w_ref[:, 2 * nqk + nv:2 * nqk + 2 * nv]))
    drain()


def _odd_front(prev, mod_l, w_in, tabs, seq):
    t, d = prev[1].shape
    tm = min(ROW_TILE, seq)
    tiles_per_seq = seq // tm
    n_steps = t // tm
    nqk = RET_HEADS * RET_DK
    nv = RET_HEADS * RET_DV
    row = lambda n: pl.BlockSpec((tm, n), lambda i, *_: (i, 0))
    const = lambda shape: pl.BlockSpec(shape, lambda i, *_: (0,) * len(shape))
    slot, operands, specs = _combine_specs(prev, tm, tiles_per_seq, d)
    grid_spec = pltpu.PrefetchScalarGridSpec(
        num_scalar_prefetch=1,
        grid=(n_steps,),
        in_specs=specs + [pl.BlockSpec((None, N_MOD, d), lambda i, *_: (i // tiles_per_seq, 0, 0)),
                          const(w_in.shape), row(2 * LANES)],
        out_specs=(row(nqk), row(nqk), row(nv), row(nv), row(d)),
        scratch_shapes=[pltpu.VMEM((2, 2, tm, d // 2), jnp.uint32), pltpu.SemaphoreType.DMA((2,))],
    )
    return pl.pallas_call(
        functools.partial(_odd_front_kernel, tm=tm, n_steps=n_steps),
        grid_spec=grid_spec,
        out_shape=(jax.ShapeDtypeStruct((t, nqk), BF16), jax.ShapeDtypeStruct((t, nqk), BF16),
                   jax.ShapeDtypeStruct((t, nv), BF16), jax.ShapeDtypeStruct((t, nv), F32),
                   jax.ShapeDtypeStruct((t, d), F32)),
        compiler_params=_params(("arbitrary",)),
    )(slot, *operands, mod_l, w_in, tabs)


def _ret_kernel(q_ref, k_ref, v_ref, gate_ref, lg_ref, gng_ref, gnb_ref, o_ref, state_ref, decay_ref, *, C):
    heads = range(RET_HEADS)
    lg = [lg_ref[h][:, 0:1] for h in heads]

    @pl.when(pl.program_id(1) == 0)
    def _():
        state_ref[...] = jnp.zeros(state_ref.shape, F32)
        ri = lax.broadcasted_iota(jnp.int32, (C, C), 0)
        ci = lax.broadcasted_iota(jnp.int32, (C, C), 1)
        diff = (ri - ci).astype(F32)
        causal = ri >= ci
        for h in heads:
            decay_ref[h] = jnp.where(causal, jnp.exp(jnp.where(causal, diff, 0.0) * lg[h]), 0.0)

    idx = lax.broadcasted_iota(jnp.int32, (C, 1), 0).astype(F32)
    qs = [q_ref[:, h * RET_DK:(h + 1) * RET_DK] for h in heads]
    ks = [k_ref[:, h * RET_DK:(h + 1) * RET_DK] for h in heads]
    vs = [v_ref[:, h * RET_DV:(h + 1) * RET_DV] for h in heads]
    states = [state_ref[h] for h in heads]
    scores = [_dot_nt(qs[h], ks[h]) for h in heads]
    cross = [_dot(qs[h], _bf(states[h])) for h in heads]
    kd = [_bf(ks[h].astype(F32) * jnp.exp((C - 1.0 - idx) * lg[h])) for h in heads]
    upd = [_dot_tn(kd[h], vs[h]) for h in heads]
    inner = [_dot(_bf(scores[h] * decay_ref[h]), vs[h]) for h in heads]
    for h in heads:
        state_ref[h] = states[h] * jnp.exp(C * lg[h]) + upd[h]
        o = inner[h] + cross[h] * jnp.exp((idx + 1.0) * lg[h])
        sl = slice(h * RET_DV, (h + 1) * RET_DV)
        o_ref[:, sl] = _bf(gate_ref[:, sl] * _layer_norm(o, gng_ref[h], gnb_ref[h]))


def _retention(q, k, v, gate, lg, gn_g, gn_b, bsz, seq):
    t = q.shape[0]
    C = min(RET_CHUNK, seq)
    nc = seq // C
    blk = lambda n: pl.BlockSpec((C, n), lambda b, j: (b * nc + j, 0))
    return pl.pallas_call(
        functools.partial(_ret_kernel, C=C),
        grid=(bsz, nc),
        in_specs=[blk(RET_HEADS * RET_DK), blk(RET_HEADS * RET_DK), blk(RET_HEADS * RET_DV), blk(RET_HEADS * RET_DV),
                  _const_spec(lg.shape), _const_spec(gn_g.shape), _const_spec(gn_b.shape)],
        out_specs=blk(RET_HEADS * RET_DV),
        out_shape=jax.ShapeDtypeStruct((t, RET_HEADS * RET_DV), BF16),
        scratch_shapes=[pltpu.VMEM((RET_HEADS, RET_DK, RET_DV), F32), pltpu.VMEM((RET_HEADS, C, C), F32)],
        compiler_params=_params(("parallel", "arbitrary")),
    )(q, k, v, gate, lg, gn_g, gn_b)


def _relayout_hy_w_in(w):
    d = w.shape[0]
    z = lambda n: jnp.zeros((d, n), w.dtype)
    c_rope = MLA_Q_RANK + MLA_KV_RANK
    c_gdn = c_rope + MLA_ROPE
    c_a = c_gdn + GDN_QKV
    c_gate = c_a + 2 * GDN_HEADS
    return _bf(jnp.concatenate([
        w[:, 0:c_rope], z(MLA_NOPE), w[:, c_rope:c_gdn], z(LANES - MLA_NOPE - MLA_ROPE),
        w[:, c_gdn:c_a], w[:, c_gate:], w[:, c_a:c_gate], z(LANES - 2 * GDN_HEADS)], axis=1))


def _relayout_w_uq(w):
    r = w.shape[0]
    wh = w.reshape(r, MLA_HEADS, MLA_NOPE + MLA_ROPE)
    pad = jnp.zeros((r, MLA_HEADS, LANES - MLA_NOPE - MLA_ROPE), w.dtype)
    return _bf(jnp.concatenate([wh, pad], axis=-1).reshape(r, MLA_HEADS * LANES))


def _relayout_w_ukv(w):
    r = w.shape[0]
    wh = w.reshape(r, MLA_HEADS, MLA_NOPE + MLA_V)
    pad = jnp.zeros((r, MLA_HEADS, LANES - MLA_NOPE), w.dtype)
    wk = jnp.concatenate([wh[:, :, :MLA_NOPE], pad], axis=-1).reshape(r, MLA_HEADS * LANES)
    wv = wh[:, :, MLA_NOPE:].reshape(r, MLA_HEADS * MLA_V).T
    return _bf(wk), _bf(wv)


def _rope_angles(positions, dim):
    inv_freq = ROPE_BASE ** (-jnp.arange(0, dim, 2, dtype=F32) / dim)
    ang = positions.astype(F32).reshape(-1)[:, None] * inv_freq
    return jnp.cos(ang), jnp.sin(ang)


def _expand_lanes(compact, placement):
    return jnp.dot(compact, jnp.asarray(placement), precision=lax.Precision.HIGHEST)


def _mla_rope_tables(positions):
    cos, sin = _rope_angles(positions, MLA_ROPE)
    half = MLA_ROPE // 2
    place = np.zeros((2 * half, 3 * LANES), np.float32)
    for j in range(half):
        place[j, MLA_NOPE + j] = place[j, MLA_NOPE + half + j] = 1.0
        place[half + j, LANES + MLA_NOPE + j] = -1.0
        place[half + j, 2 * LANES + MLA_NOPE + half + j] = 1.0
    ones = np.zeros((1, 3 * LANES), np.float32)
    ones[0, :MLA_NOPE] = 1.0
    return _expand_lanes(jnp.concatenate([cos, sin], axis=1), place) + jnp.asarray(ones)


def _ret_rope_tables(positions):
    cos, sin = _rope_angles(positions, RET_DK)
    half = RET_DK // 2
    place = np.zeros((2 * half, 2 * LANES), np.float32)
    for j in range(half):
        place[j, j] = place[j, half + j] = 1.0
        place[half + j, LANES + j] = -1.0
        place[half + j, LANES + half + j] = 1.0
    return _expand_lanes(jnp.concatenate([cos, sin], axis=1), place)


def _router_weights(w_group, b_group, w_expert, b_expert):
    d = w_group.shape[0]
    pad = LANES - N_EXPERTS - N_GROUPS
    w = jnp.concatenate([w_expert, w_group, jnp.zeros((d, pad), F32)], axis=1)
    b = jnp.concatenate([b_expert, b_group, jnp.zeros((pad,), F32)]).reshape(1, LANES)
    hi = _bf(w)
    return hi, _bf(w - hi.astype(F32)), b


def _pad_lanes(v):
    return jnp.zeros((LANES,), F32).at[:v.shape[0]].set(v)


def kernel(x, c, positions, ada_w, ada_b, ln_mix_g, ln_mix_b, ln_ffn_g, ln_ffn_b, hy_w_in, mla_q_norm, mla_w_uq, mla_kv_norm, mla_w_ukv, gdn_conv_w, gdn_a_log, gdn_dt_bias, gdn_norm, hy_w_out, ret_w_in, ret_gn_g, ret_gn_b, ret_w_out, moe_w_group, moe_b_group, moe_w_expert, moe_b_expert, moe_w_gate, moe_w_up, moe_w_down):
    bsz, seq, d = x.shape
    t = bsz * seq
    mods = _ada_mod(c, ada_w, ada_b)
    mla_tabs = _mla_rope_tables(positions)
    ret_tabs = _ret_rope_tables(positions)
    log_gamma = jnp.log(1.0 - 2.0 ** (-5.0 - jnp.arange(RET_HEADS, dtype=F32)))
    lg = jnp.broadcast_to(log_gamma[:, None, None], (RET_HEADS, 1, LANES))
    xc = x.reshape(t, d)
    for layer in range(DEPTH):
        i = layer // 2
        mod_l = mods[layer]
        if layer % 2 == 0:
            wuk_p, wuv = _relayout_w_ukv(mla_w_ukv[i])
            gparam = jnp.stack([_pad_lanes(gdn_a_log[i]), _pad_lanes(gdn_dt_bias[i])])
            front = _even_front(
                xc if layer == 0 else pending, mod_l, _relayout_hy_w_in(hy_w_in[i]), mla_q_norm[i].reshape(1, -1), _relayout_w_uq(mla_w_uq[i]),
                mla_kv_norm[i].reshape(1, -1), wuk_p, wuv, gdn_conv_w[i], gparam, mla_tabs, seq)
            (qm, km, vm, gq, gk, gv, gate, gcol, grow) = front[:9]
            if layer > 0:
                xc = front[9]
            o_mla = _attention(qm, km, vm, bsz, seq)
            n_ch = t // GDN_CHUNK
            growp = grow[:GDN_HEADS].reshape(GDN_PAIRS, 2, n_ch, GDN_CHUNK).transpose(2, 0, 1, 3).reshape(n_ch, GDN_PAIRS, LANES)
            onorm2 = jnp.concatenate([gdn_norm[i], gdn_norm[i]]).reshape(1, LANES)
            y_gdn = _gdn(gq, gk, gv, gate, gcol, growp, onorm2, bsz, seq)
            acts = [o_mla, y_gdn]
            w_out = _bf(hy_w_out[i])
        else:
            q, k, v, gate, xc = _odd_front(pending, mod_l, _bf(ret_w_in[i]), ret_tabs, seq)
            og = _retention(q, k, v, gate, lg, ret_gn_g[i].reshape(RET_HEADS, 1, RET_DV),
                            ret_gn_b[i].reshape(RET_HEADS, 1, RET_DV), bsz, seq)
            acts = [og]
            w_out = _bf(ret_w_out[i])
        wr_hi, wr_lo, br = _router_weights(moe_w_group[layer], moe_b_group[layer], moe_w_expert[layer], moe_b_expert[layer])
        x1, h2, route, er, counts = _mixer_back(acts, w_out, xc, mod_l, ln_mix_g[layer].reshape(1, d), ln_mix_b[layer].reshape(1, d),
                                    wr_hi, wr_lo, br, seq)
        slot, block_expert, n_used, pad, n_slots = _slot_tables(er, counts, t)
        xb = _scatter_rows(slot, pad, h2, n_slots, seq)
        yb = _moe_experts(block_expert, n_used, xb, moe_w_gate, moe_w_up, moe_w_down, layer)
        pending = (slot, x1, yb, route, mod_l, ln_ffn_g[layer].reshape(1, d), ln_ffn_b[layer].reshape(1, d))
        if layer == DEPTH - 1:
            xc = _moe_back(*pending, seq)
    return xc.reshape(bsz, seq, d)
```

```python
import functools

import jax
import jax.numpy as jnp
import numpy as np
from jax import lax
from jax.experimental import pallas as pl
from jax.experimental.pallas import tpu as pltpu

F32 = jnp.float32
BF16 = jnp.bfloat16

DEPTH = 4
N_MOD = 6
LANES = 128

MLA_HEADS = 8
MLA_NOPE = 64
MLA_ROPE = 32
MLA_V = 64
MLA_Q_RANK = 256
MLA_KV_RANK = 128
MLA_SCALE = (MLA_NOPE + MLA_ROPE) ** -0.5
LOG2_E = 1.4426950408889634

GDN_HEADS = 8
GDN_DK = 64
GDN_DV = 64
GDN_CONV = 4
GDN_CHUNK = 64
GDN_HD = GDN_HEADS * GDN_DK
GDN_QKV = 3 * GDN_HD
GDN_PAIRS = GDN_HEADS // 2
GDN_GROUP = 4

RET_HEADS = 8
RET_DK = 128
RET_DV = 256
RET_CHUNK = 256

N_GROUPS = 4
EXPERTS_PER_GROUP = 8
N_EXPERTS = 32
EXPERT_FF = 512
MOE_BLOCK = 512

ROPE_BASE = 10000.0
LN_EPS = 1e-5
RMS_EPS = 1e-6
DEEPNORM_ALPHA = (2.0 * DEPTH) ** 0.25

ROW_TILE = 256
WIDE_ROW_TILE = 512
ATTN_TILE = 1024
VMEM_LIMIT = 56 * 1024 * 1024

_C_MLA = 0
_C_QKV = 512
_C_GATE = 2048
_C_AB = 2560
_C_END = 2688


def _bf(x):
    return x.astype(BF16)


def _dot(a, b):
    return jnp.dot(a, b, preferred_element_type=F32)


def _dot_nt(a, b):
    return lax.dot_general(a, b, (((1,), (1,)), ((), ())), preferred_element_type=F32)


def _dot_tn(a, b):
    return lax.dot_general(a, b, (((0,), (0,)), ((), ())), preferred_element_type=F32)


def _split2(x):
    hi = _bf(x)
    return hi, _bf(x - hi.astype(F32))


def _split3(x):
    hi = _bf(x)
    r = x - hi.astype(F32)
    mid = _bf(r)
    return hi, mid, _bf(r - mid.astype(F32))


def _silu(x):
    return x * jax.nn.sigmoid(x)


def _pack_bf16_pairs(x):
    n = x.shape[1] // 2
    bits = lambda v: lax.bitcast_convert_type(_bf(v).astype(F32), jnp.uint32)
    return (bits(x[:, n:]) & jnp.uint32(0xFFFF0000)) | (bits(x[:, :n]) >> 16)


def _unpack_bf16_pairs(w):
    lo = lax.bitcast_convert_type(w << 16, F32)
    hi = lax.bitcast_convert_type(w & jnp.uint32(0xFFFF0000), F32)
    return lo, hi


def _layer_norm(z, g, b):
    mu = jnp.mean(z, -1, keepdims=True)
    zc = z - mu
    var = jnp.mean(zc * zc, -1, keepdims=True)
    return zc * lax.rsqrt(var + LN_EPS) * g + b


def _rms_norm(x, g):
    return x * lax.rsqrt(jnp.mean(x * x, -1, keepdims=True) + RMS_EPS) * g


def _params(sem):
    return pltpu.CompilerParams(dimension_semantics=sem, vmem_limit_bytes=VMEM_LIMIT)


def _const_spec(shape):
    nd = len(shape)
    return pl.BlockSpec(shape, lambda *_: (0,) * nd)


def _ada_kernel(c_ref, w_ref, b_ref, o_ref):
    c = c_ref[...]
    o_ref[...] = _dot(_bf(_silu(c)), _bf(w_ref[...])) + b_ref[...]


def _ada_mod(c, ada_w, ada_b):
    depth, d, n = ada_w.shape
    bsz = c.shape[0]
    rows = 16
    tn = 1536
    cp = jnp.zeros((rows, d), F32).at[:bsz].set(c)
    out = pl.pallas_call(
        _ada_kernel,
        grid=(depth, n // tn),
        in_specs=[pl.BlockSpec((rows, d), lambda l, j: (0, 0)),
                  pl.BlockSpec((None, d, tn), lambda l, j: (l, 0, j)),
                  pl.BlockSpec((None, 1, tn), lambda l, j: (l, 0, j))],
        out_specs=pl.BlockSpec((None, rows, tn), lambda l, j: (l, 0, j)),
        out_shape=jax.ShapeDtypeStruct((depth, rows, n), F32),
        compiler_params=_params(("parallel", "parallel")),
    )(cp, ada_w, ada_b.reshape(depth, 1, n))
    return out[:, :bsz].reshape(depth, bsz, N_MOD, d)


def _even_front_kernel(*refs, fused, tm, tiles_per_seq, n_steps):
    n_head = N_COMBINE_IN if fused else 1
    (mod_ref, w_ref, qn_ref, wuq_ref, kvn_ref, wuk_ref, wuv_ref, cw_ref, gp_ref, tab_ref, ones_ref, tri_ref,
     q_ref, k_ref, v_ref, gq_ref, gk_ref, gv_ref, gate_ref, gcol_ref, grow_ref) = refs[n_head:n_head + 21]
    if fused:
        xo_ref, xbuf, ybuf, sems = refs[n_head + 21:]
        x, drain = _fused_combine(*refs[:n_head], xo_ref, ybuf, sems, tm, n_steps)
    else:
        (xbuf,) = refs[n_head + 21:]
        x = refs[0][...]
    i = pl.program_id(0)
    hb = _bf(x * (1.0 + mod_ref[1:2, :]) + mod_ref[0:1, :])

    pm = _dot(hb, w_ref[:, _C_MLA:_C_QKV])
    ct = tab_ref[:, 0:LANES]
    s1 = tab_ref[:, LANES:2 * LANES]
    s2 = tab_ref[:, 2 * LANES:3 * LANES]

    def rope(b):
        return b * ct + pltpu.roll(b, LANES - 16, 1) * s1 + pltpu.roll(b, 16, 1) * s2

    qa = _dot(_bf(_rms_norm(pm[:, 0:MLA_Q_RANK], qn_ref[...])), wuq_ref[...])
    for h in range(MLA_HEADS):
        sl = slice(h * LANES, (h + 1) * LANES)
        q_ref[:, sl] = _bf(rope(qa[:, sl]) * (MLA_SCALE * LOG2_E))
    kvn = _bf(_rms_norm(pm[:, MLA_Q_RANK:MLA_Q_RANK + MLA_KV_RANK], kvn_ref[...]))
    ka = _dot(kvn, wuk_ref[...])
    kr = rope(pm[:, MLA_Q_RANK + MLA_KV_RANK:_C_QKV])
    for h in range(MLA_HEADS):
        sl = slice(h * LANES, (h + 1) * LANES)
        k_ref[:, sl] = _bf(ka[:, sl] + kr)
    v_ref[...] = _bf(_dot_nt(wuv_ref[...], kvn))

    @pl.when(i % tiles_per_seq == 0)
    def _():
        xbuf[0:8, :] = jnp.zeros((8, GDN_QKV), F32)

    xbuf[8:8 + tm, :] = _dot(hb, w_ref[:, _C_QKV:_C_GATE])
    cw = cw_ref[...]
    y = cw[0:1, :] * xbuf[5:5 + tm, :]
    for t in range(1, GDN_CONV):
        y = y + cw[t:t + 1, :] * xbuf[5 + t:5 + t + tm, :]
    xbuf[0:8, :] = xbuf[tm:tm + 8, :]
    y = _silu(y)

    def head_sumsq(z):
        return _dot(_bf(z * z), ones_ref[...])

    qg = y[:, 0:GDN_HD]
    kg = y[:, GDN_HD:2 * GDN_HD]
    gq_ref[...] = qg * lax.rsqrt(head_sumsq(qg) + RMS_EPS) * (GDN_DK ** -0.5)
    gk_ref[...] = kg * lax.rsqrt(head_sumsq(kg) + RMS_EPS)
    gv_ref[...] = y[:, 2 * GDN_HD:3 * GDN_HD]
    gate_ref[...] = _dot(hb, w_ref[:, _C_GATE:_C_AB])

    ab = _dot(hb, w_ref[:, _C_AB:_C_END])
    lane = lax.broadcasted_iota(jnp.int32, (tm, LANES), 1)
    z = ab + gp_ref[1:2, :]
    softplus = jnp.maximum(z, 0.0) + jnp.log(1.0 + jnp.exp(-jnp.abs(z)))
    la = jnp.where(lane < GDN_HEADS, -jnp.exp(gp_ref[0:1, :]) * softplus, 0.0)
    tri = tri_ref[...]
    l0, l1, l2 = _split3(la)
    g = _dot(tri, l0) + _dot(tri, l1) + _dot(tri, l2)
    gcol = jnp.where(lane < GDN_HEADS, g, jax.nn.sigmoid(ab))
    gcol_ref[...] = gcol
    grow_ref[...] = gcol.T[0:16, :]
    if fused:
        drain()


def _even_front(src, mod_l, w_in_p, qn, wuq_p, kvn, wuk_p, wuv, conv_w, gparam, tabs, seq):
    fused = isinstance(src, tuple)
    t, d = (src[1] if fused else src).shape
    tm = min(ROW_TILE, seq)
    tiles_per_seq = seq // tm
    n_steps = t // tm
    row = lambda n: pl.BlockSpec((tm, n), lambda i, *_: (i, 0))
    idx = np.arange(GDN_HD) // GDN_DK
    head_ones = jnp.asarray(idx[:, None] == idx[None, :], BF16)
    pos = np.arange(tm)
    chunk_tri = jnp.asarray((pos[:, None] // GDN_CHUNK == pos[None, :] // GDN_CHUNK) & (pos[None, :] <= pos[:, None]), BF16)
    out_shapes = (
        jax.ShapeDtypeStruct((t, MLA_HEADS * LANES), BF16),
        jax.ShapeDtypeStruct((t, MLA_HEADS * LANES), BF16),
        jax.ShapeDtypeStruct((MLA_HEADS * MLA_V, t), BF16),
        jax.ShapeDtypeStruct((t, GDN_HD), F32),
        jax.ShapeDtypeStruct((t, GDN_HD), F32),
        jax.ShapeDtypeStruct((t, GDN_HD), F32),
        jax.ShapeDtypeStruct((t, GDN_HD), F32),
        jax.ShapeDtypeStruct((t, LANES), F32),
        jax.ShapeDtypeStruct((16, t), F32),
    )
    if fused:
        slot, head_operands, head_specs = _combine_specs(src, tm, tiles_per_seq, d)
        prefetch = [slot]
    else:
        head_operands, head_specs, prefetch = [src], [row(d)], []
    in_specs = head_specs + [
        pl.BlockSpec((None, N_MOD, d), lambda i, *_: (i // tiles_per_seq, 0, 0)),
        _const_spec(w_in_p.shape), _const_spec(qn.shape), _const_spec(wuq_p.shape),
        _const_spec(kvn.shape), _const_spec(wuk_p.shape), _const_spec(wuv.shape),
        _const_spec(conv_w.shape), _const_spec(gparam.shape), row(3 * LANES),
        _const_spec(head_ones.shape), _const_spec(chunk_tri.shape)]
    out_specs = (row(MLA_HEADS * LANES), row(MLA_HEADS * LANES),
                 pl.BlockSpec((MLA_HEADS * MLA_V, tm), lambda i, *_: (0, i)),
                 row(GDN_HD), row(GDN_HD), row(GDN_HD), row(GDN_HD), row(LANES),
                 pl.BlockSpec((16, tm), lambda i, *_: (0, i)))
    scratch = [pltpu.VMEM((tm + 8, GDN_QKV), F32)]
    if fused:
        out_specs += (row(d),)
        out_shapes += (jax.ShapeDtypeStruct((t, d), F32),)
        scratch += [pltpu.VMEM((2, 2, tm, d // 2), jnp.uint32), pltpu.SemaphoreType.DMA((2,))]
    grid_spec = pltpu.PrefetchScalarGridSpec(
        num_scalar_prefetch=len(prefetch), grid=(n_steps,), in_specs=in_specs, out_specs=out_specs,
        scratch_shapes=scratch)
    return pl.pallas_call(
        functools.partial(_even_front_kernel, fused=fused, tm=tm, tiles_per_seq=tiles_per_seq, n_steps=n_steps),
        grid_spec=grid_spec,
        out_shape=out_shapes,
        compiler_params=_params(("arbitrary",)),
    )(*prefetch, *head_operands, mod_l, w_in_p, qn, wuq_p, kvn, wuk_p, wuv, conv_w, gparam, tabs, head_ones, chunk_tri)


def _attn_kernel(qi_ref, ki_ref, q_ref, k_ref, vt_ref, o_ref, m_ref, l_ref, acc_ref, *, tq):
    qi = qi_ref[pl.program_id(1)]
    ki = ki_ref[pl.program_id(1)]

    @pl.when(ki == 0)
    def _():
        m_ref[...] = jnp.full(m_ref.shape, -jnp.inf, F32)
        l_ref[...] = jnp.zeros(l_ref.shape, F32)
        acc_ref[...] = jnp.zeros(acc_ref.shape, F32)

    def scores(h):
        sl = slice(h * LANES, (h + 1) * LANES)
        return _dot_nt(k_ref[:, sl], q_ref[:, sl])

    def step(diagonal):
        if diagonal:
            key = lax.broadcasted_iota(jnp.int32, (tq, tq), 0)
            qry = lax.broadcasted_iota(jnp.int32, (tq, tq), 1)
            keep = key <= qry
        s_next = scores(0)
        for h in range(MLA_HEADS):
            s = s_next
            if h + 1 < MLA_HEADS:
                s_next = scores(h + 1)
            if diagonal:
                s = jnp.where(keep, s, -jnp.inf)
            rows = slice(h * MLA_V, (h + 1) * MLA_V)
            m_prev = m_ref[h:h + 1, :]
            m_new = jnp.maximum(m_prev, jnp.max(s, 0, keepdims=True))
            alpha = jnp.exp2(m_prev - m_new)
            p = jnp.exp2(s - m_new)
            l_ref[h:h + 1, :] = alpha * l_ref[h:h + 1, :] + jnp.sum(p, 0, keepdims=True)
            acc_ref[rows, :] = alpha * acc_ref[rows, :] + _dot(vt_ref[rows, :], _bf(p))
            m_ref[h:h + 1, :] = m_new

    @pl.when(ki < qi)
    def _():
        step(False)

    @pl.when(ki == qi)
    def _():
        step(True)
        for h in range(MLA_HEADS):
            rows = slice(h * MLA_V, (h + 1) * MLA_V)
            acc_ref[rows, :] = acc_ref[rows, :] / l_ref[h:h + 1, :]
        o_ref[...] = _bf(acc_ref[...].T)


def _attention(qm, km, vm, bsz, seq):
    t = qm.shape[0]
    tq = min(ATTN_TILE, seq)
    nq = seq // tq
    pairs = [(q, k) for q in range(nq) for k in range(q + 1)]
    qi_tab = jnp.asarray(np.array([p[0] for p in pairs], np.int32))
    ki_tab = jnp.asarray(np.array([p[1] for p in pairs], np.int32))
    grid_spec = pltpu.PrefetchScalarGridSpec(
        num_scalar_prefetch=2,
        grid=(bsz, len(pairs)),
        in_specs=[pl.BlockSpec((tq, MLA_HEADS * LANES), lambda b, s, qt, kt: (b * nq + qt[s], 0)),
                  pl.BlockSpec((tq, MLA_HEADS * LANES), lambda b, s, qt, kt: (b * nq + kt[s], 0)),
                  pl.BlockSpec((MLA_HEADS * MLA_V, tq), lambda b, s, qt, kt: (0, b * nq + kt[s]))],
        out_specs=pl.BlockSpec((tq, MLA_HEADS * MLA_V), lambda b, s, qt, kt: (b * nq + qt[s], 0)),
        scratch_shapes=[pltpu.VMEM((MLA_HEADS, tq), F32),
                        pltpu.VMEM((MLA_HEADS, tq), F32),
                        pltpu.VMEM((MLA_HEADS * MLA_V, tq), F32)],
    )
    return pl.pallas_call(
        functools.partial(_attn_kernel, tq=tq),
        grid_spec=grid_spec,
        out_shape=jax.ShapeDtypeStruct((t, MLA_HEADS * MLA_V), BF16),
        compiler_params=_params(("parallel", "arbitrary")),
    )(qi_tab, ki_tab, qm, km, vm)


def _gdn_kernel(q_ref, k_ref, v_ref, gate_ref, gcol_ref, grow_ref, on_ref, y_ref, state_ref, *, n_chunks):
    C = GDN_CHUNK

    @pl.when(pl.program_id(1) == 0)
    def _():
        state_ref[...] = jnp.zeros(state_ref.shape, F32)

    lane = lax.broadcasted_iota(jnp.int32, (C, LANES), 1)
    rowi = lax.broadcasted_iota(jnp.int32, (C, LANES), 0)
    first = lane < GDN_DK
    col = jnp.where(first, lane, lane - GDN_DK)
    tril = col <= rowi
    strict = col < rowi
    eye = jnp.where(col == rowi, 1.0, 0.0)
    blk16 = (col // 16) == (rowi // 16)
    blk32 = (col // 32) == (rowi // 32)
    r2 = lax.broadcasted_iota(jnp.int32, (LANES, LANES), 0)
    c2 = lax.broadcasted_iota(jnp.int32, (LANES, LANES), 1)
    first_rows = r2 < GDN_DK
    head_ones = jnp.where((r2 < GDN_DK) == (c2 < GDN_DK), 1.0, 0.0).astype(BF16)

    def bd(y):
        return jnp.concatenate([jnp.where(first, y, 0), jnp.where(first, 0, y)], axis=0)

    def pmm(x, y):
        return _dot(_bf(x), bd(_bf(y)))

    def pmm_hi(x, y):
        xh, xl = _split2(x)
        yh, yl = _split2(y)
        top = _dot(jnp.concatenate([xh, xl], axis=0), bd(yh))
        return top[:C] + top[C:] + _dot(xh, bd(yl))

    def pair_cols(a, c0, c1):
        return jnp.where(first, a[:, c0:c0 + 1], a[:, c1:c1 + 1])

    def chunk_group(gi, carry):
        pairs = range(GDN_PAIRS)
        items = [(c, p) for c in range(GDN_GROUP) for p in pairs]
        sls = [slice(p * LANES, (p + 1) * LANES) for p in pairs]
        rows = [pl.ds(pl.multiple_of((gi * GDN_GROUP + c) * C, C), C) for c in range(GDN_GROUP)]
        gc = [gcol_ref[rows[c], :] for c in range(GDN_GROUP)]
        gr = [grow_ref[gi * GDN_GROUP + c] for c in range(GDN_GROUP)]
        every = range(len(items))
        q = [q_ref[rows[c], sls[p]] for c, p in items]
        k = [k_ref[rows[c], sls[p]] for c, p in items]
        v = [v_ref[rows[c], sls[p]] for c, p in items]
        g = [pair_cols(gc[c], 2 * p, 2 * p + 1) for c, p in items]
        beta = [pair_cols(gc[c], GDN_HEADS + 2 * p, GDN_HEADS + 2 * p + 1) for c, p in items]
        decay = [jnp.where(tril, jnp.exp(jnp.where(tril, g[i] - gr[c][p:p + 1, :], 0.0)), 0.0)
                 for i, (c, p) in enumerate(items)]
        kb = [k[i] * beta[i] for i in every]
        kbd = [bd(_bf(k[i])) for i in every]
        lmat = [jnp.where(strict, _dot_nt(_bf(kb[i]), kbd[i]) * decay[i], 0.0) for i in every]
        a_intra = [jnp.where(tril, _dot_nt(_bf(q[i]), kbd[i]) * decay[i], 0.0) for i in every]
        n1 = [jnp.where(blk16, lmat[i], 0.0) for i in every]
        x = [eye - n1[i] for i in every]
        nk = n1
        for mm in (pmm_hi, pmm, pmm):
            nk = [mm(nk[i], nk[i]) for i in every]
            x = [x[i] + mm(x[i], nk[i]) for i in every]
        for off in ([jnp.where(blk32 & jnp.logical_not(blk16), lmat[i], 0.0) for i in every],
                    [jnp.where(blk32, 0.0, lmat[i]) for i in every]):
            xb = [pmm(x[i], off[i]) for i in every]
            x = [x[i] - pmm(xb[i], x[i]) for i in every]
        u = [pmm(x[i], v[i] * beta[i]) for i in every]
        w = [pmm(x[i], kb[i] * jnp.exp(g[i])) for i in every]
        for c in range(GDN_GROUP):
            idx = [c * GDN_PAIRS + p for p in pairs]
            state = [state_ref[p] for p in pairs]
            sb = [_bf(state[p]) for p in pairs]
            v_new = [u[i] - _dot(_bf(w[i]), sb[p]) for p, i in enumerate(idx)]
            o = [_dot(_bf(q[i] * jnp.exp(g[i])), sb[p]) + pmm(a_intra[i], v_new[p]) for p, i in enumerate(idx)]
            for p, i in enumerate(idx):
                g_last = g[i][C - 1:C, :]
                k_dec = k[i] * jnp.exp(g_last - g[i])
                scale = jnp.exp(jnp.where(first_rows, g_last[:, 0:1], g_last[:, GDN_DK:GDN_DK + 1]))
                state_ref[p] = state[p] * scale + _dot_tn(bd(_bf(k_dec)), bd(_bf(v_new[p])))
            for p in pairs:
                ms = _dot(_bf(o[p] * o[p]), head_ones) * (1.0 / GDN_DV)
                on = o[p] * lax.rsqrt(ms + RMS_EPS) * on_ref[...]
                y_ref[rows[c], sls[p]] = _bf(on * _silu(gate_ref[rows[c], sls[p]]))
        return carry

    lax.fori_loop(0, n_chunks // GDN_GROUP, chunk_group, 0)


def _gdn(gq, gk, gv, gate, gcol, growp, onorm2, bsz, seq):
    t = gq.shape[0]
    r = min(WIDE_ROW_TILE, seq)
    nblk = seq // r
    n_chunks = r // GDN_CHUNK
    row = lambda n: pl.BlockSpec((r, n), lambda b, j: (b * nblk + j, 0))
    return pl.pallas_call(
        functools.partial(_gdn_kernel, n_chunks=n_chunks),
        grid=(bsz, nblk),
        in_specs=[row(GDN_HD), row(GDN_HD), row(GDN_HD), row(GDN_HD), row(LANES),
                  pl.BlockSpec((n_chunks, GDN_PAIRS, LANES), lambda b, j: (b * nblk + j, 0, 0)),
                  _const_spec(onorm2.shape)],
        out_specs=row(GDN_HD),
        out_shape=jax.ShapeDtypeStruct((t, GDN_HD), BF16),
        scratch_shapes=[pltpu.VMEM((GDN_PAIRS, LANES, LANES), F32)],
        compiler_params=_params(("parallel", "arbitrary")),
    )(gq, gk, gv, gate, gcol, growp, onorm2)


def _back_kernel(*refs, n_in):
    a_refs = refs[:n_in]
    (w_ref, x_ref, mod_ref, lng_ref, lnb_ref, wrh_ref, wrl_ref, br_ref, tri_ref,
     x1_ref, h2_ref, route_ref, er_ref, counts_ref, base_ref) = refs[n_in:]
    off = 0
    y = None
    for a_ref in a_refs:
        kk = a_ref.shape[1]
        part = _dot(a_ref[...], w_ref[off:off + kk, :])
        y = part if y is None else y + part
        off += kk
    z = DEEPNORM_ALPHA * x_ref[...] + (1.0 + mod_ref[2:3, :]) * y
    x1 = _layer_norm(z, lng_ref[...], lnb_ref[...])
    x1_ref[...] = x1
    h2 = x1 * (1.0 + mod_ref[4:5, :]) + mod_ref[3:4, :]
    h2_ref[...] = _pack_bf16_pairs(h2)

    hh, hl = _split2(h2)
    logits = _dot(hh, wrh_ref[...]) + _dot(hh, wrl_ref[...]) + _dot(hl, wrh_ref[...]) + br_ref[...]
    lane = lax.broadcasted_iota(jnp.int32, logits.shape, 1)
    lane_f = lane.astype(F32)
    neg = -jnp.inf
    gmask = (lane >= N_EXPERTS) & (lane < N_EXPERTS + N_GROUPS)
    gl = jnp.where(gmask, logits, neg)
    gmax = jnp.max(gl, -1, keepdims=True)
    g_top = 1.0 / jnp.sum(jnp.where(gmask, jnp.exp(gl - gmax), 0.0), -1, keepdims=True)
    g_lane = jnp.min(jnp.where(gl == gmax, lane_f, 1e9), -1, keepdims=True)
    g_idx = g_lane.astype(jnp.int32) - N_EXPERTS
    emask = (lane < N_EXPERTS) & ((lane // EXPERTS_PER_GROUP) == g_idx)
    el = jnp.where(emask, logits, neg)
    m1 = jnp.max(el, -1, keepdims=True)
    i1 = jnp.min(jnp.where(el == m1, lane_f, 1e9), -1, keepdims=True)
    el2 = jnp.where(lane_f == i1, neg, el)
    m2 = jnp.max(el2, -1, keepdims=True)
    i2 = jnp.min(jnp.where(el2 == m2, lane_f, 1e9), -1, keepdims=True)
    e2 = jnp.exp(m2 - m1)
    w1 = g_top / (1.0 + e2)
    w2 = g_top * e2 / (1.0 + e2)

    @pl.when(pl.program_id(0) == 0)
    def _():
        base_ref[...] = jnp.zeros(base_ref.shape, F32)

    tm = logits.shape[0]
    tri = tri_ref[...]
    oh1 = jnp.where(lane_f == i1, 1.0, 0.0)
    oh2 = jnp.where(lane_f == i2, 1.0, 0.0)
    cs1 = _dot(tri, _bf(oh1))
    cs2 = _dot(tri, _bf(oh2))
    base = base_ref[...]
    tot1 = cs1[tm - 1:tm, :]
    r1 = jnp.sum(oh1 * (base + cs1), -1, keepdims=True) - 1.0
    r2 = jnp.sum(oh2 * (base + tot1 + cs2), -1, keepdims=True) - 1.0
    total = base + tot1 + cs2[tm - 1:tm, :]
    base_ref[...] = total
    counts_ref[...] = total
    route = jnp.where(lane == 0, i1, jnp.where(lane == 1, i2, jnp.where(lane == 2, w1, jnp.where(lane == 3, w2, 0.0))))
    route = jnp.where(lane == 4, r1, jnp.where(lane == 5, r2, route))
    route_ref[...] = route
    er_ref[...] = route.T[0:8, :]


def _mixer_back(acts, w_out, x2, mod_l, ln_g, ln_b, wr_hi, wr_lo, br, seq):
    t, d = x2.shape
    tm = min(WIDE_ROW_TILE, seq)
    tiles_per_seq = seq // tm
    row = lambda n: pl.BlockSpec((tm, n), lambda i: (i, 0))
    pos = np.arange(tm)
    tri = jnp.asarray(pos[None, :] <= pos[:, None], BF16)
    return pl.pallas_call(
        functools.partial(_back_kernel, n_in=len(acts)),
        grid=(t // tm,),
        in_specs=[row(a.shape[1]) for a in acts] + [
            _const_spec(w_out.shape), row(d),
            pl.BlockSpec((None, N_MOD, d), lambda i: (i // tiles_per_seq, 0, 0)),
            _const_spec(ln_g.shape), _const_spec(ln_b.shape),
            _const_spec(wr_hi.shape), _const_spec(wr_lo.shape), _const_spec(br.shape), _const_spec(tri.shape)],
        out_specs=(row(d), row(d // 2), row(LANES), pl.BlockSpec((8, tm), lambda i: (0, i)), _const_spec((1, LANES))),
        out_shape=(jax.ShapeDtypeStruct((t, d), F32), jax.ShapeDtypeStruct((t, d // 2), jnp.uint32),
                   jax.ShapeDtypeStruct((t, LANES), F32), jax.ShapeDtypeStruct((8, t), F32),
                   jax.ShapeDtypeStruct((1, LANES), F32)),
        scratch_shapes=[pltpu.VMEM((1, LANES), F32)],
        compiler_params=_params(("arbitrary",)),
    )(*acts, w_out, x2, mod_l, ln_g, ln_b, wr_hi, wr_lo, br, tri)


def _moe_kernel(be_ref, nb_ref, x_ref, wg_ref, wu_ref, wd_ref, y_ref, wg_s, wu_s, wd_s):
    i = pl.program_id(0)

    @pl.when(i < nb_ref[0])
    def _():
        prev = be_ref[jnp.maximum(i - 1, 0)]

        @pl.when((i == 0) | (be_ref[i] != prev))
        def _():
            wg_s[...] = _bf(wg_ref[...])
            wu_s[...] = _bf(wu_ref[...])
            wd_s[...] = _bf(wd_ref[...])

        lo, hi = _unpack_bf16_pairs(x_ref[...])
        x_lo, x_hi = _bf(lo), _bf(hi)
        half = x_lo.shape[1]
        gate = _dot(x_lo, wg_s[0:half, :]) + _dot(x_hi, wg_s[half:, :])
        up = _dot(x_lo, wu_s[0:half, :]) + _dot(x_hi, wu_s[half:, :])
        y_ref[...] = _pack_bf16_pairs(_dot(_bf(_silu(gate) * up), wd_s[...]))

    @pl.when(i >= nb_ref[0])
    def _():
        y_ref[...] = jnp.zeros(y_ref.shape, jnp.uint32)


def _moe_experts(block_expert, n_used, xb, w_gate, w_up, w_down, layer):
    n_slots = xb.shape[0]
    d, ff = w_gate.shape[-2:]
    n_blocks = n_slots // MOE_BLOCK
    grid_spec = pltpu.PrefetchScalarGridSpec(
        num_scalar_prefetch=2,
        grid=(n_blocks,),
        in_specs=[pl.BlockSpec((MOE_BLOCK, d // 2), lambda i, be, nb: (jnp.minimum(i, nb[0] - 1), 0)),
                  pl.BlockSpec((None, None, d, ff), lambda i, be, nb: (layer, be[i], 0, 0)),
                  pl.BlockSpec((None, None, d, ff), lambda i, be, nb: (layer, be[i], 0, 0)),
                  pl.BlockSpec((None, None, ff, d), lambda i, be, nb: (layer, be[i], 0, 0))],
        out_specs=pl.BlockSpec((MOE_BLOCK, d // 2), lambda i, be, nb: (i, 0)),
        scratch_shapes=[pltpu.VMEM((d, ff), BF16), pltpu.VMEM((d, ff), BF16), pltpu.VMEM((ff, d), BF16)],
    )
    return pl.pallas_call(
        _moe_kernel,
        grid_spec=grid_spec,
        out_shape=jax.ShapeDtypeStruct((n_slots, d // 2), jnp.uint32),
        compiler_params=_params(("arbitrary",)),
    )(block_expert, n_used, xb, w_gate, w_up, w_down)


def _slot_tables(er, counts_row, t):
    counts = counts_row[0, :N_EXPERTS].astype(jnp.int32)
    padded = (counts + MOE_BLOCK - 1) // MOE_BLOCK * MOE_BLOCK
    pend = jnp.cumsum(padded)
    pstart = pend - padded
    e = er[0:2].astype(jnp.int32)
    starts = jnp.sum(jnp.where(e[:, :, None] == jnp.arange(N_EXPERTS, dtype=jnp.int32), pstart, 0), axis=-1)
    slot = (starts + er[4:6].astype(jnp.int32)).reshape(2 * t)
    n_blocks = 2 * t // MOE_BLOCK + N_EXPERTS
    block_start = jnp.arange(n_blocks, dtype=jnp.int32) * MOE_BLOCK
    block_expert = jnp.minimum(jnp.sum((block_start[:, None] >= pend[None, :]).astype(jnp.int32), axis=1), N_EXPERTS - 1)
    n_used = (pend[-1] // MOE_BLOCK).astype(jnp.int32).reshape(1)
    n_slots = n_blocks * MOE_BLOCK
    tail = jnp.stack([pend[-1], (n_slots - pend[-1]) // (MOE_BLOCK // 2)])
    pad = jnp.concatenate([pstart + counts, padded - counts, tail]).astype(jnp.int32)
    return slot, block_expert, n_used, pad, n_slots


def _scatter_kernel(slot_ref, pad_ref, h_ref, xb_ref, stage_ref, zero_ref, sems, zsem, *, tm, n_steps):
    i = pl.program_id(0)

    def pad_copies(fn):
        def rows_at(pos, count, size):
            def body(j, c):
                at = pos + j * size
                if size > 1:
                    at = pl.multiple_of(at, size)
                fn(pltpu.make_async_copy(zero_ref.at[pl.ds(0, size)], xb_ref.at[pl.ds(at, size)], zsem))
                return c

            lax.fori_loop(0, count, body, 0)

        def per_expert(e, carry):
            start = pad_ref[e]
            length = pad_ref[N_EXPERTS + e]
            head = jnp.minimum((-start) & 7, length)
            rows_at(start, head, 1)
            rows_at(start + head, (length - head) >> 3, 8)
            rows_at(start + head + ((length - head) & ~7), (length - head) & 7, 1)
            return carry

        lax.fori_loop(0, N_EXPERTS, per_expert, 0)
        rows_at(pad_ref[2 * N_EXPERTS], pad_ref[2 * N_EXPERTS + 1], zero_ref.shape[0])

    @pl.when(i == 0)
    def _():
        zero_ref[...] = jnp.zeros(zero_ref.shape, jnp.uint32)
        pad_copies(lambda c: c.start())
        pad_copies(lambda c: c.wait())

    def start_rows(buf):
        for r in range(tm):
            for choice in range(2):
                dst = slot_ref[choice * (n_steps * tm) + i * tm + r]
                pltpu.make_async_copy(stage_ref.at[buf, pl.ds(r, 1)], xb_ref.at[pl.ds(dst, 1)], sems.at[buf]).start(priority=choice)

    def wait_rows(buf):
        for _ in range(2):
            pltpu.make_async_copy(stage_ref.at[buf], xb_ref.at[pl.ds(0, tm)], sems.at[buf]).wait()

    for buf in range(2):
        @pl.when(i % 2 == buf)
        def _(buf=buf):
            @pl.when(i >= 2)
            def _():
                wait_rows(buf)

            stage_ref[buf] = h_ref[...]
            start_rows(buf)

            @pl.when(i == n_steps - 1)
            def _():
                @pl.when(i >= 1)
                def _():
                    wait_rows(1 - buf)

                wait_rows(buf)


def _scatter_rows(slot, pad, h2, n_slots, seq):
    t, dw = h2.shape
    tm = min(ROW_TILE, seq)
    n_steps = t // tm
    grid_spec = pltpu.PrefetchScalarGridSpec(
        num_scalar_prefetch=2,
        grid=(n_steps,),
        in_specs=[pl.BlockSpec((tm, dw), lambda i, sl, pd: (i, 0))],
        out_specs=pl.BlockSpec(memory_space=pl.ANY),
        scratch_shapes=[pltpu.VMEM((2, tm, dw), jnp.uint32), pltpu.VMEM((MOE_BLOCK // 2, dw), jnp.uint32),
                        pltpu.SemaphoreType.DMA((2,)), pltpu.SemaphoreType.DMA(())],
    )
    return pl.pallas_call(
        functools.partial(_scatter_kernel, tm=tm, n_steps=n_steps),
        grid_spec=grid_spec,
        out_shape=jax.ShapeDtypeStruct((n_slots, dw), jnp.uint32),
        compiler_params=_params(("arbitrary",)),
    )(slot, pad, h2)


def _moe_back_kernel(slot_ref, x_ref, route_ref, mod_ref, lng_ref, lnb_ref, yb_ref, o_ref, ybuf, sems,
                     *, tm, n_steps):
    i = pl.program_id(0)

    def start_rows(tile, buf):
        for r in range(tm):
            for choice in range(2):
                src = slot_ref[choice * (n_steps * tm) + tile * tm + r]
                pltpu.make_async_copy(yb_ref.at[pl.ds(src, 1)], ybuf.at[buf, choice, pl.ds(r, 1)], sems.at[buf]).start(priority=choice)

    def wait_rows(buf):
        for choice in range(2):
            pltpu.make_async_copy(yb_ref.at[pl.ds(0, tm)], ybuf.at[buf, choice], sems.at[buf]).wait()

    @pl.when(i == 0)
    def _():
        start_rows(0, 0)

    for buf in range(2):
        @pl.when(i % 2 == buf)
        def _(buf=buf):
            wait_rows(buf)
            start_rows(jnp.minimum(i + 1, n_steps - 1), 1 - buf)
            rt = route_ref[...]
            y0 = jnp.concatenate(_unpack_bf16_pairs(ybuf[buf, 0]), axis=1)
            y1 = jnp.concatenate(_unpack_bf16_pairs(ybuf[buf, 1]), axis=1)
            y = y0 * rt[:, 2:3] + y1 * rt[:, 3:4]
            z = DEEPNORM_ALPHA * x_ref[...] + (1.0 + mod_ref[5:6, :]) * y
            o_ref[...] = _layer_norm(z, lng_ref[...], lnb_ref[...])

            @pl.when(i == n_steps - 1)
            def _():
                wait_rows(1 - buf)


def _moe_back(slot, x1, yb, route, mod_l, ln_g, ln_b, seq):
    t, d = x1.shape
    tm = min(ROW_TILE, seq)
    tiles_per_seq = seq // tm
    n_steps = t // tm
    row = lambda n: pl.BlockSpec((tm, n), lambda i, sl: (i, 0))
    const = lambda shape: pl.BlockSpec(shape, lambda i, sl: (0,) * len(shape))
    grid_spec = pltpu.PrefetchScalarGridSpec(
        num_scalar_prefetch=1,
        grid=(n_steps,),
        in_specs=[row(d), row(LANES),
                  pl.BlockSpec((None, N_MOD, d), lambda i, sl: (i // tiles_per_seq, 0, 0)),
                  const(ln_g.shape), const(ln_b.shape), pl.BlockSpec(memory_space=pl.ANY)],
        out_specs=row(d),
        scratch_shapes=[pltpu.VMEM((2, 2, tm, d // 2), jnp.uint32), pltpu.SemaphoreType.DMA((2,))],
    )
    return pl.pallas_call(
        functools.partial(_moe_back_kernel, tm=tm, n_steps=n_steps),
        grid_spec=grid_spec,
        out_shape=jax.ShapeDtypeStruct((t, d), F32),
        compiler_params=_params(("arbitrary",)),
    )(slot, x1, route, mod_l, ln_g, ln_b, yb)


N_COMBINE_IN = 7


def _fused_combine(slot_ref, x1_ref, route_ref, modp_ref, lng_ref, lnb_ref, yb_ref, xo_ref, ybuf, sems, tm, n_steps):
    i = pl.program_id(0)
    buf = i % 2

    def start_rows(tile, into):
        for r in range(tm):
            for choice in range(2):
                src = slot_ref[choice * (n_steps * tm) + tile * tm + r]
                pltpu.make_async_copy(yb_ref.at[pl.ds(src, 1)], ybuf.at[into, choice, pl.ds(r, 1)],
                                      sems.at[into]).start(priority=choice)

    def wait_rows(into):
        for choice in range(2):
            pltpu.make_async_copy(yb_ref.at[pl.ds(0, tm)], ybuf.at[into, choice], sems.at[into]).wait()

    @pl.when(i == 0)
    def _():
        start_rows(0, 0)

    wait_rows(buf)
    rt = route_ref[...]
    y0 = jnp.concatenate(_unpack_bf16_pairs(ybuf[buf, 0]), axis=1)
    y1 = jnp.concatenate(_unpack_bf16_pairs(ybuf[buf, 1]), axis=1)
    y = y0 * rt[:, 2:3] + y1 * rt[:, 3:4]
    z = DEEPNORM_ALPHA * x1_ref[...] + (1.0 + modp_ref[5:6, :]) * y
    x = _layer_norm(z, lng_ref[...], lnb_ref[...])
    xo_ref[...] = x
    start_rows(jnp.minimum(i + 1, n_steps - 1), 1 - buf)

    def drain():
        @pl.when(i == n_steps - 1)
        def _():
            wait_rows(1 - buf)

    return x, drain


def _combine_specs(prev, tm, tiles_per_seq, d):
    slot, x1, yb, route, mod_prev, ln_g, ln_b = prev
    row = lambda n: pl.BlockSpec((tm, n), lambda i, *_: (i, 0))
    const = lambda shape: pl.BlockSpec(shape, lambda i, *_: (0,) * len(shape))
    specs = [row(d), row(LANES), pl.BlockSpec((None, N_MOD, d), lambda i, *_: (i // tiles_per_seq, 0, 0)),
             const(ln_g.shape), const(ln_b.shape), pl.BlockSpec(memory_space=pl.ANY)]
    return slot, [x1, route, mod_prev, ln_g, ln_b, yb], specs


def _odd_front_kernel(*refs, tm, n_steps):
    (mod_ref, w_ref, tab_ref, q_ref, k_ref, v_ref, gate_ref, xo_ref, ybuf, sems) = refs[N_COMBINE_IN:]
    x, drain = _fused_combine(*refs[:N_COMBINE_IN], xo_ref, ybuf, sems, tm, n_steps)
    hb = _bf(x * (1.0 + mod_ref[1:2, :]) + mod_ref[0:1, :])
    c2 = tab_ref[:, 0:LANES]
    s2 = tab_ref[:, LANES:2 * LANES]
    nqk = RET_HEADS * RET_DK
    nv = RET_HEADS * RET_DV

    def rope(b):
        return b * c2 + pltpu.roll(b, RET_DK // 2, 1) * s2

    qa = _dot(hb, w_ref[:, 0:nqk])
    ka = _dot(hb, w_ref[:, nqk:2 * nqk])
    for h in range(RET_HEADS):
        sl = slice(h * LANES, (h + 1) * LANES)
        q_ref[:, sl] = _bf(rope(qa[:, sl]) * (RET_DK ** -0.5))
        k_ref[:, sl] = _bf(rope(ka[:, sl]))
    v_ref[...] = _bf(_dot(hb, w_ref[:, 2 * nqk:2 * nqk + nv]))
    gate_ref[...] = _silu(_dot(hb, w_ref[:, 2 * nqk + nv:2 * nqk + 2 * nv]))
    drain()


def _odd_front(prev, mod_l, w_in, tabs, seq):
    t, d = prev[1].shape
    tm = min(ROW_TILE, seq)
    tiles_per_seq = seq // tm
    n_steps = t // tm
    nqk = RET_HEADS * RET_DK
    nv = RET_HEADS * RET_DV
    row = lambda n: pl.BlockSpec((tm, n), lambda i, *_: (i, 0))
    const = lambda shape: pl.BlockSpec(shape, lambda i, *_: (0,) * len(shape))
    slot, operands, specs = _combine_specs(prev, tm, tiles_per_seq, d)
    grid_spec = pltpu.PrefetchScalarGridSpec(
        num_scalar_prefetch=1,
        grid=(n_steps,),
        in_specs=specs + [pl.BlockSpec((None, N_MOD, d), lambda i, *_: (i // tiles_per_seq, 0, 0)),
                          const(w_in.shape), row(2 * LANES)],
        out_specs=(row(nqk), row(nqk), row(nv), row(nv), row(d)),
        scratch_shapes=[pltpu.VMEM((2, 2, tm, d // 2), jnp.uint32), pltpu.SemaphoreType.DMA((2,))],
    )
    return pl.pallas_call(
        functools.partial(_odd_front_kernel, tm=tm, n_steps=n_steps),
        grid_spec=grid_spec,
        out_shape=(jax.ShapeDtypeStruct((t, nqk), BF16), jax.ShapeDtypeStruct((t, nqk), BF16),
                   jax.ShapeDtypeStruct((t, nv), BF16), jax.ShapeDtypeStruct((t, nv), F32),
                   jax.ShapeDtypeStruct((t, d), F32)),
        compiler_params=_params(("arbitrary",)),
    )(slot, *operands, mod_l, w_in, tabs)


def _ret_kernel(q_ref, k_ref, v_ref, gate_ref, lg_ref, gng_ref, gnb_ref, o_ref, state_ref, decay_ref, *, C):
    heads = range(RET_HEADS)
    lg = [lg_ref[h][:, 0:1] for h in heads]

    @pl.when(pl.program_id(1) == 0)
    def _():
        state_ref[...] = jnp.zeros(state_ref.shape, F32)
        ri = lax.broadcasted_iota(jnp.int32, (C, C), 0)
        ci = lax.broadcasted_iota(jnp.int32, (C, C), 1)
        diff = (ri - ci).astype(F32)
        causal = ri >= ci
        for h in heads:
            decay_ref[h] = jnp.where(causal, jnp.exp(jnp.where(causal, diff, 0.0) * lg[h]), 0.0)

    idx = lax.broadcasted_iota(jnp.int32, (C, 1), 0).astype(F32)
    qs = [q_ref[:, h * RET_DK:(h + 1) * RET_DK] for h in heads]
    ks = [k_ref[:, h * RET_DK:(h + 1) * RET_DK] for h in heads]
    vs = [v_ref[:, h * RET_DV:(h + 1) * RET_DV] for h in heads]
    states = [state_ref[h] for h in heads]
    scores = [_dot_nt(qs[h], ks[h]) for h in heads]
    cross = [_dot(qs[h], _bf(states[h])) for h in heads]
    kd = [_bf(ks[h].astype(F32) * jnp.exp((C - 1.0 - idx) * lg[h])) for h in heads]
    upd = [_dot_tn(kd[h], vs[h]) for h in heads]
    inner = [_dot(_bf(scores[h] * decay_ref[h]), vs[h]) for h in heads]
    for h in heads:
        state_ref[h] = states[h] * jnp.exp(C * lg[h]) + upd[h]
        o = inner[h] + cross[h] * jnp.exp((idx + 1.0) * lg[h])
        sl = slice(h * RET_DV, (h + 1) * RET_DV)
        o_ref[:, sl] = _bf(gate_ref[:, sl] * _layer_norm(o, gng_ref[h], gnb_ref[h]))


def _retention(q, k, v, gate, lg, gn_g, gn_b, bsz, seq):
    t = q.shape[0]
    C = min(RET_CHUNK, seq)
    nc = seq // C
    blk = lambda n: pl.BlockSpec((C, n), lambda b, j: (b * nc + j, 0))
    return pl.pallas_call(
        functools.partial(_ret_kernel, C=C),
        grid=(bsz, nc),
        in_specs=[blk(RET_HEADS * RET_DK), blk(RET_HEADS * RET_DK), blk(RET_HEADS * RET_DV), blk(RET_HEADS * RET_DV),
                  _const_spec(lg.shape), _const_spec(gn_g.shape), _const_spec(gn_b.shape)],
        out_specs=blk(RET_HEADS * RET_DV),
        out_shape=jax.ShapeDtypeStruct((t, RET_HEADS * RET_DV), BF16),
        scratch_shapes=[pltpu.VMEM((RET_HEADS, RET_DK, RET_DV), F32), pltpu.VMEM((RET_HEADS, C, C), F32)],
        compiler_params=_params(("parallel", "arbitrary")),
    )(q, k, v, gate, lg, gn_g, gn_b)


def _relayout_hy_w_in(w):
    d = w.shape[0]
    z = lambda n: jnp.zeros((d, n), w.dtype)
    c_rope = MLA_Q_RANK + MLA_KV_RANK
    c_gdn = c_rope + MLA_ROPE
    c_a = c_gdn + GDN_QKV
    c_gate = c_a + 2 * GDN_HEADS
    return _bf(jnp.concatenate([
        w[:, 0:c_rope], z(MLA_NOPE), w[:, c_rope:c_gdn], z(LANES - MLA_NOPE - MLA_ROPE),
        w[:, c_gdn:c_a], w[:, c_gate:], w[:, c_a:c_gate], z(LANES - 2 * GDN_HEADS)], axis=1))


def _relayout_w_uq(w):
    r = w.shape[0]
    wh = w.reshape(r, MLA_HEADS, MLA_NOPE + MLA_ROPE)
    pad = jnp.zeros((r, MLA_HEADS, LANES - MLA_NOPE - MLA_ROPE), w.dtype)
    return _bf(jnp.concatenate([wh, pad], axis=-1).reshape(r, MLA_HEADS * LANES))


def _relayout_w_ukv(w):
    r = w.shape[0]
    wh = w.reshape(r, MLA_HEADS, MLA_NOPE + MLA_V)
    pad = jnp.zeros((r, MLA_HEADS, LANES - MLA_NOPE), w.dtype)
    wk = jnp.concatenate([wh[:, :, :MLA_NOPE], pad], axis=-1).reshape(r, MLA_HEADS * LANES)
    wv = wh[:, :, MLA_NOPE:].reshape(r, MLA_HEADS * MLA_V).T
    return _bf(wk), _bf(wv)


def _rope_angles(positions, dim):
    inv_freq = ROPE_BASE ** (-jnp.arange(0, dim, 2, dtype=F32) / dim)
    ang = positions.astype(F32).reshape(-1)[:, None] * inv_freq
    return jnp.cos(ang), jnp.sin(ang)


def _expand_lanes(compact, placement):
    return jnp.dot(compact, jnp.asarray(placement), precision=lax.Precision.HIGHEST)


def _mla_rope_tables(positions):
    cos, sin = _rope_angles(positions, MLA_ROPE)
    half = MLA_ROPE // 2
    place = np.zeros((2 * half, 3 * LANES), np.float32)
    for j in range(half):
        place[j, MLA_NOPE + j] = place[j, MLA_NOPE + half + j] = 1.0
        place[half + j, LANES + MLA_NOPE + j] = -1.0
        place[half + j, 2 * LANES + MLA_NOPE + half + j] = 1.0
    ones = np.zeros((1, 3 * LANES), np.float32)
    ones[0, :MLA_NOPE] = 1.0
    return _expand_lanes(jnp.concatenate([cos, sin], axis=1), place) + jnp.asarray(ones)


def _ret_rope_tables(positions):
    cos, sin = _rope_angles(positions, RET_DK)
    half = RET_DK // 2
    place = np.zeros((2 * half, 2 * LANES), np.float32)
    for j in range(half):
        place[j, j] = place[j, half + j] = 1.0
        place[half + j, LANES + j] = -1.0
        place[half + j, LANES + half + j] = 1.0
    return _expand_lanes(jnp.concatenate([cos, sin], axis=1), place)


def _router_weights(w_group, b_group, w_expert, b_expert):
    d = w_group.shape[0]
    pad = LANES - N_EXPERTS - N_GROUPS
    w = jnp.concatenate([w_expert, w_group, jnp.zeros((d, pad), F32)], axis=1)
    b = jnp.concatenate([b_expert, b_group, jnp.zeros((pad,), F32)]).reshape(1, LANES)
    hi = _bf(w)
    return hi, _bf(w - hi.astype(F32)), b


def _pad_lanes(v):
    return jnp.zeros((LANES,), F32).at[:v.shape[0]].set(v)


def kernel(x, c, positions, ada_w, ada_b, ln_mix_g, ln_mix_b, ln_ffn_g, ln_ffn_b, hy_w_in, mla_q_norm, mla_w_uq, mla_kv_norm, mla_w_ukv, gdn_conv_w, gdn_a_log, gdn_dt_bias, gdn_norm, hy_w_out, ret_w_in, ret_gn_g, ret_gn_b, ret_w_out, moe_w_group, moe_b_group, moe_w_expert, moe_b_expert, moe_w_gate, moe_w_up, moe_w_down):
    bsz, seq, d = x.shape
    t = bsz * seq
    mods = _ada_mod(c, ada_w, ada_b)
    mla_tabs = _mla_rope_tables(positions)
    ret_tabs = _ret_rope_tables(positions)
    log_gamma = jnp.log(1.0 - 2.0 ** (-5.0 - jnp.arange(RET_HEADS, dtype=F32)))
    lg = jnp.broadcast_to(log_gamma[:, None, None], (RET_HEADS, 1, LANES))
    xc = x.reshape(t, d)
    for layer in range(DEPTH):
        i = layer // 2
        mod_l = mods[layer]
        if layer % 2 == 0:
            wuk_p, wuv = _relayout_w_ukv(mla_w_ukv[i])
            gparam = jnp.stack([_pad_lanes(gdn_a_log[i]), _pad_lanes(gdn_dt_bias[i])])
            front = _even_front(
                xc if layer == 0 else pending, mod_l, _relayout_hy_w_in(hy_w_in[i]), mla_q_norm[i].reshape(1, -1), _relayout_w_uq(mla_w_uq[i]),
                mla_kv_norm[i].reshape(1, -1), wuk_p, wuv, gdn_conv_w[i], gparam, mla_tabs, seq)
            (qm, km, vm, gq, gk, gv, gate, gcol, grow) = front[:9]
            if layer > 0:
                xc = front[9]
            o_mla = _attention(qm, km, vm, bsz, seq)
            n_ch = t // GDN_CHUNK
            growp = grow[:GDN_HEADS].reshape(GDN_PAIRS, 2, n_ch, GDN_CHUNK).transpose(2, 0, 1, 3).reshape(n_ch, GDN_PAIRS, LANES)
            onorm2 = jnp.concatenate([gdn_norm[i], gdn_norm[i]]).reshape(1, LANES)
            y_gdn = _gdn(gq, gk, gv, gate, gcol, growp, onorm2, bsz, seq)
            acts = [o_mla, y_gdn]
            w_out = _bf(hy_w_out[i])
        else:
            q, k, v, gate, xc = _odd_front(pending, mod_l, _bf(ret_w_in[i]), ret_tabs, seq)
            og = _retention(q, k, v, gate, lg, ret_gn_g[i].reshape(RET_HEADS, 1, RET_DV),
                            ret_gn_b[i].reshape(RET_HEADS, 1, RET_DV), bsz, seq)
            acts = [og]
            w_out = _bf(ret_w_out[i])
        wr_hi, wr_lo, br = _router_weights(moe_w_group[layer], moe_b_group[layer], moe_w_expert[layer], moe_b_expert[layer])
        x1, h2, route, er, counts = _mixer_back(acts, w_out, xc, mod_l, ln_mix_g[layer].reshape(1, d), ln_mix_b[layer].reshape(1, d),
                                    wr_hi, wr_lo, br, seq)
        slot, block_expert, n_used, pad, n_slots = _slot_tables(er, counts, t)
        xb = _scatter_rows(slot, pad, h2, n_slots, seq)
        yb = _moe_experts(block_expert, n_used, xb, moe_w_gate, moe_w_up, moe_w_down, layer)
        pending = (slot, x1, yb, route, mod_l, ln_ffn_g[layer].reshape(1, d), ln_ffn_b[layer].reshape(1, d))
        if layer == DEPTH - 1:
            xc = _moe_back(*pending, seq)
    return xc.reshape(bsz, seq, d)
```

```python
import functools

import jax
import jax.numpy as jnp
import numpy as np
from jax import lax
from jax.experimental import pallas as pl
from jax.experimental.pallas import tpu as pltpu

F32 = jnp.float32
BF16 = jnp.bfloat16

DEPTH = 4
N_MOD = 6
LANES = 128

MLA_HEADS = 8
MLA_NOPE = 64
MLA_ROPE = 32
MLA_V = 64
MLA_Q_RANK = 256
MLA_KV_RANK = 128
MLA_SCALE = (MLA_NOPE + MLA_ROPE) ** -0.5
LOG2_E = 1.4426950408889634

GDN_HEADS = 8
GDN_DK = 64
GDN_DV = 64
GDN_CONV = 4
GDN_CHUNK = 64
GDN_HD = GDN_HEADS * GDN_DK
GDN_QKV = 3 * GDN_HD
GDN_PAIRS = GDN_HEADS // 2
GDN_GROUP = 4

RET_HEADS = 8
RET_DK = 128
RET_DV = 256
RET_CHUNK = 256

N_GROUPS = 4
EXPERTS_PER_GROUP = 8
N_EXPERTS = 32
EXPERT_FF = 512
MOE_BLOCK = 512

ROPE_BASE = 10000.0
LN_EPS = 1e-5
RMS_EPS = 1e-6
DEEPNORM_ALPHA = (2.0 * DEPTH) ** 0.25

ROW_TILE = 256
WIDE_ROW_TILE = 512
ATTN_TILE = 1024
VMEM_LIMIT = 56 * 1024 * 1024

_C_MLA = 0
_C_QKV = 512
_C_GATE = 2048
_C_AB = 2560
_C_END = 2688


def _bf(x):
    return x.astype(BF16)


def _dot(a, b):
    return jnp.dot(a, b, preferred_element_type=F32)


def _dot_nt(a, b):
    return lax.dot_general(a, b, (((1,), (1,)), ((), ())), preferred_element_type=F32)


def _dot_tn(a, b):
    return lax.dot_general(a, b, (((0,), (0,)), ((), ())), preferred_element_type=F32)


def _split2(x):
    hi = _bf(x)
    return hi, _bf(x - hi.astype(F32))


def _split3(x):
    hi = _bf(x)
    r = x - hi.astype(F32)
    mid = _bf(r)
    return hi, mid, _bf(r - mid.astype(F32))


def _silu(x):
    return x * jax.nn.sigmoid(x)


def _pack_bf16_pairs(x):
    n = x.shape[1] // 2
    bits = lambda v: lax.bitcast_convert_type(_bf(v).astype(F32), jnp.uint32)
    return (bits(x[:, n:]) & jnp.uint32(0xFFFF0000)) | (bits(x[:, :n]) >> 16)


def _unpack_bf16_pairs(w):
    lo = lax.bitcast_convert_type(w << 16, F32)
    hi = lax.bitcast_convert_type(w & jnp.uint32(0xFFFF0000), F32)
    return lo, hi


def _layer_norm(z, g, b):
    mu = jnp.mean(z, -1, keepdims=True)
    zc = z - mu
    var = jnp.mean(zc * zc, -1, keepdims=True)
    return zc * lax.rsqrt(var + LN_EPS) * g + b


def _rms_norm(x, g):
    return x * lax.rsqrt(jnp.mean(x * x, -1, keepdims=True) + RMS_EPS) * g


def _params(sem):
    return pltpu.CompilerParams(dimension_semantics=sem, vmem_limit_bytes=VMEM_LIMIT)


def _const_spec(shape):
    nd = len(shape)
    return pl.BlockSpec(shape, lambda *_: (0,) * nd)


def _ada_kernel(c_ref, w_ref, b_ref, o_ref):
    c = c_ref[...]
    o_ref[...] = _dot(_bf(_silu(c)), _bf(w_ref[...])) + b_ref[...]


def _ada_mod(c, ada_w, ada_b):
    depth, d, n = ada_w.shape
    bsz = c.shape[0]
    rows = 16
    tn = 1536
    cp = jnp.zeros((rows, d), F32).at[:bsz].set(c)
    out = pl.pallas_call(
        _ada_kernel,
        grid=(depth, n // tn),
        in_specs=[pl.BlockSpec((rows, d), lambda l, j: (0, 0)),
                  pl.BlockSpec((None, d, tn), lambda l, j: (l, 0, j)),
                  pl.BlockSpec((None, 1, tn), lambda l, j: (l, 0, j))],
        out_specs=pl.BlockSpec((None, rows, tn), lambda l, j: (l, 0, j)),
        out_shape=jax.ShapeDtypeStruct((depth, rows, n), F32),
        compiler_params=_params(("parallel", "parallel")),
    )(cp, ada_w, ada_b.reshape(depth, 1, n))
    return out[:, :bsz].reshape(depth, bsz, N_MOD, d)


def _even_front_kernel(*refs, fused, tm, tiles_per_seq, n_steps):
    n_head = N_COMBINE_IN if fused else 1
    (mod_ref, w_ref, qn_ref, wuq_ref, kvn_ref, wuk_ref, wuv_ref, cw_ref, gp_ref, tab_ref, ones_ref, tri_ref,
     q_ref, k_ref, v_ref, gq_ref, gk_ref, gv_ref, gate_ref, gcol_ref, grow_ref) = refs[n_head:n_head + 21]
    if fused:
        xo_ref, xbuf, ybuf, sems = refs[n_head + 21:]
        x, drain = _fused_combine(*refs[:n_head], xo_ref, ybuf, sems, tm, n_steps)
    else:
        (xbuf,) = refs[n_head + 21:]
        x = refs[0][...]
    i = pl.program_id(0)
    hb = _bf(x * (1.0 + mod_ref[1:2, :]) + mod_ref[0:1, :])

    pm = _dot(hb, w_ref[:, _C_MLA:_C_QKV])
    ct = tab_ref[:, 0:LANES]
    s1 = tab_ref[:, LANES:2 * LANES]
    s2 = tab_ref[:, 2 * LANES:3 * LANES]

    def rope(b):
        return b * ct + pltpu.roll(b, LANES - 16, 1) * s1 + pltpu.roll(b, 16, 1) * s2

    qa = _dot(_bf(_rms_norm(pm[:, 0:MLA_Q_RANK], qn_ref[...])), wuq_ref[...])
    for h in range(MLA_HEADS):
        sl = slice(h * LANES, (h + 1) * LANES)
        q_ref[:, sl] = _bf(rope(qa[:, sl]) * (MLA_SCALE * LOG2_E))
    kvn = _bf(_rms_norm(pm[:, MLA_Q_RANK:MLA_Q_RANK + MLA_KV_RANK], kvn_ref[...]))
    ka = _dot(kvn, wuk_ref[...])
    kr = rope(pm[:, MLA_Q_RANK + MLA_KV_RANK:_C_QKV])
    for h in range(MLA_HEADS):
        sl = slice(h * LANES, (h + 1) * LANES)
        k_ref[:, sl] = _bf(ka[:, sl] + kr)
    v_ref[...] = _bf(_dot_nt(wuv_ref[...], kvn))

    @pl.when(i % tiles_per_seq == 0)
    def _():
        xbuf[0:8, :] = jnp.zeros((8, GDN_QKV), F32)

    xbuf[8:8 + tm, :] = _dot(hb, w_ref[:, _C_QKV:_C_GATE])
    cw = cw_ref[...]
    y = cw[0:1, :] * xbuf[5:5 + tm, :]
    for t in range(1, GDN_CONV):
        y = y + cw[t:t + 1, :] * xbuf[5 + t:5 + t + tm, :]
    xbuf[0:8, :] = xbuf[tm:tm + 8, :]
    y = _silu(y)

    def head_sumsq(z):
        return _dot(_bf(z * z), ones_ref[...])

    qg = y[:, 0:GDN_HD]
    kg = y[:, GDN_HD:2 * GDN_HD]
    gq_ref[...] = qg * lax.rsqrt(head_sumsq(qg) + RMS_EPS) * (GDN_DK ** -0.5)
    gk_ref[...] = kg * lax.rsqrt(head_sumsq(kg) + RMS_EPS)
    gv_ref[...] = y[:, 2 * GDN_HD:3 * GDN_HD]
    gate_ref[...] = _dot(hb, w_ref[:, _C_GATE:_C_AB])

    ab = _dot(hb, w_ref[:, _C_AB:_C_END])
    lane = lax.broadcasted_iota(jnp.int32, (tm, LANES), 1)
    z = ab + gp_ref[1:2, :]
    softplus = jnp.maximum(z, 0.0) + jnp.log(1.0 + jnp.exp(-jnp.abs(z)))
    la = jnp.where(lane < GDN_HEADS, -jnp.exp(gp_ref[0:1, :]) * softplus, 0.0)
    tri = tri_ref[...]
    l0, l1, l2 = _split3(la)
    g = _dot(tri, l0) + _dot(tri, l1) + _dot(tri, l2)
    gcol = jnp.where(lane < GDN_HEADS, g, jax.nn.sigmoid(ab))
    gcol_ref[...] = gcol
    grow_ref[...] = gcol.T[0:16, :]
    if fused:
        drain()


def _even_front(src, mod_l, w_in_p, qn, wuq_p, kvn, wuk_p, wuv, conv_w, gparam, tabs, seq):
    fused = isinstance(src, tuple)
    t, d = (src[1] if fused else src).shape
    tm = min(ROW_TILE, seq)
    tiles_per_seq = seq // tm
    n_steps = t // tm
    row = lambda n: pl.BlockSpec((tm, n), lambda i, *_: (i, 0))
    idx = np.arange(GDN_HD) // GDN_DK
    head_ones = jnp.asarray(idx[:, None] == idx[None, :], BF16)
    pos = np.arange(tm)
    chunk_tri = jnp.asarray((pos[:, None] // GDN_CHUNK == pos[None, :] // GDN_CHUNK) & (pos[None, :] <= pos[:, None]), BF16)
    out_shapes = (
        jax.ShapeDtypeStruct((t, MLA_HEADS * LANES), BF16),
        jax.ShapeDtypeStruct((t, MLA_HEADS * LANES), BF16),
        jax.ShapeDtypeStruct((MLA_HEADS * MLA_V, t), BF16),
        jax.ShapeDtypeStruct((t, GDN_HD), F32),
        jax.ShapeDtypeStruct((t, GDN_HD), F32),
        jax.ShapeDtypeStruct((t, GDN_HD), F32),
        jax.ShapeDtypeStruct((t, GDN_HD), F32),
        jax.ShapeDtypeStruct((t, LANES), F32),
        jax.ShapeDtypeStruct((16, t), F32),
    )
    if fused:
        slot, head_operands, head_specs = _combine_specs(src, tm, tiles_per_seq, d)
        prefetch = [slot]
    else:
        head_operands, head_specs, prefetch = [src], [row(d)], []
    in_specs = head_specs + [
        pl.BlockSpec((None, N_MOD, d), lambda i, *_: (i // tiles_per_seq, 0, 0)),
        _const_spec(w_in_p.shape), _const_spec(qn.shape), _const_spec(wuq_p.shape),
        _const_spec(kvn.shape), _const_spec(wuk_p.shape), _const_spec(wuv.shape),
        _const_spec(conv_w.shape), _const_spec(gparam.shape), row(3 * LANES),
        _const_spec(head_ones.shape), _const_spec(chunk_tri.shape)]
    out_specs = (row(MLA_HEADS * LANES), row(MLA_HEADS * LANES),
                 pl.BlockSpec((MLA_HEADS * MLA_V, tm), lambda i, *_: (0, i)),
                 row(GDN_HD), row(GDN_HD), row(GDN_HD), row(GDN_HD), row(LANES),
                 pl.BlockSpec((16, tm), lambda i, *_: (0, i)))
    scratch = [pltpu.VMEM((tm + 8, GDN_QKV), F32)]
    if fused:
        out_specs += (row(d),)
        out_shapes += (jax.ShapeDtypeStruct((t, d), F32),)
        scratch += [pltpu.VMEM((2, 2, tm, d // 2), jnp.uint32), pltpu.SemaphoreType.DMA((2,))]
    grid_spec = pltpu.PrefetchScalarGridSpec(
        num_scalar_prefetch=len(prefetch), grid=(n_steps,), in_specs=in_specs, out_specs=out_specs,
        scratch_shapes=scratch)
    return pl.pallas_call(
        functools.partial(_even_front_kernel, fused=fused, tm=tm, tiles_per_seq=tiles_per_seq, n_steps=n_steps),
        grid_spec=grid_spec,
        out_shape=out_shapes,
        compiler_params=_params(("arbitrary",)),
    )(*prefetch, *head_operands, mod_l, w_in_p, qn, wuq_p, kvn, wuk_p, wuv, conv_w, gparam, tabs, head_ones, chunk_tri)


def _attn_kernel(qi_ref, ki_ref, q_ref, k_ref, vt_ref, o_ref, m_ref, l_ref, acc_ref, *, tq):
    qi = qi_ref[pl.program_id(1)]
    ki = ki_ref[pl.program_id(1)]

    @pl.when(ki == 0)
    def _():
        m_ref[...] = jnp.full(m_ref.shape, -jnp.inf, F32)
        l_ref[...] = jnp.zeros(l_ref.shape, F32)
        acc_ref[...] = jnp.zeros(acc_ref.shape, F32)

    def scores(h):
        sl = slice(h * LANES, (h + 1) * LANES)
        return _dot_nt(k_ref[:, sl], q_ref[:, sl])

    def step(diagonal):
        if diagonal:
            key = lax.broadcasted_iota(jnp.int32, (tq, tq), 0)
            qry = lax.broadcasted_iota(jnp.int32, (tq, tq), 1)
            keep = key <= qry
        s_next = scores(0)
        for h in range(MLA_HEADS):
            s = s_next
            if h + 1 < MLA_HEADS:
                s_next = scores(h + 1)
            if diagonal:
                s = jnp.where(keep, s, -jnp.inf)
            rows = slice(h * MLA_V, (h + 1) * MLA_V)
            m_prev = m_ref[h:h + 1, :]
            m_new = jnp.maximum(m_prev, jnp.max(s, 0, keepdims=True))
            alpha = jnp.exp2(m_prev - m_new)
            p = jnp.exp2(s - m_new)
            l_ref[h:h + 1, :] = alpha * l_ref[h:h + 1, :] + jnp.sum(p, 0, keepdims=True)
            acc_ref[rows, :] = alpha * acc_ref[rows, :] + _dot(vt_ref[rows, :], _bf(p))
            m_ref[h:h + 1, :] = m_new

    @pl.when(ki < qi)
    def _():
        step(False)

    @pl.when(ki == qi)
    def _():
        step(True)
        for h in range(MLA_HEADS):
            rows = slice(h * MLA_V, (h + 1) * MLA_V)
            acc_ref[rows, :] = acc_ref[rows, :] / l_ref[h:h + 1, :]
        o_ref[...] = _bf(acc_ref[...].T)


def _attention(qm, km, vm, bsz, seq):
    t = qm.shape[0]
    tq = min(ATTN_TILE, seq)
    nq = seq // tq
    pairs = [(q, k) for q in range(nq) for k in range(q + 1)]
    qi_tab = jnp.asarray(np.array([p[0] for p in pairs], np.int32))
    ki_tab = jnp.asarray(np.array([p[1] for p in pairs], np.int32))
    grid_spec = pltpu.PrefetchScalarGridSpec(
        num_scalar_prefetch=2,
        grid=(bsz, len(pairs)),
        in_specs=[pl.BlockSpec((tq, MLA_HEADS * LANES), lambda b, s, qt, kt: (b * nq + qt[s], 0)),
                  pl.BlockSpec((tq, MLA_HEADS * LANES), lambda b, s, qt, kt: (b * nq + kt[s], 0)),
                  pl.BlockSpec((MLA_HEADS * MLA_V, tq), lambda b, s, qt, kt: (0, b * nq + kt[s]))],
        out_specs=pl.BlockSpec((tq, MLA_HEADS * MLA_V), lambda b, s, qt, kt: (b * nq + qt[s], 0)),
        scratch_shapes=[pltpu.VMEM((MLA_HEADS, tq), F32),
                        pltpu.VMEM((MLA_HEADS, tq), F32),
                        pltpu.VMEM((MLA_HEADS * MLA_V, tq), F32)],
    )
    return pl.pallas_call(
        functools.partial(_attn_kernel, tq=tq),
        grid_spec=grid_spec,
        out_shape=jax.ShapeDtypeStruct((t, MLA_HEADS * MLA_V), BF16),
        compiler_params=_params(("parallel", "arbitrary")),
    )(qi_tab, ki_tab, qm, km, vm)


def _gdn_kernel(q_ref, k_ref, v_ref, gate_ref, gcol_ref, grow_ref, on_ref, y_ref, state_ref, *, n_chunks):
    C = GDN_CHUNK

    @pl.when(pl.program_id(1) == 0)
    def _():
        state_ref[...] = jnp.zeros(state_ref.shape, F32)

    lane = lax.broadcasted_iota(jnp.int32, (C, LANES), 1)
    rowi = lax.broadcasted_iota(jnp.int32, (C, LANES), 0)
    first = lane < GDN_DK
    col = jnp.where(first, lane, lane - GDN_DK)
    tril = col <= rowi
    strict = col < rowi
    eye = jnp.where(col == rowi, 1.0, 0.0)
    blk16 = (col // 16) == (rowi // 16)
    blk32 = (col // 32) == (rowi // 32)
    r2 = lax.broadcasted_iota(jnp.int32, (LANES, LANES), 0)
    c2 = lax.broadcasted_iota(jnp.int32, (LANES, LANES), 1)
    first_rows = r2 < GDN_DK
    head_ones = jnp.where((r2 < GDN_DK) == (c2 < GDN_DK), 1.0, 0.0).astype(BF16)

    def bd(y):
        return jnp.concatenate([jnp.where(first, y, 0), jnp.where(first, 0, y)], axis=0)

    def pmm(x, y):
        return _dot(_bf(x), bd(_bf(y)))

    def pmm_hi(x, y):
        xh, xl = _split2(x)
        yh, yl = _split2(y)
        top = _dot(jnp.concatenate([xh, xl], axis=0), bd(yh))
        return top[:C] + top[C:] + _dot(xh, bd(yl))

    def pair_cols(a, c0, c1):
        return jnp.where(first, a[:, c0:c0 + 1], a[:, c1:c1 + 1])

    def chunk_group(gi, carry):
        pairs = range(GDN_PAIRS)
        items = [(c, p) for c in range(GDN_GROUP) for p in pairs]
        sls = [slice(p * LANES, (p + 1) * LANES) for p in pairs]
        rows = [pl.ds(pl.multiple_of((gi * GDN_GROUP + c) * C, C), C) for c in range(GDN_GROUP)]
        gc = [gcol_ref[rows[c], :] for c in range(GDN_GROUP)]
        gr = [grow_ref[gi * GDN_GROUP + c] for c in range(GDN_GROUP)]
        every = range(len(items))
        q = [q_ref[rows[c], sls[p]] for c, p in items]
        k = [k_ref[rows[c], sls[p]] for c, p in items]
        v = [v_ref[rows[c], sls[p]] for c, p in items]
        g = [pair_cols(gc[c], 2 * p, 2 * p + 1) for c, p in items]
        beta = [pair_cols(gc[c], GDN_HEADS + 2 * p, GDN_HEADS + 2 * p + 1) for c, p in items]
        decay = [jnp.where(tril, jnp.exp(jnp.where(tril, g[i] - gr[c][p:p + 1, :], 0.0)), 0.0)
                 for i, (c, p) in enumerate(items)]
        kb = [k[i] * beta[i] for i in every]
        kbd = [bd(_bf(k[i])) for i in every]
        lmat = [jnp.where(strict, _dot_nt(_bf(kb[i]), kbd[i]) * decay[i], 0.0) for i in every]
        a_intra = [jnp.where(tril, _dot_nt(_bf(q[i]), kbd[i]) * decay[i], 0.0) for i in every]
        n1 = [jnp.where(blk16, lmat[i], 0.0) for i in every]
        x = [eye - n1[i] for i in every]
        nk = n1
        for mm in (pmm_hi, pmm, pmm):
            nk = [mm(nk[i], nk[i]) for i in every]
            x = [x[i] + mm(x[i], nk[i]) for i in every]
        for off in ([jnp.where(blk32 & jnp.logical_not(blk16), lmat[i], 0.0) for i in every],
                    [jnp.where(blk32, 0.0, lmat[i]) for i in every]):
            xb = [pmm(x[i], off[i]) for i in every]
            x = [x[i] - pmm(xb[i], x[i]) for i in every]
        u = [pmm(x[i], v[i] * beta[i]) for i in every]
        w = [pmm(x[i], kb[i] * jnp.exp(g[i])) for i in every]
        for c in range(GDN_GROUP):
            idx = [c * GDN_PAIRS + p for p in pairs]
            state = [state_ref[p] for p in pairs]
            sb = [_bf(state[p]) for p in pairs]
            v_new = [u[i] - _dot(_bf(w[i]), sb[p]) for p, i in enumerate(idx)]
            o = [_dot(_bf(q[i] * jnp.exp(g[i])), sb[p]) + pmm(a_intra[i], v_new[p]) for p, i in enumerate(idx)]
            for p, i in enumerate(idx):
                g_last = g[i][C - 1:C, :]
                k_dec = k[i] * jnp.exp(g_last - g[i])
                scale = jnp.exp(jnp.where(first_rows, g_last[:, 0:1], g_last[:, GDN_DK:GDN_DK + 1]))
                state_ref[p] = state[p] * scale + _dot_tn(bd(_bf(k_dec)), bd(_bf(v_new[p])))
            for p in pairs:
                ms = _dot(_bf(o[p] * o[p]), head_ones) * (1.0 / GDN_DV)
                on = o[p] * lax.rsqrt(ms + RMS_EPS) * on_ref[...]
                y_ref[rows[c], sls[p]] = _bf(on * _silu(gate_ref[rows[c], sls[p]]))
        return carry

    lax.fori_loop(0, n_chunks // GDN_GROUP, chunk_group, 0)


def _gdn(gq, gk, gv, gate, gcol, growp, onorm2, bsz, seq):
    t = gq.shape[0]
    r = min(WIDE_ROW_TILE, seq)
    nblk = seq // r
    n_chunks = r // GDN_CHUNK
    row = lambda n: pl.BlockSpec((r, n), lambda b, j: (b * nblk + j, 0))
    return pl.pallas_call(
        functools.partial(_gdn_kernel, n_chunks=n_chunks),
        grid=(bsz, nblk),
        in_specs=[row(GDN_HD), row(GDN_HD), row(GDN_HD), row(GDN_HD), row(LANES),
                  pl.BlockSpec((n_chunks, GDN_PAIRS, LANES), lambda b, j: (b * nblk + j, 0, 0)),
                  _const_spec(onorm2.shape)],
        out_specs=row(GDN_HD),
        out_shape=jax.ShapeDtypeStruct((t, GDN_HD), BF16),
        scratch_shapes=[pltpu.VMEM((GDN_PAIRS, LANES, LANES), F32)],
        compiler_params=_params(("parallel", "arbitrary")),
    )(gq, gk, gv, gate, gcol, growp, onorm2)


def _back_kernel(*refs, n_in):
    a_refs = refs[:n_in]
    (w_ref, x_ref, mod_ref, lng_ref, lnb_ref, wrh_ref, wrl_ref, br_ref, tri_ref,
     x1_ref, h2_ref, route_ref, er_ref, counts_ref, base_ref) = refs[n_in:]
    off = 0
    y = None
    for a_ref in a_refs:
        kk = a_ref.shape[1]
        part = _dot(a_ref[...], w_ref[off:off + kk, :])
        y = part if y is None else y + part
        off += kk
    z = DEEPNORM_ALPHA * x_ref[...] + (1.0 + mod_ref[2:3, :]) * y
    x1 = _layer_norm(z, lng_ref[...], lnb_ref[...])
    x1_ref[...] = x1
    h2 = x1 * (1.0 + mod_ref[4:5, :]) + mod_ref[3:4, :]
    h2_ref[...] = _pack_bf16_pairs(h2)

    hh, hl = _split2(h2)
    logits = _dot(hh, wrh_ref[...]) + _dot(hh, wrl_ref[...]) + _dot(hl, wrh_ref[...]) + br_ref[...]
    lane = lax.broadcasted_iota(jnp.int32, logits.shape, 1)
    lane_f = lane.astype(F32)
    neg = -jnp.inf
    gmask = (lane >= N_EXPERTS) & (lane < N_EXPERTS + N_GROUPS)
    gl = jnp.where(gmask, logits, neg)
    gmax = jnp.max(gl, -1, keepdims=True)
    g_top = 1.0 / jnp.sum(jnp.where(gmask, jnp.exp(gl - gmax), 0.0), -1, keepdims=True)
    g_lane = jnp.min(jnp.where(gl == gmax, lane_f, 1e9), -1, keepdims=True)
    g_idx = g_lane.astype(jnp.int32) - N_EXPERTS
    emask = (lane < N_EXPERTS) & ((lane // EXPERTS_PER_GROUP) == g_idx)
    el = jnp.where(emask, logits, neg)
    m1 = jnp.max(el, -1, keepdims=True)
    i1 = jnp.min(jnp.where(el == m1, lane_f, 1e9), -1, keepdims=True)
    el2 = jnp.where(lane_f == i1, neg, el)
    m2 = jnp.max(el2, -1, keepdims=True)
    i2 = jnp.min(jnp.where(el2 == m2, lane_f, 1e9), -1, keepdims=True)
    e2 = jnp.exp(m2 - m1)
    w1 = g_top / (1.0 + e2)
    w2 = g_top * e2 / (1.0 + e2)

    @pl.when(pl.program_id(0) == 0)
    def _():
        base_ref[...] = jnp.zeros(base_ref.shape, F32)

    tm = logits.shape[0]
    tri = tri_ref[...]
    oh1 = jnp.where(lane_f == i1, 1.0, 0.0)
    oh2 = jnp.where(lane_f == i2, 1.0, 0.0)
    cs1 = _dot(tri, _bf(oh1))
    cs2 = _dot(tri, _bf(oh2))
    base = base_ref[...]
    tot1 = cs1[tm - 1:tm, :]
    r1 = jnp.sum(oh1 * (base + cs1), -1, keepdims=True) - 1.0
    r2 = jnp.sum(oh2 * (base + tot1 + cs2), -1, keepdims=True) - 1.0
    total = base + tot1 + cs2[tm - 1:tm, :]
    base_ref[...] = total
    counts_ref[...] = total
    route = jnp.where(lane == 0, i1, jnp.where(lane == 1, i2, jnp.where(lane == 2, w1, jnp.where(lane == 3, w2, 0.0))))
    route = jnp.where(lane == 4, r1, jnp.where(lane == 5, r2, route))
    route_ref[...] = route
    er_ref[...] = route.T[0:8, :]


def _mixer_back(acts, w_out, x2, mod_l, ln_g, ln_b, wr_hi, wr_lo, br, seq):
    t, d = x2.shape
    tm = min(WIDE_ROW_TILE, seq)
    tiles_per_seq = seq // tm
    row = lambda n: pl.BlockSpec((tm, n), lambda i: (i, 0))
    pos = np.arange(tm)
    tri = jnp.asarray(pos[None, :] <= pos[:, None], BF16)
    return pl.pallas_call(
        functools.partial(_back_kernel, n_in=len(acts)),
        grid=(t // tm,),
        in_specs=[row(a.shape[1]) for a in acts] + [
            _const_spec(w_out.shape), row(d),
            pl.BlockSpec((None, N_MOD, d), lambda i: (i // tiles_per_seq, 0, 0)),
            _const_spec(ln_g.shape), _const_spec(ln_b.shape),
            _const_spec(wr_hi.shape), _const_spec(wr_lo.shape), _const_spec(br.shape), _const_spec(tri.shape)],
        out_specs=(row(d), row(d // 2), row(LANES), pl.BlockSpec((8, tm), lambda i: (0, i)), _const_spec((1, LANES))),
        out_shape=(jax.ShapeDtypeStruct((t, d), F32), jax.ShapeDtypeStruct((t, d // 2), jnp.uint32),
                   jax.ShapeDtypeStruct((t, LANES), F32), jax.ShapeDtypeStruct((8, t), F32),
                   jax.ShapeDtypeStruct((1, LANES), F32)),
        scratch_shapes=[pltpu.VMEM((1, LANES), F32)],
        compiler_params=_params(("arbitrary",)),
    )(*acts, w_out, x2, mod_l, ln_g, ln_b, wr_hi, wr_lo, br, tri)


def _moe_kernel(be_ref, nb_ref, x_ref, wg_ref, wu_ref, wd_ref, y_ref, wgu_s, wd_s):
    i = pl.program_id(0)
    ff = wd_s.shape[0]

    @pl.when(i < nb_ref[0])
    def _():
        prev = be_ref[jnp.maximum(i - 1, 0)]

        @pl.when((i == 0) | (be_ref[i] != prev))
        def _():
            wgu_s[:, 0:ff] = _bf(wg_ref[...])
            wgu_s[:, ff:2 * ff] = _bf(wu_ref[...])
            wd_s[...] = _bf(wd_ref[...])

        lo, hi = _unpack_bf16_pairs(x_ref[...])
        x_lo, x_hi = _bf(lo), _bf(hi)
        half = x_lo.shape[1]
        gu = _dot(x_lo, wgu_s[0:half, :]) + _dot(x_hi, wgu_s[half:, :])
        y_ref[...] = _pack_bf16_pairs(_dot(_bf(_silu(gu[:, 0:ff]) * gu[:, ff:2 * ff]), wd_s[...]))

    @pl.when(i >= nb_ref[0])
    def _():
        y_ref[...] = jnp.zeros(y_ref.shape, jnp.uint32)


def _moe_experts(block_expert, n_used, xb, w_gate, w_up, w_down, layer):
    n_slots = xb.shape[0]
    d, ff = w_gate.shape[-2:]
    n_blocks = n_slots // MOE_BLOCK
    grid_spec = pltpu.PrefetchScalarGridSpec(
        num_scalar_prefetch=2,
        grid=(n_blocks,),
        in_specs=[pl.BlockSpec((MOE_BLOCK, d // 2), lambda i, be, nb: (jnp.minimum(i, nb[0] - 1), 0)),
                  pl.BlockSpec((None, None, d, ff), lambda i, be, nb: (layer, be[i], 0, 0)),
                  pl.BlockSpec((None, None, d, ff), lambda i, be, nb: (layer, be[i], 0, 0)),
                  pl.BlockSpec((None, None, ff, d), lambda i, be, nb: (layer, be[i], 0, 0))],
        out_specs=pl.BlockSpec((MOE_BLOCK, d // 2), lambda i, be, nb: (i, 0)),
        scratch_shapes=[pltpu.VMEM((d, 2 * ff), BF16), pltpu.VMEM((ff, d), BF16)],
    )
    return pl.pallas_call(
        _moe_kernel,
        grid_spec=grid_spec,
        out_shape=jax.ShapeDtypeStruct((n_slots, d // 2), jnp.uint32),
        compiler_params=_params(("arbitrary",)),
    )(block_expert, n_used, xb, w_gate, w_up, w_down)


def _slot_tables(er, counts_row, t):
    counts = counts_row[0, :N_EXPERTS].astype(jnp.int32)
    padded = (counts + MOE_BLOCK - 1) // MOE_BLOCK * MOE_BLOCK
    pend = jnp.cumsum(padded)
    pstart = pend - padded
    e = er[0:2].astype(jnp.int32)
    starts = jnp.sum(jnp.where(e[:, :, None] == jnp.arange(N_EXPERTS, dtype=jnp.int32), pstart, 0), axis=-1)
    slot = (starts + er[4:6].astype(jnp.int32)).reshape(2 * t)
    n_blocks = 2 * t // MOE_BLOCK + N_EXPERTS
    block_start = jnp.arange(n_blocks, dtype=jnp.int32) * MOE_BLOCK
    block_expert = jnp.minimum(jnp.sum((block_start[:, None] >= pend[None, :]).astype(jnp.int32), axis=1), N_EXPERTS - 1)
    n_used = (pend[-1] // MOE_BLOCK).astype(jnp.int32).reshape(1)
    n_slots = n_blocks * MOE_BLOCK
    tail = jnp.stack([pend[-1], (n_slots - pend[-1]) // (MOE_BLOCK // 2)])
    pad = jnp.concatenate([pstart + counts, padded - counts, tail]).astype(jnp.int32)
    return slot, block_expert, n_used, pad, n_slots


def _scatter_kernel(slot_ref, pad_ref, h_ref, xb_ref, stage_ref, zero_ref, sems, zsem, *, tm, n_steps):
    i = pl.program_id(0)

    def pad_copies(fn):
        def rows_at(pos, count, size):
            def body(j, c):
                at = pos + j * size
                if size > 1:
                    at = pl.multiple_of(at, size)
                fn(pltpu.make_async_copy(zero_ref.at[pl.ds(0, size)], xb_ref.at[pl.ds(at, size)], zsem))
                return c

            lax.fori_loop(0, count, body, 0)

        def per_expert(e, carry):
            start = pad_ref[e]
            length = pad_ref[N_EXPERTS + e]
            head = jnp.minimum((-start) & 7, length)
            rows_at(start, head, 1)
            rows_at(start + head, (length - head) >> 3, 8)
            rows_at(start + head + ((length - head) & ~7), (length - head) & 7, 1)
            return carry

        lax.fori_loop(0, N_EXPERTS, per_expert, 0)
        rows_at(pad_ref[2 * N_EXPERTS], pad_ref[2 * N_EXPERTS + 1], zero_ref.shape[0])

    @pl.when(i == 0)
    def _():
        zero_ref[...] = jnp.zeros(zero_ref.shape, jnp.uint32)
        pad_copies(lambda c: c.start())
        pad_copies(lambda c: c.wait())

    def start_rows(buf):
        for r in range(tm):
            for choice in range(2):
                dst = slot_ref[choice * (n_steps * tm) + i * tm + r]
                pltpu.make_async_copy(stage_ref.at[buf, pl.ds(r, 1)], xb_ref.at[pl.ds(dst, 1)], sems.at[buf]).start(priority=choice)

    def wait_rows(buf):
        for _ in range(2):
            pltpu.make_async_copy(stage_ref.at[buf], xb_ref.at[pl.ds(0, tm)], sems.at[buf]).wait()

    for buf in range(2):
        @pl.when(i % 2 == buf)
        def _(buf=buf):
            @pl.when(i >= 2)
            def _():
                wait_rows(buf)

            stage_ref[buf] = h_ref[...]
            start_rows(buf)

            @pl.when(i == n_steps - 1)
            def _():
                @pl.when(i >= 1)
                def _():
                    wait_rows(1 - buf)

                wait_rows(buf)


def _scatter_rows(slot, pad, h2, n_slots, seq):
    t, dw = h2.shape
    tm = min(ROW_TILE, seq)
    n_steps = t // tm
    grid_spec = pltpu.PrefetchScalarGridSpec(
        num_scalar_prefetch=2,
        grid=(n_steps,),
        in_specs=[pl.BlockSpec((tm, dw), lambda i, sl, pd: (i, 0))],
        out_specs=pl.BlockSpec(memory_space=pl.ANY),
        scratch_shapes=[pltpu.VMEM((2, tm, dw), jnp.uint32), pltpu.VMEM((MOE_BLOCK // 2, dw), jnp.uint32),
                        pltpu.SemaphoreType.DMA((2,)), pltpu.SemaphoreType.DMA(())],
    )
    return pl.pallas_call(
        functools.partial(_scatter_kernel, tm=tm, n_steps=n_steps),
        grid_spec=grid_spec,
        out_shape=jax.ShapeDtypeStruct((n_slots, dw), jnp.uint32),
        compiler_params=_params(("arbitrary",)),
    )(slot, pad, h2)


def _moe_back_kernel(slot_ref, x_ref, route_ref, mod_ref, lng_ref, lnb_ref, yb_ref, o_ref, ybuf, sems,
                     *, tm, n_steps):
    i = pl.program_id(0)

    def start_rows(tile, buf):
        for r in range(tm):
            for choice in range(2):
                src = slot_ref[choice * (n_steps * tm) + tile * tm + r]
                pltpu.make_async_copy(yb_ref.at[pl.ds(src, 1)], ybuf.at[buf, choice, pl.ds(r, 1)], sems.at[buf]).start(priority=choice)

    def wait_rows(buf):
        for choice in range(2):
            pltpu.make_async_copy(yb_ref.at[pl.ds(0, tm)], ybuf.at[buf, choice], sems.at[buf]).wait()

    @pl.when(i == 0)
    def _():
        start_rows(0, 0)

    for buf in range(2):
        @pl.when(i % 2 == buf)
        def _(buf=buf):
            wait_rows(buf)
            start_rows(jnp.minimum(i + 1, n_steps - 1), 1 - buf)
            rt = route_ref[...]
            y0 = jnp.concatenate(_unpack_bf16_pairs(ybuf[buf, 0]), axis=1)
            y1 = jnp.concatenate(_unpack_bf16_pairs(ybuf[buf, 1]), axis=1)
            y = y0 * rt[:, 2:3] + y1 * rt[:, 3:4]
            z = DEEPNORM_ALPHA * x_ref[...] + (1.0 + mod_ref[5:6, :]) * y
            o_ref[...] = _layer_norm(z, lng_ref[...], lnb_ref[...])

            @pl.when(i == n_steps - 1)
            def _():
                wait_rows(1 - buf)


def _moe_back(slot, x1, yb, route, mod_l, ln_g, ln_b, seq):
    t, d = x1.shape
    tm = min(ROW_TILE, seq)
    tiles_per_seq = seq // tm
    n_steps = t // tm
    row = lambda n: pl.BlockSpec((tm, n), lambda i, sl: (i, 0))
    const = lambda shape: pl.BlockSpec(shape, lambda i, sl: (0,) * len(shape))
    grid_spec = pltpu.PrefetchScalarGridSpec(
        num_scalar_prefetch=1,
        grid=(n_steps,),
        in_specs=[row(d), row(LANES),
                  pl.BlockSpec((None, N_MOD, d), lambda i, sl: (i // tiles_per_seq, 0, 0)),
                  const(ln_g.shape), const(ln_b.shape), pl.BlockSpec(memory_space=pl.ANY)],
        out_specs=row(d),
        scratch_shapes=[pltpu.VMEM((2, 2, tm, d // 2), jnp.uint32), pltpu.SemaphoreType.DMA((2,))],
    )
    return pl.pallas_call(
        functools.partial(_moe_back_kernel, tm=tm, n_steps=n_steps),
        grid_spec=grid_spec,
        out_shape=jax.ShapeDtypeStruct((t, d), F32),
        compiler_params=_params(("arbitrary",)),
    )(slot, x1, route, mod_l, ln_g, ln_b, yb)


N_COMBINE_IN = 7


def _fused_combine(slot_ref, x1_ref, route_ref, modp_ref, lng_ref, lnb_ref, yb_ref, xo_ref, ybuf, sems, tm, n_steps):
    i = pl.program_id(0)
    buf = i % 2

    def start_rows(tile, into):
        for r in range(tm):
            for choice in range(2):
                src = slot_ref[choice * (n_steps * tm) + tile * tm + r]
                pltpu.make_async_copy(yb_ref.at[pl.ds(src, 1)], ybuf.at[into, choice, pl.ds(r, 1)],
                                      sems.at[into]).start(priority=choice)

    def wait_rows(into):
        for choice in range(2):
            pltpu.make_async_copy(yb_ref.at[pl.ds(0, tm)], ybuf.at[into, choice], sems.at[into]).wait()

    @pl.when(i == 0)
    def _():
        start_rows(0, 0)

    wait_rows(buf)
    rt = route_ref[...]
    y0 = jnp.concatenate(_unpack_bf16_pairs(ybuf[buf, 0]), axis=1)
    y1 = jnp.concatenate(_unpack_bf16_pairs(ybuf[buf, 1]), axis=1)
    y = y0 * rt[:, 2:3] + y1 * rt[:, 3:4]
    z = DEEPNORM_ALPHA * x1_ref[...] + (1.0 + modp_ref[5:6, :]) * y
    x = _layer_norm(z, lng_ref[...], lnb_ref[...])
    xo_ref[...] = x
    start_rows(jnp.minimum(i + 1, n_steps - 1), 1 - buf)

    def drain():
        @pl.when(i == n_steps - 1)
        def _():
            wait_rows(1 - buf)

    return x, drain


def _combine_specs(prev, tm, tiles_per_seq, d):
    slot, x1, yb, route, mod_prev, ln_g, ln_b = prev
    row = lambda n: pl.BlockSpec((tm, n), lambda i, *_: (i, 0))
    const = lambda shape: pl.BlockSpec(shape, lambda i, *_: (0,) * len(shape))
    specs = [row(d), row(LANES), pl.BlockSpec((None, N_MOD, d), lambda i, *_: (i // tiles_per_seq, 0, 0)),
             const(ln_g.shape), const(ln_b.shape), pl.BlockSpec(memory_space=pl.ANY)]
    return slot, [x1, route, mod_prev, ln_g, ln_b, yb], specs


def _odd_front_kernel(*refs, tm, n_steps):
    (mod_ref, w_ref, tab_ref, q_ref, k_ref, v_ref, gate_ref, xo_ref, ybuf, sems) = refs[N_COMBINE_IN:]
    x, drain = _fused_combine(*refs[:N_COMBINE_IN], xo_ref, ybuf, sems, tm, n_steps)
    hb = _bf(x * (1.0 + mod_ref[1:2, :]) + mod_ref[0:1, :])
    c2 = tab_ref[:, 0:LANES]
    s2 = tab_ref[:, LANES:2 * LANES]
    nqk = RET_HEADS * RET_DK
    nv = RET_HEADS * RET_DV

    def rope(b):
        return b * c2 + pltpu.roll(b, RET_DK // 2, 1) * s2

    qa = _dot(hb, w_ref[:, 0:nqk])
    ka = _dot(hb, w_ref[:, nqk:2 * nqk])
    for h in range(RET_HEADS):
        sl = slice(h * LANES, (h + 1) * LANES)
        q_ref[:, sl] = _bf(rope(qa[:, sl]) * (RET_DK ** -0.5))
        k_ref[:, sl] = _bf(rope(ka[:, sl]))
    v_ref[...] = _bf(_dot(hb, w_ref[:, 2 * nqk:2 * nqk + nv]))
    gate_ref[...] = _silu(_dot(hb, w_ref[:, 2 * nqk + nv:2 * nqk + 2 * nv]))
    drain()


def _odd_front(prev, mod_l, w_in, tabs, seq):
    t, d = prev[1].shape
    tm = min(ROW_TILE, seq)
    tiles_per_seq = seq // tm
    n_steps = t // tm
    nqk = RET_HEADS * RET_DK
    nv = RET_HEADS * RET_DV
    row = lambda n: pl.BlockSpec((tm, n), lambda i, *_: (i, 0))
    const = lambda shape: pl.BlockSpec(shape, lambda i, *_: (0,) * len(shape))
    slot, operands, specs = _combine_specs(prev, tm, tiles_per_seq, d)
    grid_spec = pltpu.PrefetchScalarGridSpec(
        num_scalar_prefetch=1,
        grid=(n_steps,),
        in_specs=specs + [pl.BlockSpec((None, N_MOD, d), lambda i, *_: (i // tiles_per_seq, 0, 0)),
                          const(w_in.shape), row(2 * LANES)],
        out_specs=(row(nqk), row(nqk), row(nv), row(nv), row(d)),
        scratch_shapes=[pltpu.VMEM((2, 2, tm, d // 2), jnp.uint32), pltpu.SemaphoreType.DMA((2,))],
    )
    return pl.pallas_call(
        functools.partial(_odd_front_kernel, tm=tm, n_steps=n_steps),
        grid_spec=grid_spec,
        out_shape=(jax.ShapeDtypeStruct((t, nqk), BF16), jax.ShapeDtypeStruct((t, nqk), BF16),
                   jax.ShapeDtypeStruct((t, nv), BF16), jax.ShapeDtypeStruct((t, nv), F32),
                   jax.ShapeDtypeStruct((t, d), F32)),
        compiler_params=_params(("arbitrary",)),
    )(slot, *operands, mod_l, w_in, tabs)


def _ret_kernel(q_ref, k_ref, v_ref, gate_ref, lg_ref, gng_ref, gnb_ref, o_ref, state_ref, decay_ref, *, C):
    heads = range(RET_HEADS)
    lg = [lg_ref[h][:, 0:1] for h in heads]

    @pl.when(pl.program_id(1) == 0)
    def _():
        state_ref[...] = jnp.zeros(state_ref.shape, F32)
        ri = lax.broadcasted_iota(jnp.int32, (C, C), 0)
        ci = lax.broadcasted_iota(jnp.int32, (C, C), 1)
        diff = (ri - ci).astype(F32)
        causal = ri >= ci
        for h in heads:
            decay_ref[h] = jnp.where(causal, jnp.exp(jnp.where(causal, diff, 0.0) * lg[h]), 0.0)

    idx = lax.broadcasted_iota(jnp.int32, (C, 1), 0).astype(F32)
    qs = [q_ref[:, h * RET_DK:(h + 1) * RET_DK] for h in heads]
    ks = [k_ref[:, h * RET_DK:(h + 1) * RET_DK] for h in heads]
    vs = [v_ref[:, h * RET_DV:(h + 1) * RET_DV] for h in heads]
    states = [state_ref[h] for h in heads]
    scores = [_dot_nt(qs[h], ks[h]) for h in heads]
    cross = [_dot(qs[h], _bf(states[h])) for h in heads]
    kd = [_bf(ks[h].astype(F32) * jnp.exp((C - 1.0 - idx) * lg[h])) for h in heads]
    upd = [_dot_tn(kd[h], vs[h]) for h in heads]
    inner = [_dot(_bf(scores[h] * decay_ref[h]), vs[h]) for h in heads]
    for h in heads:
        state_ref[h] = states[h] * jnp.exp(C * lg[h]) + upd[h]
        o = inner[h] + cross[h] * jnp.exp((idx + 1.0) * lg[h])
        sl = slice(h * RET_DV, (h + 1) * RET_DV)
        o_ref[:, sl] = _bf(gate_ref[:, sl] * _layer_norm(o, gng_ref[h], gnb_ref[h]))


def _retention(q, k, v, gate, lg, gn_g, gn_b, bsz, seq):
    t = q.shape[0]
    C = min(RET_CHUNK, seq)
    nc = seq // C
    blk = lambda n: pl.BlockSpec((C, n), lambda b, j: (b * nc + j, 0))
    return pl.pallas_call(
        functools.partial(_ret_kernel, C=C),
        grid=(bsz, nc),
        in_specs=[blk(RET_HEADS * RET_DK), blk(RET_HEADS * RET_DK), blk(RET_HEADS * RET_DV), blk(RET_HEADS * RET_DV),
                  _const_spec(lg.shape), _const_spec(gn_g.shape), _const_spec(gn_b.shape)],
        out_specs=blk(RET_HEADS * RET_DV),
        out_shape=jax.ShapeDtypeStruct((t, RET_HEADS * RET_DV), BF16),
        scratch_shapes=[pltpu.VMEM((RET_HEADS, RET_DK, RET_DV), F32), pltpu.VMEM((RET_HEADS, C, C), F32)],
        compiler_params=_params(("parallel", "arbitrary")),
    )(q, k, v, gate, lg, gn_g, gn_b)


def _relayout_hy_w_in(w):
    d = w.shape[0]
    z = lambda n: jnp.zeros((d, n), w.dtype)
    c_rope = MLA_Q_RANK + MLA_KV_RANK
    c_gdn = c_rope + MLA_ROPE
    c_a = c_gdn + GDN_QKV
    c_gate = c_a + 2 * GDN_HEADS
    return _bf(jnp.concatenate([
        w[:, 0:c_rope], z(MLA_NOPE), w[:, c_rope:c_gdn], z(LANES - MLA_NOPE - MLA_ROPE),
        w[:, c_gdn:c_a], w[:, c_gate:], w[:, c_a:c_gate], z(LANES - 2 * GDN_HEADS)], axis=1))


def _relayout_w_uq(w):
    r = w.shape[0]
    wh = w.reshape(r, MLA_HEADS, MLA_NOPE + MLA_ROPE)
    pad = jnp.zeros((r, MLA_HEADS, LANES - MLA_NOPE - MLA_ROPE), w.dtype)
    return _bf(jnp.concatenate([wh, pad], axis=-1).reshape(r, MLA_HEADS * LANES))


def _relayout_w_ukv(w):
    r = w.shape[0]
    wh = w.reshape(r, MLA_HEADS, MLA_NOPE + MLA_V)
    pad = jnp.zeros((r, MLA_HEADS, LANES - MLA_NOPE), w.dtype)
    wk = jnp.concatenate([wh[:, :, :MLA_NOPE], pad], axis=-1).reshape(r, MLA_HEADS * LANES)
    wv = wh[:, :, MLA_NOPE:].reshape(r, MLA_HEADS * MLA_V).T
    return _bf(wk), _bf(wv)


def _rope_angles(positions, dim):
    inv_freq = ROPE_BASE ** (-jnp.arange(0, dim, 2, dtype=F32) / dim)
    ang = positions.astype(F32).reshape(-1)[:, None] * inv_freq
    return jnp.cos(ang), jnp.sin(ang)


def _expand_lanes(compact, placement):
    return jnp.dot(compact, jnp.asarray(placement), precision=lax.Precision.HIGHEST)


def _mla_rope_tables(positions):
    cos, sin = _rope_angles(positions, MLA_ROPE)
    half = MLA_ROPE // 2
    place = np.zeros((2 * half, 3 * LANES), np.float32)
    for j in range(half):
        place[j, MLA_NOPE + j] = place[j, MLA_NOPE + half + j] = 1.0
        place[half + j, LANES + MLA_NOPE + j] = -1.0
        place[half + j, 2 * LANES + MLA_NOPE + half + j] = 1.0
    ones = np.zeros((1, 3 * LANES), np.float32)
    ones[0, :MLA_NOPE] = 1.0
    return _expand_lanes(jnp.concatenate([cos, sin], axis=1), place) + jnp.asarray(ones)


def _ret_rope_tables(positions):
    cos, sin = _rope_angles(positions, RET_DK)
    half = RET_DK // 2
    place = np.zeros((2 * half, 2 * LANES), np.float32)
    for j in range(half):
        place[j, j] = place[j, half + j] = 1.0
        place[half + j, LANES + j] = -1.0
        place[half + j, LANES + half + j] = 1.0
    return _expand_lanes(jnp.concatenate([cos, sin], axis=1), place)


def _router_weights(w_group, b_group, w_expert, b_expert):
    d = w_group.shape[0]
    pad = LANES - N_EXPERTS - N_GROUPS
    w = jnp.concatenate([w_expert, w_group, jnp.zeros((d, pad), F32)], axis=1)
    b = jnp.concatenate([b_expert, b_group, jnp.zeros((pad,), F32)]).reshape(1, LANES)
    hi = _bf(w)
    return hi, _bf(w - hi.astype(F32)), b


def _pad_lanes(v):
    return jnp.zeros((LANES,), F32).at[:v.shape[0]].set(v)


def kernel(x, c, positions, ada_w, ada_b, ln_mix_g, ln_mix_b, ln_ffn_g, ln_ffn_b, hy_w_in, mla_q_norm, mla_w_uq, mla_kv_norm, mla_w_ukv, gdn_conv_w, gdn_a_log, gdn_dt_bias, gdn_norm, hy_w_out, ret_w_in, ret_gn_g, ret_gn_b, ret_w_out, moe_w_group, moe_b_group, moe_w_expert, moe_b_expert, moe_w_gate, moe_w_up, moe_w_down):
    bsz, seq, d = x.shape
    t = bsz * seq
    mods = _ada_mod(c, ada_w, ada_b)
    mla_tabs = _mla_rope_tables(positions)
    ret_tabs = _ret_rope_tables(positions)
    log_gamma = jnp.log(1.0 - 2.0 ** (-5.0 - jnp.arange(RET_HEADS, dtype=F32)))
    lg = jnp.broadcast_to(log_gamma[:, None, None], (RET_HEADS, 1, LANES))
    xc = x.reshape(t, d)
    for layer in range(DEPTH):
        i = layer // 2
        mod_l = mods[layer]
        if layer % 2 == 0:
            wuk_p, wuv = _relayout_w_ukv(mla_w_ukv[i])
            gparam = jnp.stack([_pad_lanes(gdn_a_log[i]), _pad_lanes(gdn_dt_bias[i])])
            front = _even_front(
                xc if layer == 0 else pending, mod_l, _relayout_hy_w_in(hy_w_in[i]), mla_q_norm[i].reshape(1, -1), _relayout_w_uq(mla_w_uq[i]),
                mla_kv_norm[i].reshape(1, -1), wuk_p, wuv, gdn_conv_w[i], gparam, mla_tabs, seq)
            (qm, km, vm, gq, gk, gv, gate, gcol, grow) = front[:9]
            if layer > 0:
                xc = front[9]
            o_mla = _attention(qm, km, vm, bsz, seq)
            n_ch = t // GDN_CHUNK
            growp = grow[:GDN_HEADS].reshape(GDN_PAIRS, 2, n_ch, GDN_CHUNK).transpose(2, 0, 1, 3).reshape(n_ch, GDN_PAIRS, LANES)
            onorm2 = jnp.concatenate([gdn_norm[i], gdn_norm[i]]).reshape(1, LANES)
            y_gdn = _gdn(gq, gk, gv, gate, gcol, growp, onorm2, bsz, seq)
            acts = [o_mla, y_gdn]
            w_out = _bf(hy_w_out[i])
        else:
            q, k, v, gate, xc = _odd_front(pending, mod_l, _bf(ret_w_in[i]), ret_tabs, seq)
            og = _retention(q, k, v, gate, lg, ret_gn_g[i].reshape(RET_HEADS, 1, RET_DV),
                            ret_gn_b[i].reshape(RET_HEADS, 1, RET_DV), bsz, seq)
            acts = [og]
            w_out = _bf(ret_w_out[i])
        wr_hi, wr_lo, br = _router_weights(moe_w_group[layer], moe_b_group[layer], moe_w_expert[layer], moe_b_expert[layer])
        x1, h2, route, er, counts = _mixer_back(acts, w_out, xc, mod_l, ln_mix_g[layer].reshape(1, d), ln_mix_b[layer].reshape(1, d),
                                    wr_hi, wr_lo, br, seq)
        slot, block_expert, n_used, pad, n_slots = _slot_tables(er, counts, t)
        xb = _scatter_rows(slot, pad, h2, n_slots, seq)
        yb = _moe_experts(block_expert, n_used, xb, moe_w_gate, moe_w_up, moe_w_down, layer)
        pending = (slot, x1, yb, route, mod_l, ln_ffn_g[layer].reshape(1, d), ln_ffn_b[layer].reshape(1, d))
        if layer == DEPTH - 1:
            xc = _moe_back(*pending, seq)
    return xc.reshape(bsz, seq, d)
```
